```python
import math
import jax
import jax.numpy as jnp
from jax import lax
import numpy as np

D_MODEL = 1024
BATCH = 32
SEQ = 256
DEPTH = 2
DEC_BATCH = 8
DEC_SEQ = 1024
PAST_LEN = 256

GRID_W = 64
ROPE_BASE = 10000.0
RMS_EPS = 1e-6
Q_BLOCK = 128

H_A = 8
QK_NOPE = 64
QK_ROPE = 32
V_HD_A = 64
Q_LORA = 256
KV_LORA = 128
H_B = 4
DIFF_HD = 64
DIFF_VD = 2 * DIFF_HD
H_C = 8
DK_C = 64
DV_C = D_MODEL // H_C
MLSTM_CHUNK = 64
I_BIAS_INIT = -2.0
F_BIAS_INIT = 3.0
D_FF = -(-8 * D_MODEL // (3 * 256)) * 256

IN_AB = Q_LORA + KV_LORA + QK_ROPE + 3 * H_B * 2 * DIFF_HD
OUT_AB = H_A * V_HD_A + H_B * DIFF_VD
IN_C = 2 * H_C * DK_C + 2 * H_C * DV_C + 4 * H_C
N_EVEN = (DEPTH + 1) // 2
N_ODD = DEPTH // 2

kernel_name = 'hybrid_mla_diff_mlstm_dit_step'


def rms_norm(x, g):
    xf = x.astype(jnp.float32)
    y = xf * lax.rsqrt(jnp.mean(xf * xf, axis=-1, keepdims=True) + RMS_EPS)
    return (y * g.astype(jnp.float32)).astype(x.dtype)


def rope1d(x, pos):
    half = x.shape[-1] // 2
    freqs = jnp.power(ROPE_BASE, -jnp.arange(half, dtype=jnp.float32) / half)
    ang = pos[:, None] * freqs[None, :]
    cos = jnp.cos(ang).astype(x.dtype)
    sin = jnp.sin(ang).astype(x.dtype)
    x1, x2 = x[..., :half], x[..., half:]
    return jnp.concatenate([x1 * cos - x2 * sin, x1 * sin + x2 * cos], axis=-1)


def rope2d(x, rows, cols):
    h = x.shape[-1] // 2
    return jnp.concatenate([rope1d(x[..., :h], rows), rope1d(x[..., h:], cols)], axis=-1)


def rope2d_pair(x, rows, cols):
    return jnp.concatenate([rope2d(x[..., :DIFF_HD], rows, cols), rope2d(x[..., DIFF_HD:], rows, cols)], axis=-1)


def grid_positions(n_tok):
    n_rows = n_tok // GRID_W
    t = jnp.arange(n_rows * GRID_W)
    return (t // GRID_W).astype(jnp.float32), (t % GRID_W).astype(jnp.float32)


def softmax_probs(q, k, scale):
    s = jnp.einsum('bhqd,bhkd->bhqk', q, k).astype(jnp.float32) * scale
    return jax.nn.softmax(s, axis=-1)


def map_query_blocks(fn, *qs):
    B, H, T, _ = qs[0].shape
    nb = T // Q_BLOCK
    blocks = tuple(q.reshape(B, H, nb, Q_BLOCK, q.shape[-1]).transpose(2, 0, 1, 3, 4) for q in qs)
    out = lax.map(lambda qb: fn(*qb), blocks)
    return out.transpose(1, 2, 0, 3, 4).reshape(B, H, T, out.shape[-1])


def adaln(cond, w, b):
    return jnp.split((jax.nn.silu(cond) @ w + b)[:, None, :], 6, axis=-1)


def modulate(x, g, shift, scale):
    return rms_norm(x, g) * (1.0 + scale) + shift


def swiglu(h, w_in, w_out):
    a, b = jnp.split(h @ w_in, 2, axis=-1)
    return (jax.nn.silu(a) * b) @ w_out


def even_mixer(h, w_in, g_ql, g_kvl, w_uq, w_ukv, lam_vec, g_sub, w_out, lam_init, pos=None, ctx=None):
    B, T, _ = h.shape
    c0 = Q_LORA
    c1 = c0 + KV_LORA
    c2 = c1 + QK_ROPE
    c3 = c2 + H_B * 2 * DIFF_HD
    c4 = c3 + H_B * 2 * DIFF_HD
    cq, ckv, kr, dq, dk, dv = jnp.split(h @ w_in, [c0, c1, c2, c3, c4], axis=-1)
    ckv = rms_norm(ckv, g_kvl)
    qa = (rms_norm(cq, g_ql) @ w_uq).reshape(B, T, H_A, QK_NOPE + QK_ROPE).transpose(0, 2, 1, 3)
    q_nope, q_rope = qa[..., :QK_NOPE], qa[..., QK_NOPE:]
    dq = dq.reshape(B, T, H_B, 2 * DIFF_HD).transpose(0, 2, 1, 3)
    dk = dk.reshape(B, T, H_B, 2 * DIFF_HD).transpose(0, 2, 1, 3)
    dv = dv.reshape(B, T, H_B, DIFF_VD).transpose(0, 2, 1, 3)
    own = (ckv, kr, dk, dv)
    if pos is None:
        ckv_all, kr_all, dk_all, dv_all = own
    else:
        rows, cols = pos
        q_rope = rope2d(q_rope, rows, cols)
        dq = rope2d_pair(dq, rows, cols)
        ckv_c, kr_c, dk_c, dv_c = ctx
        ckv_all = jnp.concatenate([ckv, ckv_c], axis=1)
        kr_all = jnp.concatenate([rope2d(kr, rows, cols), kr_c], axis=1)
        dk_all = jnp.concatenate([rope2d_pair(dk, rows, cols), dk_c], axis=2)
        dv_all = jnp.concatenate([dv, dv_c], axis=2)
    K = ckv_all.shape[1]
    kv = (ckv_all @ w_ukv).reshape(B, K, H_A, QK_NOPE + V_HD_A).transpose(0, 2, 1, 3)
    k_a = jnp.concatenate([kv[..., :QK_NOPE], jnp.broadcast_to(kr_all[:, None], (B, H_A, K, QK_ROPE))], axis=-1)
    v_a = kv[..., QK_NOPE:]
    q_a = jnp.concatenate([q_nope, q_rope], axis=-1)
    scale_a = (QK_NOPE + QK_ROPE) ** -0.5

    def mla_block(qb):
        p = softmax_probs(qb, k_a, scale_a)
        return jnp.einsum('bhqk,bhkd->bhqd', p.astype(v_a.dtype), v_a)

    out_a = map_query_blocks(mla_block, q_a)
    lv = lam_vec.astype(jnp.float32)
    lam = jnp.exp(jnp.sum(lv[0] * lv[1])) - jnp.exp(jnp.sum(lv[2] * lv[3])) + lam_init
    k1, k2 = dk_all[..., :DIFF_HD], dk_all[..., DIFF_HD:]
    scale_b = DIFF_HD ** -0.5

    def diff_block(q1, q2):
        p = softmax_probs(q1, k1, scale_b) - lam * softmax_probs(q2, k2, scale_b)
        return jnp.einsum('bhqk,bhkd->bhqd', p.astype(dv_all.dtype), dv_all)

    out_b = map_query_blocks(diff_block, dq[..., :DIFF_HD], dq[..., DIFF_HD:])
    out_b = rms_norm(out_b, g_sub) * (1.0 - lam_init)
    merged = jnp.concatenate([out_a.transpose(0, 2, 1, 3).reshape(B, T, H_A * V_HD_A),
                              out_b.transpose(0, 2, 1, 3).reshape(B, T, H_B * DIFF_VD)], axis=-1)
    return merged @ w_out, own


def mlstm_chunked(q, k, v, li, lf, C0, n0, m0):
    B, H, T, dk = q.shape
    dv = v.shape[-1]
    L = MLSTM_CHUNK
    nc = T // L
    f32 = jnp.float32
    qc = q.astype(f32).reshape(B, H, nc, L, dk)
    kc = k.astype(f32).reshape(B, H, nc, L, dk)
    vc = v.astype(f32).reshape(B, H, nc, L, dv)
    lic = li.reshape(B, H, nc, L)
    b = jnp.cumsum(lf.reshape(B, H, nc, L), axis=-1)
    g = b[..., -1]
    a = g[..., None] - b + lic
    a_max = jnp.max(a, axis=-1)
    w = jnp.exp(a - a_max[..., None])
    kv_chunk = jnp.einsum('bhcl,bhcld,bhcle->bhcde', w, kc, vc)
    kn_chunk = jnp.einsum('bhcl,bhcld->bhcd', w, kc)

    def step(carry, inp):
        C, n, m = carry
        g_j, am_j, kv_j, kn_j = inp
        m_new = jnp.maximum(g_j + m, am_j)
        s_old = jnp.exp(g_j + m - m_new)
        s_new = jnp.exp(am_j - m_new)
        C_new = s_old[..., None, None] * C + s_new[..., None, None] * kv_j
        n_new = s_old[..., None] * n + s_new[..., None] * kn_j
        return (C_new, n_new, m_new), (C, n, m)

    xs = (jnp.moveaxis(g, 2, 0), jnp.moveaxis(a_max, 2, 0), jnp.moveaxis(kv_chunk, 2, 0), jnp.moveaxis(kn_chunk, 2, 0))
    init = (C0.astype(f32), n0.astype(f32), m0.astype(f32))
    (Cf, nf, mf), (Cp, np_, mp) = lax.scan(step, init, xs)
    Cp = jnp.moveaxis(Cp, 0, 2)
    np_ = jnp.moveaxis(np_, 0, 2)
    mp = jnp.moveaxis(mp, 0, 2)
    mask = jnp.tril(jnp.ones((L, L), dtype=bool))
    dlog = jnp.where(mask, b[..., :, None] - b[..., None, :] + lic[..., None, :], -jnp.inf)
    m_inter = b + mp[..., None]
    m_t = jnp.maximum(m_inter, jnp.max(dlog, axis=-1))
    s = jnp.einsum('bhcld,bhcsd->bhcls', qc, kc) * jnp.exp(dlog - m_t[..., None])
    inter = jnp.exp(m_inter - m_t)
    num = jnp.einsum('bhcls,bhcse->bhcle', s, vc) + inter[..., None] * jnp.einsum('bhcld,bhcde->bhcle', qc, Cp)
    den = jnp.sum(s, axis=-1) + inter * jnp.einsum('bhcld,bhcd->bhcl', qc, np_)
    h = num / jnp.maximum(jnp.abs(den), jnp.exp(-m_t))[..., None]
    return h.reshape(B, H, T, dv), Cf, nf, mf


def odd_mixer(h, w_in, b_gate, g_norm, w_out, C0, n0, m0):
    B, T, _ = h.shape
    s0 = H_C * DK_C
    q, k, v, o, gates = jnp.split(h @ w_in, [s0, 2 * s0, 2 * s0 + H_C * DV_C, 2 * s0 + 2 * H_C * DV_C], axis=-1)

    def heads(x, d):
        return x.reshape(B, T, H_C, d).transpose(0, 2, 1, 3)

    q = heads(q, DK_C)
    k = heads(k, DK_C) * (DK_C ** -0.5)
    v = heads(v, DV_C)
    gts = (gates.astype(jnp.float32) + b_gate.astype(jnp.float32)).reshape(B, T, 4, H_C).transpose(2, 0, 3, 1)
    li_f, lf_f = gts[0], jax.nn.log_sigmoid(gts[1])
    li_b, lf_b = gts[2], jax.nn.log_sigmoid(gts[3])
    h_f, Cf, nf, mf = mlstm_chunked(q, k, v, li_f, lf_f, C0[:, 0], n0[:, 0], m0[:, 0])

    def flip(x):
        return jnp.flip(x, axis=2)

    h_b, Cb, nb, mb = mlstm_chunked(flip(q), flip(k), flip(v), flip(li_b), flip(lf_b), C0[:, 1], n0[:, 1], m0[:, 1])
    hs = rms_norm(h_f + flip(h_b), g_norm).transpose(0, 2, 1, 3).reshape(B, T, H_C * DV_C)
    y = (hs.astype(h.dtype) * jax.nn.sigmoid(o)) @ w_out
    return y, (jnp.stack([Cf, Cb], axis=1), jnp.stack([nf, nb], axis=1), jnp.stack([mf, mb], axis=1))


def setup_inputs(seed: int = 0) -> dict:
    key = jax.random.key(seed)
    ks = iter(jax.random.split(key, 40))

    def nrm(shape, scale=1.0):
        return scale * jax.random.normal(next(ks), shape, jnp.float32)

    def gain(shape):
        return 1.0 + nrm(shape, 0.05)

    gate_offsets = jnp.repeat(jnp.array([I_BIAS_INIT, F_BIAS_INIT, I_BIAS_INIT, F_BIAS_INIT], jnp.float32), H_C)
    return {
        'x_prompt': nrm((BATCH, SEQ, D_MODEL)),
        'x_sample': nrm((DEC_BATCH, DEC_SEQ, D_MODEL)),
        'cache_mla_ckv': nrm((DEC_BATCH, N_EVEN, PAST_LEN, KV_LORA)),
        'cache_mla_krope': nrm((DEC_BATCH, N_EVEN, PAST_LEN, QK_ROPE)),
        'cache_diff_k': nrm((DEC_BATCH, N_EVEN, H_B, PAST_LEN, 2 * DIFF_HD)),
        'cache_diff_v': nrm((DEC_BATCH, N_EVEN, H_B, PAST_LEN, DIFF_VD)),
        'state_mlstm_C': nrm((DEC_BATCH, N_ODD, 2, H_C, DK_C, DV_C), 0.3),
        'state_mlstm_n': nrm((DEC_BATCH, N_ODD, 2, H_C, DK_C), 0.3),
        'state_mlstm_m': nrm((DEC_BATCH, N_ODD, 2, H_C)),
        'c': nrm((DEC_BATCH, D_MODEL)),
        'c_ctx': nrm((D_MODEL,)),
        'w_ada': nrm((DEPTH, D_MODEL, 6 * D_MODEL), 0.5 * D_MODEL ** -0.5),
        'b_ada': nrm((DEPTH, 6 * D_MODEL), 0.1),
        'g_mix': gain((DEPTH, D_MODEL)),
        'g_ffn': gain((DEPTH, D_MODEL)),
        'w_ffn_in': nrm((DEPTH, D_MODEL, 2 * D_FF), D_MODEL ** -0.5),
        'w_ffn_out': nrm((DEPTH, D_FF, D_MODEL), D_FF ** -0.5),
        'w_in_ab': nrm((N_EVEN, D_MODEL, IN_AB), D_MODEL ** -0.5),
        'g_q_lora': gain((N_EVEN, Q_LORA)),
        'g_kv_lora': gain((N_EVEN, KV_LORA)),
        'w_uq': nrm((N_EVEN, Q_LORA, H_A * (QK_NOPE + QK_ROPE)), Q_LORA ** -0.5),
        'w_ukv': nrm((N_EVEN, KV_LORA, H_A * (QK_NOPE + V_HD_A)), KV_LORA ** -0.5),
        'diff_lambda': nrm((N_EVEN, 4, DIFF_HD), 0.1),
        'g_diff_subln': gain((N_EVEN, DIFF_VD)),
        'w_out_ab': nrm((N_EVEN, OUT_AB, D_MODEL), OUT_AB ** -0.5),
        'w_in_c': nrm((N_ODD, D_MODEL, IN_C), D_MODEL ** -0.5),
        'b_gate_c': gate_offsets[None, :] + nrm((N_ODD, 4 * H_C), 0.1),
        'g_mlstm': gain((N_ODD, DV_C)),
        'w_out_c': nrm((N_ODD, H_C * DV_C, D_MODEL), (H_C * DV_C) ** -0.5),
        'g_final': gain((D_MODEL,)),
    }


def reference(x_prompt, x_sample, cache_mla_ckv, cache_mla_krope, cache_diff_k, cache_diff_v,
              state_mlstm_C, state_mlstm_n, state_mlstm_m, c, c_ctx,
              w_ada, b_ada, g_mix, g_ffn, w_ffn_in, w_ffn_out,
              w_in_ab, g_q_lora, g_kv_lora, w_uq, w_ukv, diff_lambda, g_diff_subln, w_out_ab,
              w_in_c, b_gate_c, g_mlstm, w_out_c, g_final):
    pos = grid_positions(x_sample.shape[1])
    xp, xs = x_prompt, x_sample
    ckv_l, kr_l, dk_l, dv_l, C_l, n_l, m_l = [], [], [], [], [], [], []
    for l in range(DEPTH):
        j = l // 2
        p_sh1, p_sc1, p_g1, p_sh2, p_sc2, p_g2 = adaln(c_ctx[None], w_ada[l], b_ada[l])
        s_sh1, s_sc1, s_g1, s_sh2, s_sc2, s_g2 = adaln(c, w_ada[l], b_ada[l])
        hp = modulate(xp, g_mix[l], p_sh1, p_sc1)
        hs = modulate(xs, g_mix[l], s_sh1, s_sc1)
        if l % 2 == 0:
            ep = (w_in_ab[j], g_q_lora[j], g_kv_lora[j], w_uq[j], w_ukv[j], diff_lambda[j], g_diff_subln[j], w_out_ab[j])
            lam_init = 0.8 - 0.6 * math.exp(-0.3 * l)
            yp, (ckv_p, kr_p, dk_p, dv_p) = even_mixer(hp, *ep, lam_init)
            ctx = (cache_mla_ckv[:, j], cache_mla_krope[:, j], cache_diff_k[:, j], cache_diff_v[:, j])
            ys, _ = even_mixer(hs, *ep, lam_init, pos, ctx)
            ckv_l.append(ckv_p)
            kr_l.append(kr_p)
            dk_l.append(dk_p)
            dv_l.append(dv_p)
        else:
            op = (w_in_c[j], b_gate_c[j], g_mlstm[j], w_out_c[j])
            Bp = xp.shape[0]
            z_C = jnp.zeros((Bp, 2, H_C, DK_C, DV_C), jnp.float32)
            z_n = jnp.zeros((Bp, 2, H_C, DK_C), jnp.float32)
            z_m = jnp.zeros((Bp, 2, H_C), jnp.float32)
            yp, (C_p, n_p, m_p) = odd_mixer(hp, *op, z_C, z_n, z_m)
            ys, _ = odd_mixer(hs, *op, state_mlstm_C[:, j], state_mlstm_n[:, j], state_mlstm_m[:, j])
            C_l.append(C_p)
            n_l.append(n_p)
            m_l.append(m_p)
        xp = xp + p_g1 * yp
        xs = xs + s_g1 * ys
        xp = xp + p_g2 * swiglu(modulate(xp, g_ffn[l], p_sh2, p_sc2), w_ffn_in[l], w_ffn_out[l])
        xs = xs + s_g2 * swiglu(modulate(xs, g_ffn[l], s_sh2, s_sc2), w_ffn_in[l], w_ffn_out[l])
    y_prompt = rms_norm(xp, g_final)
    y_sample = rms_norm(xs, g_final)
    new_mla_ckv = jnp.stack(ckv_l, axis=1)
    new_mla_krope = jnp.stack(kr_l, axis=1)
    new_diff_k = jnp.stack(dk_l, axis=1)
    new_diff_v = jnp.stack(dv_l, axis=1)
    new_mlstm_C = jnp.stack(C_l, axis=1)
    new_mlstm_n = jnp.stack(n_l, axis=1)
    new_mlstm_m = jnp.stack(m_l, axis=1)
    return (y_prompt, y_sample, new_mla_ckv, new_mla_krope, new_diff_k, new_diff_v, new_mlstm_C, new_mlstm_n, new_mlstm_m)
```

```python
import functools
import math

import jax
import jax.numpy as jnp
from jax import lax
from jax.experimental import pallas as pl
from jax.experimental.pallas import tpu as pltpu

F32 = jnp.float32
BF16 = jnp.bfloat16

D_MODEL = 1024
DEPTH = 2
GRID_W = 64
ROPE_BASE = 10000.0
RMS_EPS = 1e-6
H_A = 8
QK_NOPE = 64
QK_ROPE = 32
V_HD_A = 64
Q_LORA = 256
KV_LORA = 128
H_B = 4
DIFF_HD = 64
DIFF_VD = 2 * DIFF_HD
H_C = 8
DK_C = 64
DV_C = D_MODEL // H_C
D_FF = -(-8 * D_MODEL // (3 * 256)) * 256

LANES = 128
HEAD_PAD = LANES
ROPE_LANE0 = QK_NOPE
N_MOD = 6
COND_ROWS = 16

TM_PROJ = 256
TQ_ATTN = 256
TM_TAIL = 512
FF_CHUNK = 1408
MLSTM_L = 128
ADA_TN = 1024
VMEM_LIMIT = 56 * 1024 * 1024

assert D_FF % FF_CHUNK == 0 and FF_CHUNK % LANES == 0


def _dot(a, b):
    return jnp.dot(a, b, preferred_element_type=F32)


def _dot_nt(a, b):
    return lax.dot_general(a, b, (((1,), (1,)), ((), ())), preferred_element_type=F32)


def _dot_tn(a, b):
    return lax.dot_general(a, b, (((0,), (0,)), ((), ())), preferred_element_type=F32)


def _rms(x):
    return x * lax.rsqrt(jnp.mean(x * x, axis=-1, keepdims=True) + RMS_EPS)


def _modulate(x, g, shift, scale):
    return (_rms(x) * g) * (1.0 + scale) + shift


def _params(semantics):
    return pltpu.CompilerParams(dimension_semantics=semantics, vmem_limit_bytes=VMEM_LIMIT)


def _ada_kernel(cond_ref, w_ref, b_ref, o_ref):
    c = cond_ref[...]
    s = (c * jax.nn.sigmoid(c)).astype(BF16)
    o_ref[0] = _dot(s, w_ref[0].astype(BF16)) + b_ref[0]


def _ada(cond, w_ada, b_ada):
    n_out = w_ada.shape[-1]
    return pl.pallas_call(
        _ada_kernel,
        grid=(DEPTH, n_out // ADA_TN),
        in_specs=[
            pl.BlockSpec((COND_ROWS, D_MODEL), lambda l, n: (0, 0)),
            pl.BlockSpec((1, D_MODEL, ADA_TN), lambda l, n: (l, 0, n)),
            pl.BlockSpec((1, 1, ADA_TN), lambda l, n: (l, 0, n)),
        ],
        out_specs=pl.BlockSpec((1, COND_ROWS, ADA_TN), lambda l, n: (l, 0, n)),
        out_shape=jax.ShapeDtypeStruct((DEPTH, COND_ROWS, n_out), F32),
        compiler_params=_params(("arbitrary", "arbitrary")),
        name="ada",
    )(cond, w_ada, b_ada.reshape(DEPTH, 1, n_out))


def _rope(x, c, sa, sb, off):
    return x * c + pltpu.roll(x, LANES - off, 1) * sa + pltpu.roll(x, off, 1) * sb


def _proj0_kernel(*refs, rope, emit_cache):
    x_ref, mod_ref, g_ref, w0_ref, gq_ref, gkv_ref, wq_ref, wukv_ref = refs[:8]
    refs = refs[8:]
    if rope:
        cm_ref, sam_ref, sbm_ref, cd_ref, sad_ref, sbd_ref = refs[:6]
        refs = refs[6:]
    qm_ref, km_ref, vm_ref, dq_ref, dk_ref, dv_ref = refs[:6]
    refs = refs[6:]
    if emit_cache:
        ckvf_ref, krf_ref, dkf_ref, dvf_ref = refs

    h = _modulate(x_ref[...], g_ref[...], mod_ref[0, 0:1, :], mod_ref[0, 1:2, :]).astype(BF16)
    proj = _dot(h, w0_ref[...])
    cq = proj[:, 0:Q_LORA]
    ckv = _rms(proj[:, Q_LORA:Q_LORA + KV_LORA]) * gkv_ref[...]
    krb = proj[:, 384:512]
    qa = _dot((_rms(cq) * gq_ref[...]).astype(BF16), wq_ref[...])
    kv = _dot(ckv.astype(BF16), wukv_ref[...])
    if emit_cache:
        ckvf_ref[...] = ckv
        krf_ref[...] = krb[:, ROPE_LANE0:ROPE_LANE0 + QK_ROPE]
    if rope:
        cm, sam, sbm = cm_ref[...], sam_ref[...], sbm_ref[...]
        cd, sad, sbd = cd_ref[...], sad_ref[...], sbd_ref[...]
        krb = _rope(krb, cm, sam, sbm, QK_ROPE // 4)
    for hd in range(H_A):
        sl = slice(hd * HEAD_PAD, (hd + 1) * HEAD_PAD)
        qh = qa[:, sl]
        if rope:
            qh = _rope(qh, cm, sam, sbm, QK_ROPE // 4)
        qm_ref[:, sl] = qh.astype(BF16)
        km_ref[:, sl] = (kv[:, sl] + krb).astype(BF16)
    vm_ref[...] = kv[:, H_A * HEAD_PAD:].astype(BF16)
    for hd in range(H_B):
        sl = slice(hd * LANES, (hd + 1) * LANES)
        dq = proj[:, 512 + hd * LANES:512 + (hd + 1) * LANES]
        dk = proj[:, 1024 + hd * LANES:1024 + (hd + 1) * LANES]
        dv = proj[:, 1536 + hd * LANES:1536 + (hd + 1) * LANES]
        if emit_cache:
            dkf_ref[0, hd] = dk
            dvf_ref[0, hd] = dv
        if rope:
            dq = _rope(dq, cd, sad, sbd, DIFF_HD // 4)
            dk = _rope(dk, cd, sad, sbd, DIFF_HD // 4)
        dq_ref[:, sl] = (dq * (DIFF_HD ** -0.5)).astype(BF16)
        dk_ref[:, sl] = dk.astype(BF16)
        dv_ref[:, sl] = dv.astype(BF16)


def _proj0(x, mod, mod_row, g, w0, gq, gkv, wq, wukv, rope_tabs, seq):
    n = x.shape[0]
    tm = TM_PROJ
    rope = rope_tabs is not None
    emit_cache = not rope
    tiles_per_seq = seq // tm
    full = lambda shape: pl.BlockSpec(shape, lambda i: (0,) * len(shape))
    in_specs = [
        pl.BlockSpec((tm, D_MODEL), lambda i: (i, 0)),
        pl.BlockSpec((1, N_MOD, D_MODEL), lambda i: (mod_row(i), 0, 0)),
        full((1, D_MODEL)), full(w0.shape), full((1, Q_LORA)), full((1, KV_LORA)),
        full(wq.shape), full(wukv.shape),
    ]
    args = [x, mod, g, w0, gq, gkv, wq, wukv]
    if rope:
        in_specs += [pl.BlockSpec((tm, LANES), lambda i: (i % tiles_per_seq, 0))] * 6
        args += list(rope_tabs)
    row = lambda w: pl.BlockSpec((tm, w), lambda i: (i, 0))
    out_specs = [row(1024), row(1024), row(1024), row(512), row(512), row(512)]
    out_shape = [jax.ShapeDtypeStruct((n, w), BF16) for w in (1024, 1024, 1024, 512, 512, 512)]
    if emit_cache:
        assert tiles_per_seq == 1
        cache = pl.BlockSpec((1, H_B, tm, LANES), lambda i: (i, 0, 0, 0))
        out_specs += [row(KV_LORA), row(QK_ROPE), cache, cache]
        out_shape += [jax.ShapeDtypeStruct((n, KV_LORA), F32), jax.ShapeDtypeStruct((n, QK_ROPE), F32),
                      jax.ShapeDtypeStruct((n // seq, H_B, seq, LANES), F32),
                      jax.ShapeDtypeStruct((n // seq, H_B, seq, LANES), F32)]
    return pl.pallas_call(
        functools.partial(_proj0_kernel, rope=rope, emit_cache=emit_cache),
        grid=(n // tm,),
        in_specs=in_specs, out_specs=out_specs, out_shape=out_shape,
        compiler_params=_params(("arbitrary",)),
        name="proj0_rope" if rope else "proj0_ctx",
    )(*args)


def _softmax_parts(scores, scale):
    m = functools.reduce(jnp.maximum, [jnp.max(s, axis=-1, keepdims=True) for s in scores])
    es = [jnp.exp((s - m) * scale) for s in scores]
    l = functools.reduce(jnp.add, [jnp.sum(e, axis=-1, keepdims=True) for e in es])
    return es, 1.0 / l


def _attn_kernel(*refs, has_ctx, lam_init):
    q_ref, dq_ref, k_ref, v_ref, dk_ref, dv_ref, lamv_ref, gsub_ref = refs[:8]
    refs = refs[8:]
    if has_ctx:
        ckvc_ref, krc_ref, dkc_ref, dvc_ref, wukv_ref, o_ref, kc_s, vc_s, dkc_s, dvc_s = refs

        @pl.when(pl.program_id(1) == 0)
        def _():
            kv = _dot(ckvc_ref[0].astype(BF16), wukv_ref[...])
            krb = krc_ref[0]
            for hd in range(H_A):
                sl = slice(hd * HEAD_PAD, (hd + 1) * HEAD_PAD)
                kc_s[:, sl] = (kv[:, sl] + krb).astype(BF16)
            vc_s[...] = kv[:, H_A * HEAD_PAD:].astype(BF16)
            for hd in range(H_B):
                sl = slice(hd * LANES, (hd + 1) * LANES)
                dkc_s[:, sl] = dkc_ref[0, hd].astype(BF16)
                dvc_s[:, sl] = dvc_ref[0, hd].astype(BF16)
    else:
        (o_ref,) = refs

    scale_a = (QK_NOPE + QK_ROPE) ** -0.5
    for j in range(H_A // 2):
        acc = None
        for hh in range(2):
            sl = slice((2 * j + hh) * HEAD_PAD, (2 * j + hh + 1) * HEAD_PAD)
            segs = [(k_ref[:, sl], v_ref[:, sl])]
            if has_ctx:
                segs.append((kc_s[:, sl], vc_s[:, sl]))
            qh = q_ref[:, sl]
            es, r = _softmax_parts([_dot_nt(qh, kk) for kk, _ in segs], scale_a)
            for e, (_, vv) in zip(es, segs):
                pv = _dot((e * r).astype(BF16), vv)
                acc = pv if acc is None else acc + pv
        o_ref[:, j * LANES:(j + 1) * LANES] = acc.astype(BF16)

    lv = lamv_ref[...]
    lam = (jnp.exp(jnp.sum(lv[0:1] * lv[1:2], axis=-1, keepdims=True))
           - jnp.exp(jnp.sum(lv[2:3] * lv[3:4], axis=-1, keepdims=True)) + lam_init)
    tq = dq_ref.shape[0]
    lo = lax.broadcasted_iota(jnp.int32, (tq, LANES), 1) < DIFF_HD
    for hd in range(H_B):
        sl = slice(hd * LANES, (hd + 1) * LANES)
        dq = dq_ref[:, sl].astype(F32)
        q1 = jnp.where(lo, dq, 0.0).astype(BF16)
        q2 = jnp.where(lo, 0.0, dq).astype(BF16)
        segs = [(dk_ref[:, sl], dv_ref[:, sl])]
        if has_ctx:
            segs.append((dkc_s[:, sl], dvc_s[:, sl]))
        e1, r1 = _softmax_parts([_dot_nt(q1, kk) for kk, _ in segs], 1.0)
        e2, r2 = _softmax_parts([_dot_nt(q2, kk) for kk, _ in segs], 1.0)
        r2 = lam * r2
        acc = None
        for a, b, (_, vv) in zip(e1, e2, segs):
            pv = _dot((a * r1 - b * r2).astype(BF16), vv)
            acc = pv if acc is None else acc + pv
        ob = (_rms(acc) * gsub_ref[...]) * (1.0 - lam_init)
        o_ref[:, H_A * V_HD_A + hd * LANES:H_A * V_HD_A + (hd + 1) * LANES] = ob.astype(BF16)


def _attn0(qm, km, vm, dq, dk, dv, lamv, gsub, ctx, wukv, seq, lam_init):
    n = qm.shape[0]
    nb = n // seq
    tq = TQ_ATTN
    has_ctx = ctx is not None
    full = lambda shape: pl.BlockSpec(shape, lambda b, t: (0,) * len(shape))
    qrow = lambda w: pl.BlockSpec((tq, w), lambda b, t: (b * (seq // tq) + t, 0))
    krow = lambda w: pl.BlockSpec((seq, w), lambda b, t: (b, 0))
    in_specs = [qrow(1024), qrow(512), krow(1024), krow(1024), krow(512), krow(512),
                full((4, DIFF_HD)), full((1, DIFF_VD))]
    args = [qm, dq, km, vm, dk, dv, lamv, gsub]
    scratch = []
    if has_ctx:
        ckv_c, krb_c, dk_c, dv_c = ctx
        past = ckv_c.shape[1]
        in_specs += [pl.BlockSpec((1, past, LANES), lambda b, t: (b, 0, 0)),
                     pl.BlockSpec((1, past, LANES), lambda b, t: (b, 0, 0)),
                     pl.BlockSpec((1, H_B, past, LANES), lambda b, t: (b, 0, 0, 0)),
                     pl.BlockSpec((1, H_B, past, LANES), lambda b, t: (b, 0, 0, 0)),
                     full(wukv.shape)]
        args += [ckv_c, krb_c, dk_c, dv_c, wukv]
        scratch = [pltpu.VMEM((past, 1024), BF16), pltpu.VMEM((past, 1024), BF16),
                   pltpu.VMEM((past, 512), BF16), pltpu.VMEM((past, 512), BF16)]
    return pl.pallas_call(
        functools.partial(_attn_kernel, has_ctx=has_ctx, lam_init=lam_init),
        grid=(nb, seq // tq),
        in_specs=in_specs,
        out_specs=pl.BlockSpec((tq, D_MODEL), lambda b, t: (b * (seq // tq) + t, 0)),
        out_shape=jax.ShapeDtypeStruct((n, D_MODEL), BF16),
        scratch_shapes=scratch,
        compiler_params=_params(("arbitrary", "arbitrary")),
        name="attn0_lat" if has_ctx else "attn0_ctx",
    )(*args)


def _tail_kernel(*refs, final):
    x_ref, a_ref, mod_ref, wo_ref, gf_ref, win_ref, wout_ref = refs[:7]
    if final:
        gfin_ref, o_ref = refs[7:]
    else:
        (o_ref,) = refs[7:]
    x1 = x_ref[...] + mod_ref[0, 2:3, :] * _dot(a_ref[...], wo_ref[...])
    h = _modulate(x1, gf_ref[...], mod_ref[0, 3:4, :], mod_ref[0, 4:5, :]).astype(BF16)
    acc = None
    for c in range(D_FF // FF_CHUNK):
        a = _dot(h, win_ref[:, c * FF_CHUNK:(c + 1) * FF_CHUNK])
        b = _dot(h, win_ref[:, D_FF + c * FF_CHUNK:D_FF + (c + 1) * FF_CHUNK])
        act = ((a * jax.nn.sigmoid(a)) * b).astype(BF16)
        part = _dot(act, wout_ref[c * FF_CHUNK:(c + 1) * FF_CHUNK, :])
        acc = part if acc is None else acc + part
    x2 = x1 + mod_ref[0, 5:6, :] * acc
    if final:
        x2 = _rms(x2) * gfin_ref[...]
    o_ref[...] = x2


def _tail(x, a, mod, mod_row, wo, gf, win, wout, gfin, name):
    n = x.shape[0]
    tm = TM_TAIL
    final = gfin is not None
    full = lambda shape: pl.BlockSpec(shape, lambda i: (0,) * len(shape))
    in_specs = [
        pl.BlockSpec((tm, D_MODEL), lambda i: (i, 0)),
        pl.BlockSpec((tm, D_MODEL), lambda i: (i, 0)),
        pl.BlockSpec((1, N_MOD, D_MODEL), lambda i: (mod_row(i), 0, 0)),
        full(wo.shape), full((1, D_MODEL)), full(win.shape), full(wout.shape),
    ]
    args = [x, a, mod, wo, gf, win, wout]
    if final:
        in_specs.append(full((1, D_MODEL)))
        args.append(gfin)
    return pl.pallas_call(
        functools.partial(_tail_kernel, final=final),
        grid=(n // tm,),
        in_specs=in_specs,
        out_specs=pl.BlockSpec((tm, D_MODEL), lambda i: (i, 0)),
        out_shape=jax.ShapeDtypeStruct((n, D_MODEL), F32),
        compiler_params=_params(("arbitrary",)),
        name=name,
    )(*args)


def _split3(x):
    hi = x.astype(BF16)
    r1 = x - hi.astype(F32)
    mid = r1.astype(BF16)
    lo = (r1 - mid.astype(F32)).astype(BF16)
    return hi, mid, lo


def _proj1_kernel(x_ref, mod_ref, g_ref, w1_ref, bg_ref, q_ref, k_ref, v_ref, o_ref, gc_ref, gr_ref):
    tm = x_ref.shape[0]
    h = _modulate(x_ref[...], g_ref[...], mod_ref[0, 0:1, :], mod_ref[0, 1:2, :]).astype(BF16)
    proj = _dot(h, w1_ref[...])
    hk = H_C * DK_C
    hv = H_C * DV_C
    q_ref[...] = proj[:, 0:hk].astype(BF16)
    k_ref[...] = proj[:, hk:2 * hk] * (DK_C ** -0.5)
    v_ref[...] = proj[:, 2 * hk:2 * hk + hv].astype(BF16)
    o_ref[...] = proj[:, 2 * hk + hv:2 * hk + 2 * hv]
    gates = proj[:, 2 * hk + 2 * hv:2 * hk + 2 * hv + LANES] + bg_ref[...]
    lf = jnp.minimum(gates, 0.0) - jnp.log1p(jnp.exp(-jnp.abs(gates)))
    r = lax.broadcasted_iota(jnp.int32, (tm, tm), 0)
    c = lax.broadcasted_iota(jnp.int32, (tm, tm), 1)
    same = (r // MLSTM_L) == (c // MLSTM_L)
    pre = jnp.where(same & (c <= r), 1.0, 0.0).astype(BF16)
    suf = jnp.where(same & (c >= r), 1.0, 0.0).astype(BF16)
    parts = _split3(lf)
    b_fwd = functools.reduce(jnp.add, [_dot(pre, p) for p in parts])
    b_bwd = functools.reduce(jnp.add, [_dot(suf, p) for p in parts])
    t = (lax.broadcasted_iota(jnp.int32, (tm, LANES), 1) % 8) // 2
    gc = jnp.where(t == 1, b_fwd, jnp.where(t == 3, b_bwd, gates))
    for j in range(H_C // 2):
        gc_ref[j] = gc[:, 8 * j:8 * (j + 1)]
    gr_ref[...] = jnp.transpose(gc)[0:4 * H_C, :]


def _proj1(x, mod, mod_row, g, w1, bg):
    n = x.shape[0]
    tm = TM_PROJ
    assert tm % MLSTM_L == 0
    full = lambda shape: pl.BlockSpec(shape, lambda i: (0,) * len(shape))
    row = lambda w: pl.BlockSpec((tm, w), lambda i: (i, 0))
    hk, hv = H_C * DK_C, H_C * DV_C
    return pl.pallas_call(
        _proj1_kernel,
        grid=(n // tm,),
        in_specs=[row(D_MODEL), pl.BlockSpec((1, N_MOD, D_MODEL), lambda i: (mod_row(i), 0, 0)),
                  full((1, D_MODEL)), full(w1.shape), full((1, LANES))],
        out_specs=[row(hk), row(hk), row(hv), row(hv),
                   pl.BlockSpec((H_C // 2, tm, 8), lambda i: (0, i, 0)),
                   pl.BlockSpec((4 * H_C, tm), lambda i: (0, i))],
        out_shape=[jax.ShapeDtypeStruct((n, hk), BF16), jax.ShapeDtypeStruct((n, hk), F32),
                   jax.ShapeDtypeStruct((n, hv), BF16), jax.ShapeDtypeStruct((n, hv), F32),
                   jax.ShapeDtypeStruct((H_C // 2, n, 8), F32), jax.ShapeDtypeStruct((4 * H_C, n), F32)],
        compiler_params=_params(("arbitrary",)),
        name="proj1",
    )(x, mod, g, w1, bg)


def _mlstm_kernel(*refs, has_state, emit_state):
    q_ref, k_ref, v_ref, o_ref, gc_ref, gr_ref, gn_ref = refs[:7]
    refs = refs[7:]
    if has_state:
        c0_ref, n0_ref, m0_ref = refs[:3]
        refs = refs[3:]
    hs_ref = refs[0]
    refs = refs[1:]
    if emit_state:
        cf_ref, st_ref = refs[:2]
        refs = refs[2:]
    c_s, n_s, m_s, h_s = refs

    L = MLSTM_L
    seq = q_ref.shape[0]
    nc = seq // L
    lane = lax.broadcasted_iota(jnp.int32, (1, LANES), 1)
    head_mask = [lane < DK_C, lane >= DK_C]
    ri = lax.broadcasted_iota(jnp.int32, (L, L), 0)
    ci = lax.broadcasted_iota(jnp.int32, (L, L), 1)
    causal = [ci <= ri, ci >= ri]

    h_s[...] = jnp.zeros_like(h_s)
    for d in range(2):
        for hh in range(2):
            if has_state:
                c_s[d, hh] = jnp.zeros((LANES, DV_C), F32)
                c_s[d, hh, hh * DK_C:(hh + 1) * DK_C, :] = c0_ref[0, d, hh]
                n_s[d, hh] = jnp.where(head_mask[hh], n0_ref[0, d, 0], 0.0)
                m_s[d, hh] = m0_ref[0, d, hh]
            else:
                c_s[d, hh] = jnp.zeros((LANES, DV_C), F32)
                n_s[d, hh] = jnp.zeros((1, LANES), F32)
                m_s[d, hh] = jnp.zeros((1, LANES), F32)

    def chunk_step(i, carry):
        for d in range(2):
            c = i if d == 0 else nc - 1 - i
            r0 = pl.multiple_of(c * L, L)
            qc = q_ref[pl.ds(r0, L), :]
            kc = k_ref[pl.ds(r0, L), :]
            kcb = kc.astype(BF16)
            vc = v_ref[pl.ds(r0, L), :]
            gcol = gc_ref[0, pl.ds(r0, L), :]
            grow = gr_ref[:, pl.ds(r0, L)]
            for hh in range(2):
                li_col = gcol[:, 4 * d + hh:4 * d + hh + 1]
                b_col = gcol[:, 4 * d + 2 + hh:4 * d + 3 + hh]
                li_row = grow[4 * d + hh:4 * d + hh + 1, :]
                b_row = grow[4 * d + 2 + hh:4 * d + 3 + hh, :]
                g_tot = b_col[L - 1:L, :] if d == 0 else b_col[0:1, :]
                m_prev = m_s[d, hh][:, 0:1]
                c_prev = c_s[d, hh]
                n_prev = n_s[d, hh]
                vh = vc[:, hh * DV_C:(hh + 1) * DV_C]
                qf = jnp.where(head_mask[hh], qc.astype(F32), 0.0)
                qh = qf.astype(BF16)

                dlog = jnp.where(causal[d], (b_col - b_row) + li_row, -jnp.inf)
                m_inter = b_col + m_prev
                m_t = jnp.maximum(m_inter, jnp.max(dlog, axis=-1, keepdims=True))
                s = _dot_nt(qh, kcb) * jnp.exp(dlog - m_t)
                inter = jnp.exp(m_inter - m_t)
                num = _dot(s.astype(BF16), vh) + inter * _dot(qh, c_prev.astype(BF16))
                qn = jnp.sum(qf * n_prev, axis=-1, keepdims=True)
                den = jnp.sum(s, axis=-1, keepdims=True) + inter * qn
                hval = num / jnp.maximum(jnp.abs(den), jnp.exp(-m_t))
                cur = h_s[pl.ds(r0, L), hh * DV_C:(hh + 1) * DV_C]
                h_s[pl.ds(r0, L), hh * DV_C:(hh + 1) * DV_C] = cur + hval

                a_col = (g_tot - b_col) + li_col
                a_max = jnp.max(a_col, axis=0, keepdims=True)
                m_new = jnp.maximum(g_tot + m_prev, a_max)
                s_old = jnp.exp(g_tot + m_prev - m_new)
                s_new = jnp.exp(a_max - m_new)
                kw = jnp.where(head_mask[hh], kc, 0.0) * jnp.exp(a_col - a_max)
                c_s[d, hh] = s_old * c_prev + s_new * _dot_tn(kw.astype(BF16), vh)
                n_s[d, hh] = s_old * n_prev + s_new * jnp.sum(kw, axis=0, keepdims=True)
                m_s[d, hh] = jnp.broadcast_to(m_new, (1, LANES))
        return carry

    lax.fori_loop(0, nc, chunk_step, 0)

    for hh in range(2):
        sl = slice(hh * DV_C, (hh + 1) * DV_C)
        y = (_rms(h_s[:, sl]) * gn_ref[...]).astype(F32) * jax.nn.sigmoid(o_ref[:, sl])
        hs_ref[:, sl] = y.astype(BF16)
    if emit_state:
        for d in range(2):
            for hh in range(2):
                cf_ref[0, 0, d, hh] = c_s[d, hh, hh * DK_C:(hh + 1) * DK_C, :]
                st_ref[0, 0, 2 + 2 * d + hh:3 + 2 * d + hh, :] = m_s[d, hh]
            st_ref[0, 0, d:d + 1, :] = n_s[d, 0] + n_s[d, 1]
        st_ref[0, 0, 6:8, :] = jnp.zeros((2, LANES), F32)


def _mlstm(q, k, v, o, gc, gr, gn, state, seq, emit_state):
    n = q.shape[0]
    nb = n // seq
    npair = H_C // 2
    has_state = state is not None
    in_specs = [
        pl.BlockSpec((seq, LANES), lambda b, j: (b, j)),
        pl.BlockSpec((seq, LANES), lambda b, j: (b, j)),
        pl.BlockSpec((seq, 2 * DV_C), lambda b, j: (b, j)),
        pl.BlockSpec((seq, 2 * DV_C), lambda b, j: (b, j)),
        pl.BlockSpec((1, seq, 8), lambda b, j: (j, b, 0)),
        pl.BlockSpec((8, seq), lambda b, j: (j, b)),
        pl.BlockSpec((1, DV_C), lambda b, j: (0, 0)),
    ]
    args = [q, k, v, o, gc, gr, gn]
    if has_state:
        c0, n0, m0 = state
        in_specs += [pl.BlockSpec((1, 2, 2, DK_C, DV_C), lambda b, j: (b, 0, j, 0, 0)),
                     pl.BlockSpec((1, 2, 1, 1, LANES), lambda b, j: (b, 0, j, 0, 0)),
                     pl.BlockSpec((1, 2, 2, 1, LANES), lambda b, j: (b, 0, j, 0, 0))]
        args += [c0, n0, m0]
    out_specs = [pl.BlockSpec((seq, 2 * DV_C), lambda b, j: (b, j))]
    out_shape = [jax.ShapeDtypeStruct((n, H_C * DV_C), BF16)]
    if emit_state:
        out_specs += [pl.BlockSpec((1, 1, 2, 2, DK_C, DV_C), lambda b, j: (b, 0, 0, j, 0, 0)),
                      pl.BlockSpec((1, 1, 8, LANES), lambda b, j: (b, j, 0, 0))]
        out_shape += [jax.ShapeDtypeStruct((nb, 1, 2, H_C, DK_C, DV_C), F32),
                      jax.ShapeDtypeStruct((nb, npair, 8, LANES), F32)]
    return pl.pallas_call(
        functools.partial(_mlstm_kernel, has_state=has_state, emit_state=emit_state),
        grid=(nb, npair),
        in_specs=in_specs, out_specs=out_specs, out_shape=out_shape,
        scratch_shapes=[pltpu.VMEM((2, 2, LANES, DV_C), F32), pltpu.VMEM((2, 2, 1, LANES), F32),
                        pltpu.VMEM((2, 2, 1, LANES), F32), pltpu.VMEM((seq, 2 * DV_C), F32)],
        compiler_params=_params(("arbitrary", "arbitrary")),
        name="mlstm_lat" if has_state else "mlstm_ctx",
    )(*args)


def _rope_tables(n_tok):
    t = jnp.arange(n_tok)
    rows = (t // GRID_W).astype(F32)
    cols = (t % GRID_W).astype(F32)

    def axis_tabs(width, lane0):
        half = width // 2
        quarter = half // 2
        freqs = jnp.power(ROPE_BASE, -jnp.arange(quarter, dtype=F32) / quarter)
        c = jnp.ones((n_tok, LANES), F32)
        sa = jnp.zeros((n_tok, LANES), F32)
        sb = jnp.zeros((n_tok, LANES), F32)
        for g, pos in enumerate((rows, cols)):
            ang = pos[:, None] * freqs[None, :]
            cos, sin = jnp.cos(ang), jnp.sin(ang)
            a0 = lane0 + g * half
            c = c.at[:, a0:a0 + quarter].set(cos).at[:, a0 + quarter:a0 + half].set(cos)
            sa = sa.at[:, a0:a0 + quarter].set(-sin)
            sb = sb.at[:, a0 + quarter:a0 + half].set(sin)
        return c, sa, sb

    cm, sam, sbm = axis_tabs(QK_ROPE, ROPE_LANE0)
    c0, sa0, sb0 = axis_tabs(DIFF_HD, 0)
    c1, sa1, sb1 = axis_tabs(DIFF_HD, DIFF_HD)
    cd = jnp.where(jnp.arange(LANES)[None, :] < DIFF_HD, c0, c1)
    return cm, sam, sbm, cd, sa0 + sa1, sb0 + sb1


def _prep_even(w_in_ab, w_uq, w_ukv):
    z = lambda n: jnp.zeros((D_MODEL, n), F32)
    c2 = Q_LORA + KV_LORA
    w0 = jnp.concatenate([w_in_ab[:, :c2], z(ROPE_LANE0), w_in_ab[:, c2:c2 + QK_ROPE],
                          z(LANES - ROPE_LANE0 - QK_ROPE), w_in_ab[:, c2 + QK_ROPE:]], axis=1).astype(BF16)
    wq = jnp.pad(w_uq.reshape(Q_LORA, H_A, QK_NOPE + QK_ROPE),
                 ((0, 0), (0, 0), (0, HEAD_PAD - QK_NOPE - QK_ROPE))).reshape(Q_LORA, H_A * HEAD_PAD)
    kvw = w_ukv.reshape(KV_LORA, H_A, QK_NOPE + V_HD_A)
    kpad = jnp.pad(kvw[..., :QK_NOPE], ((0, 0), (0, 0), (0, HEAD_PAD - QK_NOPE)))
    vw = kvw[..., QK_NOPE:]
    zv = jnp.zeros_like(vw)
    odd = (jnp.arange(H_A) % 2 == 1)[None, :, None]
    vpad = jnp.where(odd, jnp.concatenate([zv, vw], -1), jnp.concatenate([vw, zv], -1))
    wukv = jnp.concatenate([kpad.reshape(KV_LORA, -1), vpad.reshape(KV_LORA, -1)], axis=1)
    return w0, wq.astype(BF16), wukv.astype(BF16)


def _gate_perm():
    return jnp.array([t * H_C + 2 * j + hh for j in range(H_C // 2) for t in range(4) for hh in range(2)])


def _prep_odd(w_in_c, b_gate_c):
    ng = 4 * H_C
    base = w_in_c.shape[1] - ng
    perm = _gate_perm()
    w1 = jnp.concatenate([w_in_c[:, :base], w_in_c[:, base + perm],
                          jnp.zeros((D_MODEL, LANES - ng), F32)], axis=1).astype(BF16)
    bg = jnp.pad(b_gate_c[perm], (0, LANES - ng)).reshape(1, LANES)
    return w1, bg


def kernel(x_prompt, x_sample, cache_mla_ckv, cache_mla_krope, cache_diff_k, cache_diff_v,
           state_mlstm_C, state_mlstm_n, state_mlstm_m, c, c_ctx,
           w_ada, b_ada, g_mix, g_ffn, w_ffn_in, w_ffn_out,
           w_in_ab, g_q_lora, g_kv_lora, w_uq, w_ukv, diff_lambda, g_diff_subln, w_out_ab,
           w_in_c, b_gate_c, g_mlstm, w_out_c, g_final):
    nbp, seq_p, _ = x_prompt.shape
    nbs, seq_s, _ = x_sample.shape
    past = cache_mla_ckv.shape[2]
    assert DEPTH == 2 and 1 + nbs <= COND_ROWS

    cond = jnp.concatenate([c_ctx[None], c, jnp.zeros((COND_ROWS - 1 - nbs, D_MODEL), F32)], axis=0)
    mod = _ada(cond, w_ada, b_ada).reshape(DEPTH * COND_ROWS, N_MOD, D_MODEL)

    xp = x_prompt.reshape(nbp * seq_p, D_MODEL)
    xs = x_sample.reshape(nbs * seq_s, D_MODEL)
    row2 = lambda v: v.reshape(1, -1)

    def mod_rows(layer, tm):
        prompt = lambda i: layer * COND_ROWS
        sample = lambda i: layer * COND_ROWS + 1 + i // (seq_s // tm)
        return prompt, sample

    lam_init = 0.8 - 0.6 * math.exp(-0.3 * 0)
    w0, wq, wukv = _prep_even(w_in_ab[0], w_uq[0], w_ukv[0])
    rope_tabs = _rope_tables(seq_s)
    mrp, mrs = mod_rows(0, TM_PROJ)
    gq, gkv = row2(g_q_lora[0]), row2(g_kv_lora[0])
    outs_p = _proj0(xp, mod, mrp, row2(g_mix[0]), w0, gq, gkv, wq, wukv, None, seq_p)
    outs_s = _proj0(xs, mod, mrs, row2(g_mix[0]), w0, gq, gkv, wq, wukv, rope_tabs, seq_s)
    qm_p, km_p, vm_p, dq_p, dk_p, dv_p, ckv_new, kr_new, dk_new, dv_new = outs_p
    qm_s, km_s, vm_s, dq_s, dk_s, dv_s = outs_s
    gsub = row2(g_diff_subln[0])
    krb_c = jnp.pad(cache_mla_krope[:, 0], ((0, 0), (0, 0), (ROPE_LANE0, LANES - ROPE_LANE0 - QK_ROPE)))
    ctx = (cache_mla_ckv[:, 0], krb_c, cache_diff_k[:, 0], cache_diff_v[:, 0])
    a_p = _attn0(qm_p, km_p, vm_p, dq_p, dk_p, dv_p, diff_lambda[0], gsub, None, None, seq_p, lam_init)
    a_s = _attn0(qm_s, km_s, vm_s, dq_s, dk_s, dv_s, diff_lambda[0], gsub, ctx, wukv, seq_s, lam_init)
    mtp, mts = mod_rows(0, TM_TAIL)
    wo0 = w_out_ab[0].astype(BF16)
    win0, wout0 = w_ffn_in[0].astype(BF16), w_ffn_out[0].astype(BF16)
    xp = _tail(xp, a_p, mod, mtp, wo0, row2(g_ffn[0]), win0, wout0, None, "tail0_ctx")
    xs = _tail(xs, a_s, mod, mts, wo0, row2(g_ffn[0]), win0, wout0, None, "tail0_lat")

    w1, bg = _prep_odd(w_in_c[0], b_gate_c[0])
    mrp, mrs = mod_rows(1, TM_PROJ)
    gn = row2(g_mlstm[0])
    q_p, k_p, v_p, o_p, gc_p, gr_p = _proj1(xp, mod, mrp, row2(g_mix[1]), w1, bg)
    q_s, k_s, v_s, o_s, gc_s, gr_s = _proj1(xs, mod, mrs, row2(g_mix[1]), w1, bg)
    hs_p, c_new, stats = _mlstm(q_p, k_p, v_p, o_p, gc_p, gr_p, gn, None, seq_p, True)
    state = (state_mlstm_C[:, 0],
             state_mlstm_n[:, 0].reshape(nbs, 2, H_C // 2, 1, LANES),
             jnp.broadcast_to(state_mlstm_m[:, 0].reshape(nbs, 2, H_C, 1, 1), (nbs, 2, H_C, 1, LANES)))
    (hs_s,) = _mlstm(q_s, k_s, v_s, o_s, gc_s, gr_s, gn, state, seq_s, False)
    mtp, mts = mod_rows(1, TM_TAIL)
    wo1 = w_out_c[0].astype(BF16)
    win1, wout1 = w_ffn_in[1].astype(BF16), w_ffn_out[1].astype(BF16)
    gfin = row2(g_final)
    yp = _tail(xp, hs_p, mod, mtp, wo1, row2(g_ffn[1]), win1, wout1, gfin, "tail1_ctx")
    ys = _tail(xs, hs_s, mod, mts, wo1, row2(g_ffn[1]), win1, wout1, gfin, "tail1_lat")

    n_new = stats[:, :, 0:2, :].transpose(0, 2, 1, 3).reshape(nbp, 1, 2, H_C, DK_C)
    m_new = stats[:, :, 2:6, 0].reshape(nbp, H_C // 2, 2, 2).transpose(0, 2, 1, 3).reshape(nbp, 1, 2, H_C)
    return (yp.reshape(nbp, seq_p, D_MODEL), ys.reshape(nbs, seq_s, D_MODEL),
            ckv_new.reshape(nbp, 1, seq_p, KV_LORA), kr_new.reshape(nbp, 1, seq_p, QK_ROPE),
            dk_new.reshape(nbp, 1, H_B, seq_p, 2 * DIFF_HD), dv_new.reshape(nbp, 1, H_B, seq_p, DIFF_VD),
            c_new, n_new, m_new)
```

```python
import functools
import math

import jax
import jax.numpy as jnp
import numpy as np
from jax import lax
from jax.experimental import pallas as pl
from jax.experimental.pallas import tpu as pltpu

F32 = jnp.float32
BF16 = jnp.bfloat16

D_MODEL = 1024
DEPTH = 2
GRID_W = 64
ROPE_BASE = 10000.0
RMS_EPS = 1e-6
H_A = 8
QK_NOPE = 64
QK_ROPE = 32
V_HD_A = 64
Q_LORA = 256
KV_LORA = 128
H_B = 4
DIFF_HD = 64
DIFF_VD = 2 * DIFF_HD
H_C = 8
DK_C = 64
DV_C = D_MODEL // H_C
D_FF = -(-8 * D_MODEL // (3 * 256)) * 256

LANES = 128
HEAD_PAD = LANES
ROPE_LANE0 = QK_NOPE
N_MOD = 6
COND_ROWS = 16
N_GATE = 2 * H_C

TM_PROJ = 256
TQ_ATTN = 256
TM_TAIL = 512
FF_CHUNK = 1408
MLSTM_L = 128
ADA_TN = 1024
VMEM_LIMIT = 56 * 1024 * 1024

assert D_FF % FF_CHUNK == 0 and FF_CHUNK % LANES == 0


def _dot(a, b):
    return jnp.dot(a, b, preferred_element_type=F32)


def _dot_nt(a, b):
    return lax.dot_general(a, b, (((1,), (1,)), ((), ())), preferred_element_type=F32)


def _dot_tn(a, b):
    return lax.dot_general(a, b, (((0,), (0,)), ((), ())), preferred_element_type=F32)


def _rms(x):
    return x * lax.rsqrt(jnp.mean(x * x, axis=-1, keepdims=True) + RMS_EPS)


def _modulate(x, g, shift, scale):
    return (_rms(x) * g) * (1.0 + scale) + shift


def _params(semantics):
    return pltpu.CompilerParams(dimension_semantics=semantics, vmem_limit_bytes=VMEM_LIMIT)


def _ada_kernel(cond_ref, w_ref, b_ref, o_ref):
    c = cond_ref[...]
    s = (c * jax.nn.sigmoid(c)).astype(BF16)
    o_ref[0] = _dot(s, w_ref[0].astype(BF16)) + b_ref[0]


def _ada(cond, w_ada, b_ada):
    n_out = w_ada.shape[-1]
    return pl.pallas_call(
        _ada_kernel,
        grid=(DEPTH, n_out // ADA_TN),
        in_specs=[
            pl.BlockSpec((COND_ROWS, D_MODEL), lambda l, n: (0, 0)),
            pl.BlockSpec((1, D_MODEL, ADA_TN), lambda l, n: (l, 0, n)),
            pl.BlockSpec((1, 1, ADA_TN), lambda l, n: (l, 0, n)),
        ],
        out_specs=pl.BlockSpec((1, COND_ROWS, ADA_TN), lambda l, n: (l, 0, n)),
        out_shape=jax.ShapeDtypeStruct((DEPTH, COND_ROWS, n_out), F32),
        compiler_params=_params(("arbitrary", "arbitrary")),
        name="ada",
    )(cond, w_ada, b_ada.reshape(DEPTH, 1, n_out))


def _rope(x, c, sa, sb, off):
    return x * c + pltpu.roll(x, LANES - off, 1) * sa + pltpu.roll(x, off, 1) * sb


def _proj0_kernel(*refs, rope, emit_cache):
    x_ref, mod_ref, g_ref, w0_ref, gq_ref, gkv_ref, wq_ref, wukv_ref = refs[:8]
    refs = refs[8:]
    if rope:
        cm_ref, sam_ref, sbm_ref, cd_ref, sad_ref, sbd_ref = refs[:6]
        refs = refs[6:]
    qm_ref, km_ref, vm_ref, dq_ref, dk_ref, dv_ref = refs[:6]
    refs = refs[6:]
    if emit_cache:
        ckvf_ref, krf_ref, dkf_ref, dvf_ref = refs

    h = _modulate(x_ref[...], g_ref[...], mod_ref[0, 0:1, :], mod_ref[0, 1:2, :]).astype(BF16)
    proj = _dot(h, w0_ref[...])
    cq = proj[:, 0:Q_LORA]
    ckv = _rms(proj[:, Q_LORA:Q_LORA + KV_LORA]) * gkv_ref[...]
    krb = proj[:, 384:512]
    qa = _dot((_rms(cq) * gq_ref[...]).astype(BF16), wq_ref[...])
    kv = _dot(ckv.astype(BF16), wukv_ref[...])
    if emit_cache:
        ckvf_ref[...] = ckv
        krf_ref[...] = krb[:, ROPE_LANE0:ROPE_LANE0 + QK_ROPE]
    if rope:
        cm, sam, sbm = cm_ref[...], sam_ref[...], sbm_ref[...]
        cd, sad, sbd = cd_ref[...], sad_ref[...], sbd_ref[...]
        krb = _rope(krb, cm, sam, sbm, QK_ROPE // 4)
    for hd in range(H_A):
        sl = slice(hd * HEAD_PAD, (hd + 1) * HEAD_PAD)
        qh = qa[:, sl]
        if rope:
            qh = _rope(qh, cm, sam, sbm, QK_ROPE // 4)
        qm_ref[:, sl] = qh.astype(BF16)
        km_ref[:, sl] = (kv[:, sl] + krb).astype(BF16)
    vm_ref[...] = kv[:, H_A * HEAD_PAD:].astype(BF16)
    for hd in range(H_B):
        sl = slice(hd * LANES, (hd + 1) * LANES)
        dq = proj[:, 512 + hd * LANES:512 + (hd + 1) * LANES]
        dk = proj[:, 1024 + hd * LANES:1024 + (hd + 1) * LANES]
        dv = proj[:, 1536 + hd * LANES:1536 + (hd + 1) * LANES]
        if emit_cache:
            dkf_ref[0, hd] = dk
            dvf_ref[0, hd] = dv
        if rope:
            dq = _rope(dq, cd, sad, sbd, DIFF_HD // 4)
            dk = _rope(dk, cd, sad, sbd, DIFF_HD // 4)
        dq_ref[:, sl] = (dq * (DIFF_HD ** -0.5)).astype(BF16)
        dk_ref[:, sl] = dk.astype(BF16)
        dv_ref[:, sl] = dv.astype(BF16)


def _proj0(x, mod, mod_row, g, w0, gq, gkv, wq, wukv, rope_tabs, seq):
    n = x.shape[0]
    tm = TM_PROJ
    rope = rope_tabs is not None
    emit_cache = not rope
    tiles_per_seq = seq // tm
    full = lambda shape: pl.BlockSpec(shape, lambda i: (0,) * len(shape))
    in_specs = [
        pl.BlockSpec((tm, D_MODEL), lambda i: (i, 0)),
        pl.BlockSpec((1, N_MOD, D_MODEL), lambda i: (mod_row(i), 0, 0)),
        full((1, D_MODEL)), full(w0.shape), full((1, Q_LORA)), full((1, KV_LORA)),
        full(wq.shape), full(wukv.shape),
    ]
    args = [x, mod, g, w0, gq, gkv, wq, wukv]
    if rope:
        in_specs += [pl.BlockSpec((tm, LANES), lambda i: (i % tiles_per_seq, 0))] * 6
        args += list(rope_tabs)
    row = lambda w: pl.BlockSpec((tm, w), lambda i: (i, 0))
    out_specs = [row(1024), row(1024), row(1024), row(512), row(512), row(512)]
    out_shape = [jax.ShapeDtypeStruct((n, w), BF16) for w in (1024, 1024, 1024, 512, 512, 512)]
    if emit_cache:
        assert tiles_per_seq == 1
        cache = pl.BlockSpec((1, H_B, tm, LANES), lambda i: (i, 0, 0, 0))
        out_specs += [row(KV_LORA), row(QK_ROPE), cache, cache]
        out_shape += [jax.ShapeDtypeStruct((n, KV_LORA), F32), jax.ShapeDtypeStruct((n, QK_ROPE), F32),
                      jax.ShapeDtypeStruct((n // seq, H_B, seq, LANES), F32),
                      jax.ShapeDtypeStruct((n // seq, H_B, seq, LANES), F32)]
    return pl.pallas_call(
        functools.partial(_proj0_kernel, rope=rope, emit_cache=emit_cache),
        grid=(n // tm,),
        in_specs=in_specs, out_specs=out_specs, out_shape=out_shape,
        compiler_params=_params(("arbitrary",)),
        name="proj0_rope" if rope else "proj0_ctx",
    )(*args)


def _softmax_parts(scores, scale):
    m = functools.reduce(jnp.maximum, [jnp.max(s, axis=-1, keepdims=True) for s in scores])
    es = [jnp.exp((s - m) * scale) for s in scores]
    l = functools.reduce(jnp.add, [jnp.sum(e, axis=-1, keepdims=True) for e in es])
    return es, 1.0 / l


def _attn_kernel(*refs, has_ctx, lam_init):
    q_ref, dq_ref, k_ref, v_ref, dk_ref, dv_ref, lamv_ref, gsub_ref = refs[:8]
    refs = refs[8:]
    if has_ctx:
        ckvc_ref, krc_ref, dkc_ref, dvc_ref, wukv_ref, o_ref, kc_s, vc_s, dkc_s, dvc_s = refs

        @pl.when(pl.program_id(1) == 0)
        def _():
            kv = _dot(ckvc_ref[0].astype(BF16), wukv_ref[...])
            krb = krc_ref[0]
            for hd in range(H_A):
                sl = slice(hd * HEAD_PAD, (hd + 1) * HEAD_PAD)
                kc_s[:, sl] = (kv[:, sl] + krb).astype(BF16)
            vc_s[...] = kv[:, H_A * HEAD_PAD:].astype(BF16)
            for hd in range(H_B):
                sl = slice(hd * LANES, (hd + 1) * LANES)
                dkc_s[:, sl] = dkc_ref[0, hd].astype(BF16)
                dvc_s[:, sl] = dvc_ref[0, hd].astype(BF16)
    else:
        (o_ref,) = refs

    scale_a = (QK_NOPE + QK_ROPE) ** -0.5
    for j in range(H_A // 2):
        acc = None
        for hh in range(2):
            sl = slice((2 * j + hh) * HEAD_PAD, (2 * j + hh + 1) * HEAD_PAD)
            segs = [(k_ref[:, sl], v_ref[:, sl])]
            if has_ctx:
                segs.append((kc_s[:, sl], vc_s[:, sl]))
            qh = q_ref[:, sl]
            es, r = _softmax_parts([_dot_nt(qh, kk) for kk, _ in segs], scale_a)
            for e, (_, vv) in zip(es, segs):
                pv = _dot((e * r).astype(BF16), vv)
                acc = pv if acc is None else acc + pv
        o_ref[:, j * LANES:(j + 1) * LANES] = acc.astype(BF16)

    lv = lamv_ref[...]
    lam = (jnp.exp(jnp.sum(lv[0:1] * lv[1:2], axis=-1, keepdims=True))
           - jnp.exp(jnp.sum(lv[2:3] * lv[3:4], axis=-1, keepdims=True)) + lam_init)
    tq = dq_ref.shape[0]
    lo = lax.broadcasted_iota(jnp.int32, (tq, LANES), 1) < DIFF_HD
    for hd in range(H_B):
        sl = slice(hd * LANES, (hd + 1) * LANES)
        dq = dq_ref[:, sl].astype(F32)
        q1 = jnp.where(lo, dq, 0.0).astype(BF16)
        q2 = jnp.where(lo, 0.0, dq).astype(BF16)
        segs = [(dk_ref[:, sl], dv_ref[:, sl])]
        if has_ctx:
            segs.append((dkc_s[:, sl], dvc_s[:, sl]))
        e1, r1 = _softmax_parts([_dot_nt(q1, kk) for kk, _ in segs], 1.0)
        e2, r2 = _softmax_parts([_dot_nt(q2, kk) for kk, _ in segs], 1.0)
        r2 = lam * r2
        acc = None
        for a, b, (_, vv) in zip(e1, e2, segs):
            pv = _dot((a * r1 - b * r2).astype(BF16), vv)
            acc = pv if acc is None else acc + pv
        ob = (_rms(acc) * gsub_ref[...]) * (1.0 - lam_init)
        o_ref[:, H_A * V_HD_A + hd * LANES:H_A * V_HD_A + (hd + 1) * LANES] = ob.astype(BF16)


def _attn0(qm, km, vm, dq, dk, dv, lamv, gsub, ctx, wukv, seq, lam_init):
    n = qm.shape[0]
    nb = n // seq
    tq = TQ_ATTN
    has_ctx = ctx is not None
    full = lambda shape: pl.BlockSpec(shape, lambda b, t: (0,) * len(shape))
    qrow = lambda w: pl.BlockSpec((tq, w), lambda b, t: (b * (seq // tq) + t, 0))
    krow = lambda w: pl.BlockSpec((seq, w), lambda b, t: (b, 0))
    in_specs = [qrow(1024), qrow(512), krow(1024), krow(1024), krow(512), krow(512),
                full((4, DIFF_HD)), full((1, DIFF_VD))]
    args = [qm, dq, km, vm, dk, dv, lamv, gsub]
    scratch = []
    if has_ctx:
        ckv_c, krb_c, dk_c, dv_c = ctx
        past = ckv_c.shape[1]
        in_specs += [pl.BlockSpec((1, past, LANES), lambda b, t: (b, 0, 0)),
                     pl.BlockSpec((1, past, LANES), lambda b, t: (b, 0, 0)),
                     pl.BlockSpec((1, H_B, past, LANES), lambda b, t: (b, 0, 0, 0)),
                     pl.BlockSpec((1, H_B, past, LANES), lambda b, t: (b, 0, 0, 0)),
                     full(wukv.shape)]
        args += [ckv_c, krb_c, dk_c, dv_c, wukv]
        scratch = [pltpu.VMEM((past, 1024), BF16), pltpu.VMEM((past, 1024), BF16),
                   pltpu.VMEM((past, 512), BF16), pltpu.VMEM((past, 512), BF16)]
    return pl.pallas_call(
        functools.partial(_attn_kernel, has_ctx=has_ctx, lam_init=lam_init),
        grid=(nb, seq // tq),
        in_specs=in_specs,
        out_specs=pl.BlockSpec((tq, D_MODEL), lambda b, t: (b * (seq // tq) + t, 0)),
        out_shape=jax.ShapeDtypeStruct((n, D_MODEL), BF16),
        scratch_shapes=scratch,
        compiler_params=_params(("arbitrary", "arbitrary")),
        name="attn0_lat" if has_ctx else "attn0_ctx",
    )(*args)


def _tail_kernel(*refs, final):
    x_ref, a_ref, mod_ref, wo_ref, gf_ref, win_ref, wout_ref = refs[:7]
    if final:
        gfin_ref, o_ref = refs[7:]
    else:
        (o_ref,) = refs[7:]
    x1 = x_ref[...] + mod_ref[0, 2:3, :] * _dot(a_ref[...], wo_ref[...])
    h = _modulate(x1, gf_ref[...], mod_ref[0, 3:4, :], mod_ref[0, 4:5, :]).astype(BF16)
    acc = None
    for c in range(D_FF // FF_CHUNK):
        a = _dot(h, win_ref[:, c * FF_CHUNK:(c + 1) * FF_CHUNK])
        b = _dot(h, win_ref[:, D_FF + c * FF_CHUNK:D_FF + (c + 1) * FF_CHUNK])
        act = ((a * jax.nn.sigmoid(a)) * b).astype(BF16)
        part = _dot(act, wout_ref[c * FF_CHUNK:(c + 1) * FF_CHUNK, :])
        acc = part if acc is None else acc + part
    x2 = x1 + mod_ref[0, 5:6, :] * acc
    if final:
        x2 = _rms(x2) * gfin_ref[...]
    o_ref[...] = x2


def _tail(x, a, mod, mod_row, wo, gf, win, wout, gfin, name):
    n = x.shape[0]
    tm = TM_TAIL
    final = gfin is not None
    full = lambda shape: pl.BlockSpec(shape, lambda i: (0,) * len(shape))
    in_specs = [
        pl.BlockSpec((tm, D_MODEL), lambda i: (i, 0)),
        pl.BlockSpec((tm, D_MODEL), lambda i: (i, 0)),
        pl.BlockSpec((1, N_MOD, D_MODEL), lambda i: (mod_row(i), 0, 0)),
        full(wo.shape), full((1, D_MODEL)), full(win.shape), full(wout.shape),
    ]
    args = [x, a, mod, wo, gf, win, wout]
    if final:
        in_specs.append(full((1, D_MODEL)))
        args.append(gfin)
    return pl.pallas_call(
        functools.partial(_tail_kernel, final=final),
        grid=(n // tm,),
        in_specs=in_specs,
        out_specs=pl.BlockSpec((tm, D_MODEL), lambda i: (i, 0)),
        out_shape=jax.ShapeDtypeStruct((n, D_MODEL), F32),
        compiler_params=_params(("arbitrary",)),
        name=name,
    )(*args)


def _split3(x):
    hi = x.astype(BF16)
    r1 = x - hi.astype(F32)
    mid = r1.astype(BF16)
    lo = (r1 - mid.astype(F32)).astype(BF16)
    return hi, mid, lo


def _proj1_kernel(x_ref, mod_ref, g_ref, w1_ref, bg_ref, q_ref, k_ref, v_ref, o_ref, gc_ref, gr_ref):
    tm = x_ref.shape[0]
    L = MLSTM_L
    h = _modulate(x_ref[...], g_ref[...], mod_ref[0, 0:1, :], mod_ref[0, 1:2, :]).astype(BF16)
    proj = _dot(h, w1_ref[...])
    hk = H_C * DK_C
    hv = H_C * DV_C
    for blk in range(hk // LANES):
        sl = slice(blk * LANES, (blk + 1) * LANES)
        q_ref[sl, :] = jnp.transpose(proj[:, sl]).astype(BF16)
    for blk in range(hv // LANES):
        sl = slice(blk * LANES, (blk + 1) * LANES)
        v_ref[sl, :] = jnp.transpose(proj[:, 2 * hk + blk * LANES:2 * hk + (blk + 1) * LANES]).astype(BF16)
    k_ref[...] = proj[:, hk:2 * hk] * (DK_C ** -0.5)
    o_ref[...] = proj[:, 2 * hk + hv:2 * hk + 2 * hv]
    gates = proj[:, 2 * hk + 2 * hv:2 * hk + 2 * hv + LANES] + bg_ref[...]
    lf = jnp.minimum(gates, 0.0) - jnp.log1p(jnp.exp(-jnp.abs(gates)))
    r = lax.broadcasted_iota(jnp.int32, (tm, tm), 0)
    c = lax.broadcasted_iota(jnp.int32, (tm, tm), 1)
    same = (r // L) == (c // L)
    pre = jnp.where(same & (c <= r), 1.0, 0.0).astype(BF16)
    suf = jnp.where(same & (c >= r), 1.0, 0.0).astype(BF16)
    parts = _split3(lf)
    lane = lax.broadcasted_iota(jnp.int32, (tm, LANES), 1)
    b_sum = jnp.where((lane % 4) >= 2,
                      functools.reduce(jnp.add, [_dot(suf, p) for p in parts]),
                      functools.reduce(jnp.add, [_dot(pre, p) for p in parts]))
    b = pltpu.roll(b_sum, LANES - N_GATE, 1)
    u = gates - b
    low = lane < N_GATE
    comb_t = jnp.transpose(jnp.where(low, b, 0.0) + pltpu.roll(jnp.where(low, u, 0.0), N_GATE, 1))
    b_t = comb_t[0:N_GATE]
    u_t = comb_t[N_GATE:2 * N_GATE]
    pos = lax.broadcasted_iota(jnp.int32, (N_GATE, tm), 1) % L
    bwd = (lax.broadcasted_iota(jnp.int32, (N_GATE, tm), 0) % 4) >= 2
    cm_t = u_t
    step = 1
    while step < L:
        below = jnp.where(pos >= step, pltpu.roll(cm_t, step, 1), -jnp.inf)
        above = jnp.where(pos < L - step, pltpu.roll(cm_t, tm - step, 1), -jnp.inf)
        cm_t = jnp.maximum(cm_t, jnp.where(bwd, above, below))
        step *= 2
    rows = jnp.concatenate([b_t, cm_t, u_t, jnp.zeros((LANES - 3 * N_GATE, tm), F32)], axis=0)
    gr_ref[...] = rows[0:3 * N_GATE]
    gc_ref[...] = jnp.transpose(rows)


def _proj1(x, mod, mod_row, g, w1, bg):
    n = x.shape[0]
    tm = TM_PROJ
    assert tm % MLSTM_L == 0
    full = lambda shape: pl.BlockSpec(shape, lambda i: (0,) * len(shape))
    row = lambda w: pl.BlockSpec((tm, w), lambda i: (i, 0))
    col = lambda w: pl.BlockSpec((w, tm), lambda i: (0, i))
    hk, hv = H_C * DK_C, H_C * DV_C
    return pl.pallas_call(
        _proj1_kernel,
        grid=(n // tm,),
        in_specs=[row(D_MODEL), pl.BlockSpec((1, N_MOD, D_MODEL), lambda i: (mod_row(i), 0, 0)),
                  full((1, D_MODEL)), full(w1.shape), full((1, LANES))],
        out_specs=[col(hk), row(hk), col(hv), row(hv), row(LANES),
                   pl.BlockSpec((3 * N_GATE, tm), lambda i: (0, i))],
        out_shape=[jax.ShapeDtypeStruct((hk, n), BF16), jax.ShapeDtypeStruct((n, hk), F32),
                   jax.ShapeDtypeStruct((hv, n), BF16), jax.ShapeDtypeStruct((n, hv), F32),
                   jax.ShapeDtypeStruct((n, LANES), F32), jax.ShapeDtypeStruct((3 * N_GATE, n), F32)],
        compiler_params=_params(("arbitrary",)),
        name="proj1",
    )(x, mod, g, w1, bg)


def _chain(j, d, hh):
    return (2 * j + d) * 2 + hh


def _mlstm_kernel(*refs, has_state, emit_state):
    q_ref, k_ref, v_ref, o_ref, gc_ref, gr_ref, gn_ref = refs[:7]
    refs = refs[7:]
    if has_state:
        c0_ref, n0_ref, m0_ref = refs[:3]
        refs = refs[3:]
    hs_ref = refs[0]
    refs = refs[1:]
    if emit_state:
        cf_ref, nst_ref, mst_ref = refs[:3]
        refs = refs[3:]
    cx_s, m_s, h_s = refs

    L = MLSTM_L
    seq = k_ref.shape[0]
    nc = seq // L
    npair = H_C // 2
    lane = lax.broadcasted_iota(jnp.int32, (1, LANES), 1)
    head_mask = [lane < DK_C, lane >= DK_C]
    ri = lax.broadcasted_iota(jnp.int32, (L, L), 0)
    ci = lax.broadcasted_iota(jnp.int32, (L, L), 1)
    causal = [ri <= ci, ri >= ci]
    ones_blk = jnp.ones((DV_C, L), BF16)
    chains = [(j, d, hh) for d in range(2) for j in range(npair) for hh in range(2)]

    h_s[...] = jnp.zeros_like(h_s)
    for j, d, hh in chains:
        ch = _chain(j, d, hh)
        if has_state:
            zpad = jnp.zeros((DK_C, DV_C), F32)
            c0 = c0_ref[0, d, 2 * j + hh]
            c0 = jnp.concatenate([c0, zpad] if hh == 0 else [zpad, c0], axis=0)
            cx_s[ch, 0:DV_C, :] = jnp.transpose(c0)
            n_row = jnp.where(head_mask[hh], n0_ref[0, d, j], 0.0)
            cx_s[ch, DV_C:2 * DV_C, :] = jnp.broadcast_to(n_row, (DV_C, LANES))
            m_s[ch] = m0_ref[0, d, 2 * j + hh]
        else:
            cx_s[ch] = jnp.zeros((2 * DV_C, LANES), F32)
            m_s[ch] = jnp.zeros((1, LANES), F32)

    def chunk_step(i, carry):
        sl = [pl.ds(pl.multiple_of(i * L, L), L), pl.ds(pl.multiple_of((nc - 1 - i) * L, L), L)]
        gcol = [gc_ref[sl[d], :] for d in range(2)]
        grow = [gr_ref[:, sl[d]] for d in range(2)]
        kpair = {(d, j): k_ref[sl[d], j * LANES:(j + 1) * LANES] for d in range(2) for j in range(npair)}
        qt = {(d, j): q_ref[j * LANES:(j + 1) * LANES, sl[d]] for d in range(2) for j in range(npair)}
        st, km, u_bc, vt, row = {}, {}, {}, {}, {}
        for j, d, hh in chains:
            key = (j, d, hh)
            idx = 4 * j + 2 * d + hh
            edge = L - 1 if d == 0 else 0
            b_row = grow[d][idx:idx + 1, :]
            cm_row = grow[d][N_GATE + idx:N_GATE + idx + 1, :]
            u_col = gcol[d][:, 2 * N_GATE + idx:2 * N_GATE + idx + 1]
            m_prev = m_s[_chain(j, d, hh)][:, 0:1]
            row[key] = (b_row, cm_row, b_row[:, edge:edge + 1], cm_row[:, edge:edge + 1], m_prev)
            u_bc[key] = jnp.broadcast_to(u_col, (L, L))
            km[key] = jnp.where(head_mask[hh], kpair[(d, j)], 0.0)
            vt[key] = v_ref[(2 * j + hh) * DV_C:(2 * j + hh + 1) * DV_C, sl[d]]
            st[key] = _dot(km[key].astype(BF16), qt[(d, j)])
        for j, d, hh in chains:
            key = (j, d, hh)
            ch = _chain(j, d, hh)
            b_row, cm_row, g_tot, cm_last, m_prev = row[key]
            head = slice((2 * j + hh) * DV_C, (2 * j + hh + 1) * DV_C)
            cx = cx_s[ch]
            big_m = jnp.maximum(m_prev, cm_row)
            s = st[key] * jnp.exp(jnp.where(causal[d], u_bc[key] - big_m, -jnp.inf))
            inter = jnp.exp(m_prev - big_m)
            qcx = _dot(cx.astype(BF16), qt[(d, j)])
            num = _dot(vt[key], s.astype(BF16)) + inter * qcx[0:DV_C, :]
            den = jnp.sum(s, axis=0, keepdims=True) + inter * qcx[DV_C:DV_C + 1, :]
            hval = num * (1.0 / jnp.maximum(jnp.abs(den), jnp.exp(-(b_row + big_m))))
            h_s[head, sl[d]] = h_s[head, sl[d]] + hval

            m_top = jnp.maximum(m_prev, cm_last)
            kw = km[key] * jnp.exp(u_bc[key] - cm_last)
            vx = jnp.concatenate([vt[key], ones_blk], axis=0)
            cx_s[ch] = (jnp.exp(m_prev - m_top) * cx
                        + jnp.exp(cm_last - m_top) * _dot(vx, kw.astype(BF16)))
            m_s[ch] = jnp.broadcast_to(g_tot + m_top, (1, LANES))
        return carry

    lax.fori_loop(0, nc, chunk_step, 0)

    for hd in range(H_C):
        sl = slice(hd * DV_C, (hd + 1) * DV_C)
        hsum = jnp.transpose(h_s[sl, :])
        y = (_rms(hsum) * gn_ref[...]) * jax.nn.sigmoid(o_ref[:, sl])
        hs_ref[:, sl] = y.astype(BF16)
    if emit_state:
        for j, d, hh in chains:
            ch = _chain(j, d, hh)
            cf_ref[0, 0, d, 2 * j + hh] = jnp.transpose(cx_s[ch, 0:DV_C, :])[hh * DK_C:(hh + 1) * DK_C, :]
            mst_ref[0, d * H_C + 2 * j + hh:d * H_C + 2 * j + hh + 1, :] = m_s[ch]
        for j in range(npair):
            for d in range(2):
                nst_ref[0, d * npair + j:d * npair + j + 1, :] = (
                    cx_s[_chain(j, d, 0), DV_C:DV_C + 1, :] + cx_s[_chain(j, d, 1), DV_C:DV_C + 1, :])


def _mlstm(q, k, v, o, gc, gr, gn, state, seq, emit_state):
    n = k.shape[0]
    nb = n // seq
    npair = H_C // 2
    has_state = state is not None
    hk, hv = H_C * DK_C, H_C * DV_C
    in_specs = [
        pl.BlockSpec((hk, seq), lambda b: (0, b)),
        pl.BlockSpec((seq, hk), lambda b: (b, 0)),
        pl.BlockSpec((hv, seq), lambda b: (0, b)),
        pl.BlockSpec((seq, hv), lambda b: (b, 0)),
        pl.BlockSpec((seq, LANES), lambda b: (b, 0)),
        pl.BlockSpec((3 * N_GATE, seq), lambda b: (0, b)),
        pl.BlockSpec((1, DV_C), lambda b: (0, 0)),
    ]
    args = [q, k, v, o, gc, gr, gn]
    if has_state:
        c0, n0, m0 = state
        in_specs += [pl.BlockSpec((1, 2, H_C, DK_C, DV_C), lambda b: (b, 0, 0, 0, 0)),
                     pl.BlockSpec((1, 2, npair, 1, LANES), lambda b: (b, 0, 0, 0, 0)),
                     pl.BlockSpec((1, 2, H_C, 1, LANES), lambda b: (b, 0, 0, 0, 0))]
        args += [c0, n0, m0]
    out_specs = [pl.BlockSpec((seq, hv), lambda b: (b, 0))]
    out_shape = [jax.ShapeDtypeStruct((n, hv), BF16)]
    if emit_state:
        out_specs += [pl.BlockSpec((1, 1, 2, H_C, DK_C, DV_C), lambda b: (b, 0, 0, 0, 0, 0)),
                      pl.BlockSpec((1, 2 * npair, LANES), lambda b: (b, 0, 0)),
                      pl.BlockSpec((1, 2 * H_C, LANES), lambda b: (b, 0, 0))]
        out_shape += [jax.ShapeDtypeStruct((nb, 1, 2, H_C, DK_C, DV_C), F32),
                      jax.ShapeDtypeStruct((nb, 2 * npair, LANES), F32),
                      jax.ShapeDtypeStruct((nb, 2 * H_C, LANES), F32)]
    n_chain = 2 * H_C
    return pl.pallas_call(
        functools.partial(_mlstm_kernel, has_state=has_state, emit_state=emit_state),
        grid=(nb,),
        in_specs=in_specs, out_specs=out_specs, out_shape=out_shape,
        scratch_shapes=[pltpu.VMEM((n_chain, 2 * DV_C, LANES), F32), pltpu.VMEM((n_chain, 1, LANES), F32),
                        pltpu.VMEM((hv, seq), F32)],
        compiler_params=_params(("arbitrary",)),
        name="mlstm_lat" if has_state else "mlstm_ctx",
    )(*args)


def _rope_tables(n_tok):
    t = np.arange(n_tok)
    rows = (t // GRID_W).astype(np.float64)
    cols = (t % GRID_W).astype(np.float64)

    def axis_tabs(width, lane0):
        half = width // 2
        quarter = half // 2
        freqs = np.power(ROPE_BASE, -np.arange(quarter, dtype=np.float64) / quarter)
        c = np.ones((n_tok, LANES))
        sa = np.zeros((n_tok, LANES))
        sb = np.zeros((n_tok, LANES))
        for g, pos in enumerate((rows, cols)):
            ang = pos[:, None] * freqs[None, :]
            a0 = lane0 + g * half
            c[:, a0:a0 + quarter] = np.cos(ang)
            c[:, a0 + quarter:a0 + half] = np.cos(ang)
            sa[:, a0:a0 + quarter] = -np.sin(ang)
            sb[:, a0 + quarter:a0 + half] = np.sin(ang)
        return c, sa, sb

    cm, sam, sbm = axis_tabs(QK_ROPE, ROPE_LANE0)
    c0, sa0, sb0 = axis_tabs(DIFF_HD, 0)
    c1, sa1, sb1 = axis_tabs(DIFF_HD, DIFF_HD)
    cd = np.where(np.arange(LANES)[None, :] < DIFF_HD, c0, c1)
    return tuple(jnp.asarray(a, F32) for a in (cm, sam, sbm, cd, sa0 + sa1, sb0 + sb1))


def _prep_even(w_in_ab, w_uq, w_ukv):
    z = lambda n: jnp.zeros((D_MODEL, n), F32)
    c2 = Q_LORA + KV_LORA
    w0 = jnp.concatenate([w_in_ab[:, :c2], z(ROPE_LANE0), w_in_ab[:, c2:c2 + QK_ROPE],
                          z(LANES - ROPE_LANE0 - QK_ROPE), w_in_ab[:, c2 + QK_ROPE:]], axis=1).astype(BF16)
    wq = jnp.pad(w_uq.reshape(Q_LORA, H_A, QK_NOPE + QK_ROPE),
                 ((0, 0), (0, 0), (0, HEAD_PAD - QK_NOPE - QK_ROPE))).reshape(Q_LORA, H_A * HEAD_PAD)
    kvw = w_ukv.reshape(KV_LORA, H_A, QK_NOPE + V_HD_A)
    kpad = jnp.pad(kvw[..., :QK_NOPE], ((0, 0), (0, 0), (0, HEAD_PAD - QK_NOPE)))
    vw = kvw[..., QK_NOPE:]
    zv = jnp.zeros_like(vw)
    odd = (jnp.arange(H_A) % 2 == 1)[None, :, None]
    vpad = jnp.where(odd, jnp.concatenate([zv, vw], -1), jnp.concatenate([vw, zv], -1))
    wukv = jnp.concatenate([kpad.reshape(KV_LORA, -1), vpad.reshape(KV_LORA, -1)], axis=1)
    return w0, wq.astype(BF16), wukv.astype(BF16)


def _gate_order(g):
    lead = g.shape[:-1]
    g = g.reshape(lead + (2, 2, H_C // 2, 2))
    perm = tuple(range(len(lead))) + tuple(len(lead) + a for a in (1, 2, 0, 3))
    return g.transpose(perm).reshape(lead + (4 * H_C,))


def _prep_odd(w_in_c, b_gate_c):
    ng = 4 * H_C
    base = w_in_c.shape[1] - ng
    w1 = jnp.concatenate([w_in_c[:, :base], _gate_order(w_in_c[:, base:]),
                          jnp.zeros((D_MODEL, LANES - ng), F32)], axis=1).astype(BF16)
    bg = jnp.pad(_gate_order(b_gate_c), (0, LANES - ng)).reshape(1, LANES)
    return w1, bg


def kernel(x_prompt, x_sample, cache_mla_ckv, cache_mla_krope, cache_diff_k, cache_diff_v,
           state_mlstm_C, state_mlstm_n, state_mlstm_m, c, c_ctx,
           w_ada, b_ada, g_mix, g_ffn, w_ffn_in, w_ffn_out,
           w_in_ab, g_q_lora, g_kv_lora, w_uq, w_ukv, diff_lambda, g_diff_subln, w_out_ab,
           w_in_c, b_gate_c, g_mlstm, w_out_c, g_final):
    nbp, seq_p, _ = x_prompt.shape
    nbs, seq_s, _ = x_sample.shape
    past = cache_mla_ckv.shape[2]
    assert DEPTH == 2 and 1 + nbs <= COND_ROWS
    assert cache_mla_ckv.shape[1] == 1 and state_mlstm_C.shape[1] == 1

    cond = jnp.concatenate([c_ctx[None], c, jnp.zeros((COND_ROWS - 1 - nbs, D_MODEL), F32)], axis=0)
    mod = _ada(cond, w_ada, b_ada).reshape(DEPTH * COND_ROWS, N_MOD, D_MODEL)

    xp = x_prompt.reshape(nbp * seq_p, D_MODEL)
    xs = x_sample.reshape(nbs * seq_s, D_MODEL)
    row2 = lambda v: v.reshape(1, -1)

    def mod_rows(layer, tm):
        prompt = lambda i: layer * COND_ROWS
        sample = lambda i: layer * COND_ROWS + 1 + i // (seq_s // tm)
        return prompt, sample

    lam_init = 0.8 - 0.6 * math.exp(-0.3 * 0)
    w0, wq, wukv = _prep_even(w_in_ab[0], w_uq[0], w_ukv[0])
    rope_tabs = _rope_tables(seq_s)
    mrp, mrs = mod_rows(0, TM_PROJ)
    gq, gkv = row2(g_q_lora[0]), row2(g_kv_lora[0])
    outs_p = _proj0(xp, mod, mrp, row2(g_mix[0]), w0, gq, gkv, wq, wukv, None, seq_p)
    outs_s = _proj0(xs, mod, mrs, row2(g_mix[0]), w0, gq, gkv, wq, wukv, rope_tabs, seq_s)
    qm_p, km_p, vm_p, dq_p, dk_p, dv_p, ckv_new, kr_new, dk_new, dv_new = outs_p
    qm_s, km_s, vm_s, dq_s, dk_s, dv_s = outs_s
    gsub = row2(g_diff_subln[0])
    krb_c = jnp.pad(cache_mla_krope.reshape(nbs, past, QK_ROPE),
                    ((0, 0), (0, 0), (ROPE_LANE0, LANES - ROPE_LANE0 - QK_ROPE)))
    ctx = (cache_mla_ckv.reshape(nbs, past, KV_LORA), krb_c,
           cache_diff_k.reshape(nbs, H_B, past, 2 * DIFF_HD), cache_diff_v.reshape(nbs, H_B, past, DIFF_VD))
    a_p = _attn0(qm_p, km_p, vm_p, dq_p, dk_p, dv_p, diff_lambda[0], gsub, None, None, seq_p, lam_init)
    a_s = _attn0(qm_s, km_s, vm_s, dq_s, dk_s, dv_s, diff_lambda[0], gsub, ctx, wukv, seq_s, lam_init)
    mtp, mts = mod_rows(0, TM_TAIL)
    wo0 = w_out_ab[0].astype(BF16)
    win0, wout0 = w_ffn_in[0].astype(BF16), w_ffn_out[0].astype(BF16)
    xp = _tail(xp, a_p, mod, mtp, wo0, row2(g_ffn[0]), win0, wout0, None, "tail0_ctx")
    xs = _tail(xs, a_s, mod, mts, wo0, row2(g_ffn[0]), win0, wout0, None, "tail0_lat")

    w1, bg = _prep_odd(w_in_c[0], b_gate_c[0])
    mrp, mrs = mod_rows(1, TM_PROJ)
    gn = row2(g_mlstm[0])
    q_p, k_p, v_p, o_p, gc_p, gr_p = _proj1(xp, mod, mrp, row2(g_mix[1]), w1, bg)
    q_s, k_s, v_s, o_s, gc_s, gr_s = _proj1(xs, mod, mrs, row2(g_mix[1]), w1, bg)
    hs_p, c_new, nst, mst = _mlstm(q_p, k_p, v_p, o_p, gc_p, gr_p, gn, None, seq_p, True)
    state = (state_mlstm_C.reshape(nbs, 2, H_C, DK_C, DV_C),
             state_mlstm_n.reshape(nbs, 2, H_C // 2, 1, LANES),
             jnp.broadcast_to(state_mlstm_m.reshape(nbs, 2, H_C, 1, 1), (nbs, 2, H_C, 1, LANES)))
    (hs_s,) = _mlstm(q_s, k_s, v_s, o_s, gc_s, gr_s, gn, state, seq_s, False)
    mtp, mts = mod_rows(1, TM_TAIL)
    wo1 = w_out_c[0].astype(BF16)
    win1, wout1 = w_ffn_in[1].astype(BF16), w_ffn_out[1].astype(BF16)
    gfin = row2(g_final)
    yp = _tail(xp, hs_p, mod, mtp, wo1, row2(g_ffn[1]), win1, wout1, gfin, "tail1_ctx")
    ys = _tail(xs, hs_s, mod, mts, wo1, row2(g_ffn[1]), win1, wout1, gfin, "tail1_lat")

    return (yp.reshape(nbp, seq_p, D_MODEL), ys.reshape(nbs, seq_s, D_MODEL),
            ckv_new.reshape(nbp, 1, seq_p, KV_LORA), kr_new.reshape(nbp, 1, seq_p, QK_ROPE),
            dk_new.reshape(nbp, 1, H_B, seq_p, 2 * DIFF_HD), dv_new.reshape(nbp, 1, H_B, seq_p, DIFF_VD),
            c_new, nst.reshape(nbp, 1, 2, H_C, DK_C), mst[:, :, 0].reshape(nbp, 1, 2, H_C))
```

```python
import functools
import math

import jax
import jax.numpy as jnp
import numpy as np
from jax import lax
from jax.experimental import pallas as pl
from jax.experimental.pallas import tpu as pltpu

F32 = jnp.float32
BF16 = jnp.bfloat16

D_MODEL = 1024
DEPTH = 2
GRID_W = 64
ROPE_BASE = 10000.0
RMS_EPS = 1e-6
H_A = 8
QK_NOPE = 64
QK_ROPE = 32
V_HD_A = 64
Q_LORA = 256
KV_LORA = 128
H_B = 4
DIFF_HD = 64
DIFF_VD = 2 * DIFF_HD
H_C = 8
DK_C = 64
DV_C = D_MODEL // H_C
D_FF = -(-8 * D_MODEL // (3 * 256)) * 256

LANES = 128
HEAD_PAD = LANES
ROPE_LANE0 = QK_NOPE
N_MOD = 6
COND_ROWS = 16
N_GATE = 2 * H_C

TM_PROJ = 256
TQ_ATTN = 256
TM_TAIL = 512
FF_CHUNK = 1408
MLSTM_L = 128
ADA_TN = 1024
VMEM_LIMIT = 56 * 1024 * 1024

assert D_FF % FF_CHUNK == 0 and FF_CHUNK % LANES == 0


def _dot(a, b):
    return jnp.dot(a, b, preferred_element_type=F32)


def _dot_nt(a, b):
    return lax.dot_general(a, b, (((1,), (1,)), ((), ())), preferred_element_type=F32)


def _dot_tn(a, b):
    return lax.dot_general(a, b, (((0,), (0,)), ((), ())), preferred_element_type=F32)


def _rms(x):
    return x * lax.rsqrt(jnp.mean(x * x, axis=-1, keepdims=True) + RMS_EPS)


def _modulate(x, g, shift, scale):
    return (_rms(x) * g) * (1.0 + scale) + shift


def _params(semantics):
    return pltpu.CompilerParams(dimension_semantics=semantics, vmem_limit_bytes=VMEM_LIMIT)


def _ada_kernel(cond_ref, w_ref, b_ref, o_ref):
    c = cond_ref[...]
    s = (c * jax.nn.sigmoid(c)).astype(BF16)
    o_ref[0] = _dot(s, w_ref[0].astype(BF16)) + b_ref[0]


def _ada(cond, w_ada, b_ada):
    n_out = w_ada.shape[-1]
    return pl.pallas_call(
        _ada_kernel,
        grid=(DEPTH, n_out // ADA_TN),
        in_specs=[
            pl.BlockSpec((COND_ROWS, D_MODEL), lambda l, n: (0, 0)),
            pl.BlockSpec((1, D_MODEL, ADA_TN), lambda l, n: (l, 0, n)),
            pl.BlockSpec((1, 1, ADA_TN), lambda l, n: (l, 0, n)),
        ],
        out_specs=pl.BlockSpec((1, COND_ROWS, ADA_TN), lambda l, n: (l, 0, n)),
        out_shape=jax.ShapeDtypeStruct((DEPTH, COND_ROWS, n_out), F32),
        compiler_params=_params(("arbitrary", "arbitrary")),
        name="ada",
    )(cond, w_ada, b_ada.reshape(DEPTH, 1, n_out))


def _rope(x, c, sa, sb, off):
    return x * c + pltpu.roll(x, LANES - off, 1) * sa + pltpu.roll(x, off, 1) * sb


def _proj0_kernel(*refs, rope, emit_cache):
    x_ref, mod_ref, g_ref, w0_ref, gq_ref, gkv_ref, wq_ref, wukv_ref = refs[:8]
    refs = refs[8:]
    if rope:
        cm_ref, sam_ref, sbm_ref, cd_ref, sad_ref, sbd_ref = refs[:6]
        refs = refs[6:]
    qm_ref, km_ref, vm_ref, dq_ref, dk_ref, dv_ref = refs[:6]
    refs = refs[6:]
    if emit_cache:
        ckvf_ref, krf_ref, dkf_ref, dvf_ref = refs

    h = _modulate(x_ref[...], g_ref[...], mod_ref[0, 0:1, :], mod_ref[0, 1:2, :]).astype(BF16)
    proj = _dot(h, w0_ref[...])
    cq = proj[:, 0:Q_LORA]
    ckv = _rms(proj[:, Q_LORA:Q_LORA + KV_LORA]) * gkv_ref[...]
    krb = proj[:, 384:512]
    qa = _dot((_rms(cq) * gq_ref[...]).astype(BF16), wq_ref[...])
    kv = _dot(ckv.astype(BF16), wukv_ref[...])
    if emit_cache:
        ckvf_ref[...] = ckv
        krf_ref[...] = krb[:, ROPE_LANE0:ROPE_LANE0 + QK_ROPE]
    if rope:
        cm, sam, sbm = cm_ref[...], sam_ref[...], sbm_ref[...]
        cd, sad, sbd = cd_ref[...], sad_ref[...], sbd_ref[...]
        krb = _rope(krb, cm, sam, sbm, QK_ROPE // 4)
    for hd in range(H_A):
        sl = slice(hd * HEAD_PAD, (hd + 1) * HEAD_PAD)
        qh = qa[:, sl]
        if rope:
            qh = _rope(qh, cm, sam, sbm, QK_ROPE // 4)
        qm_ref[:, sl] = qh.astype(BF16)
        km_ref[:, sl] = (kv[:, sl] + krb).astype(BF16)
    vm_ref[...] = kv[:, H_A * HEAD_PAD:].astype(BF16)
    for hd in range(H_B):
        sl = slice(hd * LANES, (hd + 1) * LANES)
        dq = proj[:, 512 + hd * LANES:512 + (hd + 1) * LANES]
        dk = proj[:, 1024 + hd * LANES:1024 + (hd + 1) * LANES]
        dv = proj[:, 1536 + hd * LANES:1536 + (hd + 1) * LANES]
        if emit_cache:
            dkf_ref[0, hd] = dk
            dvf_ref[0, hd] = dv
        if rope:
            dq = _rope(dq, cd, sad, sbd, DIFF_HD // 4)
            dk = _rope(dk, cd, sad, sbd, DIFF_HD // 4)
        dq_ref[:, sl] = (dq * (DIFF_HD ** -0.5)).astype(BF16)
        dk_ref[:, sl] = dk.astype(BF16)
        dv_ref[:, sl] = dv.astype(BF16)


def _proj0(x, mod, mod_row, g, w0, gq, gkv, wq, wukv, rope_tabs, seq):
    n = x.shape[0]
    tm = TM_PROJ
    rope = rope_tabs is not None
    emit_cache = not rope
    tiles_per_seq = seq // tm
    full = lambda shape: pl.BlockSpec(shape, lambda i: (0,) * len(shape))
    in_specs = [
        pl.BlockSpec((tm, D_MODEL), lambda i: (i, 0)),
        pl.BlockSpec((1, N_MOD, D_MODEL), lambda i: (mod_row(i), 0, 0)),
        full((1, D_MODEL)), full(w0.shape), full((1, Q_LORA)), full((1, KV_LORA)),
        full(wq.shape), full(wukv.shape),
    ]
    args = [x, mod, g, w0, gq, gkv, wq, wukv]
    if rope:
        in_specs += [pl.BlockSpec((tm, LANES), lambda i: (i % tiles_per_seq, 0))] * 6
        args += list(rope_tabs)
    row = lambda w: pl.BlockSpec((tm, w), lambda i: (i, 0))
    out_specs = [row(1024), row(1024), row(1024), row(512), row(512), row(512)]
    out_shape = [jax.ShapeDtypeStruct((n, w), BF16) for w in (1024, 1024, 1024, 512, 512, 512)]
    if emit_cache:
        assert tiles_per_seq == 1
        cache = pl.BlockSpec((1, H_B, tm, LANES), lambda i: (i, 0, 0, 0))
        out_specs += [row(KV_LORA), row(QK_ROPE), cache, cache]
        out_shape += [jax.ShapeDtypeStruct((n, KV_LORA), F32), jax.ShapeDtypeStruct((n, QK_ROPE), F32),
                      jax.ShapeDtypeStruct((n // seq, H_B, seq, LANES), F32),
                      jax.ShapeDtypeStruct((n // seq, H_B, seq, LANES), F32)]
    return pl.pallas_call(
        functools.partial(_proj0_kernel, rope=rope, emit_cache=emit_cache),
        grid=(n // tm,),
        in_specs=in_specs, out_specs=out_specs, out_shape=out_shape,
        compiler_params=_params(("arbitrary",)),
        name="proj0_rope" if rope else "proj0_ctx",
    )(*args)


def _attn_kernel(*refs, has_ctx, lam_init):
    q_ref, dq_ref, k_ref, v_ref, dk_ref, dv_ref, lamv_ref, gsub_ref = refs[:8]
    refs = refs[8:]
    if has_ctx:
        ckvc_ref, krc_ref, dkc_ref, dvc_ref, wukv_ref = refs[:5]
        refs = refs[5:]
    o_ref, kx_s, vx_s, dkx_s, dvx_s = refs
    seq = k_ref.shape[0]
    ktot = kx_s.shape[0]

    @pl.when(pl.program_id(1) == 0)
    def _():
        kx_s[0:seq, :] = k_ref[...]
        dkx_s[0:seq, :] = dk_ref[...]
        ones_own = jnp.ones((seq, LANES), BF16)
        for hd in range(H_A):
            vx_s[0:seq, 2 * hd * LANES:(2 * hd + 1) * LANES] = v_ref[:, hd * LANES:(hd + 1) * LANES]
            vx_s[0:seq, (2 * hd + 1) * LANES:(2 * hd + 2) * LANES] = ones_own
        for hd in range(H_B):
            dvx_s[0:seq, 2 * hd * LANES:(2 * hd + 1) * LANES] = dv_ref[:, hd * LANES:(hd + 1) * LANES]
            dvx_s[0:seq, (2 * hd + 1) * LANES:(2 * hd + 2) * LANES] = ones_own
        if has_ctx:
            kv = _dot(ckvc_ref[0].astype(BF16), wukv_ref[...])
            krb = krc_ref[0]
            ones_ctx = jnp.ones((ktot - seq, LANES), BF16)
            for hd in range(H_A):
                sl = slice(hd * HEAD_PAD, (hd + 1) * HEAD_PAD)
                kx_s[seq:ktot, sl] = (kv[:, sl] + krb).astype(BF16)
                vx_s[seq:ktot, 2 * hd * LANES:(2 * hd + 1) * LANES] = (
                    kv[:, (H_A + hd) * HEAD_PAD:(H_A + hd + 1) * HEAD_PAD].astype(BF16))
                vx_s[seq:ktot, (2 * hd + 1) * LANES:(2 * hd + 2) * LANES] = ones_ctx
            for hd in range(H_B):
                sl = slice(hd * LANES, (hd + 1) * LANES)
                dkx_s[seq:ktot, sl] = dkc_ref[0, hd].astype(BF16)
                dvx_s[seq:ktot, 2 * hd * LANES:(2 * hd + 1) * LANES] = dvc_ref[0, hd].astype(BF16)
                dvx_s[seq:ktot, (2 * hd + 1) * LANES:(2 * hd + 2) * LANES] = ones_ctx

    log2e = 1.0 / math.log(2.0)
    tq = dq_ref.shape[0]
    lo = lax.broadcasted_iota(jnp.int32, (tq, LANES), 1) < DIFF_HD

    jobs = []
    for hd in range(H_A):
        sl = slice(hd * HEAD_PAD, (hd + 1) * HEAD_PAD)
        jobs.append((lambda sl=sl: q_ref[:, sl], (kx_s, sl), (vx_s, hd), (QK_NOPE + QK_ROPE) ** -0.5 * log2e))
    for hd in range(H_B):
        sl = slice(hd * LANES, (hd + 1) * LANES)
        for part in range(2):
            def qfn(sl=sl, part=part):
                dq = dq_ref[:, sl].astype(F32)
                return (jnp.where(lo, dq, 0.0) if part == 0 else jnp.where(lo, 0.0, dq)).astype(BF16)
            jobs.append((qfn, (dkx_s, sl), (dvx_s, hd), log2e))

    def scores(job):
        qfn, (kref, sl), _, _ = job
        return _dot_nt(qfn(), kref[:, sl])

    def finish(s, job):
        _, _, (vref, hd), c = job
        m = jnp.max(s, axis=-1, keepdims=True)
        e = jnp.exp2((s - m) * c).astype(BF16)
        res = _dot(e, vref[:, 2 * hd * LANES:(2 * hd + 2) * LANES])
        return res[:, 0:LANES] * (1.0 / res[:, LANES:2 * LANES])

    outs = []
    s_next = scores(jobs[0])
    for i, job in enumerate(jobs):
        s_cur = s_next
        if i + 1 < len(jobs):
            s_next = scores(jobs[i + 1])
        outs.append(finish(s_cur, job))

    for j in range(H_A // 2):
        o_ref[:, j * LANES:(j + 1) * LANES] = (outs[2 * j] + outs[2 * j + 1]).astype(BF16)
    lv = lamv_ref[...]
    lam = (jnp.exp(jnp.sum(lv[0:1] * lv[1:2], axis=-1, keepdims=True))
           - jnp.exp(jnp.sum(lv[2:3] * lv[3:4], axis=-1, keepdims=True)) + lam_init)
    for hd in range(H_B):
        acc = outs[H_A + 2 * hd] - lam * outs[H_A + 2 * hd + 1]
        ob = (_rms(acc) * gsub_ref[...]) * (1.0 - lam_init)
        o_ref[:, H_A * V_HD_A + hd * LANES:H_A * V_HD_A + (hd + 1) * LANES] = ob.astype(BF16)


def _attn0(qm, km, vm, dq, dk, dv, lamv, gsub, ctx, wukv, seq, lam_init):
    n = qm.shape[0]
    nb = n // seq
    tq = TQ_ATTN
    has_ctx = ctx is not None
    full = lambda shape: pl.BlockSpec(shape, lambda b, t: (0,) * len(shape))
    qrow = lambda w: pl.BlockSpec((tq, w), lambda b, t: (b * (seq // tq) + t, 0))
    krow = lambda w: pl.BlockSpec((seq, w), lambda b, t: (b, 0))
    in_specs = [qrow(1024), qrow(512), krow(1024), krow(1024), krow(512), krow(512),
                full((4, DIFF_HD)), full((1, DIFF_VD))]
    args = [qm, dq, km, vm, dk, dv, lamv, gsub]
    past = 0
    if has_ctx:
        ckv_c, krb_c, dk_c, dv_c = ctx
        past = ckv_c.shape[1]
        in_specs += [pl.BlockSpec((1, past, LANES), lambda b, t: (b, 0, 0)),
                     pl.BlockSpec((1, past, LANES), lambda b, t: (b, 0, 0)),
                     pl.BlockSpec((1, H_B, past, LANES), lambda b, t: (b, 0, 0, 0)),
                     pl.BlockSpec((1, H_B, past, LANES), lambda b, t: (b, 0, 0, 0)),
                     full(wukv.shape)]
        args += [ckv_c, krb_c, dk_c, dv_c, wukv]
    ktot = seq + past
    scratch = [pltpu.VMEM((ktot, H_A * HEAD_PAD), BF16), pltpu.VMEM((ktot, 2 * H_A * LANES), BF16),
               pltpu.VMEM((ktot, H_B * LANES), BF16), pltpu.VMEM((ktot, 2 * H_B * LANES), BF16)]
    return pl.pallas_call(
        functools.partial(_attn_kernel, has_ctx=has_ctx, lam_init=lam_init),
        grid=(nb, seq // tq),
        in_specs=in_specs,
        out_specs=pl.BlockSpec((tq, D_MODEL), lambda b, t: (b * (seq // tq) + t, 0)),
        out_shape=jax.ShapeDtypeStruct((n, D_MODEL), BF16),
        scratch_shapes=scratch,
        compiler_params=_params(("arbitrary", "arbitrary")),
        name="attn0_lat" if has_ctx else "attn0_ctx",
    )(*args)


def _tail_kernel(*refs, final):
    x_ref, a_ref, mod_ref, wo_ref, gf_ref, win_ref, wout_ref = refs[:7]
    if final:
        gfin_ref, o_ref = refs[7:]
    else:
        (o_ref,) = refs[7:]
    x1 = x_ref[...] + mod_ref[0, 2:3, :] * _dot(a_ref[...], wo_ref[...])
    h = _modulate(x1, gf_ref[...], mod_ref[0, 3:4, :], mod_ref[0, 4:5, :]).astype(BF16)
    acc = None
    for c in range(D_FF // FF_CHUNK):
        a = _dot(h, win_ref[:, c * FF_CHUNK:(c + 1) * FF_CHUNK])
        b = _dot(h, win_ref[:, D_FF + c * FF_CHUNK:D_FF + (c + 1) * FF_CHUNK])
        act = ((a * jax.nn.sigmoid(a)) * b).astype(BF16)
        part = _dot(act, wout_ref[c * FF_CHUNK:(c + 1) * FF_CHUNK, :])
        acc = part if acc is None else acc + part
    x2 = x1 + mod_ref[0, 5:6, :] * acc
    if final:
        x2 = _rms(x2) * gfin_ref[...]
    o_ref[...] = x2


def _tail(x, a, mod, mod_row, wo, gf, win, wout, gfin, name):
    n = x.shape[0]
    tm = TM_TAIL
    final = gfin is not None
    full = lambda shape: pl.BlockSpec(shape, lambda i: (0,) * len(shape))
    in_specs = [
        pl.BlockSpec((tm, D_MODEL), lambda i: (i, 0)),
        pl.BlockSpec((tm, D_MODEL), lambda i: (i, 0)),
        pl.BlockSpec((1, N_MOD, D_MODEL), lambda i: (mod_row(i), 0, 0)),
        full(wo.shape), full((1, D_MODEL)), full(win.shape), full(wout.shape),
    ]
    args = [x, a, mod, wo, gf, win, wout]
    if final:
        in_specs.append(full((1, D_MODEL)))
        args.append(gfin)
    return pl.pallas_call(
        functools.partial(_tail_kernel, final=final),
        grid=(n // tm,),
        in_specs=in_specs,
        out_specs=pl.BlockSpec((tm, D_MODEL), lambda i: (i, 0)),
        out_shape=jax.ShapeDtypeStruct((n, D_MODEL), F32),
        compiler_params=_params(("arbitrary",)),
        name=name,
    )(*args)


def _split3(x):
    hi = x.astype(BF16)
    r1 = x - hi.astype(F32)
    mid = r1.astype(BF16)
    lo = (r1 - mid.astype(F32)).astype(BF16)
    return hi, mid, lo


def _proj1_kernel(x_ref, mod_ref, g_ref, w1_ref, bg_ref, q_ref, k_ref, v_ref, o_ref, gc_ref, gr_ref):
    tm = x_ref.shape[0]
    L = MLSTM_L
    h = _modulate(x_ref[...], g_ref[...], mod_ref[0, 0:1, :], mod_ref[0, 1:2, :]).astype(BF16)
    proj = _dot(h, w1_ref[...])
    hk = H_C * DK_C
    hv = H_C * DV_C
    for blk in range(hk // LANES):
        sl = slice(blk * LANES, (blk + 1) * LANES)
        q_ref[sl, :] = jnp.transpose(proj[:, sl]).astype(BF16)
    for blk in range(hv // LANES):
        sl = slice(blk * LANES, (blk + 1) * LANES)
        v_ref[sl, :] = jnp.transpose(proj[:, 2 * hk + blk * LANES:2 * hk + (blk + 1) * LANES]).astype(BF16)
    k_ref[...] = proj[:, hk:2 * hk] * (DK_C ** -0.5)
    o_ref[...] = proj[:, 2 * hk + hv:2 * hk + 2 * hv]
    gates = proj[:, 2 * hk + 2 * hv:2 * hk + 2 * hv + LANES] + bg_ref[...]
    lf = jnp.minimum(gates, 0.0) - jnp.log1p(jnp.exp(-jnp.abs(gates)))
    r = lax.broadcasted_iota(jnp.int32, (tm, tm), 0)
    c = lax.broadcasted_iota(jnp.int32, (tm, tm), 1)
    same = (r // L) == (c // L)
    pre = jnp.where(same & (c <= r), 1.0, 0.0).astype(BF16)
    suf = jnp.where(same & (c >= r), 1.0, 0.0).astype(BF16)
    parts = _split3(lf)
    lane = lax.broadcasted_iota(jnp.int32, (tm, LANES), 1)
    b_sum = jnp.where((lane % 4) >= 2,
                      functools.reduce(jnp.add, [_dot(suf, p) for p in parts]),
                      functools.reduce(jnp.add, [_dot(pre, p) for p in parts]))
    b = pltpu.roll(b_sum, LANES - N_GATE, 1)
    u = gates - b
    low = lane < N_GATE
    comb_t = jnp.transpose(jnp.where(low, b, 0.0) + pltpu.roll(jnp.where(low, u, 0.0), N_GATE, 1))
    b_t = comb_t[0:N_GATE]
    u_t = comb_t[N_GATE:2 * N_GATE]
    pos = lax.broadcasted_iota(jnp.int32, (N_GATE, tm), 1) % L
    bwd = (lax.broadcasted_iota(jnp.int32, (N_GATE, tm), 0) % 4) >= 2
    cm_t = u_t
    step = 1
    while step < L:
        below = jnp.where(pos >= step, pltpu.roll(cm_t, step, 1), -jnp.inf)
        above = jnp.where(pos < L - step, pltpu.roll(cm_t, tm - step, 1), -jnp.inf)
        cm_t = jnp.maximum(cm_t, jnp.where(bwd, above, below))
        step *= 2
    rows = jnp.concatenate([b_t, cm_t, u_t, jnp.zeros((LANES - 3 * N_GATE, tm), F32)], axis=0)
    gr_ref[...] = rows[0:3 * N_GATE]
    gc_ref[...] = jnp.transpose(rows)


def _proj1(x, mod, mod_row, g, w1, bg):
    n = x.shape[0]
    tm = TM_PROJ
    assert tm % MLSTM_L == 0
    full = lambda shape: pl.BlockSpec(shape, lambda i: (0,) * len(shape))
    row = lambda w: pl.BlockSpec((tm, w), lambda i: (i, 0))
    col = lambda w: pl.BlockSpec((w, tm), lambda i: (0, i))
    hk, hv = H_C * DK_C, H_C * DV_C
    return pl.pallas_call(
        _proj1_kernel,
        grid=(n // tm,),
        in_specs=[row(D_MODEL), pl.BlockSpec((1, N_MOD, D_MODEL), lambda i: (mod_row(i), 0, 0)),
                  full((1, D_MODEL)), full(w1.shape), full((1, LANES))],
        out_specs=[col(hk), row(hk), col(hv), row(hv), row(LANES),
                   pl.BlockSpec((3 * N_GATE, tm), lambda i: (0, i))],
        out_shape=[jax.ShapeDtypeStruct((hk, n), BF16), jax.ShapeDtypeStruct((n, hk), F32),
                   jax.ShapeDtypeStruct((hv, n), BF16), jax.ShapeDtypeStruct((n, hv), F32),
                   jax.ShapeDtypeStruct((n, LANES), F32), jax.ShapeDtypeStruct((3 * N_GATE, n), F32)],
        compiler_params=_params(("arbitrary",)),
        name="proj1",
    )(x, mod, g, w1, bg)


def _chain(j, d, hh):
    return (2 * j + d) * 2 + hh


def _mlstm_kernel(*refs, has_state, emit_state):
    q_ref, k_ref, v_ref, o_ref, gc_ref, gr_ref, gn_ref = refs[:7]
    refs = refs[7:]
    if has_state:
        c0_ref, n0_ref, m0_ref = refs[:3]
        refs = refs[3:]
    hs_ref = refs[0]
    refs = refs[1:]
    if emit_state:
        cf_ref, nst_ref, mst_ref = refs[:3]
        refs = refs[3:]
    cx_s, m_s, h_s = refs

    L = MLSTM_L
    seq = k_ref.shape[0]
    nc = seq // L
    npair = H_C // 2
    lane = lax.broadcasted_iota(jnp.int32, (1, LANES), 1)
    head_mask = [lane < DK_C, lane >= DK_C]
    ri = lax.broadcasted_iota(jnp.int32, (L, L), 0)
    ci = lax.broadcasted_iota(jnp.int32, (L, L), 1)
    causal = [ri <= ci, ri >= ci]
    ones_blk = jnp.ones((DV_C, L), BF16)
    chains = [(j, d, hh) for d in range(2) for j in range(npair) for hh in range(2)]

    h_s[...] = jnp.zeros_like(h_s)
    for j, d, hh in chains:
        ch = _chain(j, d, hh)
        if has_state:
            zpad = jnp.zeros((DK_C, DV_C), F32)
            c0 = c0_ref[0, d, 2 * j + hh]
            c0 = jnp.concatenate([c0, zpad] if hh == 0 else [zpad, c0], axis=0)
            cx_s[ch, 0:DV_C, :] = jnp.transpose(c0)
            n_row = jnp.where(head_mask[hh], n0_ref[0, d, j], 0.0)
            cx_s[ch, DV_C:2 * DV_C, :] = jnp.broadcast_to(n_row, (DV_C, LANES))
            m_s[ch] = m0_ref[0, d, 2 * j + hh]
        else:
            cx_s[ch] = jnp.zeros((2 * DV_C, LANES), F32)
            m_s[ch] = jnp.zeros((1, LANES), F32)

    def chunk_step(i, carry):
        sl = [pl.ds(pl.multiple_of(i * L, L), L), pl.ds(pl.multiple_of((nc - 1 - i) * L, L), L)]
        gcol = [gc_ref[sl[d], :] for d in range(2)]
        grow = [gr_ref[:, sl[d]] for d in range(2)]
        kpair = {(d, j): k_ref[sl[d], j * LANES:(j + 1) * LANES] for d in range(2) for j in range(npair)}
        qt = {(d, j): q_ref[j * LANES:(j + 1) * LANES, sl[d]] for d in range(2) for j in range(npair)}
        st, km, u_bc, vt, row = {}, {}, {}, {}, {}
        for j, d, hh in chains:
            key = (j, d, hh)
            idx = 4 * j + 2 * d + hh
            edge = L - 1 if d == 0 else 0
            b_row = grow[d][idx:idx + 1, :]
            cm_row = grow[d][N_GATE + idx:N_GATE + idx + 1, :]
            u_col = gcol[d][:, 2 * N_GATE + idx:2 * N_GATE + idx + 1]
            m_prev = m_s[_chain(j, d, hh)][:, 0:1]
            row[key] = (b_row, cm_row, b_row[:, edge:edge + 1], cm_row[:, edge:edge + 1], m_prev)
            u_bc[key] = jnp.broadcast_to(u_col, (L, L))
            km[key] = jnp.where(head_mask[hh], kpair[(d, j)], 0.0)
            vt[key] = v_ref[(2 * j + hh) * DV_C:(2 * j + hh + 1) * DV_C, sl[d]]
            st[key] = _dot(km[key].astype(BF16), qt[(d, j)])
        for j, d, hh in chains:
            key = (j, d, hh)
            ch = _chain(j, d, hh)
            b_row, cm_row, g_tot, cm_last, m_prev = row[key]
            head = slice((2 * j + hh) * DV_C, (2 * j + hh + 1) * DV_C)
            cx = cx_s[ch]
            big_m = jnp.maximum(m_prev, cm_row)
            s = st[key] * jnp.exp(jnp.where(causal[d], u_bc[key] - big_m, -jnp.inf))
            inter = jnp.exp(m_prev - big_m)
            qcx = _dot(cx.astype(BF16), qt[(d, j)])
            num = _dot(vt[key], s.astype(BF16)) + inter * qcx[0:DV_C, :]
            den = jnp.sum(s, axis=0, keepdims=True) + inter * qcx[DV_C:DV_C + 1, :]
            hval = num * (1.0 / jnp.maximum(jnp.abs(den), jnp.exp(-(b_row + big_m))))
            h_s[head, sl[d]] = h_s[head, sl[d]] + hval

            m_top = jnp.maximum(m_prev, cm_last)
            kw = km[key] * jnp.exp(u_bc[key] - cm_last)
            vx = jnp.concatenate([vt[key], ones_blk], axis=0)
            cx_s[ch] = (jnp.exp(m_prev - m_top) * cx
                        + jnp.exp(cm_last - m_top) * _dot(vx, kw.astype(BF16)))
            m_s[ch] = jnp.broadcast_to(g_tot + m_top, (1, LANES))
        return carry

    lax.fori_loop(0, nc, chunk_step, 0)

    for hd in range(H_C):
        sl = slice(hd * DV_C, (hd + 1) * DV_C)
        hsum = jnp.transpose(h_s[sl, :])
        y = (_rms(hsum) * gn_ref[...]) * jax.nn.sigmoid(o_ref[:, sl])
        hs_ref[:, sl] = y.astype(BF16)
    if emit_state:
        for j, d, hh in chains:
            ch = _chain(j, d, hh)
            cf_ref[0, 0, d, 2 * j + hh] = jnp.transpose(cx_s[ch, 0:DV_C, :])[hh * DK_C:(hh + 1) * DK_C, :]
            mst_ref[0, d * H_C + 2 * j + hh:d * H_C + 2 * j + hh + 1, :] = m_s[ch]
        for j in range(npair):
            for d in range(2):
                nst_ref[0, d * npair + j:d * npair + j + 1, :] = (
                    cx_s[_chain(j, d, 0), DV_C:DV_C + 1, :] + cx_s[_chain(j, d, 1), DV_C:DV_C + 1, :])


def _mlstm(q, k, v, o, gc, gr, gn, state, seq, emit_state):
    n = k.shape[0]
    nb = n // seq
    npair = H_C // 2
    has_state = state is not None
    hk, hv = H_C * DK_C, H_C * DV_C
    in_specs = [
        pl.BlockSpec((hk, seq), lambda b: (0, b)),
        pl.BlockSpec((seq, hk), lambda b: (b, 0)),
        pl.BlockSpec((hv, seq), lambda b: (0, b)),
        pl.BlockSpec((seq, hv), lambda b: (b, 0)),
        pl.BlockSpec((seq, LANES), lambda b: (b, 0)),
        pl.BlockSpec((3 * N_GATE, seq), lambda b: (0, b)),
        pl.BlockSpec((1, DV_C), lambda b: (0, 0)),
    ]
    args = [q, k, v, o, gc, gr, gn]
    if has_state:
        c0, n0, m0 = state
        in_specs += [pl.BlockSpec((1, 2, H_C, DK_C, DV_C), lambda b: (b, 0, 0, 0, 0)),
                     pl.BlockSpec((1, 2, npair, 1, LANES), lambda b: (b, 0, 0, 0, 0)),
                     pl.BlockSpec((1, 2, H_C, 1, LANES), lambda b: (b, 0, 0, 0, 0))]
        args += [c0, n0, m0]
    out_specs = [pl.BlockSpec((seq, hv), lambda b: (b, 0))]
    out_shape = [jax.ShapeDtypeStruct((n, hv), BF16)]
    if emit_state:
        out_specs += [pl.BlockSpec((1, 1, 2, H_C, DK_C, DV_C), lambda b: (b, 0, 0, 0, 0, 0)),
                      pl.BlockSpec((1, 2 * npair, LANES), lambda b: (b, 0, 0)),
                      pl.BlockSpec((1, 2 * H_C, LANES), lambda b: (b, 0, 0))]
        out_shape += [jax.ShapeDtypeStruct((nb, 1, 2, H_C, DK_C, DV_C), F32),
                      jax.ShapeDtypeStruct((nb, 2 * npair, LANES), F32),
                      jax.ShapeDtypeStruct((nb, 2 * H_C, LANES), F32)]
    n_chain = 2 * H_C
    return pl.pallas_call(
        functools.partial(_mlstm_kernel, has_state=has_state, emit_state=emit_state),
        grid=(nb,),
        in_specs=in_specs, out_specs=out_specs, out_shape=out_shape,
        scratch_shapes=[pltpu.VMEM((n_chain, 2 * DV_C, LANES), F32), pltpu.VMEM((n_chain, 1, LANES), F32),
                        pltpu.VMEM((hv, seq), F32)],
        compiler_params=_params(("arbitrary",)),
        name="mlstm_lat" if has_state else "mlstm_ctx",
    )(*args)


def _rope_tables(n_tok):
    t = np.arange(n_tok)
    rows = (t // GRID_W).astype(np.float64)
    cols = (t % GRID_W).astype(np.float64)

    def axis_tabs(width, lane0):
        half = width // 2
        quarter = half // 2
        freqs = np.power(ROPE_BASE, -np.arange(quarter, dtype=np.float64) / quarter)
        c = np.ones((n_tok, LANES))
        sa = np.zeros((n_tok, LANES))
        sb = np.zeros((n_tok, LANES))
        for g, pos in enumerate((rows, cols)):
            ang = pos[:, None] * freqs[None, :]
            a0 = lane0 + g * half
            c[:, a0:a0 + quarter] = np.cos(ang)
            c[:, a0 + quarter:a0 + half] = np.cos(ang)
            sa[:, a0:a0 + quarter] = -np.sin(ang)
            sb[:, a0 + quarter:a0 + half] = np.sin(ang)
        return c, sa, sb

    cm, sam, sbm = axis_tabs(QK_ROPE, ROPE_LANE0)
    c0, sa0, sb0 = axis_tabs(DIFF_HD, 0)
    c1, sa1, sb1 = axis_tabs(DIFF_HD, DIFF_HD)
    cd = np.where(np.arange(LANES)[None, :] < DIFF_HD, c0, c1)
    return tuple(jnp.asarray(a, F32) for a in (cm, sam, sbm, cd, sa0 + sa1, sb0 + sb1))


def _prep_even(w_in_ab, w_uq, w_ukv):
    z = lambda n: jnp.zeros((D_MODEL, n), F32)
    c2 = Q_LORA + KV_LORA
    w0 = jnp.concatenate([w_in_ab[:, :c2], z(ROPE_LANE0), w_in_ab[:, c2:c2 + QK_ROPE],
                          z(LANES - ROPE_LANE0 - QK_ROPE), w_in_ab[:, c2 + QK_ROPE:]], axis=1).astype(BF16)
    wq = jnp.pad(w_uq.reshape(Q_LORA, H_A, QK_NOPE + QK_ROPE),
                 ((0, 0), (0, 0), (0, HEAD_PAD - QK_NOPE - QK_ROPE))).reshape(Q_LORA, H_A * HEAD_PAD)
    kvw = w_ukv.reshape(KV_LORA, H_A, QK_NOPE + V_HD_A)
    kpad = jnp.pad(kvw[..., :QK_NOPE], ((0, 0), (0, 0), (0, HEAD_PAD - QK_NOPE)))
    vw = kvw[..., QK_NOPE:]
    zv = jnp.zeros_like(vw)
    odd = (jnp.arange(H_A) % 2 == 1)[None, :, None]
    vpad = jnp.where(odd, jnp.concatenate([zv, vw], -1), jnp.concatenate([vw, zv], -1))
    wukv = jnp.concatenate([kpad.reshape(KV_LORA, -1), vpad.reshape(KV_LORA, -1)], axis=1)
    return w0, wq.astype(BF16), wukv.astype(BF16)


def _gate_order(g):
    lead = g.shape[:-1]
    g = g.reshape(lead + (2, 2, H_C // 2, 2))
    perm = tuple(range(len(lead))) + tuple(len(lead) + a for a in (1, 2, 0, 3))
    return g.transpose(perm).reshape(lead + (4 * H_C,))


def _prep_odd(w_in_c, b_gate_c):
    ng = 4 * H_C
    base = w_in_c.shape[1] - ng
    w1 = jnp.concatenate([w_in_c[:, :base], _gate_order(w_in_c[:, base:]),
                          jnp.zeros((D_MODEL, LANES - ng), F32)], axis=1).astype(BF16)
    bg = jnp.pad(_gate_order(b_gate_c), (0, LANES - ng)).reshape(1, LANES)
    return w1, bg


def kernel(x_prompt, x_sample, cache_mla_ckv, cache_mla_krope, cache_diff_k, cache_diff_v,
           state_mlstm_C, state_mlstm_n, state_mlstm_m, c, c_ctx,
           w_ada, b_ada, g_mix, g_ffn, w_ffn_in, w_ffn_out,
           w_in_ab, g_q_lora, g_kv_lora, w_uq, w_ukv, diff_lambda, g_diff_subln, w_out_ab,
           w_in_c, b_gate_c, g_mlstm, w_out_c, g_final):
    nbp, seq_p, _ = x_prompt.shape
    nbs, seq_s, _ = x_sample.shape
    past = cache_mla_ckv.shape[2]
    assert DEPTH == 2 and 1 + nbs <= COND_ROWS
    assert cache_mla_ckv.shape[1] == 1 and state_mlstm_C.shape[1] == 1

    cond = jnp.concatenate([c_ctx[None], c, jnp.zeros((COND_ROWS - 1 - nbs, D_MODEL), F32)], axis=0)
    mod = _ada(cond, w_ada, b_ada).reshape(DEPTH * COND_ROWS, N_MOD, D_MODEL)

    xp = x_prompt.reshape(nbp * seq_p, D_MODEL)
    xs = x_sample.reshape(nbs * seq_s, D_MODEL)
    row2 = lambda v: v.reshape(1, -1)

    def mod_rows(layer, tm):
        prompt = lambda i: layer * COND_ROWS
        sample = lambda i: layer * COND_ROWS + 1 + i // (seq_s // tm)
        return prompt, sample

    lam_init = 0.8 - 0.6 * math.exp(-0.3 * 0)
    w0, wq, wukv = _prep_even(w_in_ab[0], w_uq[0], w_ukv[0])
    rope_tabs = _rope_tables(seq_s)
    mrp, mrs = mod_rows(0, TM_PROJ)
    gq, gkv = row2(g_q_lora[0]), row2(g_kv_lora[0])
    outs_p = _proj0(xp, mod, mrp, row2(g_mix[0]), w0, gq, gkv, wq, wukv, None, seq_p)
    outs_s = _proj0(xs, mod, mrs, row2(g_mix[0]), w0, gq, gkv, wq, wukv, rope_tabs, seq_s)
    qm_p, km_p, vm_p, dq_p, dk_p, dv_p, ckv_new, kr_new, dk_new, dv_new = outs_p
    qm_s, km_s, vm_s, dq_s, dk_s, dv_s = outs_s
    gsub = row2(g_diff_subln[0])
    krb_c = jnp.pad(cache_mla_krope.reshape(nbs, past, QK_ROPE),
                    ((0, 0), (0, 0), (ROPE_LANE0, LANES - ROPE_LANE0 - QK_ROPE)))
    ctx = (cache_mla_ckv.reshape(nbs, past, KV_LORA), krb_c,
           cache_diff_k.reshape(nbs, H_B, past, 2 * DIFF_HD), cache_diff_v.reshape(nbs, H_B, past, DIFF_VD))
    a_p = _attn0(qm_p, km_p, vm_p, dq_p, dk_p, dv_p, diff_lambda[0], gsub, None, None, seq_p, lam_init)
    a_s = _attn0(qm_s, km_s, vm_s, dq_s, dk_s, dv_s, diff_lambda[0], gsub, ctx, wukv, seq_s, lam_init)
    mtp, mts = mod_rows(0, TM_TAIL)
    wo0 = w_out_ab[0].astype(BF16)
    win0, wout0 = w_ffn_in[0].astype(BF16), w_ffn_out[0].astype(BF16)
    xp = _tail(xp, a_p, mod, mtp, wo0, row2(g_ffn[0]), win0, wout0, None, "tail0_ctx")
    xs = _tail(xs, a_s, mod, mts, wo0, row2(g_ffn[0]), win0, wout0, None, "tail0_lat")

    w1, bg = _prep_odd(w_in_c[0], b_gate_c[0])
    mrp, mrs = mod_rows(1, TM_PROJ)
    gn = row2(g_mlstm[0])
    q_p, k_p, v_p, o_p, gc_p, gr_p = _proj1(xp, mod, mrp, row2(g_mix[1]), w1, bg)
    q_s, k_s, v_s, o_s, gc_s, gr_s = _proj1(xs, mod, mrs, row2(g_mix[1]), w1, bg)
    hs_p, c_new, nst, mst = _mlstm(q_p, k_p, v_p, o_p, gc_p, gr_p, gn, None, seq_p, True)
    state = (state_mlstm_C.reshape(nbs, 2, H_C, DK_C, DV_C),
             state_mlstm_n.reshape(nbs, 2, H_C // 2, 1, LANES),
             jnp.broadcast_to(state_mlstm_m.reshape(nbs, 2, H_C, 1, 1), (nbs, 2, H_C, 1, LANES)))
    (hs_s,) = _mlstm(q_s, k_s, v_s, o_s, gc_s, gr_s, gn, state, seq_s, False)
    mtp, mts = mod_rows(1, TM_TAIL)
    wo1 = w_out_c[0].astype(BF16)
    win1, wout1 = w_ffn_in[1].astype(BF16), w_ffn_out[1].astype(BF16)
    gfin = row2(g_final)
    yp = _tail(xp, hs_p, mod, mtp, wo1, row2(g_ffn[1]), win1, wout1, gfin, "tail1_ctx")
    ys = _tail(xs, hs_s, mod, mts, wo1, row2(g_ffn[1]), win1, wout1, gfin, "tail1_lat")

    return (yp.reshape(nbp, seq_p, D_MODEL), ys.reshape(nbs, seq_s, D_MODEL),
            ckv_new.reshape(nbp, 1, seq_p, KV_LORA), kr_new.reshape(nbp, 1, seq_p, QK_ROPE),
            dk_new.reshape(nbp, 1, H_B, seq_p, 2 * DIFF_HD), dv_new.reshape(nbp, 1, H_B, seq_p, DIFF_VD),
            c_new, nst.reshape(nbp, 1, 2, H_C, DK_C), mst[:, :, 0].reshape(nbp, 1, 2, H_C))
```

```python
import functools
import math

import jax
import jax.numpy as jnp
import numpy as np
from jax import lax
from jax.experimental import pallas as pl
from jax.experimental.pallas import tpu as pltpu

F32 = jnp.float32
BF16 = jnp.bfloat16

D_MODEL = 1024
DEPTH = 2
GRID_W = 64
ROPE_BASE = 10000.0
RMS_EPS = 1e-6
H_A = 8
QK_NOPE = 64
QK_ROPE = 32
V_HD_A = 64
Q_LORA = 256
KV_LORA = 128
H_B = 4
DIFF_HD = 64
DIFF_VD = 2 * DIFF_HD
H_C = 8
DK_C = 64
DV_C = D_MODEL // H_C
D_FF = -(-8 * D_MODEL // (3 * 256)) * 256

LANES = 128
HEAD_PAD = LANES
ROPE_LANE0 = QK_NOPE
N_MOD = 6
COND_ROWS = 16
N_GATE = 2 * H_C

TM_PROJ = 512
TM_PROJ1 = 512
TQ_ATTN = 256
TM_TAIL = 512
FF_CHUNK = 1408
MLSTM_L = 128
ADA_TN = 1024
VMEM_LIMIT = 56 * 1024 * 1024

assert D_FF % FF_CHUNK == 0 and FF_CHUNK % LANES == 0


def _dot(a, b):
    return jnp.dot(a, b, preferred_element_type=F32)


def _dot_nt(a, b):
    return lax.dot_general(a, b, (((1,), (1,)), ((), ())), preferred_element_type=F32)


def _dot_tn(a, b):
    return lax.dot_general(a, b, (((0,), (0,)), ((), ())), preferred_element_type=F32)


def _rms(x):
    return x * lax.rsqrt(jnp.mean(x * x, axis=-1, keepdims=True) + RMS_EPS)


def _modulate(x, g, shift, scale):
    return (_rms(x) * g) * (1.0 + scale) + shift


def _params(semantics):
    return pltpu.CompilerParams(dimension_semantics=semantics, vmem_limit_bytes=VMEM_LIMIT)


def _ada_kernel(cond_ref, w_ref, b_ref, o_ref):
    c = cond_ref[...]
    s = (c * jax.nn.sigmoid(c)).astype(BF16)
    o_ref[0] = _dot(s, w_ref[0].astype(BF16)) + b_ref[0]


def _ada(cond, w_ada, b_ada):
    n_out = w_ada.shape[-1]
    return pl.pallas_call(
        _ada_kernel,
        grid=(DEPTH, n_out // ADA_TN),
        in_specs=[
            pl.BlockSpec((COND_ROWS, D_MODEL), lambda l, n: (0, 0)),
            pl.BlockSpec((1, D_MODEL, ADA_TN), lambda l, n: (l, 0, n)),
            pl.BlockSpec((1, 1, ADA_TN), lambda l, n: (l, 0, n)),
        ],
        out_specs=pl.BlockSpec((1, COND_ROWS, ADA_TN), lambda l, n: (l, 0, n)),
        out_shape=jax.ShapeDtypeStruct((DEPTH, COND_ROWS, n_out), F32),
        compiler_params=_params(("arbitrary", "arbitrary")),
        name="ada",
    )(cond, w_ada, b_ada.reshape(DEPTH, 1, n_out))


def _rope(x, c, sa, sb, off):
    return x * c + pltpu.roll(x, LANES - off, 1) * sa + pltpu.roll(x, off, 1) * sb


def _proj0_kernel(*refs, rope, emit_cache):
    x_ref, mod_ref, g_ref, w0_ref, gq_ref, gkv_ref, wq_ref, wukv_ref = refs[:8]
    refs = refs[8:]
    if rope:
        cm_ref, sam_ref, sbm_ref, cd_ref, sad_ref, sbd_ref = refs[:6]
        refs = refs[6:]
    qm_ref, km_ref, vm_ref, dq_ref, dk_ref, dv_ref = refs[:6]
    refs = refs[6:]
    if emit_cache:
        ckvf_ref, krf_ref, dkf_ref, dvf_ref = refs

    h = _modulate(x_ref[...], g_ref[...], mod_ref[0, 0:1, :], mod_ref[0, 1:2, :]).astype(BF16)
    proj = _dot(h, w0_ref[...])
    cq = proj[:, 0:Q_LORA]
    ckv = _rms(proj[:, Q_LORA:Q_LORA + KV_LORA]) * gkv_ref[...]
    krb = proj[:, 384:512]
    qa = _dot((_rms(cq) * gq_ref[...]).astype(BF16), wq_ref[...])
    kv = _dot(ckv.astype(BF16), wukv_ref[...])
    if emit_cache:
        ckvf_ref[...] = ckv
        krf_ref[...] = krb[:, ROPE_LANE0:ROPE_LANE0 + QK_ROPE]
    if rope:
        cm, sam, sbm = cm_ref[...], sam_ref[...], sbm_ref[...]
        cd, sad, sbd = cd_ref[...], sad_ref[...], sbd_ref[...]
        krb = _rope(krb, cm, sam, sbm, QK_ROPE // 4)
    for hd in range(H_A):
        sl = slice(hd * HEAD_PAD, (hd + 1) * HEAD_PAD)
        qh = qa[:, sl]
        if rope:
            qh = _rope(qh, cm, sam, sbm, QK_ROPE // 4)
        qm_ref[:, sl] = qh.astype(BF16)
        km_ref[:, sl] = (kv[:, sl] + krb).astype(BF16)
    vm_ref[...] = kv[:, H_A * HEAD_PAD:].astype(BF16)
    for hd in range(H_B):
        sl = slice(hd * LANES, (hd + 1) * LANES)
        dq = proj[:, 512 + hd * LANES:512 + (hd + 1) * LANES]
        dk = proj[:, 1024 + hd * LANES:1024 + (hd + 1) * LANES]
        dv = proj[:, 1536 + hd * LANES:1536 + (hd + 1) * LANES]
        if emit_cache:
            seq = dkf_ref.shape[2]
            for bi in range(dkf_ref.shape[0]):
                dkf_ref[bi, hd] = dk[bi * seq:(bi + 1) * seq]
                dvf_ref[bi, hd] = dv[bi * seq:(bi + 1) * seq]
        if rope:
            dq = _rope(dq, cd, sad, sbd, DIFF_HD // 4)
            dk = _rope(dk, cd, sad, sbd, DIFF_HD // 4)
        dq_ref[:, sl] = (dq * (DIFF_HD ** -0.5)).astype(BF16)
        dk_ref[:, sl] = dk.astype(BF16)
        dv_ref[:, sl] = dv.astype(BF16)


def _proj0(x, mod, mod_row, g, w0, gq, gkv, wq, wukv, rope_tabs, seq):
    n = x.shape[0]
    tm = TM_PROJ
    rope = rope_tabs is not None
    emit_cache = not rope
    tiles_per_seq = max(seq // tm, 1)
    full = lambda shape: pl.BlockSpec(shape, lambda i: (0,) * len(shape))
    in_specs = [
        pl.BlockSpec((tm, D_MODEL), lambda i: (i, 0)),
        pl.BlockSpec((1, N_MOD, D_MODEL), lambda i: (mod_row(i), 0, 0)),
        full((1, D_MODEL)), full(w0.shape), full((1, Q_LORA)), full((1, KV_LORA)),
        full(wq.shape), full(wukv.shape),
    ]
    args = [x, mod, g, w0, gq, gkv, wq, wukv]
    if rope:
        in_specs += [pl.BlockSpec((tm, LANES), lambda i: (i % tiles_per_seq, 0))] * 6
        args += list(rope_tabs)
    row = lambda w: pl.BlockSpec((tm, w), lambda i: (i, 0))
    out_specs = [row(1024), row(1024), row(1024), row(512), row(512), row(512)]
    out_shape = [jax.ShapeDtypeStruct((n, w), BF16) for w in (1024, 1024, 1024, 512, 512, 512)]
    if emit_cache:
        assert tm % seq == 0
        cache = pl.BlockSpec((tm // seq, H_B, seq, LANES), lambda i: (i, 0, 0, 0))
        out_specs += [row(KV_LORA), row(QK_ROPE), cache, cache]
        out_shape += [jax.ShapeDtypeStruct((n, KV_LORA), F32), jax.ShapeDtypeStruct((n, QK_ROPE), F32),
                      jax.ShapeDtypeStruct((n // seq, H_B, seq, LANES), F32),
                      jax.ShapeDtypeStruct((n // seq, H_B, seq, LANES), F32)]
    return pl.pallas_call(
        functools.partial(_proj0_kernel, rope=rope, emit_cache=emit_cache),
        grid=(n // tm,),
        in_specs=in_specs, out_specs=out_specs, out_shape=out_shape,
        compiler_params=_params(("arbitrary",)),
        name="proj0_rope" if rope else "proj0_ctx",
    )(*args)


def _attn_kernel(*refs, has_ctx, lam_init):
    q_ref, dq_ref, k_ref, v_ref, dk_ref, dv_ref, lamv_ref, gsub_ref = refs[:8]
    refs = refs[8:]
    if has_ctx:
        ckvc_ref, krc_ref, dkc_ref, dvc_ref, wukv_ref = refs[:5]
        refs = refs[5:]
    o_ref, kx_s, vx_s, dkx_s, dvx_s = refs
    seq = k_ref.shape[0]
    ktot = kx_s.shape[0]

    @pl.when(pl.program_id(1) == 0)
    def _():
        kx_s[0:seq, :] = k_ref[...]
        dkx_s[0:seq, :] = dk_ref[...]
        ones_own = jnp.ones((seq, LANES), BF16)
        for hd in range(H_A):
            vx_s[0:seq, 2 * hd * LANES:(2 * hd + 1) * LANES] = v_ref[:, hd * LANES:(hd + 1) * LANES]
            vx_s[0:seq, (2 * hd + 1) * LANES:(2 * hd + 2) * LANES] = ones_own
        for hd in range(H_B):
            dvx_s[0:seq, 2 * hd * LANES:(2 * hd + 1) * LANES] = dv_ref[:, hd * LANES:(hd + 1) * LANES]
            dvx_s[0:seq, (2 * hd + 1) * LANES:(2 * hd + 2) * LANES] = ones_own
        if has_ctx:
            kv = _dot(ckvc_ref[0].astype(BF16), wukv_ref[...])
            krb = krc_ref[0]
            ones_ctx = jnp.ones((ktot - seq, LANES), BF16)
            for hd in range(H_A):
                sl = slice(hd * HEAD_PAD, (hd + 1) * HEAD_PAD)
                kx_s[seq:ktot, sl] = (kv[:, sl] + krb).astype(BF16)
                vx_s[seq:ktot, 2 * hd * LANES:(2 * hd + 1) * LANES] = (
                    kv[:, (H_A + hd) * HEAD_PAD:(H_A + hd + 1) * HEAD_PAD].astype(BF16))
                vx_s[seq:ktot, (2 * hd + 1) * LANES:(2 * hd + 2) * LANES] = ones_ctx
            for hd in range(H_B):
                sl = slice(hd * LANES, (hd + 1) * LANES)
                dkx_s[seq:ktot, sl] = dkc_ref[0, hd].astype(BF16)
                dvx_s[seq:ktot, 2 * hd * LANES:(2 * hd + 1) * LANES] = dvc_ref[0, hd].astype(BF16)
                dvx_s[seq:ktot, (2 * hd + 1) * LANES:(2 * hd + 2) * LANES] = ones_ctx

    log2e = 1.0 / math.log(2.0)
    tq = dq_ref.shape[0]
    lo = lax.broadcasted_iota(jnp.int32, (tq, LANES), 1) < DIFF_HD

    jobs = []
    for hd in range(H_A):
        sl = slice(hd * HEAD_PAD, (hd + 1) * HEAD_PAD)
        jobs.append((lambda sl=sl: q_ref[:, sl], (kx_s, sl), (vx_s, hd), (QK_NOPE + QK_ROPE) ** -0.5 * log2e))
    for hd in range(H_B):
        sl = slice(hd * LANES, (hd + 1) * LANES)
        for part in range(2):
            def qfn(sl=sl, part=part):
                dq = dq_ref[:, sl].astype(F32)
                return (jnp.where(lo, dq, 0.0) if part == 0 else jnp.where(lo, 0.0, dq)).astype(BF16)
            jobs.append((qfn, (dkx_s, sl), (dvx_s, hd), log2e))

    def scores(job):
        qfn, (kref, sl), _, _ = job
        return _dot_nt(qfn(), kref[:, sl])

    def finish(s, job):
        _, _, (vref, hd), c = job
        m = jnp.max(s, axis=-1, keepdims=True)
        e = jnp.exp2((s - m) * c).astype(BF16)
        res = _dot(e, vref[:, 2 * hd * LANES:(2 * hd + 2) * LANES])
        return res[:, 0:LANES] * (1.0 / res[:, LANES:2 * LANES])

    outs = []
    s_next = scores(jobs[0])
    for i, job in enumerate(jobs):
        s_cur = s_next
        if i + 1 < len(jobs):
            s_next = scores(jobs[i + 1])
        outs.append(finish(s_cur, job))

    for j in range(H_A // 2):
        o_ref[:, j * LANES:(j + 1) * LANES] = (outs[2 * j] + outs[2 * j + 1]).astype(BF16)
    lv = lamv_ref[...]
    lam = (jnp.exp(jnp.sum(lv[0:1] * lv[1:2], axis=-1, keepdims=True))
           - jnp.exp(jnp.sum(lv[2:3] * lv[3:4], axis=-1, keepdims=True)) + lam_init)
    for hd in range(H_B):
        acc = outs[H_A + 2 * hd] - lam * outs[H_A + 2 * hd + 1]
        ob = (_rms(acc) * gsub_ref[...]) * (1.0 - lam_init)
        o_ref[:, H_A * V_HD_A + hd * LANES:H_A * V_HD_A + (hd + 1) * LANES] = ob.astype(BF16)


def _attn0(qm, km, vm, dq, dk, dv, lamv, gsub, ctx, wukv, seq, lam_init):
    n = qm.shape[0]
    nb = n // seq
    tq = TQ_ATTN
    has_ctx = ctx is not None
    full = lambda shape: pl.BlockSpec(shape, lambda b, t: (0,) * len(shape))
    qrow = lambda w: pl.BlockSpec((tq, w), lambda b, t: (b * (seq // tq) + t, 0))
    krow = lambda w: pl.BlockSpec((seq, w), lambda b, t: (b, 0))
    in_specs = [qrow(1024), qrow(512), krow(1024), krow(1024), krow(512), krow(512),
                full((4, DIFF_HD)), full((1, DIFF_VD))]
    args = [qm, dq, km, vm, dk, dv, lamv, gsub]
    past = 0
    if has_ctx:
        ckv_c, krb_c, dk_c, dv_c = ctx
        past = ckv_c.shape[1]
        in_specs += [pl.BlockSpec((1, past, LANES), lambda b, t: (b, 0, 0)),
                     pl.BlockSpec((1, past, LANES), lambda b, t: (b, 0, 0)),
                     pl.BlockSpec((1, H_B, past, LANES), lambda b, t: (b, 0, 0, 0)),
                     pl.BlockSpec((1, H_B, past, LANES), lambda b, t: (b, 0, 0, 0)),
                     full(wukv.shape)]
        args += [ckv_c, krb_c, dk_c, dv_c, wukv]
    ktot = seq + past
    scratch = [pltpu.VMEM((ktot, H_A * HEAD_PAD), BF16), pltpu.VMEM((ktot, 2 * H_A * LANES), BF16),
               pltpu.VMEM((ktot, H_B * LANES), BF16), pltpu.VMEM((ktot, 2 * H_B * LANES), BF16)]
    return pl.pallas_call(
        functools.partial(_attn_kernel, has_ctx=has_ctx, lam_init=lam_init),
        grid=(nb, seq // tq),
        in_specs=in_specs,
        out_specs=pl.BlockSpec((tq, D_MODEL), lambda b, t: (b * (seq // tq) + t, 0)),
        out_shape=jax.ShapeDtypeStruct((n, D_MODEL), BF16),
        scratch_shapes=scratch,
        compiler_params=_params(("arbitrary", "arbitrary")),
        name="attn0_lat" if has_ctx else "attn0_ctx",
    )(*args)


def _tail_kernel(*refs, final):
    x_ref, a_ref, mod_ref, wo_ref, gf_ref, win_ref, wout_ref = refs[:7]
    if final:
        gfin_ref, o_ref = refs[7:]
    else:
        (o_ref,) = refs[7:]
    x1 = x_ref[...] + mod_ref[0, 2:3, :] * _dot(a_ref[...], wo_ref[...])
    h = _modulate(x1, gf_ref[...], mod_ref[0, 3:4, :], mod_ref[0, 4:5, :]).astype(BF16)
    acc = None
    for c in range(D_FF // FF_CHUNK):
        a = _dot(h, win_ref[:, c * FF_CHUNK:(c + 1) * FF_CHUNK])
        b = _dot(h, win_ref[:, D_FF + c * FF_CHUNK:D_FF + (c + 1) * FF_CHUNK])
        act = ((a * jax.nn.sigmoid(a)) * b).astype(BF16)
        part = _dot(act, wout_ref[c * FF_CHUNK:(c + 1) * FF_CHUNK, :])
        acc = part if acc is None else acc + part
    x2 = x1 + mod_ref[0, 5:6, :] * acc
    if final:
        x2 = _rms(x2) * gfin_ref[...]
    o_ref[...] = x2


def _tail(x, a, mod, mod_row, wo, gf, win, wout, gfin, name):
    n = x.shape[0]
    tm = TM_TAIL
    final = gfin is not None
    full = lambda shape: pl.BlockSpec(shape, lambda i: (0,) * len(shape))
    in_specs = [
        pl.BlockSpec((tm, D_MODEL), lambda i: (i, 0)),
        pl.BlockSpec((tm, D_MODEL), lambda i: (i, 0)),
        pl.BlockSpec((1, N_MOD, D_MODEL), lambda i: (mod_row(i), 0, 0)),
        full(wo.shape), full((1, D_MODEL)), full(win.shape), full(wout.shape),
    ]
    args = [x, a, mod, wo, gf, win, wout]
    if final:
        in_specs.append(full((1, D_MODEL)))
        args.append(gfin)
    return pl.pallas_call(
        functools.partial(_tail_kernel, final=final),
        grid=(n // tm,),
        in_specs=in_specs,
        out_specs=pl.BlockSpec((tm, D_MODEL), lambda i: (i, 0)),
        out_shape=jax.ShapeDtypeStruct((n, D_MODEL), F32),
        compiler_params=_params(("arbitrary",)),
        name=name,
    )(*args)


def _split3(x):
    hi = x.astype(BF16)
    r1 = x - hi.astype(F32)
    mid = r1.astype(BF16)
    lo = (r1 - mid.astype(F32)).astype(BF16)
    return hi, mid, lo


def _proj1_kernel(x_ref, mod_ref, g_ref, w1_ref, bg_ref, q_ref, k_ref, v_ref, o_ref, gc_ref, gr_ref):
    tm = x_ref.shape[0]
    L = MLSTM_L
    hk = H_C * DK_C
    hv = H_C * DV_C
    h = _modulate(x_ref[...], g_ref[...], mod_ref[0, 0:1, :], mod_ref[0, 1:2, :]).astype(BF16)
    gates = _dot(h, w1_ref[:, 2 * hk + 2 * hv:2 * hk + 2 * hv + LANES]) + bg_ref[...]
    proj = _dot(h, w1_ref[:, 0:2 * hk + 2 * hv])
    for blk in range(hk // LANES):
        sl = slice(blk * LANES, (blk + 1) * LANES)
        q_ref[sl, :] = jnp.transpose(proj[:, sl]).astype(BF16)
    for blk in range(hv // LANES):
        sl = slice(blk * LANES, (blk + 1) * LANES)
        v_ref[sl, :] = jnp.transpose(proj[:, 2 * hk + blk * LANES:2 * hk + (blk + 1) * LANES]).astype(BF16)
    k_ref[...] = proj[:, hk:2 * hk] * (DK_C ** -0.5)
    o_ref[...] = proj[:, 2 * hk + hv:2 * hk + 2 * hv]
    lf = jnp.minimum(gates, 0.0) - jnp.log1p(jnp.exp(-jnp.abs(gates)))
    r = lax.broadcasted_iota(jnp.int32, (L, L), 0)
    c = lax.broadcasted_iota(jnp.int32, (L, L), 1)
    pre = jnp.where(c <= r, 1.0, 0.0).astype(BF16)
    suf = jnp.where(c >= r, 1.0, 0.0).astype(BF16)
    parts = jnp.concatenate(_split3(lf), axis=1)
    fold = lambda t: t[:, 0:LANES] + t[:, LANES:2 * LANES] + t[:, 2 * LANES:3 * LANES]
    chunks = [parts[ck * L:(ck + 1) * L] for ck in range(tm // L)]
    lane = lax.broadcasted_iota(jnp.int32, (tm, LANES), 1)
    b_sum = jnp.where((lane % 4) >= 2,
                      jnp.concatenate([fold(_dot(suf, p)) for p in chunks], axis=0),
                      jnp.concatenate([fold(_dot(pre, p)) for p in chunks], axis=0))
    b = pltpu.roll(b_sum, LANES - N_GATE, 1)
    u = gates - b
    pos = lax.broadcasted_iota(jnp.int32, (tm, LANES), 0) % L
    bwd = (lane % 4) >= 2
    cm = u
    step = 1
    while step < L:
        below = jnp.where(pos >= step, pltpu.roll(cm, step, 0), -jnp.inf)
        above = jnp.where(pos < L - step, pltpu.roll(cm, tm - step, 0), -jnp.inf)
        cm = jnp.maximum(cm, jnp.where(bwd, above, below))
        step *= 2
    low = lane < N_GATE
    packed = (jnp.where(low, b, 0.0) + pltpu.roll(jnp.where(low, cm, 0.0), N_GATE, 1)
              + pltpu.roll(jnp.where(low, u, 0.0), 2 * N_GATE, 1))
    gc_ref[...] = packed
    gr_ref[...] = jnp.transpose(packed)[0:3 * N_GATE]


def _proj1(x, mod, mod_row, g, w1, bg):
    n = x.shape[0]
    tm = TM_PROJ1
    assert tm % MLSTM_L == 0
    full = lambda shape: pl.BlockSpec(shape, lambda i: (0,) * len(shape))
    row = lambda w: pl.BlockSpec((tm, w), lambda i: (i, 0))
    col = lambda w: pl.BlockSpec((w, tm), lambda i: (0, i))
    hk, hv = H_C * DK_C, H_C * DV_C
    return pl.pallas_call(
        _proj1_kernel,
        grid=(n // tm,),
        in_specs=[row(D_MODEL), pl.BlockSpec((1, N_MOD, D_MODEL), lambda i: (mod_row(i), 0, 0)),
                  full((1, D_MODEL)), full(w1.shape), full((1, LANES))],
        out_specs=[col(hk), row(hk), col(hv), row(hv), row(LANES),
                   pl.BlockSpec((3 * N_GATE, tm), lambda i: (0, i))],
        out_shape=[jax.ShapeDtypeStruct((hk, n), BF16), jax.ShapeDtypeStruct((n, hk), F32),
                   jax.ShapeDtypeStruct((hv, n), BF16), jax.ShapeDtypeStruct((n, hv), F32),
                   jax.ShapeDtypeStruct((n, LANES), F32), jax.ShapeDtypeStruct((3 * N_GATE, n), F32)],
        compiler_params=_params(("arbitrary",)),
        name="proj1",
    )(x, mod, g, w1, bg)


def _chain(j, d, hh):
    return (2 * j + d) * 2 + hh


def _mlstm_kernel(*refs, has_state, emit_state):
    q_ref, k_ref, v_ref, o_ref, gc_ref, gr_ref, gn_ref = refs[:7]
    refs = refs[7:]
    if has_state:
        c0_ref, n0_ref, m0_ref = refs[:3]
        refs = refs[3:]
    hs_ref = refs[0]
    refs = refs[1:]
    if emit_state:
        cf_ref, nst_ref, mst_ref = refs[:3]
        refs = refs[3:]
    cx_s, m_s, h_s = refs

    L = MLSTM_L
    seq = k_ref.shape[0]
    nc = seq // L
    npair = H_C // 2
    lane = lax.broadcasted_iota(jnp.int32, (1, LANES), 1)
    head_mask = [lane < DK_C, lane >= DK_C]
    ri = lax.broadcasted_iota(jnp.int32, (L, L), 0)
    ci = lax.broadcasted_iota(jnp.int32, (L, L), 1)
    causal = [ri <= ci, ri >= ci]
    ones_blk = jnp.ones((DV_C, L), BF16)
    chains = [(j, d, hh) for d in range(2) for j in range(npair) for hh in range(2)]

    h_s[...] = jnp.zeros_like(h_s)
    for j, d, hh in chains:
        ch = _chain(j, d, hh)
        if has_state:
            zpad = jnp.zeros((DK_C, DV_C), F32)
            c0 = c0_ref[0, d, 2 * j + hh]
            c0 = jnp.concatenate([c0, zpad] if hh == 0 else [zpad, c0], axis=0)
            cx_s[ch, 0:DV_C, :] = jnp.transpose(c0)
            n_row = jnp.where(head_mask[hh], n0_ref[0, d, j], 0.0)
            cx_s[ch, DV_C:2 * DV_C, :] = jnp.broadcast_to(n_row, (DV_C, LANES))
            m_s[ch] = m0_ref[0, d, 2 * j + hh]
        else:
            cx_s[ch] = jnp.zeros((2 * DV_C, LANES), F32)
            m_s[ch] = jnp.zeros((1, LANES), F32)

    def chunk_step(i, carry):
        sl = [pl.ds(pl.multiple_of(i * L, L), L), pl.ds(pl.multiple_of((nc - 1 - i) * L, L), L)]
        gcol = [gc_ref[sl[d], :] for d in range(2)]
        grow = [gr_ref[:, sl[d]] for d in range(2)]
        kpair = {(d, j): k_ref[sl[d], j * LANES:(j + 1) * LANES] for d in range(2) for j in range(npair)}
        qt = {(d, j): q_ref[j * LANES:(j + 1) * LANES, sl[d]] for d in range(2) for j in range(npair)}
        st, km, u_bc, vt, row = {}, {}, {}, {}, {}
        for j, d, hh in chains:
            key = (j, d, hh)
            idx = 4 * j + 2 * d + hh
            edge = L - 1 if d == 0 else 0
            b_row = grow[d][idx:idx + 1, :]
            cm_row = grow[d][N_GATE + idx:N_GATE + idx + 1, :]
            u_col = gcol[d][:, 2 * N_GATE + idx:2 * N_GATE + idx + 1]
            m_prev = m_s[_chain(j, d, hh)][:, 0:1]
            row[key] = (b_row, cm_row, b_row[:, edge:edge + 1], cm_row[:, edge:edge + 1], m_prev)
            u_bc[key] = jnp.broadcast_to(u_col, (L, L))
            km[key] = jnp.where(head_mask[hh], kpair[(d, j)], 0.0)
            vt[key] = v_ref[(2 * j + hh) * DV_C:(2 * j + hh + 1) * DV_C, sl[d]]
            st[key] = _dot(km[key].astype(BF16), qt[(d, j)])
        for j, d, hh in chains:
            key = (j, d, hh)
            ch = _chain(j, d, hh)
            b_row, cm_row, g_tot, cm_last, m_prev = row[key]
            head = slice((2 * j + hh) * DV_C, (2 * j + hh + 1) * DV_C)
            cx = cx_s[ch]
            big_m = jnp.maximum(m_prev, cm_row)
            s = st[key] * jnp.exp(jnp.where(causal[d], u_bc[key] - big_m, -jnp.inf))
            inter = jnp.exp(m_prev - big_m)
            qcx = _dot(cx.astype(BF16), qt[(d, j)])
            num = _dot(vt[key], s.astype(BF16)) + inter * qcx[0:DV_C, :]
            den = jnp.sum(s, axis=0, keepdims=True) + inter * qcx[DV_C:DV_C + 1, :]
            hval = num * (1.0 / jnp.maximum(jnp.abs(den), jnp.exp(-(b_row + big_m))))
            h_s[head, sl[d]] = h_s[head, sl[d]] + hval

            m_top = jnp.maximum(m_prev, cm_last)
            kw = km[key] * jnp.exp(u_bc[key] - cm_last)
            vx = jnp.concatenate([vt[key], ones_blk], axis=0)
            cx_s[ch] = (jnp.exp(m_prev - m_top) * cx
                        + jnp.exp(cm_last - m_top) * _dot(vx, kw.astype(BF16)))
            m_s[ch] = jnp.broadcast_to(g_tot + m_top, (1, LANES))
        return carry

    lax.fori_loop(0, nc, chunk_step, 0)

    for hd in range(H_C):
        sl = slice(hd * DV_C, (hd + 1) * DV_C)
        hsum = jnp.transpose(h_s[sl, :])
        y = (_rms(hsum) * gn_ref[...]) * jax.nn.sigmoid(o_ref[:, sl])
        hs_ref[:, sl] = y.astype(BF16)
    if emit_state:
        for j, d, hh in chains:
            ch = _chain(j, d, hh)
            cf_ref[0, 0, d, 2 * j + hh] = jnp.transpose(cx_s[ch, 0:DV_C, :])[hh * DK_C:(hh + 1) * DK_C, :]
            mst_ref[0, d * H_C + 2 * j + hh:d * H_C + 2 * j + hh + 1, :] = m_s[ch]
        for j in range(npair):
            for d in range(2):
                nst_ref[0, d * npair + j:d * npair + j + 1, :] = (
                    cx_s[_chain(j, d, 0), DV_C:DV_C + 1, :] + cx_s[_chain(j, d, 1), DV_C:DV_C + 1, :])


def _mlstm(q, k, v, o, gc, gr, gn, state, seq, emit_state):
    n = k.shape[0]
    nb = n // seq
    npair = H_C // 2
    has_state = state is not None
    hk, hv = H_C * DK_C, H_C * DV_C
    in_specs = [
        pl.BlockSpec((hk, seq), lambda b: (0, b)),
        pl.BlockSpec((seq, hk), lambda b: (b, 0)),
        pl.BlockSpec((hv, seq), lambda b: (0, b)),
        pl.BlockSpec((seq, hv), lambda b: (b, 0)),
        pl.BlockSpec((seq, LANES), lambda b: (b, 0)),
        pl.BlockSpec((3 * N_GATE, seq), lambda b: (0, b)),
        pl.BlockSpec((1, DV_C), lambda b: (0, 0)),
    ]
    args = [q, k, v, o, gc, gr, gn]
    if has_state:
        c0, n0, m0 = state
        in_specs += [pl.BlockSpec((1, 2, H_C, DK_C, DV_C), lambda b: (b, 0, 0, 0, 0)),
                     pl.BlockSpec((1, 2, npair, 1, LANES), lambda b: (b, 0, 0, 0, 0)),
                     pl.BlockSpec((1, 2, H_C, 1, LANES), lambda b: (b, 0, 0, 0, 0))]
        args += [c0, n0, m0]
    out_specs = [pl.BlockSpec((seq, hv), lambda b: (b, 0))]
    out_shape = [jax.ShapeDtypeStruct((n, hv), BF16)]
    if emit_state:
        out_specs += [pl.BlockSpec((1, 1, 2, H_C, DK_C, DV_C), lambda b: (b, 0, 0, 0, 0, 0)),
                      pl.BlockSpec((1, 2 * npair, LANES), lambda b: (b, 0, 0)),
                      pl.BlockSpec((1, 2 * H_C, LANES), lambda b: (b, 0, 0))]
        out_shape += [jax.ShapeDtypeStruct((nb, 1, 2, H_C, DK_C, DV_C), F32),
                      jax.ShapeDtypeStruct((nb, 2 * npair, LANES), F32),
                      jax.ShapeDtypeStruct((nb, 2 * H_C, LANES), F32)]
    n_chain = 2 * H_C
    return pl.pallas_call(
        functools.partial(_mlstm_kernel, has_state=has_state, emit_state=emit_state),
        grid=(nb,),
        in_specs=in_specs, out_specs=out_specs, out_shape=out_shape,
        scratch_shapes=[pltpu.VMEM((n_chain, 2 * DV_C, LANES), F32), pltpu.VMEM((n_chain, 1, LANES), F32),
                        pltpu.VMEM((hv, seq), F32)],
        compiler_params=_params(("arbitrary",)),
        name="mlstm_lat" if has_state else "mlstm_ctx",
    )(*args)


def _rope_tables(n_tok):
    t = np.arange(n_tok)
    rows = (t // GRID_W).astype(np.float64)
    cols = (t % GRID_W).astype(np.float64)

    def axis_tabs(width, lane0):
        half = width // 2
        quarter = half // 2
        freqs = np.power(ROPE_BASE, -np.arange(quarter, dtype=np.float64) / quarter)
        c = np.ones((n_tok, LANES))
        sa = np.zeros((n_tok, LANES))
        sb = np.zeros((n_tok, LANES))
        for g, pos in enumerate((rows, cols)):
            ang = pos[:, None] * freqs[None, :]
            a0 = lane0 + g * half
            c[:, a0:a0 + quarter] = np.cos(ang)
            c[:, a0 + quarter:a0 + half] = np.cos(ang)
            sa[:, a0:a0 + quarter] = -np.sin(ang)
            sb[:, a0 + quarter:a0 + half] = np.sin(ang)
        return c, sa, sb

    cm, sam, sbm = axis_tabs(QK_ROPE, ROPE_LANE0)
    c0, sa0, sb0 = axis_tabs(DIFF_HD, 0)
    c1, sa1, sb1 = axis_tabs(DIFF_HD, DIFF_HD)
    cd = np.where(np.arange(LANES)[None, :] < DIFF_HD, c0, c1)
    return tuple(jnp.asarray(a, F32) for a in (cm, sam, sbm, cd, sa0 + sa1, sb0 + sb1))


def _prep_even(w_in_ab, w_uq, w_ukv):
    z = lambda n: jnp.zeros((D_MODEL, n), F32)
    c2 = Q_LORA + KV_LORA
    w0 = jnp.concatenate([w_in_ab[:, :c2], z(ROPE_LANE0), w_in_ab[:, c2:c2 + QK_ROPE],
                          z(LANES - ROPE_LANE0 - QK_ROPE), w_in_ab[:, c2 + QK_ROPE:]], axis=1).astype(BF16)
    wq = jnp.pad(w_uq.reshape(Q_LORA, H_A, QK_NOPE + QK_ROPE),
                 ((0, 0), (0, 0), (0, HEAD_PAD - QK_NOPE - QK_ROPE))).reshape(Q_LORA, H_A * HEAD_PAD)
    kvw = w_ukv.reshape(KV_LORA, H_A, QK_NOPE + V_HD_A)
    kpad = jnp.pad(kvw[..., :QK_NOPE], ((0, 0), (0, 0), (0, HEAD_PAD - QK_NOPE)))
    vw = kvw[..., QK_NOPE:]
    zv = jnp.zeros_like(vw)
    odd = (jnp.arange(H_A) % 2 == 1)[None, :, None]
    vpad = jnp.where(odd, jnp.concatenate([zv, vw], -1), jnp.concatenate([vw, zv], -1))
    wukv = jnp.concatenate([kpad.reshape(KV_LORA, -1), vpad.reshape(KV_LORA, -1)], axis=1)
    return w0, wq.astype(BF16), wukv.astype(BF16)


def _gate_order(g):
    lead = g.shape[:-1]
    g = g.reshape(lead + (2, 2, H_C // 2, 2))
    perm = tuple(range(len(lead))) + tuple(len(lead) + a for a in (1, 2, 0, 3))
    return g.transpose(perm).reshape(lead + (4 * H_C,))


def _prep_odd(w_in_c, b_gate_c):
    ng = 4 * H_C
    base = w_in_c.shape[1] - ng
    w1 = jnp.concatenate([w_in_c[:, :base], _gate_order(w_in_c[:, base:]),
                          jnp.zeros((D_MODEL, LANES - ng), F32)], axis=1).astype(BF16)
    bg = jnp.pad(_gate_order(b_gate_c), (0, LANES - ng)).reshape(1, LANES)
    return w1, bg


def kernel(x_prompt, x_sample, cache_mla_ckv, cache_mla_krope, cache_diff_k, cache_diff_v,
           state_mlstm_C, state_mlstm_n, state_mlstm_m, c, c_ctx,
           w_ada, b_ada, g_mix, g_ffn, w_ffn_in, w_ffn_out,
           w_in_ab, g_q_lora, g_kv_lora, w_uq, w_ukv, diff_lambda, g_diff_subln, w_out_ab,
           w_in_c, b_gate_c, g_mlstm, w_out_c, g_final):
    nbp, seq_p, _ = x_prompt.shape
    nbs, seq_s, _ = x_sample.shape
    past = cache_mla_ckv.shape[2]
    assert DEPTH == 2 and 1 + nbs <= COND_ROWS
    assert cache_mla_ckv.shape[1] == 1 and state_mlstm_C.shape[1] == 1

    cond = jnp.concatenate([c_ctx[None], c, jnp.zeros((COND_ROWS - 1 - nbs, D_MODEL), F32)], axis=0)
    mod = _ada(cond, w_ada, b_ada).reshape(DEPTH * COND_ROWS, N_MOD, D_MODEL)

    xp = x_prompt.reshape(nbp * seq_p, D_MODEL)
    xs = x_sample.reshape(nbs * seq_s, D_MODEL)
    row2 = lambda v: v.reshape(1, -1)

    def mod_rows(layer, tm):
        prompt = lambda i: layer * COND_ROWS
        sample = lambda i: layer * COND_ROWS + 1 + i // (seq_s // tm)
        return prompt, sample

    lam_init = 0.8 - 0.6 * math.exp(-0.3 * 0)
    w0, wq, wukv = _prep_even(w_in_ab[0], w_uq[0], w_ukv[0])
    rope_tabs = _rope_tables(seq_s)
    mrp, mrs = mod_rows(0, TM_PROJ)
    gq, gkv = row2(g_q_lora[0]), row2(g_kv_lora[0])
    outs_p = _proj0(xp, mod, mrp, row2(g_mix[0]), w0, gq, gkv, wq, wukv, None, seq_p)
    outs_s = _proj0(xs, mod, mrs, row2(g_mix[0]), w0, gq, gkv, wq, wukv, rope_tabs, seq_s)
    qm_p, km_p, vm_p, dq_p, dk_p, dv_p, ckv_new, kr_new, dk_new, dv_new = outs_p
    qm_s, km_s, vm_s, dq_s, dk_s, dv_s = outs_s
    gsub = row2(g_diff_subln[0])
    krb_c = jnp.pad(cache_mla_krope.reshape(nbs, past, QK_ROPE),
                    ((0, 0), (0, 0), (ROPE_LANE0, LANES - ROPE_LANE0 - QK_ROPE)))
    ctx = (cache_mla_ckv.reshape(nbs, past, KV_LORA), krb_c,
           cache_diff_k.reshape(nbs, H_B, past, 2 * DIFF_HD), cache_diff_v.reshape(nbs, H_B, past, DIFF_VD))
    a_p = _attn0(qm_p, km_p, vm_p, dq_p, dk_p, dv_p, diff_lambda[0], gsub, None, None, seq_p, lam_init)
    a_s = _attn0(qm_s, km_s, vm_s, dq_s, dk_s, dv_s, diff_lambda[0], gsub, ctx, wukv, seq_s, lam_init)
    mtp, mts = mod_rows(0, TM_TAIL)
    wo0 = w_out_ab[0].astype(BF16)
    win0, wout0 = w_ffn_in[0].astype(BF16), w_ffn_out[0].astype(BF16)
    xp = _tail(xp, a_p, mod, mtp, wo0, row2(g_ffn[0]), win0, wout0, None, "tail0_ctx")
    xs = _tail(xs, a_s, mod, mts, wo0, row2(g_ffn[0]), win0, wout0, None, "tail0_lat")

    w1, bg = _prep_odd(w_in_c[0], b_gate_c[0])
    mrp, mrs = mod_rows(1, TM_PROJ1)
    gn = row2(g_mlstm[0])
    q_p, k_p, v_p, o_p, gc_p, gr_p = _proj1(xp, mod, mrp, row2(g_mix[1]), w1, bg)
    q_s, k_s, v_s, o_s, gc_s, gr_s = _proj1(xs, mod, mrs, row2(g_mix[1]), w1, bg)
    hs_p, c_new, nst, mst = _mlstm(q_p, k_p, v_p, o_p, gc_p, gr_p, gn, None, seq_p, True)
    state = (state_mlstm_C.reshape(nbs, 2, H_C, DK_C, DV_C),
             state_mlstm_n.reshape(nbs, 2, H_C // 2, 1, LANES),
             jnp.broadcast_to(state_mlstm_m.reshape(nbs, 2, H_C, 1, 1), (nbs, 2, H_C, 1, LANES)))
    (hs_s,) = _mlstm(q_s, k_s, v_s, o_s, gc_s, gr_s, gn, state, seq_s, False)
    mtp, mts = mod_rows(1, TM_TAIL)
    wo1 = w_out_c[0].astype(BF16)
    win1, wout1 = w_ffn_in[1].astype(BF16), w_ffn_out[1].astype(BF16)
    gfin = row2(g_final)
    yp = _tail(xp, hs_p, mod, mtp, wo1, row2(g_ffn[1]), win1, wout1, gfin, "tail1_ctx")
    ys = _tail(xs, hs_s, mod, mts, wo1, row2(g_ffn[1]), win1, wout1, gfin, "tail1_lat")

    return (yp.reshape(nbp, seq_p, D_MODEL), ys.reshape(nbs, seq_s, D_MODEL),
            ckv_new.reshape(nbp, 1, seq_p, KV_LORA), kr_new.reshape(nbp, 1, seq_p, QK_ROPE),
            dk_new.reshape(nbp, 1, H_B, seq_p, 2 * DIFF_HD), dv_new.reshape(nbp, 1, H_B, seq_p, DIFF_VD),
            c_new, nst.reshape(nbp, 1, 2, H_C, DK_C), mst[:, :, 0].reshape(nbp, 1, 2, H_C))
```

```python
import functools
import math

import jax
import jax.numpy as jnp
import numpy as np
from jax import lax
from jax.experimental import pallas as pl
from jax.experimental.pallas import tpu as pltpu

F32 = jnp.float32
BF16 = jnp.bfloat16

D_MODEL = 1024
DEPTH = 2
GRID_W = 64
ROPE_BASE = 10000.0
RMS_EPS = 1e-6
H_A = 8
QK_NOPE = 64
QK_ROPE = 32
V_HD_A = 64
Q_LORA = 256
KV_LORA = 128
H_B = 4
DIFF_HD = 64
DIFF_VD = 2 * DIFF_HD
H_C = 8
DK_C = 64
DV_C = D_MODEL // H_C
D_FF = -(-8 * D_MODEL // (3 * 256)) * 256

LANES = 128
HEAD_PAD = LANES
ROPE_LANE0 = QK_NOPE
N_MOD = 6
COND_ROWS = 16
N_GATE = 2 * H_C

TM_PROJ = 512
TM_PROJ1 = 512
TQ_ATTN = 512
TM_TAIL = 512
FF_CHUNK = 1408
MLSTM_L = 128
ADA_TN = 1024
VMEM_LIMIT = 56 * 1024 * 1024

assert D_FF % FF_CHUNK == 0 and FF_CHUNK % LANES == 0


def _dot(a, b):
    return jnp.dot(a, b, preferred_element_type=F32)


def _dot_nt(a, b):
    return lax.dot_general(a, b, (((1,), (1,)), ((), ())), preferred_element_type=F32)


def _dot_tn(a, b):
    return lax.dot_general(a, b, (((0,), (0,)), ((), ())), preferred_element_type=F32)


def _rms(x):
    return x * lax.rsqrt(jnp.mean(x * x, axis=-1, keepdims=True) + RMS_EPS)


def _modulate(x, g, shift, scale):
    return (_rms(x) * g) * (1.0 + scale) + shift


def _params(semantics):
    return pltpu.CompilerParams(dimension_semantics=semantics, vmem_limit_bytes=VMEM_LIMIT)


def _ada_kernel(cond_ref, w_ref, b_ref, o_ref):
    c = cond_ref[...]
    s = (c * jax.nn.sigmoid(c)).astype(BF16)
    o_ref[0] = _dot(s, w_ref[0].astype(BF16)) + b_ref[0]


def _ada(cond, w_ada, b_ada):
    n_out = w_ada.shape[-1]
    return pl.pallas_call(
        _ada_kernel,
        grid=(DEPTH, n_out // ADA_TN),
        in_specs=[
            pl.BlockSpec((COND_ROWS, D_MODEL), lambda l, n: (0, 0)),
            pl.BlockSpec((1, D_MODEL, ADA_TN), lambda l, n: (l, 0, n)),
            pl.BlockSpec((1, 1, ADA_TN), lambda l, n: (l, 0, n)),
        ],
        out_specs=pl.BlockSpec((1, COND_ROWS, ADA_TN), lambda l, n: (l, 0, n)),
        out_shape=jax.ShapeDtypeStruct((DEPTH, COND_ROWS, n_out), F32),
        compiler_params=_params(("arbitrary", "arbitrary")),
        name="ada",
    )(cond, w_ada, b_ada.reshape(DEPTH, 1, n_out))


def _rope(x, c, sa, sb, off):
    return x * c + pltpu.roll(x, LANES - off, 1) * sa + pltpu.roll(x, off, 1) * sb


def _proj0_kernel(*refs, rope, emit_cache):
    x_ref, mod_ref, g_ref, wa_ref, wd_ref, gq_ref, gkv_ref, wq_ref, wukv_ref = refs[:9]
    refs = refs[9:]
    if rope:
        cm_ref, sam_ref, sbm_ref, cd_ref, sad_ref, sbd_ref = refs[:6]
        refs = refs[6:]
    qm_ref, km_ref, vm_ref, dq_ref, dk_ref, dv_ref = refs[:6]
    refs = refs[6:]
    if emit_cache:
        ckvf_ref, krf_ref, dkf_ref, dvf_ref = refs

    h = _modulate(x_ref[...], g_ref[...], mod_ref[0, 0:1, :], mod_ref[0, 1:2, :]).astype(BF16)
    pa = _dot(h, wa_ref[...])
    pd = _dot(h, wd_ref[...])
    lane = lax.broadcasted_iota(jnp.int32, (pa.shape[0], LANES), 1)
    krb = jnp.where((lane >= ROPE_LANE0) & (lane < ROPE_LANE0 + QK_ROPE),
                    pltpu.roll(pa[:, Q_LORA + KV_LORA:Q_LORA + KV_LORA + LANES], ROPE_LANE0, 1), 0.0)
    cq = pa[:, 0:Q_LORA]
    ckv = _rms(pa[:, Q_LORA:Q_LORA + KV_LORA]) * gkv_ref[...]
    qa = _dot((_rms(cq) * gq_ref[...]).astype(BF16), wq_ref[...])
    kv = _dot(ckv.astype(BF16), wukv_ref[...])
    if emit_cache:
        ckvf_ref[...] = ckv
        krf_ref[...] = krb[:, ROPE_LANE0:ROPE_LANE0 + QK_ROPE]
    if rope:
        cm, sam, sbm = cm_ref[...], sam_ref[...], sbm_ref[...]
        cd, sad, sbd = cd_ref[...], sad_ref[...], sbd_ref[...]
        krb = _rope(krb, cm, sam, sbm, QK_ROPE // 4)
    for hd in range(H_A):
        sl = slice(hd * HEAD_PAD, (hd + 1) * HEAD_PAD)
        qh = qa[:, sl]
        if rope:
            qh = _rope(qh, cm, sam, sbm, QK_ROPE // 4)
        qm_ref[:, sl] = qh.astype(BF16)
        km_ref[:, sl] = (kv[:, sl] + krb).astype(BF16)
    vm_ref[...] = kv[:, H_A * HEAD_PAD:].astype(BF16)
    for hd in range(H_B):
        sl = slice(hd * LANES, (hd + 1) * LANES)
        dq = pd[:, hd * LANES:(hd + 1) * LANES]
        dk = pd[:, (H_B + hd) * LANES:(H_B + hd + 1) * LANES]
        dv = pd[:, (2 * H_B + hd) * LANES:(2 * H_B + hd + 1) * LANES]
        if emit_cache:
            seq = dkf_ref.shape[2]
            for bi in range(dkf_ref.shape[0]):
                dkf_ref[bi, hd] = dk[bi * seq:(bi + 1) * seq]
                dvf_ref[bi, hd] = dv[bi * seq:(bi + 1) * seq]
        if rope:
            dq = _rope(dq, cd, sad, sbd, DIFF_HD // 4)
            dk = _rope(dk, cd, sad, sbd, DIFF_HD // 4)
        dq_ref[:, sl] = (dq * (DIFF_HD ** -0.5)).astype(BF16)
        dk_ref[:, sl] = dk.astype(BF16)
        dv_ref[:, sl] = dv.astype(BF16)


def _proj0(x, mod, mod_row, g, w_ab, wd, gq, gkv, wq, wukv, rope_tabs, seq):
    n = x.shape[0]
    tm = TM_PROJ
    rope = rope_tabs is not None
    emit_cache = not rope
    tiles_per_seq = max(seq // tm, 1)
    full = lambda shape: pl.BlockSpec(shape, lambda i: (0,) * len(shape))
    in_specs = [
        pl.BlockSpec((tm, D_MODEL), lambda i: (i, 0)),
        pl.BlockSpec((1, N_MOD, D_MODEL), lambda i: (mod_row(i), 0, 0)),
        full((1, D_MODEL)), full((D_MODEL, Q_LORA + KV_LORA + LANES)), full(wd.shape),
        full((1, Q_LORA)), full((1, KV_LORA)), full(wq.shape), full(wukv.shape),
    ]
    args = [x, mod, g, w_ab, wd, gq, gkv, wq, wukv]
    if rope:
        in_specs += [pl.BlockSpec((tm, LANES), lambda i: (i % tiles_per_seq, 0))] * 6
        args += list(rope_tabs)
    row = lambda w: pl.BlockSpec((tm, w), lambda i: (i, 0))
    out_specs = [row(1024), row(1024), row(1024), row(512), row(512), row(512)]
    out_shape = [jax.ShapeDtypeStruct((n, w), BF16) for w in (1024, 1024, 1024, 512, 512, 512)]
    if emit_cache:
        assert tm % seq == 0
        cache = pl.BlockSpec((tm // seq, H_B, seq, LANES), lambda i: (i, 0, 0, 0))
        out_specs += [row(KV_LORA), row(QK_ROPE), cache, cache]
        out_shape += [jax.ShapeDtypeStruct((n, KV_LORA), F32), jax.ShapeDtypeStruct((n, QK_ROPE), F32),
                      jax.ShapeDtypeStruct((n // seq, H_B, seq, LANES), F32),
                      jax.ShapeDtypeStruct((n // seq, H_B, seq, LANES), F32)]
    return pl.pallas_call(
        functools.partial(_proj0_kernel, rope=rope, emit_cache=emit_cache),
        grid=(n // tm,),
        in_specs=in_specs, out_specs=out_specs, out_shape=out_shape,
        compiler_params=_params(("arbitrary",)),
        name="proj0_rope" if rope else "proj0_ctx",
    )(*args)


def _attn_kernel(*refs, has_ctx, lam_init):
    q_ref, dq_ref, k_ref, v_ref, dk_ref, dv_ref, lamv_ref, gsub_ref = refs[:8]
    refs = refs[8:]
    if has_ctx:
        ckvc_ref, krc_ref, dkc_ref, dvc_ref, wukv_ref = refs[:5]
        refs = refs[5:]
    o_ref, kx_s, vx_s, dkx_s, dvx_s = refs
    seq = k_ref.shape[0]
    ktot = kx_s.shape[0]

    @pl.when(pl.program_id(1) == 0)
    def _():
        kx_s[0:seq, :] = k_ref[...]
        dkx_s[0:seq, :] = dk_ref[...]
        ones_own = jnp.ones((seq, LANES), BF16)
        for hd in range(H_A):
            vx_s[0:seq, 2 * hd * LANES:(2 * hd + 1) * LANES] = v_ref[:, hd * LANES:(hd + 1) * LANES]
            vx_s[0:seq, (2 * hd + 1) * LANES:(2 * hd + 2) * LANES] = ones_own
        for hd in range(H_B):
            dvx_s[0:seq, 2 * hd * LANES:(2 * hd + 1) * LANES] = dv_ref[:, hd * LANES:(hd + 1) * LANES]
            dvx_s[0:seq, (2 * hd + 1) * LANES:(2 * hd + 2) * LANES] = ones_own
        if has_ctx:
            kv = _dot(ckvc_ref[0].astype(BF16), wukv_ref[...])
            krb = krc_ref[0]
            ones_ctx = jnp.ones((ktot - seq, LANES), BF16)
            for hd in range(H_A):
                sl = slice(hd * HEAD_PAD, (hd + 1) * HEAD_PAD)
                kx_s[seq:ktot, sl] = (kv[:, sl] + krb).astype(BF16)
                vx_s[seq:ktot, 2 * hd * LANES:(2 * hd + 1) * LANES] = (
                    kv[:, (H_A + hd) * HEAD_PAD:(H_A + hd + 1) * HEAD_PAD].astype(BF16))
                vx_s[seq:ktot, (2 * hd + 1) * LANES:(2 * hd + 2) * LANES] = ones_ctx
            for hd in range(H_B):
                sl = slice(hd * LANES, (hd + 1) * LANES)
                dkx_s[seq:ktot, sl] = dkc_ref[0, hd].astype(BF16)
                dvx_s[seq:ktot, 2 * hd * LANES:(2 * hd + 1) * LANES] = dvc_ref[0, hd].astype(BF16)
                dvx_s[seq:ktot, (2 * hd + 1) * LANES:(2 * hd + 2) * LANES] = ones_ctx

    log2e = 1.0 / math.log(2.0)
    tq = dq_ref.shape[0]
    lo = lax.broadcasted_iota(jnp.int32, (tq, LANES), 1) < DIFF_HD

    jobs = []
    for hd in range(H_A):
        sl = slice(hd * HEAD_PAD, (hd + 1) * HEAD_PAD)
        jobs.append((lambda sl=sl: q_ref[:, sl], (kx_s, sl), (vx_s, hd), (QK_NOPE + QK_ROPE) ** -0.5 * log2e))
    for hd in range(H_B):
        sl = slice(hd * LANES, (hd + 1) * LANES)
        for part in range(2):
            def qfn(sl=sl, part=part):
                dq = dq_ref[:, sl].astype(F32)
                return (jnp.where(lo, dq, 0.0) if part == 0 else jnp.where(lo, 0.0, dq)).astype(BF16)
            jobs.append((qfn, (dkx_s, sl), (dvx_s, hd), log2e))

    def scores(job):
        qfn, (kref, sl), _, _ = job
        return _dot_nt(qfn(), kref[:, sl])

    def finish(s, job):
        _, _, (vref, hd), c = job
        m = jnp.max(s, axis=-1, keepdims=True)
        e = jnp.exp2((s - m) * c).astype(BF16)
        res = _dot(e, vref[:, 2 * hd * LANES:(2 * hd + 2) * LANES])
        return res[:, 0:LANES] * (1.0 / res[:, LANES:2 * LANES])

    outs = []
    s_next = scores(jobs[0])
    for i, job in enumerate(jobs):
        s_cur = s_next
        if i + 1 < len(jobs):
            s_next = scores(jobs[i + 1])
        outs.append(finish(s_cur, job))

    for j in range(H_A // 2):
        o_ref[:, j * LANES:(j + 1) * LANES] = (outs[2 * j] + outs[2 * j + 1]).astype(BF16)
    lv = lamv_ref[...]
    lam = (jnp.exp(jnp.sum(lv[0:1] * lv[1:2], axis=-1, keepdims=True))
           - jnp.exp(jnp.sum(lv[2:3] * lv[3:4], axis=-1, keepdims=True)) + lam_init)
    for hd in range(H_B):
        acc = outs[H_A + 2 * hd] - lam * outs[H_A + 2 * hd + 1]
        ob = (_rms(acc) * gsub_ref[...]) * (1.0 - lam_init)
        o_ref[:, H_A * V_HD_A + hd * LANES:H_A * V_HD_A + (hd + 1) * LANES] = ob.astype(BF16)


def _attn0(qm, km, vm, dq, dk, dv, lamv, gsub, ctx, wukv, seq, lam_init):
    n = qm.shape[0]
    nb = n // seq
    tq = min(TQ_ATTN, seq)
    has_ctx = ctx is not None
    full = lambda shape: pl.BlockSpec(shape, lambda b, t: (0,) * len(shape))
    qrow = lambda w: pl.BlockSpec((tq, w), lambda b, t: (b * (seq // tq) + t, 0))
    krow = lambda w: pl.BlockSpec((seq, w), lambda b, t: (b, 0))
    in_specs = [qrow(1024), qrow(512), krow(1024), krow(1024), krow(512), krow(512),
                full((4, DIFF_HD)), full((1, DIFF_VD))]
    args = [qm, dq, km, vm, dk, dv, lamv, gsub]
    past = 0
    if has_ctx:
        ckv_c, krb_c, dk_c, dv_c = ctx
        past = ckv_c.shape[1]
        in_specs += [pl.BlockSpec((1, past, LANES), lambda b, t: (b, 0, 0)),
                     pl.BlockSpec((1, past, LANES), lambda b, t: (b, 0, 0)),
                     pl.BlockSpec((1, H_B, past, LANES), lambda b, t: (b, 0, 0, 0)),
                     pl.BlockSpec((1, H_B, past, LANES), lambda b, t: (b, 0, 0, 0)),
                     full(wukv.shape)]
        args += [ckv_c, krb_c, dk_c, dv_c, wukv]
    ktot = seq + past
    scratch = [pltpu.VMEM((ktot, H_A * HEAD_PAD), BF16), pltpu.VMEM((ktot, 2 * H_A * LANES), BF16),
               pltpu.VMEM((ktot, H_B * LANES), BF16), pltpu.VMEM((ktot, 2 * H_B * LANES), BF16)]
    return pl.pallas_call(
        functools.partial(_attn_kernel, has_ctx=has_ctx, lam_init=lam_init),
        grid=(nb, seq // tq),
        in_specs=in_specs,
        out_specs=pl.BlockSpec((tq, D_MODEL), lambda b, t: (b * (seq // tq) + t, 0)),
        out_shape=jax.ShapeDtypeStruct((n, D_MODEL), BF16),
        scratch_shapes=scratch,
        compiler_params=_params(("arbitrary", "arbitrary")),
        name="attn0_lat" if has_ctx else "attn0_ctx",
    )(*args)


def _tail_kernel(*refs, final):
    x_ref, a_ref, mod_ref, wo_ref, gf_ref, win_ref, wout_ref = refs[:7]
    if final:
        gfin_ref, o_ref = refs[7:]
    else:
        (o_ref,) = refs[7:]
    x1 = x_ref[...] + mod_ref[0, 2:3, :] * _dot(a_ref[...], wo_ref[...])
    h = _modulate(x1, gf_ref[...], mod_ref[0, 3:4, :], mod_ref[0, 4:5, :]).astype(BF16)
    acc = None
    for c in range(D_FF // FF_CHUNK):
        a = _dot(h, win_ref[0, :, c * FF_CHUNK:(c + 1) * FF_CHUNK])
        b = _dot(h, win_ref[0, :, D_FF + c * FF_CHUNK:D_FF + (c + 1) * FF_CHUNK])
        act = ((a * jax.nn.sigmoid(a)) * b).astype(BF16)
        part = _dot(act, wout_ref[0, c * FF_CHUNK:(c + 1) * FF_CHUNK, :])
        acc = part if acc is None else acc + part
    x2 = x1 + mod_ref[0, 5:6, :] * acc
    if final:
        x2 = _rms(x2) * gfin_ref[...]
    o_ref[...] = x2


def _tail(x, a, mod, mod_row, wo, gf, win, wout, layer, gfin, name):
    n = x.shape[0]
    tm = TM_TAIL
    final = gfin is not None
    full = lambda shape: pl.BlockSpec(shape, lambda i: (0,) * len(shape))
    resident = lambda shape: pl.BlockSpec((1,) + shape[1:], lambda i: (layer, 0, 0),
                                          pipeline_mode=pl.Buffered(1))
    in_specs = [
        pl.BlockSpec((tm, D_MODEL), lambda i: (i, 0)),
        pl.BlockSpec((tm, D_MODEL), lambda i: (i, 0)),
        pl.BlockSpec((1, N_MOD, D_MODEL), lambda i: (mod_row(i), 0, 0)),
        full(wo.shape), full((1, D_MODEL)), resident(win.shape), resident(wout.shape),
    ]
    args = [x, a, mod, wo, gf, win, wout]
    if final:
        in_specs.append(full((1, D_MODEL)))
        args.append(gfin)
    return pl.pallas_call(
        functools.partial(_tail_kernel, final=final),
        grid=(n // tm,),
        in_specs=in_specs,
        out_specs=pl.BlockSpec((tm, D_MODEL), lambda i: (i, 0)),
        out_shape=jax.ShapeDtypeStruct((n, D_MODEL), F32),
        compiler_params=_params(("arbitrary",)),
        name=name,
    )(*args)


def _split3(x):
    hi = x.astype(BF16)
    r1 = x - hi.astype(F32)
    mid = r1.astype(BF16)
    lo = (r1 - mid.astype(F32)).astype(BF16)
    return hi, mid, lo


def _proj1_kernel(x_ref, mod_ref, g_ref, w1_ref, wg_ref, bg_ref, q_ref, k_ref, v_ref, o_ref, gc_ref, gr_ref):
    tm = x_ref.shape[0]
    L = MLSTM_L
    hk = H_C * DK_C
    hv = H_C * DV_C
    h = _modulate(x_ref[...], g_ref[...], mod_ref[0, 0:1, :], mod_ref[0, 1:2, :]).astype(BF16)
    gates = _dot(h, wg_ref[...]) + bg_ref[...]
    proj = _dot(h, w1_ref[...])
    for blk in range(hk // LANES):
        sl = slice(blk * LANES, (blk + 1) * LANES)
        q_ref[sl, :] = jnp.transpose(proj[:, sl]).astype(BF16)
    for blk in range(hv // LANES):
        sl = slice(blk * LANES, (blk + 1) * LANES)
        v_ref[sl, :] = jnp.transpose(proj[:, 2 * hk + blk * LANES:2 * hk + (blk + 1) * LANES]).astype(BF16)
    k_ref[...] = proj[:, hk:2 * hk] * (DK_C ** -0.5)
    o_ref[...] = proj[:, 2 * hk + hv:2 * hk + 2 * hv]
    lf = jnp.minimum(gates, 0.0) - jnp.log1p(jnp.exp(-jnp.abs(gates)))
    r = lax.broadcasted_iota(jnp.int32, (L, L), 0)
    c = lax.broadcasted_iota(jnp.int32, (L, L), 1)
    pre = jnp.where(c <= r, 1.0, 0.0).astype(BF16)
    suf = jnp.where(c >= r, 1.0, 0.0).astype(BF16)
    parts = jnp.concatenate(_split3(lf), axis=1)
    fold = lambda t: t[:, 0:LANES] + t[:, LANES:2 * LANES] + t[:, 2 * LANES:3 * LANES]
    chunks = [parts[ck * L:(ck + 1) * L] for ck in range(tm // L)]
    lane = lax.broadcasted_iota(jnp.int32, (tm, LANES), 1)
    b_sum = jnp.where((lane % 4) >= 2,
                      jnp.concatenate([fold(_dot(suf, p)) for p in chunks], axis=0),
                      jnp.concatenate([fold(_dot(pre, p)) for p in chunks], axis=0))
    b = pltpu.roll(b_sum, LANES - N_GATE, 1)
    u = gates - b
    pos = lax.broadcasted_iota(jnp.int32, (tm, LANES), 0) % L
    bwd = (lane % 4) >= 2
    cm = u
    step = 1
    while step < L:
        below = jnp.where(pos >= step, pltpu.roll(cm, step, 0), -jnp.inf)
        above = jnp.where(pos < L - step, pltpu.roll(cm, tm - step, 0), -jnp.inf)
        cm = jnp.maximum(cm, jnp.where(bwd, above, below))
        step *= 2
    low = lane < N_GATE
    packed = (jnp.where(low, b, 0.0) + pltpu.roll(jnp.where(low, cm, 0.0), N_GATE, 1)
              + pltpu.roll(jnp.where(low, u, 0.0), 2 * N_GATE, 1))
    gc_ref[...] = packed
    gr_ref[...] = jnp.transpose(packed)[0:3 * N_GATE]


def _proj1(x, mod, mod_row, g, w1, wg, bg):
    n = x.shape[0]
    tm = TM_PROJ1
    assert tm % MLSTM_L == 0
    full = lambda shape: pl.BlockSpec(shape, lambda i: (0,) * len(shape))
    row = lambda w: pl.BlockSpec((tm, w), lambda i: (i, 0))
    col = lambda w: pl.BlockSpec((w, tm), lambda i: (0, i))
    hk, hv = H_C * DK_C, H_C * DV_C
    return pl.pallas_call(
        _proj1_kernel,
        grid=(n // tm,),
        in_specs=[row(D_MODEL), pl.BlockSpec((1, N_MOD, D_MODEL), lambda i: (mod_row(i), 0, 0)),
                  full((1, D_MODEL)), full((D_MODEL, 2 * hk + 2 * hv)), full(wg.shape), full((1, LANES))],
        out_specs=[col(hk), row(hk), col(hv), row(hv), row(LANES),
                   pl.BlockSpec((3 * N_GATE, tm), lambda i: (0, i))],
        out_shape=[jax.ShapeDtypeStruct((hk, n), BF16), jax.ShapeDtypeStruct((n, hk), F32),
                   jax.ShapeDtypeStruct((hv, n), BF16), jax.ShapeDtypeStruct((n, hv), F32),
                   jax.ShapeDtypeStruct((n, LANES), F32), jax.ShapeDtypeStruct((3 * N_GATE, n), F32)],
        compiler_params=_params(("arbitrary",)),
        name="proj1",
    )(x, mod, g, w1, wg, bg)


def _chain(j, d, hh):
    return (2 * j + d) * 2 + hh


def _mlstm_kernel(*refs, has_state, emit_state):
    q_ref, k_ref, v_ref, o_ref, gc_ref, gr_ref, gn_ref = refs[:7]
    refs = refs[7:]
    if has_state:
        c0_ref, n0_ref, m0_ref = refs[:3]
        refs = refs[3:]
    hs_ref = refs[0]
    refs = refs[1:]
    if emit_state:
        cf_ref, nst_ref, mst_ref = refs[:3]
        refs = refs[3:]
    cx_s, m_s, h_s = refs

    L = MLSTM_L
    seq = k_ref.shape[0]
    nc = seq // L
    npair = H_C // 2
    lane = lax.broadcasted_iota(jnp.int32, (1, LANES), 1)
    head_mask = [lane < DK_C, lane >= DK_C]
    ri = lax.broadcasted_iota(jnp.int32, (L, L), 0)
    ci = lax.broadcasted_iota(jnp.int32, (L, L), 1)
    causal = [ri <= ci, ri >= ci]
    ones_blk = jnp.ones((DV_C, L), BF16)
    chains = [(j, d, hh) for d in range(2) for j in range(npair) for hh in range(2)]

    h_s[...] = jnp.zeros_like(h_s)
    for j, d, hh in chains:
        ch = _chain(j, d, hh)
        if has_state:
            zpad = jnp.zeros((DK_C, DV_C), F32)
            c0 = c0_ref[0, d, 2 * j + hh]
            c0 = jnp.concatenate([c0, zpad] if hh == 0 else [zpad, c0], axis=0)
            cx_s[ch, 0:DV_C, :] = jnp.transpose(c0)
            n_row = jnp.where(head_mask[hh], n0_ref[0, d, j], 0.0)
            cx_s[ch, DV_C:2 * DV_C, :] = jnp.broadcast_to(n_row, (DV_C, LANES))
            m_s[ch] = m0_ref[0, d, 2 * j + hh]
        else:
            cx_s[ch] = jnp.zeros((2 * DV_C, LANES), F32)
            m_s[ch] = jnp.zeros((1, LANES), F32)

    def chunk_step(i, carry):
        sl = [pl.ds(pl.multiple_of(i * L, L), L), pl.ds(pl.multiple_of((nc - 1 - i) * L, L), L)]
        gcol = [gc_ref[sl[d], :] for d in range(2)]
        grow = [gr_ref[:, sl[d]] for d in range(2)]
        kpair = {(d, j): k_ref[sl[d], j * LANES:(j + 1) * LANES] for d in range(2) for j in range(npair)}
        qt = {(d, j): q_ref[j * LANES:(j + 1) * LANES, sl[d]] for d in range(2) for j in range(npair)}
        st, km, u_bc, vt, row = {}, {}, {}, {}, {}
        for j, d, hh in chains:
            key = (j, d, hh)
            idx = 4 * j + 2 * d + hh
            edge = L - 1 if d == 0 else 0
            b_row = grow[d][idx:idx + 1, :]
            cm_row = grow[d][N_GATE + idx:N_GATE + idx + 1, :]
            u_col = gcol[d][:, 2 * N_GATE + idx:2 * N_GATE + idx + 1]
            m_prev = m_s[_chain(j, d, hh)][:, 0:1]
            row[key] = (b_row, cm_row, b_row[:, edge:edge + 1], cm_row[:, edge:edge + 1], m_prev)
            u_bc[key] = jnp.broadcast_to(u_col, (L, L))
            km[key] = jnp.where(head_mask[hh], kpair[(d, j)], 0.0)
            vt[key] = v_ref[(2 * j + hh) * DV_C:(2 * j + hh + 1) * DV_C, sl[d]]
            st[key] = _dot(km[key].astype(BF16), qt[(d, j)])
        for j, d, hh in chains:
            key = (j, d, hh)
            ch = _chain(j, d, hh)
            b_row, cm_row, g_tot, cm_last, m_prev = row[key]
            head = slice((2 * j + hh) * DV_C, (2 * j + hh + 1) * DV_C)
            cx = cx_s[ch]
            big_m = jnp.maximum(m_prev, cm_row)
            s = st[key] * jnp.exp(jnp.where(causal[d], u_bc[key] - big_m, -jnp.inf))
            inter = jnp.exp(m_prev - big_m)
            qcx = _dot(cx.astype(BF16), qt[(d, j)])
            num = _dot(vt[key], s.astype(BF16)) + inter * qcx[0:DV_C, :]
            den = jnp.sum(s, axis=0, keepdims=True) + inter * qcx[DV_C:DV_C + 1, :]
            hval = num * (1.0 / jnp.maximum(jnp.abs(den), jnp.exp(-(b_row + big_m))))
            h_s[head, sl[d]] = h_s[head, sl[d]] + hval

            m_top = jnp.maximum(m_prev, cm_last)
            kw = km[key] * jnp.exp(u_bc[key] - cm_last)
            vx = jnp.concatenate([vt[key], ones_blk], axis=0)
            cx_s[ch] = (jnp.exp(m_prev - m_top) * cx
                        + jnp.exp(cm_last - m_top) * _dot(vx, kw.astype(BF16)))
            m_s[ch] = jnp.broadcast_to(g_tot + m_top, (1, LANES))
        return carry

    lax.fori_loop(0, nc, chunk_step, 0)

    for hd in range(H_C):
        sl = slice(hd * DV_C, (hd + 1) * DV_C)
        hsum = jnp.transpose(h_s[sl, :])
        y = (_rms(hsum) * gn_ref[...]) * jax.nn.sigmoid(o_ref[:, sl])
        hs_ref[:, sl] = y.astype(BF16)
    if emit_state:
        for j, d, hh in chains:
            ch = _chain(j, d, hh)
            cf_ref[0, 0, d, 2 * j + hh] = jnp.transpose(cx_s[ch, 0:DV_C, :])[hh * DK_C:(hh + 1) * DK_C, :]
            mst_ref[0, d * H_C + 2 * j + hh:d * H_C + 2 * j + hh + 1, :] = m_s[ch]
        for j in range(npair):
            for d in range(2):
                nst_ref[0, d * npair + j:d * npair + j + 1, :] = (
                    cx_s[_chain(j, d, 0), DV_C:DV_C + 1, :] + cx_s[_chain(j, d, 1), DV_C:DV_C + 1, :])


def _mlstm(q, k, v, o, gc, gr, gn, state, seq, emit_state):
    n = k.shape[0]
    nb = n // seq
    npair = H_C // 2
    has_state = state is not None
    hk, hv = H_C * DK_C, H_C * DV_C
    in_specs = [
        pl.BlockSpec((hk, seq), lambda b: (0, b)),
        pl.BlockSpec((seq, hk), lambda b: (b, 0)),
        pl.BlockSpec((hv, seq), lambda b: (0, b)),
        pl.BlockSpec((seq, hv), lambda b: (b, 0)),
        pl.BlockSpec((seq, LANES), lambda b: (b, 0)),
        pl.BlockSpec((3 * N_GATE, seq), lambda b: (0, b)),
        pl.BlockSpec((1, DV_C), lambda b: (0, 0)),
    ]
    args = [q, k, v, o, gc, gr, gn]
    if has_state:
        c0, n0, m0 = state
        in_specs += [pl.BlockSpec((1, 2, H_C, DK_C, DV_C), lambda b: (b, 0, 0, 0, 0)),
                     pl.BlockSpec((1, 2, npair, 1, LANES), lambda b: (b, 0, 0, 0, 0)),
                     pl.BlockSpec((1, 2, H_C, 1, LANES), lambda b: (b, 0, 0, 0, 0))]
        args += [c0, n0, m0]
    out_specs = [pl.BlockSpec((seq, hv), lambda b: (b, 0))]
    out_shape = [jax.ShapeDtypeStruct((n, hv), BF16)]
    if emit_state:
        out_specs += [pl.BlockSpec((1, 1, 2, H_C, DK_C, DV_C), lambda b: (b, 0, 0, 0, 0, 0)),
                      pl.BlockSpec((1, 2 * npair, LANES), lambda b: (b, 0, 0)),
                      pl.BlockSpec((1, 2 * H_C, LANES), lambda b: (b, 0, 0))]
        out_shape += [jax.ShapeDtypeStruct((nb, 1, 2, H_C, DK_C, DV_C), F32),
                      jax.ShapeDtypeStruct((nb, 2 * npair, LANES), F32),
                      jax.ShapeDtypeStruct((nb, 2 * H_C, LANES), F32)]
    n_chain = 2 * H_C
    return pl.pallas_call(
        functools.partial(_mlstm_kernel, has_state=has_state, emit_state=emit_state),
        grid=(nb,),
        in_specs=in_specs, out_specs=out_specs, out_shape=out_shape,
        scratch_shapes=[pltpu.VMEM((n_chain, 2 * DV_C, LANES), F32), pltpu.VMEM((n_chain, 1, LANES), F32),
                        pltpu.VMEM((hv, seq), F32)],
        compiler_params=_params(("arbitrary",)),
        name="mlstm_lat" if has_state else "mlstm_ctx",
    )(*args)


def _rope_tables(n_tok):
    t = np.arange(n_tok)
    rows = (t // GRID_W).astype(np.float64)
    cols = (t % GRID_W).astype(np.float64)

    def axis_tabs(width, lane0):
        half = width // 2
        quarter = half // 2
        freqs = np.power(ROPE_BASE, -np.arange(quarter, dtype=np.float64) / quarter)
        c = np.ones((n_tok, LANES))
        sa = np.zeros((n_tok, LANES))
        sb = np.zeros((n_tok, LANES))
        for g, pos in enumerate((rows, cols)):
            ang = pos[:, None] * freqs[None, :]
            a0 = lane0 + g * half
            c[:, a0:a0 + quarter] = np.cos(ang)
            c[:, a0 + quarter:a0 + half] = np.cos(ang)
            sa[:, a0:a0 + quarter] = -np.sin(ang)
            sb[:, a0 + quarter:a0 + half] = np.sin(ang)
        return c, sa, sb

    cm, sam, sbm = axis_tabs(QK_ROPE, ROPE_LANE0)
    c0, sa0, sb0 = axis_tabs(DIFF_HD, 0)
    c1, sa1, sb1 = axis_tabs(DIFF_HD, DIFF_HD)
    cd = np.where(np.arange(LANES)[None, :] < DIFF_HD, c0, c1)
    return tuple(jnp.asarray(a, F32) for a in (cm, sam, sbm, cd, sa0 + sa1, sb0 + sb1))


def _prep_even(w_in_ab, w_uq, w_ukv):
    c2 = Q_LORA + KV_LORA
    w_ab = w_in_ab.astype(BF16)
    wd = w_ab[:, c2 + QK_ROPE:]
    wq = jnp.pad(w_uq.reshape(Q_LORA, H_A, QK_NOPE + QK_ROPE),
                 ((0, 0), (0, 0), (0, HEAD_PAD - QK_NOPE - QK_ROPE))).reshape(Q_LORA, H_A * HEAD_PAD)
    kvw = w_ukv.reshape(KV_LORA, H_A, QK_NOPE + V_HD_A)
    kpad = jnp.pad(kvw[..., :QK_NOPE], ((0, 0), (0, 0), (0, HEAD_PAD - QK_NOPE)))
    vw = kvw[..., QK_NOPE:]
    zv = jnp.zeros_like(vw)
    odd = (jnp.arange(H_A) % 2 == 1)[None, :, None]
    vpad = jnp.where(odd, jnp.concatenate([zv, vw], -1), jnp.concatenate([vw, zv], -1))
    wukv = jnp.concatenate([kpad.reshape(KV_LORA, -1), vpad.reshape(KV_LORA, -1)], axis=1)
    return (w_ab, wd), wq.astype(BF16), wukv.astype(BF16)


def _gate_order(g):
    lead = g.shape[:-1]
    g = g.reshape(lead + (2, 2, H_C // 2, 2))
    perm = tuple(range(len(lead))) + tuple(len(lead) + a for a in (1, 2, 0, 3))
    return g.transpose(perm).reshape(lead + (4 * H_C,))


def _prep_odd(w_in_c, b_gate_c):
    ng = 4 * H_C
    base = w_in_c.shape[1] - ng
    w1 = w_in_c.astype(BF16)
    wg = jnp.pad(_gate_order(w1[:, base:]), ((0, 0), (0, LANES - ng)))
    bg = jnp.pad(_gate_order(b_gate_c), (0, LANES - ng)).reshape(1, LANES)
    return w1, wg, bg


def kernel(x_prompt, x_sample, cache_mla_ckv, cache_mla_krope, cache_diff_k, cache_diff_v,
           state_mlstm_C, state_mlstm_n, state_mlstm_m, c, c_ctx,
           w_ada, b_ada, g_mix, g_ffn, w_ffn_in, w_ffn_out,
           w_in_ab, g_q_lora, g_kv_lora, w_uq, w_ukv, diff_lambda, g_diff_subln, w_out_ab,
           w_in_c, b_gate_c, g_mlstm, w_out_c, g_final):
    nbp, seq_p, _ = x_prompt.shape
    nbs, seq_s, _ = x_sample.shape
    past = cache_mla_ckv.shape[2]
    assert DEPTH == 2 and 1 + nbs <= COND_ROWS
    assert cache_mla_ckv.shape[1] == 1 and state_mlstm_C.shape[1] == 1

    cond = jnp.concatenate([c_ctx[None], c, jnp.zeros((COND_ROWS - 1 - nbs, D_MODEL), F32)], axis=0)
    mod = _ada(cond, w_ada, b_ada).reshape(DEPTH * COND_ROWS, N_MOD, D_MODEL)

    xp = x_prompt.reshape(nbp * seq_p, D_MODEL)
    xs = x_sample.reshape(nbs * seq_s, D_MODEL)
    row2 = lambda v: v.reshape(1, -1)

    def mod_rows(layer, tm):
        prompt = lambda i: layer * COND_ROWS
        sample = lambda i: layer * COND_ROWS + 1 + i // (seq_s // tm)
        return prompt, sample

    lam_init = 0.8 - 0.6 * math.exp(-0.3 * 0)
    w0, wq, wukv = _prep_even(w_in_ab[0], w_uq[0], w_ukv[0])
    rope_tabs = _rope_tables(seq_s)
    mrp, mrs = mod_rows(0, TM_PROJ)
    gq, gkv = row2(g_q_lora[0]), row2(g_kv_lora[0])
    outs_p = _proj0(xp, mod, mrp, row2(g_mix[0]), *w0, gq, gkv, wq, wukv, None, seq_p)
    outs_s = _proj0(xs, mod, mrs, row2(g_mix[0]), *w0, gq, gkv, wq, wukv, rope_tabs, seq_s)
    qm_p, km_p, vm_p, dq_p, dk_p, dv_p, ckv_new, kr_new, dk_new, dv_new = outs_p
    qm_s, km_s, vm_s, dq_s, dk_s, dv_s = outs_s
    gsub = row2(g_diff_subln[0])
    krb_c = jnp.pad(cache_mla_krope.reshape(nbs, past, QK_ROPE),
                    ((0, 0), (0, 0), (ROPE_LANE0, LANES - ROPE_LANE0 - QK_ROPE)))
    ctx = (cache_mla_ckv.reshape(nbs, past, KV_LORA), krb_c,
           cache_diff_k.reshape(nbs, H_B, past, 2 * DIFF_HD), cache_diff_v.reshape(nbs, H_B, past, DIFF_VD))
    a_p = _attn0(qm_p, km_p, vm_p, dq_p, dk_p, dv_p, diff_lambda[0], gsub, None, None, seq_p, lam_init)
    a_s = _attn0(qm_s, km_s, vm_s, dq_s, dk_s, dv_s, diff_lambda[0], gsub, ctx, wukv, seq_s, lam_init)
    mtp, mts = mod_rows(0, TM_TAIL)
    wo0 = w_out_ab[0].astype(BF16)
    win, wout = w_ffn_in.astype(BF16), w_ffn_out.astype(BF16)
    xp = _tail(xp, a_p, mod, mtp, wo0, row2(g_ffn[0]), win, wout, 0, None, "tail0_ctx")
    xs = _tail(xs, a_s, mod, mts, wo0, row2(g_ffn[0]), win, wout, 0, None, "tail0_lat")

    w1, wg, bg = _prep_odd(w_in_c[0], b_gate_c[0])
    mrp, mrs = mod_rows(1, TM_PROJ1)
    gn = row2(g_mlstm[0])
    q_p, k_p, v_p, o_p, gc_p, gr_p = _proj1(xp, mod, mrp, row2(g_mix[1]), w1, wg, bg)
    q_s, k_s, v_s, o_s, gc_s, gr_s = _proj1(xs, mod, mrs, row2(g_mix[1]), w1, wg, bg)
    hs_p, c_new, nst, mst = _mlstm(q_p, k_p, v_p, o_p, gc_p, gr_p, gn, None, seq_p, True)
    state = (state_mlstm_C.reshape(nbs, 2, H_C, DK_C, DV_C),
             state_mlstm_n.reshape(nbs, 2, H_C // 2, 1, LANES),
             jnp.broadcast_to(state_mlstm_m.reshape(nbs, 2, H_C, 1, 1), (nbs, 2, H_C, 1, LANES)))
    (hs_s,) = _mlstm(q_s, k_s, v_s, o_s, gc_s, gr_s, gn, state, seq_s, False)
    mtp, mts = mod_rows(1, TM_TAIL)
    wo1 = w_out_c[0].astype(BF16)
    gfin = row2(g_final)
    yp = _tail(xp, hs_p, mod, mtp, wo1, row2(g_ffn[1]), win, wout, 1, gfin, "tail1_ctx")
    ys = _tail(xs, hs_s, mod, mts, wo1, row2(g_ffn[1]), win, wout, 1, gfin, "tail1_lat")

    return (yp.reshape(nbp, seq_p, D_MODEL), ys.reshape(nbs, seq_s, D_MODEL),
            ckv_new.reshape(nbp, 1, seq_p, KV_LORA), kr_new.reshape(nbp, 1, seq_p, QK_ROPE),
            dk_new.reshape(nbp, 1, H_B, seq_p, 2 * DIFF_HD), dv_new.reshape(nbp, 1, H_B, seq_p, DIFF_VD),
            c_new, nst.reshape(nbp, 1, 2, H_C, DK_C), mst[:, :, 0].reshape(nbp, 1, 2, H_C))
```

```python
import functools
import math

import jax
import jax.numpy as jnp
import numpy as np
from jax import lax
from jax.experimental import pallas as pl
from jax.experimental.pallas import tpu as pltpu

F32 = jnp.float32
BF16 = jnp.bfloat16

D_MODEL = 1024
DEPTH = 2
GRID_W = 64
ROPE_BASE = 10000.0
RMS_EPS = 1e-6
H_A = 8
QK_NOPE = 64
QK_ROPE = 32
V_HD_A = 64
Q_LORA = 256
KV_LORA = 128
H_B = 4
DIFF_HD = 64
DIFF_VD = 2 * DIFF_HD
H_C = 8
DK_C = 64
DV_C = D_MODEL // H_C
D_FF = -(-8 * D_MODEL // (3 * 256)) * 256

LANES = 128
HEAD_PAD = LANES
ROPE_LANE0 = QK_NOPE
N_MOD = 6
COND_ROWS = 16
N_GATE = 2 * H_C

TM_PROJ = 512
TM_PROJ1 = 1024
TQ_ATTN = 512
TM_TAIL = 1024
FF_CHUNK = 256
MLSTM_L = 128
ADA_TN = 1024
VMEM_LIMIT = 56 * 1024 * 1024

assert D_FF % LANES == 0 and FF_CHUNK % LANES == 0


def _dot(a, b):
    return jnp.dot(a, b, preferred_element_type=F32)


def _dot_nt(a, b):
    return lax.dot_general(a, b, (((1,), (1,)), ((), ())), preferred_element_type=F32)


def _dot_tn(a, b):
    return lax.dot_general(a, b, (((0,), (0,)), ((), ())), preferred_element_type=F32)


def _rms(x):
    return x * lax.rsqrt(jnp.mean(x * x, axis=-1, keepdims=True) + RMS_EPS)


def _modulate(x, g, shift, scale):
    return (_rms(x) * g) * (1.0 + scale) + shift


def _params(semantics):
    return pltpu.CompilerParams(dimension_semantics=semantics, vmem_limit_bytes=VMEM_LIMIT)


def _ada_kernel(cond_ref, w_ref, b_ref, o_ref):
    c = cond_ref[...]
    s = (c * jax.nn.sigmoid(c)).astype(BF16)
    o_ref[0] = _dot(s, w_ref[0].astype(BF16)) + b_ref[0]


def _ada(cond, w_ada, b_ada):
    n_out = w_ada.shape[-1]
    return pl.pallas_call(
        _ada_kernel,
        grid=(DEPTH, n_out // ADA_TN),
        in_specs=[
            pl.BlockSpec((COND_ROWS, D_MODEL), lambda l, n: (0, 0)),
            pl.BlockSpec((1, D_MODEL, ADA_TN), lambda l, n: (l, 0, n)),
            pl.BlockSpec((1, 1, ADA_TN), lambda l, n: (l, 0, n)),
        ],
        out_specs=pl.BlockSpec((1, COND_ROWS, ADA_TN), lambda l, n: (l, 0, n)),
        out_shape=jax.ShapeDtypeStruct((DEPTH, COND_ROWS, n_out), F32),
        compiler_params=_params(("arbitrary", "arbitrary")),
        name="ada",
    )(cond, w_ada, b_ada.reshape(DEPTH, 1, n_out))


def _rope(x, c, sa, sb, off):
    return x * c + pltpu.roll(x, LANES - off, 1) * sa + pltpu.roll(x, off, 1) * sb


def _proj0_kernel(*refs, rope, emit_cache):
    x_ref, mod_ref, g_ref, wa_ref, wd_ref, gq_ref, gkv_ref, wq_ref, wukv_ref = refs[:9]
    refs = refs[9:]
    if rope:
        cm_ref, sam_ref, sbm_ref, cd_ref, sad_ref, sbd_ref = refs[:6]
        refs = refs[6:]
    qm_ref, km_ref, vm_ref, dq_ref, dk_ref, dv_ref = refs[:6]
    refs = refs[6:]
    if emit_cache:
        ckvf_ref, krf_ref, dkf_ref, dvf_ref = refs

    h = _modulate(x_ref[...], g_ref[...], mod_ref[0, 0:1, :], mod_ref[0, 1:2, :]).astype(BF16)
    pa = _dot(h, wa_ref[...])
    pd = _dot(h, wd_ref[...])
    lane = lax.broadcasted_iota(jnp.int32, (pa.shape[0], LANES), 1)
    krb = jnp.where((lane >= ROPE_LANE0) & (lane < ROPE_LANE0 + QK_ROPE),
                    pltpu.roll(pa[:, Q_LORA + KV_LORA:Q_LORA + KV_LORA + LANES], ROPE_LANE0, 1), 0.0)
    cq = pa[:, 0:Q_LORA]
    ckv = _rms(pa[:, Q_LORA:Q_LORA + KV_LORA]) * gkv_ref[...]
    qa = _dot((_rms(cq) * gq_ref[...]).astype(BF16), wq_ref[...])
    kv = _dot(ckv.astype(BF16), wukv_ref[...])
    if emit_cache:
        ckvf_ref[...] = ckv
        krf_ref[...] = krb[:, ROPE_LANE0:ROPE_LANE0 + QK_ROPE]
    if rope:
        cm, sam, sbm = cm_ref[...], sam_ref[...], sbm_ref[...]
        cd, sad, sbd = cd_ref[...], sad_ref[...], sbd_ref[...]
        krb = _rope(krb, cm, sam, sbm, QK_ROPE // 4)
    for hd in range(H_A):
        sl = slice(hd * HEAD_PAD, (hd + 1) * HEAD_PAD)
        qh = qa[:, sl]
        if rope:
            qh = _rope(qh, cm, sam, sbm, QK_ROPE // 4)
        qm_ref[:, sl] = qh.astype(BF16)
        km_ref[:, sl] = (kv[:, sl] + krb).astype(BF16)
    vm_ref[...] = kv[:, H_A * HEAD_PAD:].astype(BF16)
    for hd in range(H_B):
        sl = slice(hd * LANES, (hd + 1) * LANES)
        dq = pd[:, hd * LANES:(hd + 1) * LANES]
        dk = pd[:, (H_B + hd) * LANES:(H_B + hd + 1) * LANES]
        dv = pd[:, (2 * H_B + hd) * LANES:(2 * H_B + hd + 1) * LANES]
        if emit_cache:
            seq = dkf_ref.shape[2]
            for bi in range(dkf_ref.shape[0]):
                dkf_ref[bi, hd] = dk[bi * seq:(bi + 1) * seq]
                dvf_ref[bi, hd] = dv[bi * seq:(bi + 1) * seq]
        if rope:
            dq = _rope(dq, cd, sad, sbd, DIFF_HD // 4)
            dk = _rope(dk, cd, sad, sbd, DIFF_HD // 4)
        dq_ref[:, sl] = (dq * (DIFF_HD ** -0.5)).astype(BF16)
        dk_ref[:, sl] = dk.astype(BF16)
        dv_ref[:, sl] = dv.astype(BF16)


def _proj0(x, mod, mod_row, g, w_ab, wd, gq, gkv, wq, wukv, rope_tabs, seq):
    n = x.shape[0]
    tm = TM_PROJ
    rope = rope_tabs is not None
    emit_cache = not rope
    tiles_per_seq = max(seq // tm, 1)
    full = lambda shape: pl.BlockSpec(shape, lambda i: (0,) * len(shape))
    in_specs = [
        pl.BlockSpec((tm, D_MODEL), lambda i: (i, 0)),
        pl.BlockSpec((1, N_MOD, D_MODEL), lambda i: (mod_row(i), 0, 0)),
        full((1, D_MODEL)), full((D_MODEL, Q_LORA + KV_LORA + LANES)), full(wd.shape),
        full((1, Q_LORA)), full((1, KV_LORA)), full(wq.shape), full(wukv.shape),
    ]
    args = [x, mod, g, w_ab, wd, gq, gkv, wq, wukv]
    if rope:
        in_specs += [pl.BlockSpec((tm, LANES), lambda i: (i % tiles_per_seq, 0))] * 6
        args += list(rope_tabs)
    row = lambda w: pl.BlockSpec((tm, w), lambda i: (i, 0))
    out_specs = [row(1024), row(1024), row(1024), row(512), row(512), row(512)]
    out_shape = [jax.ShapeDtypeStruct((n, w), BF16) for w in (1024, 1024, 1024, 512, 512, 512)]
    if emit_cache:
        assert tm % seq == 0
        cache = pl.BlockSpec((tm // seq, H_B, seq, LANES), lambda i: (i, 0, 0, 0))
        out_specs += [row(KV_LORA), row(QK_ROPE), cache, cache]
        out_shape += [jax.ShapeDtypeStruct((n, KV_LORA), F32), jax.ShapeDtypeStruct((n, QK_ROPE), F32),
                      jax.ShapeDtypeStruct((n // seq, H_B, seq, LANES), F32),
                      jax.ShapeDtypeStruct((n // seq, H_B, seq, LANES), F32)]
    return pl.pallas_call(
        functools.partial(_proj0_kernel, rope=rope, emit_cache=emit_cache),
        grid=(n // tm,),
        in_specs=in_specs, out_specs=out_specs, out_shape=out_shape,
        compiler_params=_params(("arbitrary",)),
        name="proj0_rope" if rope else "proj0_ctx",
    )(*args)


def _attn_kernel(*refs, has_ctx, lam_init):
    q_ref, dq_ref, k_ref, v_ref, dk_ref, dv_ref, lamv_ref, gsub_ref = refs[:8]
    refs = refs[8:]
    if has_ctx:
        ckvc_ref, krc_ref, dkc_ref, dvc_ref, wukv_ref = refs[:5]
        refs = refs[5:]
    o_ref, kx_s, vx_s, dkx_s, dvx_s = refs
    seq = k_ref.shape[0]
    ktot = kx_s.shape[0]

    @pl.when(pl.program_id(1) == 0)
    def _():
        kx_s[0:seq, :] = k_ref[...]
        dkx_s[0:seq, :] = dk_ref[...]
        ones_own = jnp.ones((seq, LANES), BF16)
        for hd in range(H_A):
            vx_s[0:seq, 2 * hd * LANES:(2 * hd + 1) * LANES] = v_ref[:, hd * LANES:(hd + 1) * LANES]
            vx_s[0:seq, (2 * hd + 1) * LANES:(2 * hd + 2) * LANES] = ones_own
        for hd in range(H_B):
            dvx_s[0:seq, 2 * hd * LANES:(2 * hd + 1) * LANES] = dv_ref[:, hd * LANES:(hd + 1) * LANES]
            dvx_s[0:seq, (2 * hd + 1) * LANES:(2 * hd + 2) * LANES] = ones_own
        if has_ctx:
            kv = _dot(ckvc_ref[0].astype(BF16), wukv_ref[...])
            krb = krc_ref[0]
            ones_ctx = jnp.ones((ktot - seq, LANES), BF16)
            for hd in range(H_A):
                sl = slice(hd * HEAD_PAD, (hd + 1) * HEAD_PAD)
                kx_s[seq:ktot, sl] = (kv[:, sl] + krb).astype(BF16)
                vx_s[seq:ktot, 2 * hd * LANES:(2 * hd + 1) * LANES] = (
                    kv[:, (H_A + hd) * HEAD_PAD:(H_A + hd + 1) * HEAD_PAD].astype(BF16))
                vx_s[seq:ktot, (2 * hd + 1) * LANES:(2 * hd + 2) * LANES] = ones_ctx
            for hd in range(H_B):
                sl = slice(hd * LANES, (hd + 1) * LANES)
                dkx_s[seq:ktot, sl] = dkc_ref[0, hd].astype(BF16)
                dvx_s[seq:ktot, 2 * hd * LANES:(2 * hd + 1) * LANES] = dvc_ref[0, hd].astype(BF16)
                dvx_s[seq:ktot, (2 * hd + 1) * LANES:(2 * hd + 2) * LANES] = ones_ctx

    log2e = 1.0 / math.log(2.0)
    tq = dq_ref.shape[0]
    lo = lax.broadcasted_iota(jnp.int32, (tq, LANES), 1) < DIFF_HD

    jobs = []
    for hd in range(H_A):
        sl = slice(hd * HEAD_PAD, (hd + 1) * HEAD_PAD)
        jobs.append((lambda sl=sl: q_ref[:, sl], (kx_s, sl), (vx_s, hd), (QK_NOPE + QK_ROPE) ** -0.5 * log2e))
    for hd in range(H_B):
        sl = slice(hd * LANES, (hd + 1) * LANES)
        for part in range(2):
            def qfn(sl=sl, part=part):
                dq = dq_ref[:, sl].astype(F32)
                return (jnp.where(lo, dq, 0.0) if part == 0 else jnp.where(lo, 0.0, dq)).astype(BF16)
            jobs.append((qfn, (dkx_s, sl), (dvx_s, hd), log2e))

    def scores(job):
        qfn, (kref, sl), _, _ = job
        return _dot_nt(qfn(), kref[:, sl])

    def finish(s, job):
        _, _, (vref, hd), c = job
        m = jnp.max(s, axis=-1, keepdims=True)
        e = jnp.exp2((s - m) * c).astype(BF16)
        res = _dot(e, vref[:, 2 * hd * LANES:(2 * hd + 2) * LANES])
        return res[:, 0:LANES] * (1.0 / res[:, LANES:2 * LANES])

    outs = []
    s_next = scores(jobs[0])
    for i, job in enumerate(jobs):
        s_cur = s_next
        if i + 1 < len(jobs):
            s_next = scores(jobs[i + 1])
        outs.append(finish(s_cur, job))

    for j in range(H_A // 2):
        o_ref[:, j * LANES:(j + 1) * LANES] = (outs[2 * j] + outs[2 * j + 1]).astype(BF16)
    lv = lamv_ref[...]
    lam = (jnp.exp(jnp.sum(lv[0:1] * lv[1:2], axis=-1, keepdims=True))
           - jnp.exp(jnp.sum(lv[2:3] * lv[3:4], axis=-1, keepdims=True)) + lam_init)
    for hd in range(H_B):
        acc = outs[H_A + 2 * hd] - lam * outs[H_A + 2 * hd + 1]
        ob = (_rms(acc) * gsub_ref[...]) * (1.0 - lam_init)
        o_ref[:, H_A * V_HD_A + hd * LANES:H_A * V_HD_A + (hd + 1) * LANES] = ob.astype(BF16)


def _attn0(qm, km, vm, dq, dk, dv, lamv, gsub, ctx, wukv, seq, lam_init):
    n = qm.shape[0]
    nb = n // seq
    tq = min(TQ_ATTN, seq)
    has_ctx = ctx is not None
    full = lambda shape: pl.BlockSpec(shape, lambda b, t: (0,) * len(shape))
    qrow = lambda w: pl.BlockSpec((tq, w), lambda b, t: (b * (seq // tq) + t, 0))
    krow = lambda w: pl.BlockSpec((seq, w), lambda b, t: (b, 0))
    in_specs = [qrow(1024), qrow(512), krow(1024), krow(1024), krow(512), krow(512),
                full((4, DIFF_HD)), full((1, DIFF_VD))]
    args = [qm, dq, km, vm, dk, dv, lamv, gsub]
    past = 0
    if has_ctx:
        ckv_c, krb_c, dk_c, dv_c = ctx
        past = ckv_c.shape[1]
        in_specs += [pl.BlockSpec((1, past, LANES), lambda b, t: (b, 0, 0)),
                     pl.BlockSpec((1, past, LANES), lambda b, t: (b, 0, 0)),
                     pl.BlockSpec((1, H_B, past, LANES), lambda b, t: (b, 0, 0, 0)),
                     pl.BlockSpec((1, H_B, past, LANES), lambda b, t: (b, 0, 0, 0)),
                     full(wukv.shape)]
        args += [ckv_c, krb_c, dk_c, dv_c, wukv]
    ktot = seq + past
    scratch = [pltpu.VMEM((ktot, H_A * HEAD_PAD), BF16), pltpu.VMEM((ktot, 2 * H_A * LANES), BF16),
               pltpu.VMEM((ktot, H_B * LANES), BF16), pltpu.VMEM((ktot, 2 * H_B * LANES), BF16)]
    return pl.pallas_call(
        functools.partial(_attn_kernel, has_ctx=has_ctx, lam_init=lam_init),
        grid=(nb, seq // tq),
        in_specs=in_specs,
        out_specs=pl.BlockSpec((tq, D_MODEL), lambda b, t: (b * (seq // tq) + t, 0)),
        out_shape=jax.ShapeDtypeStruct((n, D_MODEL), BF16),
        scratch_shapes=scratch,
        compiler_params=_params(("arbitrary", "arbitrary")),
        name="attn0_lat" if has_ctx else "attn0_ctx",
    )(*args)


def _tail_kernel(*refs, final):
    x_ref, a_ref, mod_ref, wo_ref, gf_ref, win_ref, wout_ref = refs[:7]
    if final:
        gfin_ref, o_ref = refs[7:]
    else:
        (o_ref,) = refs[7:]
    x1 = x_ref[...] + mod_ref[0, 2:3, :] * _dot(a_ref[...], wo_ref[...])
    h = _modulate(x1, gf_ref[...], mod_ref[0, 3:4, :], mod_ref[0, 4:5, :]).astype(BF16)
    bounds = list(range(0, D_FF, FF_CHUNK)) + [D_FF]
    acc = None
    for lo, hi in zip(bounds[:-1], bounds[1:]):
        a = _dot(h, win_ref[0, :, lo:hi])
        b = _dot(h, win_ref[0, :, D_FF + lo:D_FF + hi])
        act = ((a * jax.nn.sigmoid(a)) * b).astype(BF16)
        part = _dot(act, wout_ref[0, lo:hi, :])
        acc = part if acc is None else acc + part
    x2 = x1 + mod_ref[0, 5:6, :] * acc
    if final:
        x2 = _rms(x2) * gfin_ref[...]
    o_ref[...] = x2


def _tail(x, a, mod, mod_row, wo, gf, win, wout, layer, gfin, name):
    n = x.shape[0]
    tm = TM_TAIL
    final = gfin is not None
    full = lambda shape: pl.BlockSpec(shape, lambda i: (0,) * len(shape))
    resident = lambda shape: pl.BlockSpec((1,) + shape[1:], lambda i: (layer, 0, 0),
                                          pipeline_mode=pl.Buffered(1))
    in_specs = [
        pl.BlockSpec((tm, D_MODEL), lambda i: (i, 0)),
        pl.BlockSpec((tm, D_MODEL), lambda i: (i, 0)),
        pl.BlockSpec((1, N_MOD, D_MODEL), lambda i: (mod_row(i), 0, 0)),
        full(wo.shape), full((1, D_MODEL)), resident(win.shape), resident(wout.shape),
    ]
    args = [x, a, mod, wo, gf, win, wout]
    if final:
        in_specs.append(full((1, D_MODEL)))
        args.append(gfin)
    return pl.pallas_call(
        functools.partial(_tail_kernel, final=final),
        grid=(n // tm,),
        in_specs=in_specs,
        out_specs=pl.BlockSpec((tm, D_MODEL), lambda i: (i, 0)),
        out_shape=jax.ShapeDtypeStruct((n, D_MODEL), F32),
        compiler_params=_params(("arbitrary",)),
        name=name,
    )(*args)


def _split3(x):
    hi = x.astype(BF16)
    r1 = x - hi.astype(F32)
    mid = r1.astype(BF16)
    lo = (r1 - mid.astype(F32)).astype(BF16)
    return hi, mid, lo


def _proj1_kernel(x_ref, mod_ref, g_ref, w1_ref, wg_ref, bg_ref, q_ref, k_ref, v_ref, o_ref, gc_ref, gr_ref):
    tm = x_ref.shape[0]
    L = MLSTM_L
    hk = H_C * DK_C
    hv = H_C * DV_C
    h = _modulate(x_ref[...], g_ref[...], mod_ref[0, 0:1, :], mod_ref[0, 1:2, :]).astype(BF16)
    gates = _dot(h, wg_ref[...]) + bg_ref[...]
    proj = _dot(h, w1_ref[...])
    for blk in range(hk // LANES):
        sl = slice(blk * LANES, (blk + 1) * LANES)
        q_ref[sl, :] = jnp.transpose(proj[:, sl]).astype(BF16)
    for blk in range(hv // LANES):
        sl = slice(blk * LANES, (blk + 1) * LANES)
        v_ref[sl, :] = jnp.transpose(proj[:, 2 * hk + blk * LANES:2 * hk + (blk + 1) * LANES]).astype(BF16)
    k_ref[...] = proj[:, hk:2 * hk] * (DK_C ** -0.5)
    o_ref[...] = proj[:, 2 * hk + hv:2 * hk + 2 * hv]
    lf = jnp.minimum(gates, 0.0) - jnp.log1p(jnp.exp(-jnp.abs(gates)))
    r = lax.broadcasted_iota(jnp.int32, (L, L), 0)
    c = lax.broadcasted_iota(jnp.int32, (L, L), 1)
    pre = jnp.where(c <= r, 1.0, 0.0).astype(BF16)
    suf = jnp.where(c >= r, 1.0, 0.0).astype(BF16)
    parts = jnp.concatenate(_split3(lf), axis=1)
    fold = lambda t: t[:, 0:LANES] + t[:, LANES:2 * LANES] + t[:, 2 * LANES:3 * LANES]
    chunks = [parts[ck * L:(ck + 1) * L] for ck in range(tm // L)]
    lane = lax.broadcasted_iota(jnp.int32, (tm, LANES), 1)
    b_sum = jnp.where((lane % 4) >= 2,
                      jnp.concatenate([fold(_dot(suf, p)) for p in chunks], axis=0),
                      jnp.concatenate([fold(_dot(pre, p)) for p in chunks], axis=0))
    b = pltpu.roll(b_sum, LANES - N_GATE, 1)
    u = gates - b
    pos = lax.broadcasted_iota(jnp.int32, (tm, LANES), 0) % L
    bwd = (lane % 4) >= 2
    cm = u
    step = 1
    while step < L:
        below = jnp.where(pos >= step, pltpu.roll(cm, step, 0), -jnp.inf)
        above = jnp.where(pos < L - step, pltpu.roll(cm, tm - step, 0), -jnp.inf)
        cm = jnp.maximum(cm, jnp.where(bwd, above, below))
        step *= 2
    low = lane < N_GATE
    packed = (jnp.where(low, b, 0.0) + pltpu.roll(jnp.where(low, cm, 0.0), N_GATE, 1)
              + pltpu.roll(jnp.where(low, u, 0.0), 2 * N_GATE, 1))
    gc_ref[...] = packed
    gr_ref[...] = jnp.transpose(packed)[0:3 * N_GATE]


def _proj1(x, mod, mod_row, g, w1, wg, bg):
    n = x.shape[0]
    tm = TM_PROJ1
    assert tm % MLSTM_L == 0
    full = lambda shape: pl.BlockSpec(shape, lambda i: (0,) * len(shape))
    row = lambda w: pl.BlockSpec((tm, w), lambda i: (i, 0))
    col = lambda w: pl.BlockSpec((w, tm), lambda i: (0, i))
    hk, hv = H_C * DK_C, H_C * DV_C
    return pl.pallas_call(
        _proj1_kernel,
        grid=(n // tm,),
        in_specs=[row(D_MODEL), pl.BlockSpec((1, N_MOD, D_MODEL), lambda i: (mod_row(i), 0, 0)),
                  full((1, D_MODEL)), full((D_MODEL, 2 * hk + 2 * hv)), full(wg.shape), full((1, LANES))],
        out_specs=[col(hk), row(hk), col(hv), row(hv), row(LANES),
                   pl.BlockSpec((3 * N_GATE, tm), lambda i: (0, i))],
        out_shape=[jax.ShapeDtypeStruct((hk, n), BF16), jax.ShapeDtypeStruct((n, hk), F32),
                   jax.ShapeDtypeStruct((hv, n), BF16), jax.ShapeDtypeStruct((n, hv), F32),
                   jax.ShapeDtypeStruct((n, LANES), F32), jax.ShapeDtypeStruct((3 * N_GATE, n), F32)],
        compiler_params=_params(("arbitrary",)),
        name="proj1",
    )(x, mod, g, w1, wg, bg)


def _chain(j, d, hh):
    return (2 * j + d) * 2 + hh


def _mlstm_kernel(*refs, has_state, emit_state):
    q_ref, k_ref, v_ref, o_ref, gc_ref, gr_ref, gn_ref = refs[:7]
    refs = refs[7:]
    if has_state:
        c0_ref, n0_ref, m0_ref = refs[:3]
        refs = refs[3:]
    hs_ref = refs[0]
    refs = refs[1:]
    if emit_state:
        cf_ref, nst_ref, mst_ref = refs[:3]
        refs = refs[3:]
    cx_s, m_s, h_s = refs

    L = MLSTM_L
    seq = k_ref.shape[0]
    nc = seq // L
    npair = H_C // 2
    lane = lax.broadcasted_iota(jnp.int32, (1, LANES), 1)
    head_mask = [lane < DK_C, lane >= DK_C]
    ri = lax.broadcasted_iota(jnp.int32, (L, L), 0)
    ci = lax.broadcasted_iota(jnp.int32, (L, L), 1)
    causal = [ri <= ci, ri >= ci]
    ones_blk = jnp.ones((DV_C, L), BF16)
    chains = [(j, d, hh) for d in range(2) for j in range(npair) for hh in range(2)]

    h_s[...] = jnp.zeros_like(h_s)
    for j, d, hh in chains:
        ch = _chain(j, d, hh)
        if has_state:
            zpad = jnp.zeros((DK_C, DV_C), F32)
            c0 = c0_ref[0, d, 2 * j + hh]
            c0 = jnp.concatenate([c0, zpad] if hh == 0 else [zpad, c0], axis=0)
            cx_s[ch, 0:DV_C, :] = jnp.transpose(c0)
            n_row = jnp.where(head_mask[hh], n0_ref[0, d, j], 0.0)
            cx_s[ch, DV_C:2 * DV_C, :] = jnp.broadcast_to(n_row, (DV_C, LANES))
            m_s[ch] = m0_ref[0, d, 2 * j + hh]
        else:
            cx_s[ch] = jnp.zeros((2 * DV_C, LANES), F32)
            m_s[ch] = jnp.zeros((1, LANES), F32)

    def chunk_step(i, carry):
        sl = [pl.ds(pl.multiple_of(i * L, L), L), pl.ds(pl.multiple_of((nc - 1 - i) * L, L), L)]
        gcol = [gc_ref[sl[d], :] for d in range(2)]
        grow = [gr_ref[:, sl[d]] for d in range(2)]
        kpair = {(d, j): k_ref[sl[d], j * LANES:(j + 1) * LANES] for d in range(2) for j in range(npair)}
        qt = {(d, j): q_ref[j * LANES:(j + 1) * LANES, sl[d]] for d in range(2) for j in range(npair)}
        st, km, u_bc, vt, row = {}, {}, {}, {}, {}
        for j, d, hh in chains:
            key = (j, d, hh)
            idx = 4 * j + 2 * d + hh
            edge = L - 1 if d == 0 else 0
            b_row = grow[d][idx:idx + 1, :]
            cm_row = grow[d][N_GATE + idx:N_GATE + idx + 1, :]
            u_col = gcol[d][:, 2 * N_GATE + idx:2 * N_GATE + idx + 1]
            m_prev = m_s[_chain(j, d, hh)][:, 0:1]
            row[key] = (b_row, cm_row, b_row[:, edge:edge + 1], cm_row[:, edge:edge + 1], m_prev)
            u_bc[key] = jnp.broadcast_to(u_col, (L, L))
            km[key] = jnp.where(head_mask[hh], kpair[(d, j)], 0.0)
            vt[key] = v_ref[(2 * j + hh) * DV_C:(2 * j + hh + 1) * DV_C, sl[d]]
            st[key] = _dot(km[key].astype(BF16), qt[(d, j)])
        for j, d, hh in chains:
            key = (j, d, hh)
            ch = _chain(j, d, hh)
            b_row, cm_row, g_tot, cm_last, m_prev = row[key]
            head = slice((2 * j + hh) * DV_C, (2 * j + hh + 1) * DV_C)
            cx = cx_s[ch]
            big_m = jnp.maximum(m_prev, cm_row)
            s = st[key] * jnp.exp(jnp.where(causal[d], u_bc[key] - big_m, -jnp.inf))
            inter = jnp.exp(m_prev - big_m)
            qcx = _dot(cx.astype(BF16), qt[(d, j)])
            num = _dot(vt[key], s.astype(BF16)) + inter * qcx[0:DV_C, :]
            den = jnp.sum(s, axis=0, keepdims=True) + inter * qcx[DV_C:DV_C + 1, :]
            hval = num * (1.0 / jnp.maximum(jnp.abs(den), jnp.exp(-(b_row + big_m))))
            h_s[head, sl[d]] = h_s[head, sl[d]] + hval

            m_top = jnp.maximum(m_prev, cm_last)
            kw = km[key] * jnp.exp(u_bc[key] - cm_last)
            vx = jnp.concatenate([vt[key], ones_blk], axis=0)
            cx_s[ch] = (jnp.exp(m_prev - m_top) * cx
                        + jnp.exp(cm_last - m_top) * _dot(vx, kw.astype(BF16)))
            m_s[ch] = jnp.broadcast_to(g_tot + m_top, (1, LANES))
        return carry

    lax.fori_loop(0, nc, chunk_step, 0)

    for hd in range(H_C):
        sl = slice(hd * DV_C, (hd + 1) * DV_C)
        hsum = jnp.transpose(h_s[sl, :])
        y = (_rms(hsum) * gn_ref[...]) * jax.nn.sigmoid(o_ref[:, sl])
        hs_ref[:, sl] = y.astype(BF16)
    if emit_state:
        for j, d, hh in chains:
            ch = _chain(j, d, hh)
            cf_ref[0, 0, d, 2 * j + hh] = jnp.transpose(cx_s[ch, 0:DV_C, :])[hh * DK_C:(hh + 1) * DK_C, :]
            mst_ref[0, d * H_C + 2 * j + hh:d * H_C + 2 * j + hh + 1, :] = m_s[ch]
        for j in range(npair):
            for d in range(2):
                nst_ref[0, d * npair + j:d * npair + j + 1, :] = (
                    cx_s[_chain(j, d, 0), DV_C:DV_C + 1, :] + cx_s[_chain(j, d, 1), DV_C:DV_C + 1, :])


def _mlstm(q, k, v, o, gc, gr, gn, state, seq, emit_state):
    n = k.shape[0]
    nb = n // seq
    npair = H_C // 2
    has_state = state is not None
    hk, hv = H_C * DK_C, H_C * DV_C
    in_specs = [
        pl.BlockSpec((hk, seq), lambda b: (0, b)),
        pl.BlockSpec((seq, hk), lambda b: (b, 0)),
        pl.BlockSpec((hv, seq), lambda b: (0, b)),
        pl.BlockSpec((seq, hv), lambda b: (b, 0)),
        pl.BlockSpec((seq, LANES), lambda b: (b, 0)),
        pl.BlockSpec((3 * N_GATE, seq), lambda b: (0, b)),
        pl.BlockSpec((1, DV_C), lambda b: (0, 0)),
    ]
    args = [q, k, v, o, gc, gr, gn]
    if has_state:
        c0, n0, m0 = state
        in_specs += [pl.BlockSpec((1, 2, H_C, DK_C, DV_C), lambda b: (b, 0, 0, 0, 0)),
                     pl.BlockSpec((1, 2, npair, 1, LANES), lambda b: (b, 0, 0, 0, 0)),
                     pl.BlockSpec((1, 2, H_C, 1, LANES), lambda b: (b, 0, 0, 0, 0))]
        args += [c0, n0, m0]
    out_specs = [pl.BlockSpec((seq, hv), lambda b: (b, 0))]
    out_shape = [jax.ShapeDtypeStruct((n, hv), BF16)]
    if emit_state:
        out_specs += [pl.BlockSpec((1, 1, 2, H_C, DK_C, DV_C), lambda b: (b, 0, 0, 0, 0, 0)),
                      pl.BlockSpec((1, 2 * npair, LANES), lambda b: (b, 0, 0)),
                      pl.BlockSpec((1, 2 * H_C, LANES), lambda b: (b, 0, 0))]
        out_shape += [jax.ShapeDtypeStruct((nb, 1, 2, H_C, DK_C, DV_C), F32),
                      jax.ShapeDtypeStruct((nb, 2 * npair, LANES), F32),
                      jax.ShapeDtypeStruct((nb, 2 * H_C, LANES), F32)]
    n_chain = 2 * H_C
    return pl.pallas_call(
        functools.partial(_mlstm_kernel, has_state=has_state, emit_state=emit_state),
        grid=(nb,),
        in_specs=in_specs, out_specs=out_specs, out_shape=out_shape,
        scratch_shapes=[pltpu.VMEM((n_chain, 2 * DV_C, LANES), F32), pltpu.VMEM((n_chain, 1, LANES), F32),
                        pltpu.VMEM((hv, seq), F32)],
        compiler_params=_params(("arbitrary",)),
        name="mlstm_lat" if has_state else "mlstm_ctx",
    )(*args)


def _rope_tables(n_tok):
    t = np.arange(n_tok)
    rows = (t // GRID_W).astype(np.float64)
    cols = (t % GRID_W).astype(np.float64)

    def axis_tabs(width, lane0):
        half = width // 2
        quarter = half // 2
        freqs = np.power(ROPE_BASE, -np.arange(quarter, dtype=np.float64) / quarter)
        c = np.ones((n_tok, LANES))
        sa = np.zeros((n_tok, LANES))
        sb = np.zeros((n_tok, LANES))
        for g, pos in enumerate((rows, cols)):
            ang = pos[:, None] * freqs[None, :]
            a0 = lane0 + g * half
            c[:, a0:a0 + quarter] = np.cos(ang)
            c[:, a0 + quarter:a0 + half] = np.cos(ang)
            sa[:, a0:a0 + quarter] = -np.sin(ang)
            sb[:, a0 + quarter:a0 + half] = np.sin(ang)
        return c, sa, sb

    cm, sam, sbm = axis_tabs(QK_ROPE, ROPE_LANE0)
    c0, sa0, sb0 = axis_tabs(DIFF_HD, 0)
    c1, sa1, sb1 = axis_tabs(DIFF_HD, DIFF_HD)
    cd = np.where(np.arange(LANES)[None, :] < DIFF_HD, c0, c1)
    return tuple(jnp.asarray(a, F32) for a in (cm, sam, sbm, cd, sa0 + sa1, sb0 + sb1))


def _prep_even(w_in_ab, w_uq, w_ukv):
    c2 = Q_LORA + KV_LORA
    w_ab = w_in_ab.astype(BF16)
    wd = w_ab[:, c2 + QK_ROPE:]
    wq = jnp.pad(w_uq.reshape(Q_LORA, H_A, QK_NOPE + QK_ROPE),
                 ((0, 0), (0, 0), (0, HEAD_PAD - QK_NOPE - QK_ROPE))).reshape(Q_LORA, H_A * HEAD_PAD)
    kvw = w_ukv.reshape(KV_LORA, H_A, QK_NOPE + V_HD_A)
    kpad = jnp.pad(kvw[..., :QK_NOPE], ((0, 0), (0, 0), (0, HEAD_PAD - QK_NOPE)))
    vw = kvw[..., QK_NOPE:]
    zv = jnp.zeros_like(vw)
    odd = (jnp.arange(H_A) % 2 == 1)[None, :, None]
    vpad = jnp.where(odd, jnp.concatenate([zv, vw], -1), jnp.concatenate([vw, zv], -1))
    wukv = jnp.concatenate([kpad.reshape(KV_LORA, -1), vpad.reshape(KV_LORA, -1)], axis=1)
    return (w_ab, wd), wq.astype(BF16), wukv.astype(BF16)


def _gate_order(g):
    lead = g.shape[:-1]
    g = g.reshape(lead + (2, 2, H_C // 2, 2))
    perm = tuple(range(len(lead))) + tuple(len(lead) + a for a in (1, 2, 0, 3))
    return g.transpose(perm).reshape(lead + (4 * H_C,))


def _prep_odd(w_in_c, b_gate_c):
    ng = 4 * H_C
    base = w_in_c.shape[1] - ng
    w1 = w_in_c.astype(BF16)
    wg = jnp.pad(_gate_order(w1[:, base:]), ((0, 0), (0, LANES - ng)))
    bg = jnp.pad(_gate_order(b_gate_c), (0, LANES - ng)).reshape(1, LANES)
    return w1, wg, bg


def kernel(x_prompt, x_sample, cache_mla_ckv, cache_mla_krope, cache_diff_k, cache_diff_v,
           state_mlstm_C, state_mlstm_n, state_mlstm_m, c, c_ctx,
           w_ada, b_ada, g_mix, g_ffn, w_ffn_in, w_ffn_out,
           w_in_ab, g_q_lora, g_kv_lora, w_uq, w_ukv, diff_lambda, g_diff_subln, w_out_ab,
           w_in_c, b_gate_c, g_mlstm, w_out_c, g_final):
    nbp, seq_p, _ = x_prompt.shape
    nbs, seq_s, _ = x_sample.shape
    past = cache_mla_ckv.shape[2]
    assert DEPTH == 2 and 1 + nbs <= COND_ROWS
    assert cache_mla_ckv.shape[1] == 1 and state_mlstm_C.shape[1] == 1

    cond = jnp.concatenate([c_ctx[None], c, jnp.zeros((COND_ROWS - 1 - nbs, D_MODEL), F32)], axis=0)
    mod = _ada(cond, w_ada, b_ada).reshape(DEPTH * COND_ROWS, N_MOD, D_MODEL)

    xp = x_prompt.reshape(nbp * seq_p, D_MODEL)
    xs = x_sample.reshape(nbs * seq_s, D_MODEL)
    row2 = lambda v: v.reshape(1, -1)

    def mod_rows(layer, tm):
        prompt = lambda i: layer * COND_ROWS
        sample = lambda i: layer * COND_ROWS + 1 + i // (seq_s // tm)
        return prompt, sample

    lam_init = 0.8 - 0.6 * math.exp(-0.3 * 0)
    w0, wq, wukv = _prep_even(w_in_ab[0], w_uq[0], w_ukv[0])
    rope_tabs = _rope_tables(seq_s)
    mrp, mrs = mod_rows(0, TM_PROJ)
    gq, gkv = row2(g_q_lora[0]), row2(g_kv_lora[0])
    outs_p = _proj0(xp, mod, mrp, row2(g_mix[0]), *w0, gq, gkv, wq, wukv, None, seq_p)
    outs_s = _proj0(xs, mod, mrs, row2(g_mix[0]), *w0, gq, gkv, wq, wukv, rope_tabs, seq_s)
    qm_p, km_p, vm_p, dq_p, dk_p, dv_p, ckv_new, kr_new, dk_new, dv_new = outs_p
    qm_s, km_s, vm_s, dq_s, dk_s, dv_s = outs_s
    gsub = row2(g_diff_subln[0])
    krb_c = jnp.pad(cache_mla_krope.reshape(nbs, past, QK_ROPE),
                    ((0, 0), (0, 0), (ROPE_LANE0, LANES - ROPE_LANE0 - QK_ROPE)))
    ctx = (cache_mla_ckv.reshape(nbs, past, KV_LORA), krb_c,
           cache_diff_k.reshape(nbs, H_B, past, 2 * DIFF_HD), cache_diff_v.reshape(nbs, H_B, past, DIFF_VD))
    a_p = _attn0(qm_p, km_p, vm_p, dq_p, dk_p, dv_p, diff_lambda[0], gsub, None, None, seq_p, lam_init)
    a_s = _attn0(qm_s, km_s, vm_s, dq_s, dk_s, dv_s, diff_lambda[0], gsub, ctx, wukv, seq_s, lam_init)
    mtp, mts = mod_rows(0, TM_TAIL)
    wo0 = w_out_ab[0].astype(BF16)
    win, wout = w_ffn_in.astype(BF16), w_ffn_out.astype(BF16)
    xp = _tail(xp, a_p, mod, mtp, wo0, row2(g_ffn[0]), win, wout, 0, None, "tail0_ctx")
    xs = _tail(xs, a_s, mod, mts, wo0, row2(g_ffn[0]), win, wout, 0, None, "tail0_lat")

    w1, wg, bg = _prep_odd(w_in_c[0], b_gate_c[0])
    mrp, mrs = mod_rows(1, TM_PROJ1)
    gn = row2(g_mlstm[0])
    q_p, k_p, v_p, o_p, gc_p, gr_p = _proj1(xp, mod, mrp, row2(g_mix[1]), w1, wg, bg)
    q_s, k_s, v_s, o_s, gc_s, gr_s = _proj1(xs, mod, mrs, row2(g_mix[1]), w1, wg, bg)
    hs_p, c_new, nst, mst = _mlstm(q_p, k_p, v_p, o_p, gc_p, gr_p, gn, None, seq_p, True)
    state = (state_mlstm_C.reshape(nbs, 2, H_C, DK_C, DV_C),
             state_mlstm_n.reshape(nbs, 2, H_C // 2, 1, LANES),
             jnp.broadcast_to(state_mlstm_m.reshape(nbs, 2, H_C, 1, 1), (nbs, 2, H_C, 1, LANES)))
    (hs_s,) = _mlstm(q_s, k_s, v_s, o_s, gc_s, gr_s, gn, state, seq_s, False)
    mtp, mts = mod_rows(1, TM_TAIL)
    wo1 = w_out_c[0].astype(BF16)
    gfin = row2(g_final)
    yp = _tail(xp, hs_p, mod, mtp, wo1, row2(g_ffn[1]), win, wout, 1, gfin, "tail1_ctx")
    ys = _tail(xs, hs_s, mod, mts, wo1, row2(g_ffn[1]), win, wout, 1, gfin, "tail1_lat")

    return (yp.reshape(nbp, seq_p, D_MODEL), ys.reshape(nbs, seq_s, D_MODEL),
            ckv_new.reshape(nbp, 1, seq_p, KV_LORA), kr_new.reshape(nbp, 1, seq_p, QK_ROPE),
            dk_new.reshape(nbp, 1, H_B, seq_p, 2 * DIFF_HD), dv_new.reshape(nbp, 1, H_B, seq_p, DIFF_VD),
            c_new, nst.reshape(nbp, 1, 2, H_C, DK_C), mst[:, :, 0].reshape(nbp, 1, 2, H_C))
```

```python
import functools
import math

import jax
import jax.numpy as jnp
import numpy as np
from jax import lax
from jax.experimental import pallas as pl
from jax.experimental.pallas import tpu as pltpu

F32 = jnp.float32
BF16 = jnp.bfloat16

D_MODEL = 1024
DEPTH = 2
GRID_W = 64
ROPE_BASE = 10000.0
RMS_EPS = 1e-6
H_A = 8
QK_NOPE = 64
QK_ROPE = 32
V_HD_A = 64
Q_LORA = 256
KV_LORA = 128
H_B = 4
DIFF_HD = 64
DIFF_VD = 2 * DIFF_HD
H_C = 8
DK_C = 64
DV_C = D_MODEL // H_C
D_FF = -(-8 * D_MODEL // (3 * 256)) * 256

LANES = 128
HEAD_PAD = LANES
ROPE_LANE0 = QK_NOPE
N_MOD = 6
COND_ROWS = 16
N_GATE = 2 * H_C

TM_PROJ = 512
TM_PROJ1 = 1024
TQ_ATTN = 512
TM_TAIL = 1024
FF_CHUNK = 256
MLSTM_L = 128
ADA_TN = 1024
VMEM_LIMIT = 56 * 1024 * 1024

assert D_FF % LANES == 0 and FF_CHUNK % LANES == 0


def _dot(a, b):
    return jnp.dot(a, b, preferred_element_type=F32)


def _dot_nt(a, b):
    return lax.dot_general(a, b, (((1,), (1,)), ((), ())), preferred_element_type=F32)


def _dot_tn(a, b):
    return lax.dot_general(a, b, (((0,), (0,)), ((), ())), preferred_element_type=F32)


def _rms(x):
    return x * lax.rsqrt(jnp.mean(x * x, axis=-1, keepdims=True) + RMS_EPS)


def _modulate(x, g, shift, scale):
    return (_rms(x) * g) * (1.0 + scale) + shift


def _params(semantics):
    return pltpu.CompilerParams(dimension_semantics=semantics, vmem_limit_bytes=VMEM_LIMIT)


def _ada_kernel(cond_ref, w_ref, b_ref, o_ref):
    c = cond_ref[...]
    s = (c * jax.nn.sigmoid(c)).astype(BF16)
    o_ref[0] = _dot(s, w_ref[0].astype(BF16)) + b_ref[0]


def _ada(cond, w_ada, b_ada):
    n_out = w_ada.shape[-1]
    return pl.pallas_call(
        _ada_kernel,
        grid=(DEPTH, n_out // ADA_TN),
        in_specs=[
            pl.BlockSpec((COND_ROWS, D_MODEL), lambda l, n: (0, 0)),
            pl.BlockSpec((1, D_MODEL, ADA_TN), lambda l, n: (l, 0, n)),
            pl.BlockSpec((1, 1, ADA_TN), lambda l, n: (l, 0, n)),
        ],
        out_specs=pl.BlockSpec((1, COND_ROWS, ADA_TN), lambda l, n: (l, 0, n)),
        out_shape=jax.ShapeDtypeStruct((DEPTH, COND_ROWS, n_out), F32),
        compiler_params=_params(("arbitrary", "arbitrary")),
        name="ada",
    )(cond, w_ada, b_ada.reshape(DEPTH, 1, n_out))


def _rope(x, c, sa, sb, off):
    return x * c + pltpu.roll(x, LANES - off, 1) * sa + pltpu.roll(x, off, 1) * sb


def _proj0_kernel(*refs, rope, emit_cache):
    x_ref, mod_ref, g_ref, wa_ref, wd_ref, gq_ref, gkv_ref, wq_ref, wukv_ref = refs[:9]
    refs = refs[9:]
    if rope:
        cm_ref, sam_ref, sbm_ref, cd_ref, sad_ref, sbd_ref = refs[:6]
        refs = refs[6:]
    qm_ref, km_ref, vm_ref, dq_ref, dk_ref, dv_ref = refs[:6]
    refs = refs[6:]
    if emit_cache:
        ckvf_ref, krf_ref, dkf_ref, dvf_ref = refs

    h = _modulate(x_ref[...], g_ref[...], mod_ref[0, 0:1, :], mod_ref[0, 1:2, :]).astype(BF16)
    pa = _dot(h, wa_ref[...])
    pd = _dot(h, wd_ref[...])
    lane = lax.broadcasted_iota(jnp.int32, (pa.shape[0], LANES), 1)
    krb = jnp.where((lane >= ROPE_LANE0) & (lane < ROPE_LANE0 + QK_ROPE),
                    pltpu.roll(pa[:, Q_LORA + KV_LORA:Q_LORA + KV_LORA + LANES], ROPE_LANE0, 1), 0.0)
    cq = pa[:, 0:Q_LORA]
    ckv = _rms(pa[:, Q_LORA:Q_LORA + KV_LORA]) * gkv_ref[...]
    qa = _dot((_rms(cq) * gq_ref[...]).astype(BF16), wq_ref[...])
    kv = _dot(ckv.astype(BF16), wukv_ref[...])
    if emit_cache:
        ckvf_ref[...] = ckv
        krf_ref[...] = krb[:, ROPE_LANE0:ROPE_LANE0 + QK_ROPE]
    if rope:
        cm, sam, sbm = cm_ref[...], sam_ref[...], sbm_ref[...]
        cd, sad, sbd = cd_ref[...], sad_ref[...], sbd_ref[...]
        krb = _rope(krb, cm, sam, sbm, QK_ROPE // 4)
    for hd in range(H_A):
        sl = slice(hd * HEAD_PAD, (hd + 1) * HEAD_PAD)
        qh = qa[:, sl]
        if rope:
            qh = _rope(qh, cm, sam, sbm, QK_ROPE // 4)
        qm_ref[:, sl] = qh.astype(BF16)
        km_ref[:, sl] = (kv[:, sl] + krb).astype(BF16)
    vm_ref[...] = kv[:, H_A * HEAD_PAD:].astype(BF16)
    for hd in range(H_B):
        sl = slice(hd * LANES, (hd + 1) * LANES)
        dq = pd[:, hd * LANES:(hd + 1) * LANES]
        dk = pd[:, (H_B + hd) * LANES:(H_B + hd + 1) * LANES]
        dv = pd[:, (2 * H_B + hd) * LANES:(2 * H_B + hd + 1) * LANES]
        if emit_cache:
            seq = dkf_ref.shape[2]
            for bi in range(dkf_ref.shape[0]):
                dkf_ref[bi, hd] = dk[bi * seq:(bi + 1) * seq]
                dvf_ref[bi, hd] = dv[bi * seq:(bi + 1) * seq]
        if rope:
            dq = _rope(dq, cd, sad, sbd, DIFF_HD // 4)
            dk = _rope(dk, cd, sad, sbd, DIFF_HD // 4)
        dq_ref[:, sl] = (dq * (DIFF_HD ** -0.5)).astype(BF16)
        dk_ref[:, sl] = dk.astype(BF16)
        dv_ref[:, sl] = dv.astype(BF16)


def _proj0(x, mod, mod_row, g, w_ab, wd, gq, gkv, wq, wukv, rope_tabs, seq):
    n = x.shape[0]
    tm = TM_PROJ
    rope = rope_tabs is not None
    emit_cache = not rope
    tiles_per_seq = max(seq // tm, 1)
    full = lambda shape: pl.BlockSpec(shape, lambda i: (0,) * len(shape))
    in_specs = [
        pl.BlockSpec((tm, D_MODEL), lambda i: (i, 0)),
        pl.BlockSpec((1, N_MOD, D_MODEL), lambda i: (mod_row(i), 0, 0)),
        full((1, D_MODEL)), full((D_MODEL, Q_LORA + KV_LORA + LANES)), full(wd.shape),
        full((1, Q_LORA)), full((1, KV_LORA)), full(wq.shape), full(wukv.shape),
    ]
    args = [x, mod, g, w_ab, wd, gq, gkv, wq, wukv]
    if rope:
        in_specs += [pl.BlockSpec((tm, LANES), lambda i: (i % tiles_per_seq, 0))] * 6
        args += list(rope_tabs)
    row = lambda w: pl.BlockSpec((tm, w), lambda i: (i, 0))
    out_specs = [row(1024), row(1024), row(1024), row(512), row(512), row(512)]
    out_shape = [jax.ShapeDtypeStruct((n, w), BF16) for w in (1024, 1024, 1024, 512, 512, 512)]
    if emit_cache:
        assert tm % seq == 0
        cache = pl.BlockSpec((tm // seq, H_B, seq, LANES), lambda i: (i, 0, 0, 0))
        out_specs += [row(KV_LORA), row(QK_ROPE), cache, cache]
        out_shape += [jax.ShapeDtypeStruct((n, KV_LORA), F32), jax.ShapeDtypeStruct((n, QK_ROPE), F32),
                      jax.ShapeDtypeStruct((n // seq, H_B, seq, LANES), F32),
                      jax.ShapeDtypeStruct((n // seq, H_B, seq, LANES), F32)]
    return pl.pallas_call(
        functools.partial(_proj0_kernel, rope=rope, emit_cache=emit_cache),
        grid=(n // tm,),
        in_specs=in_specs, out_specs=out_specs, out_shape=out_shape,
        compiler_params=_params(("arbitrary",)),
        name="proj0_rope" if rope else "proj0_ctx",
    )(*args)


def _attn_kernel(*refs, has_ctx, lam_init):
    q_ref, dq_ref, k_ref, v_ref, dk_ref, dv_ref, lamv_ref, gsub_ref = refs[:8]
    refs = refs[8:]
    if has_ctx:
        ckvc_ref, krc_ref, dkc_ref, dvc_ref, wukv_ref = refs[:5]
        refs = refs[5:]
    o_ref, kx_s, vx_s, dkx_s, dvx_s = refs
    seq = k_ref.shape[0]
    ktot = kx_s.shape[0]

    @pl.when(pl.program_id(1) == 0)
    def _():
        kx_s[0:seq, :] = k_ref[...]
        dkx_s[0:seq, :] = dk_ref[...]
        ones_own = jnp.ones((seq, LANES), BF16)
        for hd in range(H_A):
            vx_s[0:seq, 2 * hd * LANES:(2 * hd + 1) * LANES] = v_ref[:, hd * LANES:(hd + 1) * LANES]
            vx_s[0:seq, (2 * hd + 1) * LANES:(2 * hd + 2) * LANES] = ones_own
        for hd in range(H_B):
            dvx_s[0:seq, 2 * hd * LANES:(2 * hd + 1) * LANES] = dv_ref[:, hd * LANES:(hd + 1) * LANES]
            dvx_s[0:seq, (2 * hd + 1) * LANES:(2 * hd + 2) * LANES] = ones_own
        if has_ctx:
            kv = _dot(ckvc_ref[0].astype(BF16), wukv_ref[...])
            krb = krc_ref[0]
            ones_ctx = jnp.ones((ktot - seq, LANES), BF16)
            for hd in range(H_A):
                sl = slice(hd * HEAD_PAD, (hd + 1) * HEAD_PAD)
                kx_s[seq:ktot, sl] = (kv[:, sl] + krb).astype(BF16)
                vx_s[seq:ktot, 2 * hd * LANES:(2 * hd + 1) * LANES] = (
                    kv[:, (H_A + hd) * HEAD_PAD:(H_A + hd + 1) * HEAD_PAD].astype(BF16))
                vx_s[seq:ktot, (2 * hd + 1) * LANES:(2 * hd + 2) * LANES] = ones_ctx
            for hd in range(H_B):
                sl = slice(hd * LANES, (hd + 1) * LANES)
                dkx_s[seq:ktot, sl] = dkc_ref[0, hd].astype(BF16)
                dvx_s[seq:ktot, 2 * hd * LANES:(2 * hd + 1) * LANES] = dvc_ref[0, hd].astype(BF16)
                dvx_s[seq:ktot, (2 * hd + 1) * LANES:(2 * hd + 2) * LANES] = ones_ctx

    log2e = 1.0 / math.log(2.0)
    tq = dq_ref.shape[0]
    lo = lax.broadcasted_iota(jnp.int32, (tq, LANES), 1) < DIFF_HD

    jobs = []
    for hd in range(H_A):
        sl = slice(hd * HEAD_PAD, (hd + 1) * HEAD_PAD)
        jobs.append((lambda sl=sl: q_ref[:, sl], (kx_s, sl), (vx_s, hd), (QK_NOPE + QK_ROPE) ** -0.5 * log2e))
    for hd in range(H_B):
        sl = slice(hd * LANES, (hd + 1) * LANES)
        for part in range(2):
            def qfn(sl=sl, part=part):
                dq = dq_ref[:, sl].astype(F32)
                return (jnp.where(lo, dq, 0.0) if part == 0 else jnp.where(lo, 0.0, dq)).astype(BF16)
            jobs.append((qfn, (dkx_s, sl), (dvx_s, hd), log2e))

    def scores(job):
        qfn, (kref, sl), _, _ = job
        return _dot_nt(qfn(), kref[:, sl])

    def finish(s, job):
        _, _, (vref, hd), c = job
        m = jnp.max(s, axis=-1, keepdims=True)
        e = jnp.exp2((s - m) * c).astype(BF16)
        res = _dot(e, vref[:, 2 * hd * LANES:(2 * hd + 2) * LANES])
        return res[:, 0:LANES] * (1.0 / res[:, LANES:2 * LANES])

    outs = []
    s_next = scores(jobs[0])
    for i, job in enumerate(jobs):
        s_cur = s_next
        if i + 1 < len(jobs):
            s_next = scores(jobs[i + 1])
        outs.append(finish(s_cur, job))

    for j in range(H_A // 2):
        o_ref[:, j * LANES:(j + 1) * LANES] = (outs[2 * j] + outs[2 * j + 1]).astype(BF16)
    lv = lamv_ref[...]
    lam = (jnp.exp(jnp.sum(lv[0:1] * lv[1:2], axis=-1, keepdims=True))
           - jnp.exp(jnp.sum(lv[2:3] * lv[3:4], axis=-1, keepdims=True)) + lam_init)
    for hd in range(H_B):
        acc = outs[H_A + 2 * hd] - lam * outs[H_A + 2 * hd + 1]
        ob = (_rms(acc) * gsub_ref[...]) * (1.0 - lam_init)
        o_ref[:, H_A * V_HD_A + hd * LANES:H_A * V_HD_A + (hd + 1) * LANES] = ob.astype(BF16)


def _attn0(qm, km, vm, dq, dk, dv, lamv, gsub, ctx, wukv, seq, lam_init):
    n = qm.shape[0]
    nb = n // seq
    tq = min(TQ_ATTN, seq)
    has_ctx = ctx is not None
    full = lambda shape: pl.BlockSpec(shape, lambda b, t: (0,) * len(shape))
    qrow = lambda w: pl.BlockSpec((tq, w), lambda b, t: (b * (seq // tq) + t, 0))
    krow = lambda w: pl.BlockSpec((seq, w), lambda b, t: (b, 0))
    in_specs = [qrow(1024), qrow(512), krow(1024), krow(1024), krow(512), krow(512),
                full((4, DIFF_HD)), full((1, DIFF_VD))]
    args = [qm, dq, km, vm, dk, dv, lamv, gsub]
    past = 0
    if has_ctx:
        ckv_c, krb_c, dk_c, dv_c = ctx
        past = ckv_c.shape[1]
        in_specs += [pl.BlockSpec((1, past, LANES), lambda b, t: (b, 0, 0)),
                     pl.BlockSpec((1, past, LANES), lambda b, t: (b, 0, 0)),
                     pl.BlockSpec((1, H_B, past, LANES), lambda b, t: (b, 0, 0, 0)),
                     pl.BlockSpec((1, H_B, past, LANES), lambda b, t: (b, 0, 0, 0)),
                     full(wukv.shape)]
        args += [ckv_c, krb_c, dk_c, dv_c, wukv]
    ktot = seq + past
    scratch = [pltpu.VMEM((ktot, H_A * HEAD_PAD), BF16), pltpu.VMEM((ktot, 2 * H_A * LANES), BF16),
               pltpu.VMEM((ktot, H_B * LANES), BF16), pltpu.VMEM((ktot, 2 * H_B * LANES), BF16)]
    return pl.pallas_call(
        functools.partial(_attn_kernel, has_ctx=has_ctx, lam_init=lam_init),
        grid=(nb, seq // tq),
        in_specs=in_specs,
        out_specs=pl.BlockSpec((tq, D_MODEL), lambda b, t: (b * (seq // tq) + t, 0)),
        out_shape=jax.ShapeDtypeStruct((n, D_MODEL), BF16),
        scratch_shapes=scratch,
        compiler_params=_params(("arbitrary", "arbitrary")),
        name="attn0_lat" if has_ctx else "attn0_ctx",
    )(*args)


def _tail_kernel(*refs, final):
    x_ref, a_ref, mod_ref, wo_ref, gf_ref, win_ref, wout_ref = refs[:7]
    if final:
        gfin_ref, o_ref = refs[7:]
    else:
        (o_ref,) = refs[7:]
    x1 = x_ref[...] + mod_ref[0, 2:3, :] * _dot(a_ref[...], wo_ref[...])
    h = _modulate(x1, gf_ref[...], mod_ref[0, 3:4, :], mod_ref[0, 4:5, :]).astype(BF16)
    bounds = list(range(0, D_FF, FF_CHUNK)) + [D_FF]
    acc = None
    for lo, hi in zip(bounds[:-1], bounds[1:]):
        a = _dot(h, win_ref[0, :, lo:hi])
        b = _dot(h, win_ref[0, :, D_FF + lo:D_FF + hi])
        act = ((a * jax.nn.sigmoid(a)) * b).astype(BF16)
        part = _dot(act, wout_ref[0, lo:hi, :])
        acc = part if acc is None else acc + part
    x2 = x1 + mod_ref[0, 5:6, :] * acc
    if final:
        x2 = _rms(x2) * gfin_ref[...]
    o_ref[...] = x2


def _tail(x, a, mod, mod_row, wo, gf, win, wout, layer, gfin, name):
    n = x.shape[0]
    tm = TM_TAIL
    final = gfin is not None
    full = lambda shape: pl.BlockSpec(shape, lambda i: (0,) * len(shape))
    resident = lambda shape: pl.BlockSpec((1,) + shape[1:], lambda i: (layer, 0, 0),
                                          pipeline_mode=pl.Buffered(1))
    in_specs = [
        pl.BlockSpec((tm, D_MODEL), lambda i: (i, 0)),
        pl.BlockSpec((tm, D_MODEL), lambda i: (i, 0)),
        pl.BlockSpec((1, N_MOD, D_MODEL), lambda i: (mod_row(i), 0, 0)),
        full(wo.shape), full((1, D_MODEL)), resident(win.shape), resident(wout.shape),
    ]
    args = [x, a, mod, wo, gf, win, wout]
    if final:
        in_specs.append(full((1, D_MODEL)))
        args.append(gfin)
    return pl.pallas_call(
        functools.partial(_tail_kernel, final=final),
        grid=(n // tm,),
        in_specs=in_specs,
        out_specs=pl.BlockSpec((tm, D_MODEL), lambda i: (i, 0)),
        out_shape=jax.ShapeDtypeStruct((n, D_MODEL), F32),
        compiler_params=_params(("arbitrary",)),
        name=name,
    )(*args)


def _split3(x):
    hi = x.astype(BF16)
    r1 = x - hi.astype(F32)
    mid = r1.astype(BF16)
    lo = (r1 - mid.astype(F32)).astype(BF16)
    return hi, mid, lo


def _proj1_kernel(x_ref, mod_ref, g_ref, w1_ref, wg_ref, bg_ref, q_ref, k_ref, v_ref, o_ref, gc_ref, gr_ref):
    tm = x_ref.shape[0]
    L = MLSTM_L
    hk = H_C * DK_C
    hv = H_C * DV_C
    h = _modulate(x_ref[...], g_ref[...], mod_ref[0, 0:1, :], mod_ref[0, 1:2, :]).astype(BF16)
    gates = _dot(h, wg_ref[...]) + bg_ref[...]
    proj = _dot(h, w1_ref[...])
    for blk in range(hk // LANES):
        sl = slice(blk * LANES, (blk + 1) * LANES)
        q_ref[sl, :] = jnp.transpose(proj[:, sl]).astype(BF16)
    for blk in range(hv // LANES):
        sl = slice(blk * LANES, (blk + 1) * LANES)
        v_ref[sl, :] = jnp.transpose(proj[:, 2 * hk + blk * LANES:2 * hk + (blk + 1) * LANES]).astype(BF16)
    k_ref[...] = proj[:, hk:2 * hk] * (DK_C ** -0.5)
    o_ref[...] = proj[:, 2 * hk + hv:2 * hk + 2 * hv]
    lf = jnp.minimum(gates, 0.0) - jnp.log1p(jnp.exp(-jnp.abs(gates)))
    r = lax.broadcasted_iota(jnp.int32, (L, L), 0)
    c = lax.broadcasted_iota(jnp.int32, (L, L), 1)
    pre = jnp.where(c <= r, 1.0, 0.0).astype(BF16)
    suf = jnp.where(c >= r, 1.0, 0.0).astype(BF16)
    parts = jnp.concatenate(_split3(lf), axis=1)
    fold = lambda t: t[:, 0:LANES] + t[:, LANES:2 * LANES] + t[:, 2 * LANES:3 * LANES]
    chunks = [parts[ck * L:(ck + 1) * L] for ck in range(tm // L)]
    lane = lax.broadcasted_iota(jnp.int32, (tm, LANES), 1)
    b_sum = jnp.where((lane % 4) >= 2,
                      jnp.concatenate([fold(_dot(suf, p)) for p in chunks], axis=0),
                      jnp.concatenate([fold(_dot(pre, p)) for p in chunks], axis=0))
    b = pltpu.roll(b_sum, LANES - N_GATE, 1)
    u = gates - b
    pos = lax.broadcasted_iota(jnp.int32, (tm, LANES), 0) % L
    bwd = (lane % 4) >= 2
    cm = u
    step = 1
    while step < L:
        below = jnp.where(pos >= step, pltpu.roll(cm, step, 0), -jnp.inf)
        above = jnp.where(pos < L - step, pltpu.roll(cm, tm - step, 0), -jnp.inf)
        cm = jnp.maximum(cm, jnp.where(bwd, above, below))
        step *= 2
    low = lane < N_GATE
    packed = (jnp.where(low, b, 0.0) + pltpu.roll(jnp.where(low, cm, 0.0), N_GATE, 1)
              + pltpu.roll(jnp.where(low, u, 0.0), 2 * N_GATE, 1))
    gc_ref[...] = packed
    gr_ref[...] = jnp.transpose(packed)[0:3 * N_GATE]


def _proj1(x, mod, mod_row, g, w1, wg, bg):
    n = x.shape[0]
    tm = TM_PROJ1
    assert tm % MLSTM_L == 0
    full = lambda shape: pl.BlockSpec(shape, lambda i: (0,) * len(shape))
    row = lambda w: pl.BlockSpec((tm, w), lambda i: (i, 0))
    col = lambda w: pl.BlockSpec((w, tm), lambda i: (0, i))
    hk, hv = H_C * DK_C, H_C * DV_C
    return pl.pallas_call(
        _proj1_kernel,
        grid=(n // tm,),
        in_specs=[row(D_MODEL), pl.BlockSpec((1, N_MOD, D_MODEL), lambda i: (mod_row(i), 0, 0)),
                  full((1, D_MODEL)), full((D_MODEL, 2 * hk + 2 * hv)), full(wg.shape), full((1, LANES))],
        out_specs=[col(hk), row(hk), col(hv), row(hv), row(LANES),
                   pl.BlockSpec((3 * N_GATE, tm), lambda i: (0, i))],
        out_shape=[jax.ShapeDtypeStruct((hk, n), BF16), jax.ShapeDtypeStruct((n, hk), F32),
                   jax.ShapeDtypeStruct((hv, n), BF16), jax.ShapeDtypeStruct((n, hv), F32),
                   jax.ShapeDtypeStruct((n, LANES), F32), jax.ShapeDtypeStruct((3 * N_GATE, n), F32)],
        compiler_params=_params(("arbitrary",)),
        name="proj1",
    )(x, mod, g, w1, wg, bg)


def _chain(j, d, hh):
    return (2 * j + d) * 2 + hh


def _mlstm_kernel(*refs, has_state, emit_state):
    q_ref, k_ref, v_ref, o_ref, gc_ref, gr_ref, gn_ref = refs[:7]
    refs = refs[7:]
    if has_state:
        c0_ref, n0_ref, m0_ref = refs[:3]
        refs = refs[3:]
    hs_ref = refs[0]
    refs = refs[1:]
    if emit_state:
        cf_ref, nst_ref, mst_ref = refs[:3]
        refs = refs[3:]
    cx_s, m_s, h_s = refs

    L = MLSTM_L
    seq = k_ref.shape[0]
    nc = seq // L
    npair = H_C // 2
    lane = lax.broadcasted_iota(jnp.int32, (1, LANES), 1)
    head_mask = [lane < DK_C, lane >= DK_C]
    ri = lax.broadcasted_iota(jnp.int32, (L, L), 0)
    ci = lax.broadcasted_iota(jnp.int32, (L, L), 1)
    causal = [ri <= ci, ri >= ci]
    ones_blk = jnp.ones((DV_C, L), BF16)
    chains = [(j, d, hh) for d in range(2) for j in range(npair) for hh in range(2)]

    h_s[...] = jnp.zeros_like(h_s)
    for j, d, hh in chains:
        ch = _chain(j, d, hh)
        if has_state:
            zpad = jnp.zeros((DK_C, DV_C), F32)
            c0 = c0_ref[0, d, 2 * j + hh]
            c0 = jnp.concatenate([c0, zpad] if hh == 0 else [zpad, c0], axis=0)
            cx_s[ch, 0:DV_C, :] = jnp.transpose(c0)
            n_row = jnp.where(head_mask[hh], n0_ref[0, d, j], 0.0)
            cx_s[ch, DV_C:2 * DV_C, :] = jnp.broadcast_to(n_row, (DV_C, LANES))
            m_s[ch] = m0_ref[0, d, 2 * j + hh]
        else:
            cx_s[ch] = jnp.zeros((2 * DV_C, LANES), F32)
            m_s[ch] = jnp.zeros((1, LANES), F32)

    def chunk_step(i, carry):
        sl = [pl.ds(pl.multiple_of(i * L, L), L), pl.ds(pl.multiple_of((nc - 1 - i) * L, L), L)]
        gcol = [gc_ref[sl[d], :] for d in range(2)]
        grow = [gr_ref[:, sl[d]] for d in range(2)]
        kpair = {(d, j): k_ref[sl[d], j * LANES:(j + 1) * LANES] for d in range(2) for j in range(npair)}
        kbf = {key: kk.astype(BF16) for key, kk in kpair.items()}
        zq = jnp.zeros((DK_C, L), BF16)
        qt = {}
        for d in range(2):
            for j in range(npair):
                qt[(j, d, 0)] = jnp.concatenate([q_ref[j * LANES:j * LANES + DK_C, sl[d]], zq], axis=0)
                qt[(j, d, 1)] = jnp.concatenate([zq, q_ref[j * LANES + DK_C:(j + 1) * LANES, sl[d]]], axis=0)
        st, u_bc, vt, row = {}, {}, {}, {}
        for j, d, hh in chains:
            key = (j, d, hh)
            idx = 4 * j + 2 * d + hh
            edge = L - 1 if d == 0 else 0
            b_row = grow[d][idx:idx + 1, :]
            cm_row = grow[d][N_GATE + idx:N_GATE + idx + 1, :]
            u_col = gcol[d][:, 2 * N_GATE + idx:2 * N_GATE + idx + 1]
            m_prev = m_s[_chain(j, d, hh)][:, 0:1]
            row[key] = (b_row, cm_row, b_row[:, edge:edge + 1], cm_row[:, edge:edge + 1], m_prev)
            u_bc[key] = jnp.broadcast_to(u_col, (L, L))
            vt[key] = v_ref[(2 * j + hh) * DV_C:(2 * j + hh + 1) * DV_C, sl[d]]
            st[key] = _dot(kbf[(d, j)], qt[key])
        for j, d, hh in chains:
            key = (j, d, hh)
            ch = _chain(j, d, hh)
            b_row, cm_row, g_tot, cm_last, m_prev = row[key]
            head = slice((2 * j + hh) * DV_C, (2 * j + hh + 1) * DV_C)
            cx = cx_s[ch]
            big_m = jnp.maximum(m_prev, cm_row)
            s = st[key] * jnp.exp(jnp.where(causal[d], u_bc[key] - big_m, -jnp.inf))
            inter = jnp.exp(m_prev - big_m)
            qcx = _dot(cx.astype(BF16), qt[key])
            num = _dot(vt[key], s.astype(BF16)) + inter * qcx[0:DV_C, :]
            den = jnp.sum(s, axis=0, keepdims=True) + inter * qcx[DV_C:DV_C + 1, :]
            hval = num * (1.0 / jnp.maximum(jnp.abs(den), jnp.exp(-(b_row + big_m))))
            h_s[head, sl[d]] = h_s[head, sl[d]] + hval

            m_top = jnp.maximum(m_prev, cm_last)
            kw = kpair[(d, j)] * jnp.exp(u_bc[key] - m_top)
            vx = jnp.concatenate([vt[key], ones_blk], axis=0)
            cx_s[ch] = jnp.exp(m_prev - m_top) * cx + _dot(vx, kw.astype(BF16))
            m_s[ch] = jnp.broadcast_to(g_tot + m_top, (1, LANES))
        return carry

    lax.fori_loop(0, nc, chunk_step, 0)

    gn_col = jnp.broadcast_to(gn_ref[...], (DV_C, LANES))
    gn_col = jnp.concatenate([gn_col] * (seq // LANES), axis=1)
    for hd in range(H_C):
        sl = slice(hd * DV_C, (hd + 1) * DV_C)
        ht = h_s[sl, :]
        ms = jnp.mean(ht * ht, axis=0, keepdims=True)
        y = jnp.transpose((ht * lax.rsqrt(ms + RMS_EPS)) * gn_col) * jax.nn.sigmoid(o_ref[:, sl])
        hs_ref[:, sl] = y.astype(BF16)
    if emit_state:
        for j, d, hh in chains:
            ch = _chain(j, d, hh)
            cf_ref[0, 0, d, 2 * j + hh] = jnp.transpose(cx_s[ch, 0:DV_C, :])[hh * DK_C:(hh + 1) * DK_C, :]
            mst_ref[0, d * H_C + 2 * j + hh:d * H_C + 2 * j + hh + 1, :] = m_s[ch]
        for j in range(npair):
            for d in range(2):
                nst_ref[0, d * npair + j:d * npair + j + 1, :] = jnp.where(
                    head_mask[0], cx_s[_chain(j, d, 0), DV_C:DV_C + 1, :], cx_s[_chain(j, d, 1), DV_C:DV_C + 1, :])


def _mlstm(q, k, v, o, gc, gr, gn, state, seq, emit_state):
    n = k.shape[0]
    nb = n // seq
    npair = H_C // 2
    has_state = state is not None
    hk, hv = H_C * DK_C, H_C * DV_C
    in_specs = [
        pl.BlockSpec((hk, seq), lambda b: (0, b)),
        pl.BlockSpec((seq, hk), lambda b: (b, 0)),
        pl.BlockSpec((hv, seq), lambda b: (0, b)),
        pl.BlockSpec((seq, hv), lambda b: (b, 0)),
        pl.BlockSpec((seq, LANES), lambda b: (b, 0)),
        pl.BlockSpec((3 * N_GATE, seq), lambda b: (0, b)),
        pl.BlockSpec((DV_C, 1), lambda b: (0, 0)),
    ]
    args = [q, k, v, o, gc, gr, gn]
    if has_state:
        c0, n0, m0 = state
        in_specs += [pl.BlockSpec((1, 2, H_C, DK_C, DV_C), lambda b: (b, 0, 0, 0, 0)),
                     pl.BlockSpec((1, 2, npair, 1, LANES), lambda b: (b, 0, 0, 0, 0)),
                     pl.BlockSpec((1, 2, H_C, 1, LANES), lambda b: (b, 0, 0, 0, 0))]
        args += [c0, n0, m0]
    out_specs = [pl.BlockSpec((seq, hv), lambda b: (b, 0))]
    out_shape = [jax.ShapeDtypeStruct((n, hv), BF16)]
    if emit_state:
        out_specs += [pl.BlockSpec((1, 1, 2, H_C, DK_C, DV_C), lambda b: (b, 0, 0, 0, 0, 0)),
                      pl.BlockSpec((1, 2 * npair, LANES), lambda b: (b, 0, 0)),
                      pl.BlockSpec((1, 2 * H_C, LANES), lambda b: (b, 0, 0))]
        out_shape += [jax.ShapeDtypeStruct((nb, 1, 2, H_C, DK_C, DV_C), F32),
                      jax.ShapeDtypeStruct((nb, 2 * npair, LANES), F32),
                      jax.ShapeDtypeStruct((nb, 2 * H_C, LANES), F32)]
    n_chain = 2 * H_C
    return pl.pallas_call(
        functools.partial(_mlstm_kernel, has_state=has_state, emit_state=emit_state),
        grid=(nb,),
        in_specs=in_specs, out_specs=out_specs, out_shape=out_shape,
        scratch_shapes=[pltpu.VMEM((n_chain, 2 * DV_C, LANES), F32), pltpu.VMEM((n_chain, 1, LANES), F32),
                        pltpu.VMEM((hv, seq), F32)],
        compiler_params=_params(("arbitrary",)),
        name="mlstm_lat" if has_state else "mlstm_ctx",
    )(*args)


def _rope_tables(n_tok):
    t = np.arange(n_tok)
    rows = (t // GRID_W).astype(np.float64)
    cols = (t % GRID_W).astype(np.float64)

    def axis_tabs(width, lane0):
        half = width // 2
        quarter = half // 2
        freqs = np.power(ROPE_BASE, -np.arange(quarter, dtype=np.float64) / quarter)
        c = np.ones((n_tok, LANES))
        sa = np.zeros((n_tok, LANES))
        sb = np.zeros((n_tok, LANES))
        for g, pos in enumerate((rows, cols)):
            ang = pos[:, None] * freqs[None, :]
            a0 = lane0 + g * half
            c[:, a0:a0 + quarter] = np.cos(ang)
            c[:, a0 + quarter:a0 + half] = np.cos(ang)
            sa[:, a0:a0 + quarter] = -np.sin(ang)
            sb[:, a0 + quarter:a0 + half] = np.sin(ang)
        return c, sa, sb

    cm, sam, sbm = axis_tabs(QK_ROPE, ROPE_LANE0)
    c0, sa0, sb0 = axis_tabs(DIFF_HD, 0)
    c1, sa1, sb1 = axis_tabs(DIFF_HD, DIFF_HD)
    cd = np.where(np.arange(LANES)[None, :] < DIFF_HD, c0, c1)
    return tuple(jnp.asarray(a, F32) for a in (cm, sam, sbm, cd, sa0 + sa1, sb0 + sb1))


def _prep_even(w_in_ab, w_uq, w_ukv):
    c2 = Q_LORA + KV_LORA
    w_ab = w_in_ab.astype(BF16)
    wd = w_ab[:, c2 + QK_ROPE:]
    wq = jnp.pad(w_uq.reshape(Q_LORA, H_A, QK_NOPE + QK_ROPE),
                 ((0, 0), (0, 0), (0, HEAD_PAD - QK_NOPE - QK_ROPE))).reshape(Q_LORA, H_A * HEAD_PAD)
    kvw = w_ukv.reshape(KV_LORA, H_A, QK_NOPE + V_HD_A)
    kpad = jnp.pad(kvw[..., :QK_NOPE], ((0, 0), (0, 0), (0, HEAD_PAD - QK_NOPE)))
    vw = kvw[..., QK_NOPE:]
    zv = jnp.zeros_like(vw)
    odd = (jnp.arange(H_A) % 2 == 1)[None, :, None]
    vpad = jnp.where(odd, jnp.concatenate([zv, vw], -1), jnp.concatenate([vw, zv], -1))
    wukv = jnp.concatenate([kpad.reshape(KV_LORA, -1), vpad.reshape(KV_LORA, -1)], axis=1)
    return (w_ab, wd), wq.astype(BF16), wukv.astype(BF16)


def _gate_order(g):
    lead = g.shape[:-1]
    g = g.reshape(lead + (2, 2, H_C // 2, 2))
    perm = tuple(range(len(lead))) + tuple(len(lead) + a for a in (1, 2, 0, 3))
    return g.transpose(perm).reshape(lead + (4 * H_C,))


def _prep_odd(w_in_c, b_gate_c):
    ng = 4 * H_C
    base = w_in_c.shape[1] - ng
    w1 = w_in_c.astype(BF16)
    wg = jnp.pad(_gate_order(w1[:, base:]), ((0, 0), (0, LANES - ng)))
    bg = jnp.pad(_gate_order(b_gate_c), (0, LANES - ng)).reshape(1, LANES)
    return w1, wg, bg


def kernel(x_prompt, x_sample, cache_mla_ckv, cache_mla_krope, cache_diff_k, cache_diff_v,
           state_mlstm_C, state_mlstm_n, state_mlstm_m, c, c_ctx,
           w_ada, b_ada, g_mix, g_ffn, w_ffn_in, w_ffn_out,
           w_in_ab, g_q_lora, g_kv_lora, w_uq, w_ukv, diff_lambda, g_diff_subln, w_out_ab,
           w_in_c, b_gate_c, g_mlstm, w_out_c, g_final):
    nbp, seq_p, _ = x_prompt.shape
    nbs, seq_s, _ = x_sample.shape
    past = cache_mla_ckv.shape[2]
    assert DEPTH == 2 and 1 + nbs <= COND_ROWS
    assert cache_mla_ckv.shape[1] == 1 and state_mlstm_C.shape[1] == 1

    cond = jnp.concatenate([c_ctx[None], c, jnp.zeros((COND_ROWS - 1 - nbs, D_MODEL), F32)], axis=0)
    mod = _ada(cond, w_ada, b_ada).reshape(DEPTH * COND_ROWS, N_MOD, D_MODEL)

    xp = x_prompt.reshape(nbp * seq_p, D_MODEL)
    xs = x_sample.reshape(nbs * seq_s, D_MODEL)
    row2 = lambda v: v.reshape(1, -1)

    def mod_rows(layer, tm):
        prompt = lambda i: layer * COND_ROWS
        sample = lambda i: layer * COND_ROWS + 1 + i // (seq_s // tm)
        return prompt, sample

    lam_init = 0.8 - 0.6 * math.exp(-0.3 * 0)
    w0, wq, wukv = _prep_even(w_in_ab[0], w_uq[0], w_ukv[0])
    rope_tabs = _rope_tables(seq_s)
    mrp, mrs = mod_rows(0, TM_PROJ)
    gq, gkv = row2(g_q_lora[0]), row2(g_kv_lora[0])
    outs_p = _proj0(xp, mod, mrp, row2(g_mix[0]), *w0, gq, gkv, wq, wukv, None, seq_p)
    outs_s = _proj0(xs, mod, mrs, row2(g_mix[0]), *w0, gq, gkv, wq, wukv, rope_tabs, seq_s)
    qm_p, km_p, vm_p, dq_p, dk_p, dv_p, ckv_new, kr_new, dk_new, dv_new = outs_p
    qm_s, km_s, vm_s, dq_s, dk_s, dv_s = outs_s
    gsub = row2(g_diff_subln[0])
    krb_c = jnp.pad(cache_mla_krope.reshape(nbs, past, QK_ROPE),
                    ((0, 0), (0, 0), (ROPE_LANE0, LANES - ROPE_LANE0 - QK_ROPE)))
    ctx = (cache_mla_ckv.reshape(nbs, past, KV_LORA), krb_c,
           cache_diff_k.reshape(nbs, H_B, past, 2 * DIFF_HD), cache_diff_v.reshape(nbs, H_B, past, DIFF_VD))
    a_p = _attn0(qm_p, km_p, vm_p, dq_p, dk_p, dv_p, diff_lambda[0], gsub, None, None, seq_p, lam_init)
    a_s = _attn0(qm_s, km_s, vm_s, dq_s, dk_s, dv_s, diff_lambda[0], gsub, ctx, wukv, seq_s, lam_init)
    mtp, mts = mod_rows(0, TM_TAIL)
    wo0 = w_out_ab[0].astype(BF16)
    win, wout = w_ffn_in.astype(BF16), w_ffn_out.astype(BF16)
    xp = _tail(xp, a_p, mod, mtp, wo0, row2(g_ffn[0]), win, wout, 0, None, "tail0_ctx")
    xs = _tail(xs, a_s, mod, mts, wo0, row2(g_ffn[0]), win, wout, 0, None, "tail0_lat")

    w1, wg, bg = _prep_odd(w_in_c[0], b_gate_c[0])
    mrp, mrs = mod_rows(1, TM_PROJ1)
    gn = g_mlstm[0].reshape(DV_C, 1)
    q_p, k_p, v_p, o_p, gc_p, gr_p = _proj1(xp, mod, mrp, row2(g_mix[1]), w1, wg, bg)
    q_s, k_s, v_s, o_s, gc_s, gr_s = _proj1(xs, mod, mrs, row2(g_mix[1]), w1, wg, bg)
    hs_p, c_new, nst, mst = _mlstm(q_p, k_p, v_p, o_p, gc_p, gr_p, gn, None, seq_p, True)
    state = (state_mlstm_C.reshape(nbs, 2, H_C, DK_C, DV_C),
             state_mlstm_n.reshape(nbs, 2, H_C // 2, 1, LANES),
             jnp.broadcast_to(state_mlstm_m.reshape(nbs, 2, H_C, 1, 1), (nbs, 2, H_C, 1, LANES)))
    (hs_s,) = _mlstm(q_s, k_s, v_s, o_s, gc_s, gr_s, gn, state, seq_s, False)
    mtp, mts = mod_rows(1, TM_TAIL)
    wo1 = w_out_c[0].astype(BF16)
    gfin = row2(g_final)
    yp = _tail(xp, hs_p, mod, mtp, wo1, row2(g_ffn[1]), win, wout, 1, gfin, "tail1_ctx")
    ys = _tail(xs, hs_s, mod, mts, wo1, row2(g_ffn[1]), win, wout, 1, gfin, "tail1_lat")

    return (yp.reshape(nbp, seq_p, D_MODEL), ys.reshape(nbs, seq_s, D_MODEL),
            ckv_new.reshape(nbp, 1, seq_p, KV_LORA), kr_new.reshape(nbp, 1, seq_p, QK_ROPE),
            dk_new.reshape(nbp, 1, H_B, seq_p, 2 * DIFF_HD), dv_new.reshape(nbp, 1, H_B, seq_p, DIFF_VD),
            c_new, nst.reshape(nbp, 1, 2, H_C, DK_C), mst[:, :, 0].reshape(nbp, 1, 2, H_C))
```

```python
import functools
import math

import jax
import jax.numpy as jnp
import numpy as np
from jax import lax
from jax.experimental import pallas as pl
from jax.experimental.pallas import tpu as pltpu

F32 = jnp.float32
BF16 = jnp.bfloat16

D_MODEL = 1024
DEPTH = 2
GRID_W = 64
ROPE_BASE = 10000.0
RMS_EPS = 1e-6
H_A = 8
QK_NOPE = 64
QK_ROPE = 32
V_HD_A = 64
Q_LORA = 256
KV_LORA = 128
H_B = 4
DIFF_HD = 64
DIFF_VD = 2 * DIFF_HD
H_C = 8
DK_C = 64
DV_C = D_MODEL // H_C
D_FF = -(-8 * D_MODEL // (3 * 256)) * 256

LANES = 128
HEAD_PAD = LANES
ROPE_LANE0 = QK_NOPE
N_MOD = 6
COND_ROWS = 16
N_GATE = 2 * H_C
N_ROWS_N = 16

TM_PROJ = 1024
TM_PROJ_ROPE = 512
TM_PROJ1 = 1024
TQ_ATTN = 512
TM_TAIL = 1024
FF_CHUNK = 256
MLSTM_L = 128
ADA_TN = 1024
VMEM_LIMIT = 56 * 1024 * 1024

assert D_FF % LANES == 0 and FF_CHUNK % LANES == 0


def _dot(a, b):
    return jnp.dot(a, b, preferred_element_type=F32)


def _dot_nt(a, b):
    return lax.dot_general(a, b, (((1,), (1,)), ((), ())), preferred_element_type=F32)


def _dot_tn(a, b):
    return lax.dot_general(a, b, (((0,), (0,)), ((), ())), preferred_element_type=F32)


def _rms(x):
    return x * lax.rsqrt(jnp.mean(x * x, axis=-1, keepdims=True) + RMS_EPS)


def _modulate(x, g, shift, scale):
    return _rms(x) * (g * (1.0 + scale)) + shift


def _params(semantics):
    return pltpu.CompilerParams(dimension_semantics=semantics, vmem_limit_bytes=VMEM_LIMIT)


def _ada_kernel(cond_ref, w_ref, b_ref, o_ref):
    c = cond_ref[...]
    s = (c * jax.nn.sigmoid(c)).astype(BF16)
    o_ref[0] = _dot(s, w_ref[0].astype(BF16)) + b_ref[0]


def _ada(cond, w_ada, b_ada):
    n_out = w_ada.shape[-1]
    return pl.pallas_call(
        _ada_kernel,
        grid=(DEPTH, n_out // ADA_TN),
        in_specs=[
            pl.BlockSpec((COND_ROWS, D_MODEL), lambda l, n: (0, 0)),
            pl.BlockSpec((1, D_MODEL, ADA_TN), lambda l, n: (l, 0, n)),
            pl.BlockSpec((1, 1, ADA_TN), lambda l, n: (l, 0, n)),
        ],
        out_specs=pl.BlockSpec((1, COND_ROWS, ADA_TN), lambda l, n: (l, 0, n)),
        out_shape=jax.ShapeDtypeStruct((DEPTH, COND_ROWS, n_out), F32),
        compiler_params=_params(("arbitrary", "arbitrary")),
        name="ada",
    )(cond, w_ada, b_ada.reshape(DEPTH, 1, n_out))


def _rope(x, c, sa, sb, off):
    return x * c + pltpu.roll(x, LANES - off, 1) * sa + pltpu.roll(x, off, 1) * sb


def _proj0_kernel(*refs, rope, emit_cache):
    x_ref, mod_ref, g_ref, wa_ref, wd_ref, gq_ref, gkv_ref, wq_ref, wukv_ref = refs[:9]
    refs = refs[9:]
    if rope:
        cm_ref, sam_ref, sbm_ref, cd_ref, sad_ref, sbd_ref = refs[:6]
        refs = refs[6:]
    qm_ref, km_ref, vm_ref, dq_ref, dk_ref, dv_ref = refs[:6]
    refs = refs[6:]
    if emit_cache:
        ckvf_ref, krf_ref, dkf_ref, dvf_ref = refs

    h = _modulate(x_ref[...], g_ref[...], mod_ref[0, 0:1, :], mod_ref[0, 1:2, :]).astype(BF16)
    pa = _dot(h, wa_ref[...])
    pd = _dot(h, wd_ref[...])
    lane = lax.broadcasted_iota(jnp.int32, (pa.shape[0], LANES), 1)
    krb = jnp.where((lane >= ROPE_LANE0) & (lane < ROPE_LANE0 + QK_ROPE),
                    pltpu.roll(pa[:, Q_LORA + KV_LORA:Q_LORA + KV_LORA + LANES], ROPE_LANE0, 1), 0.0)
    cq = pa[:, 0:Q_LORA]
    ckv = _rms(pa[:, Q_LORA:Q_LORA + KV_LORA]) * gkv_ref[...]
    qa = _dot((_rms(cq) * gq_ref[...]).astype(BF16), wq_ref[...])
    kv = _dot(ckv.astype(BF16), wukv_ref[...])
    if emit_cache:
        ckvf_ref[...] = ckv
        krf_ref[...] = krb[:, ROPE_LANE0:ROPE_LANE0 + QK_ROPE]
    if rope:
        cm, sam, sbm = cm_ref[...], sam_ref[...], sbm_ref[...]
        cd, sad, sbd = cd_ref[...], sad_ref[...], sbd_ref[...]
        krb = _rope(krb, cm, sam, sbm, QK_ROPE // 4)
    for hd in range(H_A):
        sl = slice(hd * HEAD_PAD, (hd + 1) * HEAD_PAD)
        qh = qa[:, sl]
        if rope:
            qh = _rope(qh, cm, sam, sbm, QK_ROPE // 4)
        qm_ref[:, sl] = qh.astype(BF16)
        km_ref[:, sl] = (kv[:, sl] + krb).astype(BF16)
    vm_ref[...] = kv[:, H_A * HEAD_PAD:].astype(BF16)
    for hd in range(H_B):
        sl = slice(hd * LANES, (hd + 1) * LANES)
        dq = pd[:, hd * LANES:(hd + 1) * LANES]
        dk = pd[:, (H_B + hd) * LANES:(H_B + hd + 1) * LANES]
        dv = pd[:, (2 * H_B + hd) * LANES:(2 * H_B + hd + 1) * LANES]
        if emit_cache:
            seq = dkf_ref.shape[2]
            for bi in range(dkf_ref.shape[0]):
                dkf_ref[bi, hd] = dk[bi * seq:(bi + 1) * seq]
                dvf_ref[bi, hd] = dv[bi * seq:(bi + 1) * seq]
        if rope:
            dq = _rope(dq, cd, sad, sbd, DIFF_HD // 4)
            dk = _rope(dk, cd, sad, sbd, DIFF_HD // 4)
        dq_ref[:, sl] = (dq * (DIFF_HD ** -0.5)).astype(BF16)
        dk_ref[:, sl] = dk.astype(BF16)
        dv_ref[:, sl] = dv.astype(BF16)


def _proj0(x, mod, mod_row, g, w_ab, wd, gq, gkv, wq, wukv, rope_tabs, seq):
    n = x.shape[0]
    rope = rope_tabs is not None
    emit_cache = not rope
    tm = TM_PROJ_ROPE if rope else TM_PROJ
    tiles_per_seq = max(seq // tm, 1)
    full = lambda shape: pl.BlockSpec(shape, lambda i: (0,) * len(shape))
    in_specs = [
        pl.BlockSpec((tm, D_MODEL), lambda i: (i, 0)),
        pl.BlockSpec((1, N_MOD, D_MODEL), lambda i: (mod_row(i), 0, 0)),
        full((1, D_MODEL)), full((D_MODEL, Q_LORA + KV_LORA + LANES)), full(wd.shape),
        full((1, Q_LORA)), full((1, KV_LORA)), full(wq.shape), full(wukv.shape),
    ]
    args = [x, mod, g, w_ab, wd, gq, gkv, wq, wukv]
    if rope:
        in_specs += [pl.BlockSpec((tm, LANES), lambda i: (i % tiles_per_seq, 0))] * 6
        args += list(rope_tabs)
    row = lambda w: pl.BlockSpec((tm, w), lambda i: (i, 0))
    out_specs = [row(1024), row(1024), row(1024), row(512), row(512), row(512)]
    out_shape = [jax.ShapeDtypeStruct((n, w), BF16) for w in (1024, 1024, 1024, 512, 512, 512)]
    if emit_cache:
        assert tm % seq == 0
        cache = pl.BlockSpec((tm // seq, H_B, seq, LANES), lambda i: (i, 0, 0, 0))
        out_specs += [row(KV_LORA), row(QK_ROPE), cache, cache]
        out_shape += [jax.ShapeDtypeStruct((n, KV_LORA), F32), jax.ShapeDtypeStruct((n, QK_ROPE), F32),
                      jax.ShapeDtypeStruct((n // seq, H_B, seq, LANES), F32),
                      jax.ShapeDtypeStruct((n // seq, H_B, seq, LANES), F32)]
    return pl.pallas_call(
        functools.partial(_proj0_kernel, rope=rope, emit_cache=emit_cache),
        grid=(n // tm,),
        in_specs=in_specs, out_specs=out_specs, out_shape=out_shape,
        compiler_params=_params(("arbitrary",)),
        name="proj0_rope" if rope else "proj0_ctx",
    )(*args)


def _attn_kernel(*refs, has_ctx, lam_init):
    q_ref, dq_ref, k_ref, v_ref, dk_ref, dv_ref, lamv_ref, gsub_ref = refs[:8]
    refs = refs[8:]
    if has_ctx:
        ckvc_ref, krc_ref, dkc_ref, dvc_ref, wukv_ref = refs[:5]
        refs = refs[5:]
    o_ref, kx_s, vx_s, dkx_s, dvx_s = refs
    seq = k_ref.shape[0]
    ktot = kx_s.shape[0]

    @pl.when(pl.program_id(1) == 0)
    def _():
        kx_s[0:seq, :] = k_ref[...]
        dkx_s[0:seq, :] = dk_ref[...]
        ones_own = jnp.ones((seq, LANES), BF16)
        for hd in range(H_A):
            vx_s[0:seq, 2 * hd * LANES:(2 * hd + 1) * LANES] = v_ref[:, hd * LANES:(hd + 1) * LANES]
            vx_s[0:seq, (2 * hd + 1) * LANES:(2 * hd + 2) * LANES] = ones_own
        for hd in range(H_B):
            dvx_s[0:seq, 2 * hd * LANES:(2 * hd + 1) * LANES] = dv_ref[:, hd * LANES:(hd + 1) * LANES]
            dvx_s[0:seq, (2 * hd + 1) * LANES:(2 * hd + 2) * LANES] = ones_own
        if has_ctx:
            kv = _dot(ckvc_ref[0].astype(BF16), wukv_ref[...])
            krb = krc_ref[0]
            ones_ctx = jnp.ones((ktot - seq, LANES), BF16)
            for hd in range(H_A):
                sl = slice(hd * HEAD_PAD, (hd + 1) * HEAD_PAD)
                kx_s[seq:ktot, sl] = (kv[:, sl] + krb).astype(BF16)
                vx_s[seq:ktot, 2 * hd * LANES:(2 * hd + 1) * LANES] = (
                    kv[:, (H_A + hd) * HEAD_PAD:(H_A + hd + 1) * HEAD_PAD].astype(BF16))
                vx_s[seq:ktot, (2 * hd + 1) * LANES:(2 * hd + 2) * LANES] = ones_ctx
            for hd in range(H_B):
                sl = slice(hd * LANES, (hd + 1) * LANES)
                dkx_s[seq:ktot, sl] = dkc_ref[0, hd].astype(BF16)
                dvx_s[seq:ktot, 2 * hd * LANES:(2 * hd + 1) * LANES] = dvc_ref[0, hd].astype(BF16)
                dvx_s[seq:ktot, (2 * hd + 1) * LANES:(2 * hd + 2) * LANES] = ones_ctx

    log2e = 1.0 / math.log(2.0)
    tq = dq_ref.shape[0]
    lo = lax.broadcasted_iota(jnp.int32, (tq, LANES), 1) < DIFF_HD

    jobs = []
    for hd in range(H_A):
        sl = slice(hd * HEAD_PAD, (hd + 1) * HEAD_PAD)
        jobs.append((lambda sl=sl: q_ref[:, sl], (kx_s, sl), (vx_s, hd), (QK_NOPE + QK_ROPE) ** -0.5 * log2e))
    for hd in range(H_B):
        sl = slice(hd * LANES, (hd + 1) * LANES)
        for part in range(2):
            def qfn(sl=sl, part=part):
                dq = dq_ref[:, sl].astype(F32)
                return (jnp.where(lo, dq, 0.0) if part == 0 else jnp.where(lo, 0.0, dq)).astype(BF16)
            jobs.append((qfn, (dkx_s, sl), (dvx_s, hd), log2e))

    def scores(job):
        qfn, (kref, sl), _, _ = job
        return _dot_nt(qfn(), kref[:, sl])

    def finish(s, job):
        _, _, (vref, hd), c = job
        m = jnp.max(s, axis=-1, keepdims=True)
        e = jnp.exp2((s - m) * c).astype(BF16)
        res = _dot(e, vref[:, 2 * hd * LANES:(2 * hd + 2) * LANES])
        return res[:, 0:LANES] * (1.0 / res[:, LANES:2 * LANES])

    outs = []
    s_next = scores(jobs[0])
    for i, job in enumerate(jobs):
        s_cur = s_next
        if i + 1 < len(jobs):
            s_next = scores(jobs[i + 1])
        outs.append(finish(s_cur, job))

    for j in range(H_A // 2):
        o_ref[:, j * LANES:(j + 1) * LANES] = (outs[2 * j] + outs[2 * j + 1]).astype(BF16)
    lv = lamv_ref[...]
    lam = (jnp.exp(jnp.sum(lv[0:1] * lv[1:2], axis=-1, keepdims=True))
           - jnp.exp(jnp.sum(lv[2:3] * lv[3:4], axis=-1, keepdims=True)) + lam_init)
    for hd in range(H_B):
        acc = outs[H_A + 2 * hd] - lam * outs[H_A + 2 * hd + 1]
        ob = (_rms(acc) * gsub_ref[...]) * (1.0 - lam_init)
        o_ref[:, H_A * V_HD_A + hd * LANES:H_A * V_HD_A + (hd + 1) * LANES] = ob.astype(BF16)


def _attn0(qm, km, vm, dq, dk, dv, lamv, gsub, ctx, wukv, seq, lam_init):
    n = qm.shape[0]
    nb = n // seq
    tq = min(TQ_ATTN, seq)
    has_ctx = ctx is not None
    full = lambda shape: pl.BlockSpec(shape, lambda b, t: (0,) * len(shape))
    qrow = lambda w: pl.BlockSpec((tq, w), lambda b, t: (b * (seq // tq) + t, 0))
    krow = lambda w: pl.BlockSpec((seq, w), lambda b, t: (b, 0))
    in_specs = [qrow(1024), qrow(512), krow(1024), krow(1024), krow(512), krow(512),
                full((4, DIFF_HD)), full((1, DIFF_VD))]
    args = [qm, dq, km, vm, dk, dv, lamv, gsub]
    past = 0
    if has_ctx:
        ckv_c, krb_c, dk_c, dv_c = ctx
        past = ckv_c.shape[1]
        in_specs += [pl.BlockSpec((1, past, LANES), lambda b, t: (b, 0, 0)),
                     pl.BlockSpec((1, past, LANES), lambda b, t: (b, 0, 0)),
                     pl.BlockSpec((1, H_B, past, LANES), lambda b, t: (b, 0, 0, 0)),
                     pl.BlockSpec((1, H_B, past, LANES), lambda b, t: (b, 0, 0, 0)),
                     full(wukv.shape)]
        args += [ckv_c, krb_c, dk_c, dv_c, wukv]
    ktot = seq + past
    scratch = [pltpu.VMEM((ktot, H_A * HEAD_PAD), BF16), pltpu.VMEM((ktot, 2 * H_A * LANES), BF16),
               pltpu.VMEM((ktot, H_B * LANES), BF16), pltpu.VMEM((ktot, 2 * H_B * LANES), BF16)]
    return pl.pallas_call(
        functools.partial(_attn_kernel, has_ctx=has_ctx, lam_init=lam_init),
        grid=(nb, seq // tq),
        in_specs=in_specs,
        out_specs=pl.BlockSpec((tq, D_MODEL), lambda b, t: (b * (seq // tq) + t, 0)),
        out_shape=jax.ShapeDtypeStruct((n, D_MODEL), BF16),
        scratch_shapes=scratch,
        compiler_params=_params(("arbitrary", "arbitrary")),
        name="attn0_lat" if has_ctx else "attn0_ctx",
    )(*args)


def _tail_kernel(*refs, final):
    x_ref, a_ref, mod_ref, wo_ref, gf_ref, win_ref, wout_ref = refs[:7]
    if final:
        gfin_ref, o_ref = refs[7:]
    else:
        (o_ref,) = refs[7:]
    x1 = x_ref[...] + mod_ref[0, 2:3, :] * _dot(a_ref[...], wo_ref[...])
    h = _modulate(x1, gf_ref[...], mod_ref[0, 3:4, :], mod_ref[0, 4:5, :]).astype(BF16)
    bounds = list(range(0, D_FF, FF_CHUNK)) + [D_FF]
    acc = None
    for lo, hi in zip(bounds[:-1], bounds[1:]):
        a = _dot(h, win_ref[0, :, lo:hi])
        b = _dot(h, win_ref[0, :, D_FF + lo:D_FF + hi])
        act = ((a * jax.nn.sigmoid(a)) * b).astype(BF16)
        part = _dot(act, wout_ref[0, lo:hi, :])
        acc = part if acc is None else acc + part
    x2 = x1 + mod_ref[0, 5:6, :] * acc
    if final:
        x2 = _rms(x2) * gfin_ref[...]
    o_ref[...] = x2


def _tail(x, a, mod, mod_row, wo, gf, win, wout, layer, gfin, name):
    n = x.shape[0]
    tm = TM_TAIL
    final = gfin is not None
    full = lambda shape: pl.BlockSpec(shape, lambda i: (0,) * len(shape))
    resident = lambda shape: pl.BlockSpec((1,) + shape[1:], lambda i: (layer, 0, 0),
                                          pipeline_mode=pl.Buffered(1))
    in_specs = [
        pl.BlockSpec((tm, D_MODEL), lambda i: (i, 0)),
        pl.BlockSpec((tm, D_MODEL), lambda i: (i, 0)),
        pl.BlockSpec((1, N_MOD, D_MODEL), lambda i: (mod_row(i), 0, 0)),
        full(wo.shape), full((1, D_MODEL)), resident(win.shape), resident(wout.shape),
    ]
    args = [x, a, mod, wo, gf, win, wout]
    if final:
        in_specs.append(full((1, D_MODEL)))
        args.append(gfin)
    return pl.pallas_call(
        functools.partial(_tail_kernel, final=final),
        grid=(n // tm,),
        in_specs=in_specs,
        out_specs=pl.BlockSpec((tm, D_MODEL), lambda i: (i, 0)),
        out_shape=jax.ShapeDtypeStruct((n, D_MODEL), F32),
        compiler_params=_params(("arbitrary",)),
        name=name,
    )(*args)


def _split3(x):
    hi = x.astype(BF16)
    r1 = x - hi.astype(F32)
    mid = r1.astype(BF16)
    lo = (r1 - mid.astype(F32)).astype(BF16)
    return hi, mid, lo


def _proj1_kernel(x_ref, mod_ref, g_ref, w1_ref, wg_ref, bg_ref, q_ref, k_ref, v_ref, o_ref, gc_ref, gr_ref):
    tm = x_ref.shape[0]
    L = MLSTM_L
    hk = H_C * DK_C
    hv = H_C * DV_C
    h = _modulate(x_ref[...], g_ref[...], mod_ref[0, 0:1, :], mod_ref[0, 1:2, :]).astype(BF16)
    gates = _dot(h, wg_ref[...]) + bg_ref[...]
    proj = _dot(h, w1_ref[...])
    for blk in range(hk // LANES):
        sl = slice(blk * LANES, (blk + 1) * LANES)
        q_ref[sl, :] = jnp.transpose(proj[:, sl]).astype(BF16)
    for blk in range(hv // LANES):
        sl = slice(blk * LANES, (blk + 1) * LANES)
        v_ref[sl, :] = jnp.transpose(proj[:, 2 * hk + blk * LANES:2 * hk + (blk + 1) * LANES]).astype(BF16)
    k_ref[...] = proj[:, hk:2 * hk] * (DK_C ** -0.5)
    o_ref[...] = proj[:, 2 * hk + hv:2 * hk + 2 * hv]
    lf = jnp.minimum(gates, 0.0) - jnp.log1p(jnp.exp(-jnp.abs(gates)))
    r = lax.broadcasted_iota(jnp.int32, (L, L), 0)
    c = lax.broadcasted_iota(jnp.int32, (L, L), 1)
    pre = jnp.where(c <= r, 1.0, 0.0).astype(BF16)
    suf = jnp.where(c >= r, 1.0, 0.0).astype(BF16)
    parts = jnp.concatenate(_split3(lf), axis=1)
    fold = lambda t: t[:, 0:LANES] + t[:, LANES:2 * LANES] + t[:, 2 * LANES:3 * LANES]
    chunks = [parts[ck * L:(ck + 1) * L] for ck in range(tm // L)]
    lane = lax.broadcasted_iota(jnp.int32, (tm, LANES), 1)
    b_sum = jnp.where((lane % 4) >= 2,
                      jnp.concatenate([fold(_dot(suf, p)) for p in chunks], axis=0),
                      jnp.concatenate([fold(_dot(pre, p)) for p in chunks], axis=0))
    b = pltpu.roll(b_sum, LANES - N_GATE, 1)
    u = gates - b
    pos = lax.broadcasted_iota(jnp.int32, (tm, LANES), 0) % L
    bwd = (lane % 4) >= 2
    cm = u
    step = 1
    while step < L:
        below = jnp.where(pos >= step, pltpu.roll(cm, step, 0), -jnp.inf)
        above = jnp.where(pos < L - step, pltpu.roll(cm, tm - step, 0), -jnp.inf)
        cm = jnp.maximum(cm, jnp.where(bwd, above, below))
        step *= 2
    low = lane < N_GATE
    packed = (jnp.where(low, b, 0.0) + pltpu.roll(jnp.where(low, cm, 0.0), N_GATE, 1)
              + pltpu.roll(jnp.where(low, u, 0.0), 2 * N_GATE, 1))
    gc_ref[...] = packed
    gr_ref[...] = jnp.transpose(packed)[0:3 * N_GATE]


def _proj1(x, mod, mod_row, g, w1, wg, bg):
    n = x.shape[0]
    tm = TM_PROJ1
    assert tm % MLSTM_L == 0
    full = lambda shape: pl.BlockSpec(shape, lambda i: (0,) * len(shape))
    row = lambda w: pl.BlockSpec((tm, w), lambda i: (i, 0))
    col = lambda w: pl.BlockSpec((w, tm), lambda i: (0, i))
    hk, hv = H_C * DK_C, H_C * DV_C
    return pl.pallas_call(
        _proj1_kernel,
        grid=(n // tm,),
        in_specs=[row(D_MODEL), pl.BlockSpec((1, N_MOD, D_MODEL), lambda i: (mod_row(i), 0, 0)),
                  full((1, D_MODEL)), full((D_MODEL, 2 * hk + 2 * hv)), full(wg.shape), full((1, LANES))],
        out_specs=[col(hk), row(hk), col(hv), row(hv), row(LANES),
                   pl.BlockSpec((3 * N_GATE, tm), lambda i: (0, i))],
        out_shape=[jax.ShapeDtypeStruct((hk, n), BF16), jax.ShapeDtypeStruct((n, hk), F32),
                   jax.ShapeDtypeStruct((hv, n), BF16), jax.ShapeDtypeStruct((n, hv), F32),
                   jax.ShapeDtypeStruct((n, LANES), F32), jax.ShapeDtypeStruct((3 * N_GATE, n), F32)],
        compiler_params=_params(("arbitrary",)),
        name="proj1",
    )(x, mod, g, w1, wg, bg)


def _chain(j, d, hh):
    return (2 * j + d) * 2 + hh


def _mlstm_kernel(*refs, has_state, emit_state):
    q_ref, k_ref, v_ref, o_ref, gc_ref, gr_ref, gn_ref = refs[:7]
    refs = refs[7:]
    if has_state:
        c0_ref, n0_ref, m0_ref = refs[:3]
        refs = refs[3:]
    hs_ref = refs[0]
    refs = refs[1:]
    if emit_state:
        cf_ref, nst_ref, mst_ref = refs[:3]
        refs = refs[3:]
    cx_s, m_s, h_s = refs

    L = MLSTM_L
    seq = k_ref.shape[0]
    nc = seq // L
    npair = H_C // 2
    lane = lax.broadcasted_iota(jnp.int32, (1, LANES), 1)
    head_mask = [lane < DK_C, lane >= DK_C]
    ri = lax.broadcasted_iota(jnp.int32, (L, L), 0)
    ci = lax.broadcasted_iota(jnp.int32, (L, L), 1)
    causal = [ri <= ci, ri >= ci]
    ones_blk = jnp.ones((N_ROWS_N, L), BF16)
    chains = [(j, d, hh) for d in range(2) for j in range(npair) for hh in range(2)]

    h_s[...] = jnp.zeros_like(h_s)
    for j, d, hh in chains:
        ch = _chain(j, d, hh)
        if has_state:
            zpad = jnp.zeros((DK_C, DV_C), F32)
            c0 = c0_ref[0, d, 2 * j + hh]
            c0 = jnp.concatenate([c0, zpad] if hh == 0 else [zpad, c0], axis=0)
            cx_s[ch, 0:DV_C, :] = jnp.transpose(c0)
            n_row = jnp.where(head_mask[hh], n0_ref[0, d, j], 0.0)
            cx_s[ch, DV_C:DV_C + N_ROWS_N, :] = jnp.broadcast_to(n_row, (N_ROWS_N, LANES))
            m_s[ch] = m0_ref[0, d, 2 * j + hh]
        else:
            cx_s[ch] = jnp.zeros((DV_C + N_ROWS_N, LANES), F32)
            m_s[ch] = jnp.zeros((1, LANES), F32)

    def chunk_step(i, carry):
        sl = [pl.ds(pl.multiple_of(i * L, L), L), pl.ds(pl.multiple_of((nc - 1 - i) * L, L), L)]
        gcol = [gc_ref[sl[d], :] for d in range(2)]
        grow = [gr_ref[:, sl[d]] for d in range(2)]
        kpair = {(d, j): k_ref[sl[d], j * LANES:(j + 1) * LANES] for d in range(2) for j in range(npair)}
        kbf = {key: kk.astype(BF16) for key, kk in kpair.items()}
        zq = jnp.zeros((DK_C, L), BF16)
        qt = {}
        for d in range(2):
            for j in range(npair):
                qt[(j, d, 0)] = jnp.concatenate([q_ref[j * LANES:j * LANES + DK_C, sl[d]], zq], axis=0)
                qt[(j, d, 1)] = jnp.concatenate([zq, q_ref[j * LANES + DK_C:(j + 1) * LANES, sl[d]]], axis=0)
        st, u_bc, vt, row = {}, {}, {}, {}
        for j, d, hh in chains:
            key = (j, d, hh)
            idx = 4 * j + 2 * d + hh
            edge = L - 1 if d == 0 else 0
            b_row = grow[d][idx:idx + 1, :]
            cm_row = grow[d][N_GATE + idx:N_GATE + idx + 1, :]
            u_col = gcol[d][:, 2 * N_GATE + idx:2 * N_GATE + idx + 1]
            m_prev = m_s[_chain(j, d, hh)][:, 0:1]
            row[key] = (b_row, cm_row, b_row[:, edge:edge + 1], cm_row[:, edge:edge + 1], m_prev)
            u_bc[key] = jnp.broadcast_to(u_col, (L, L))
            vt[key] = v_ref[(2 * j + hh) * DV_C:(2 * j + hh + 1) * DV_C, sl[d]]
            st[key] = _dot(kbf[(d, j)], qt[key])
        for j, d, hh in chains:
            key = (j, d, hh)
            ch = _chain(j, d, hh)
            b_row, cm_row, g_tot, cm_last, m_prev = row[key]
            head = slice((2 * j + hh) * DV_C, (2 * j + hh + 1) * DV_C)
            cx = cx_s[ch]
            big_m = jnp.maximum(m_prev, cm_row)
            s = st[key] * jnp.exp(jnp.where(causal[d], u_bc[key] - big_m, -jnp.inf))
            inter = jnp.exp(m_prev - big_m)
            vx = jnp.concatenate([vt[key], ones_blk], axis=0)
            lhs = jnp.concatenate([vx, cx.astype(BF16)], axis=1)
            rhs = jnp.concatenate([s.astype(BF16), (inter * qt[key].astype(F32)).astype(BF16)], axis=0)
            res = _dot(lhs, rhs)
            den = res[DV_C:DV_C + 1, :]
            hval = res[0:DV_C, :] * (1.0 / jnp.maximum(jnp.abs(den), jnp.exp(-(b_row + big_m))))
            h_s[head, sl[d]] = h_s[head, sl[d]] + hval

            m_top = jnp.maximum(m_prev, cm_last)
            kw = kpair[(d, j)] * jnp.exp(u_bc[key] - m_top)
            cx_s[ch] = jnp.exp(m_prev - m_top) * cx + _dot(vx, kw.astype(BF16))
            m_s[ch] = jnp.broadcast_to(g_tot + m_top, (1, LANES))
        return carry

    lax.fori_loop(0, nc, chunk_step, 0)

    gn_col = jnp.broadcast_to(gn_ref[...], (DV_C, LANES))
    gn_col = jnp.concatenate([gn_col] * (seq // LANES), axis=1)
    for hd in range(H_C):
        sl = slice(hd * DV_C, (hd + 1) * DV_C)
        ht = h_s[sl, :]
        ms = jnp.mean(ht * ht, axis=0, keepdims=True)
        y = jnp.transpose((ht * lax.rsqrt(ms + RMS_EPS)) * gn_col) * jax.nn.sigmoid(o_ref[:, sl])
        hs_ref[:, sl] = y.astype(BF16)
    if emit_state:
        for j, d, hh in chains:
            ch = _chain(j, d, hh)
            cf_ref[0, 0, d, 2 * j + hh] = jnp.transpose(cx_s[ch, 0:DV_C, :])[hh * DK_C:(hh + 1) * DK_C, :]
            mst_ref[0, d * H_C + 2 * j + hh:d * H_C + 2 * j + hh + 1, :] = m_s[ch]
        for j in range(npair):
            for d in range(2):
                nst_ref[0, d * npair + j:d * npair + j + 1, :] = jnp.where(
                    head_mask[0], cx_s[_chain(j, d, 0), DV_C:DV_C + 1, :], cx_s[_chain(j, d, 1), DV_C:DV_C + 1, :])


def _mlstm(q, k, v, o, gc, gr, gn, state, seq, emit_state):
    n = k.shape[0]
    nb = n // seq
    npair = H_C // 2
    has_state = state is not None
    hk, hv = H_C * DK_C, H_C * DV_C
    in_specs = [
        pl.BlockSpec((hk, seq), lambda b: (0, b)),
        pl.BlockSpec((seq, hk), lambda b: (b, 0)),
        pl.BlockSpec((hv, seq), lambda b: (0, b)),
        pl.BlockSpec((seq, hv), lambda b: (b, 0)),
        pl.BlockSpec((seq, LANES), lambda b: (b, 0)),
        pl.BlockSpec((3 * N_GATE, seq), lambda b: (0, b)),
        pl.BlockSpec((DV_C, 1), lambda b: (0, 0)),
    ]
    args = [q, k, v, o, gc, gr, gn]
    if has_state:
        c0, n0, m0 = state
        in_specs += [pl.BlockSpec((1, 2, H_C, DK_C, DV_C), lambda b: (b, 0, 0, 0, 0)),
                     pl.BlockSpec((1, 2, npair, 1, LANES), lambda b: (b, 0, 0, 0, 0)),
                     pl.BlockSpec((1, 2, H_C, 1, LANES), lambda b: (b, 0, 0, 0, 0))]
        args += [c0, n0, m0]
    out_specs = [pl.BlockSpec((seq, hv), lambda b: (b, 0))]
    out_shape = [jax.ShapeDtypeStruct((n, hv), BF16)]
    if emit_state:
        out_specs += [pl.BlockSpec((1, 1, 2, H_C, DK_C, DV_C), lambda b: (b, 0, 0, 0, 0, 0)),
                      pl.BlockSpec((1, 2 * npair, LANES), lambda b: (b, 0, 0)),
                      pl.BlockSpec((1, 2 * H_C, LANES), lambda b: (b, 0, 0))]
        out_shape += [jax.ShapeDtypeStruct((nb, 1, 2, H_C, DK_C, DV_C), F32),
                      jax.ShapeDtypeStruct((nb, 2 * npair, LANES), F32),
                      jax.ShapeDtypeStruct((nb, 2 * H_C, LANES), F32)]
    n_chain = 2 * H_C
    return pl.pallas_call(
        functools.partial(_mlstm_kernel, has_state=has_state, emit_state=emit_state),
        grid=(nb,),
        in_specs=in_specs, out_specs=out_specs, out_shape=out_shape,
        scratch_shapes=[pltpu.VMEM((n_chain, DV_C + N_ROWS_N, LANES), F32), pltpu.VMEM((n_chain, 1, LANES), F32),
                        pltpu.VMEM((hv, seq), F32)],
        compiler_params=_params(("arbitrary",)),
        name="mlstm_lat" if has_state else "mlstm_ctx",
    )(*args)


def _rope_tables(n_tok):
    t = np.arange(n_tok)
    rows = (t // GRID_W).astype(np.float64)
    cols = (t % GRID_W).astype(np.float64)

    def axis_tabs(width, lane0):
        half = width // 2
        quarter = half // 2
        freqs = np.power(ROPE_BASE, -np.arange(quarter, dtype=np.float64) / quarter)
        c = np.ones((n_tok, LANES))
        sa = np.zeros((n_tok, LANES))
        sb = np.zeros((n_tok, LANES))
        for g, pos in enumerate((rows, cols)):
            ang = pos[:, None] * freqs[None, :]
            a0 = lane0 + g * half
            c[:, a0:a0 + quarter] = np.cos(ang)
            c[:, a0 + quarter:a0 + half] = np.cos(ang)
            sa[:, a0:a0 + quarter] = -np.sin(ang)
            sb[:, a0 + quarter:a0 + half] = np.sin(ang)
        return c, sa, sb

    cm, sam, sbm = axis_tabs(QK_ROPE, ROPE_LANE0)
    c0, sa0, sb0 = axis_tabs(DIFF_HD, 0)
    c1, sa1, sb1 = axis_tabs(DIFF_HD, DIFF_HD)
    cd = np.where(np.arange(LANES)[None, :] < DIFF_HD, c0, c1)
    return tuple(jnp.asarray(a, F32) for a in (cm, sam, sbm, cd, sa0 + sa1, sb0 + sb1))


def _prep_even(w_in_ab, w_uq, w_ukv):
    c2 = Q_LORA + KV_LORA
    w_ab = w_in_ab.astype(BF16)
    wd = w_ab[:, c2 + QK_ROPE:]
    wq = jnp.pad(w_uq.reshape(Q_LORA, H_A, QK_NOPE + QK_ROPE),
                 ((0, 0), (0, 0), (0, HEAD_PAD - QK_NOPE - QK_ROPE))).reshape(Q_LORA, H_A * HEAD_PAD)
    kvw = w_ukv.reshape(KV_LORA, H_A, QK_NOPE + V_HD_A)
    kpad = jnp.pad(kvw[..., :QK_NOPE], ((0, 0), (0, 0), (0, HEAD_PAD - QK_NOPE)))
    vw = kvw[..., QK_NOPE:]
    zv = jnp.zeros_like(vw)
    odd = (jnp.arange(H_A) % 2 == 1)[None, :, None]
    vpad = jnp.where(odd, jnp.concatenate([zv, vw], -1), jnp.concatenate([vw, zv], -1))
    wukv = jnp.concatenate([kpad.reshape(KV_LORA, -1), vpad.reshape(KV_LORA, -1)], axis=1)
    return (w_ab, wd), wq.astype(BF16), wukv.astype(BF16)


def _gate_order(g):
    lead = g.shape[:-1]
    g = g.reshape(lead + (2, 2, H_C // 2, 2))
    perm = tuple(range(len(lead))) + tuple(len(lead) + a for a in (1, 2, 0, 3))
    return g.transpose(perm).reshape(lead + (4 * H_C,))


def _prep_odd(w_in_c, b_gate_c):
    ng = 4 * H_C
    base = w_in_c.shape[1] - ng
    w1 = w_in_c.astype(BF16)
    wg = jnp.pad(_gate_order(w1[:, base:]), ((0, 0), (0, LANES - ng)))
    bg = jnp.pad(_gate_order(b_gate_c), (0, LANES - ng)).reshape(1, LANES)
    return w1, wg, bg


def kernel(x_prompt, x_sample, cache_mla_ckv, cache_mla_krope, cache_diff_k, cache_diff_v,
           state_mlstm_C, state_mlstm_n, state_mlstm_m, c, c_ctx,
           w_ada, b_ada, g_mix, g_ffn, w_ffn_in, w_ffn_out,
           w_in_ab, g_q_lora, g_kv_lora, w_uq, w_ukv, diff_lambda, g_diff_subln, w_out_ab,
           w_in_c, b_gate_c, g_mlstm, w_out_c, g_final):
    nbp, seq_p, _ = x_prompt.shape
    nbs, seq_s, _ = x_sample.shape
    past = cache_mla_ckv.shape[2]
    assert DEPTH == 2 and 1 + nbs <= COND_ROWS
    assert cache_mla_ckv.shape[1] == 1 and state_mlstm_C.shape[1] == 1

    cond = jnp.concatenate([c_ctx[None], c, jnp.zeros((COND_ROWS - 1 - nbs, D_MODEL), F32)], axis=0)
    mod = _ada(cond, w_ada, b_ada).reshape(DEPTH * COND_ROWS, N_MOD, D_MODEL)

    xp = x_prompt.reshape(nbp * seq_p, D_MODEL)
    xs = x_sample.reshape(nbs * seq_s, D_MODEL)
    row2 = lambda v: v.reshape(1, -1)

    def mod_rows(layer, tm):
        prompt = lambda i: layer * COND_ROWS
        sample = lambda i: layer * COND_ROWS + 1 + i // (seq_s // tm)
        return prompt, sample

    lam_init = 0.8 - 0.6 * math.exp(-0.3 * 0)
    w0, wq, wukv = _prep_even(w_in_ab[0], w_uq[0], w_ukv[0])
    rope_tabs = _rope_tables(seq_s)
    mrp, _ = mod_rows(0, TM_PROJ)
    _, mrs = mod_rows(0, TM_PROJ_ROPE)
    gq, gkv = row2(g_q_lora[0]), row2(g_kv_lora[0])
    outs_p = _proj0(xp, mod, mrp, row2(g_mix[0]), *w0, gq, gkv, wq, wukv, None, seq_p)
    outs_s = _proj0(xs, mod, mrs, row2(g_mix[0]), *w0, gq, gkv, wq, wukv, rope_tabs, seq_s)
    qm_p, km_p, vm_p, dq_p, dk_p, dv_p, ckv_new, kr_new, dk_new, dv_new = outs_p
    qm_s, km_s, vm_s, dq_s, dk_s, dv_s = outs_s
    gsub = row2(g_diff_subln[0])
    krb_c = jnp.pad(cache_mla_krope.reshape(nbs, past, QK_ROPE),
                    ((0, 0), (0, 0), (ROPE_LANE0, LANES - ROPE_LANE0 - QK_ROPE)))
    ctx = (cache_mla_ckv.reshape(nbs, past, KV_LORA), krb_c,
           cache_diff_k.reshape(nbs, H_B, past, 2 * DIFF_HD), cache_diff_v.reshape(nbs, H_B, past, DIFF_VD))
    a_p = _attn0(qm_p, km_p, vm_p, dq_p, dk_p, dv_p, diff_lambda[0], gsub, None, None, seq_p, lam_init)
    a_s = _attn0(qm_s, km_s, vm_s, dq_s, dk_s, dv_s, diff_lambda[0], gsub, ctx, wukv, seq_s, lam_init)
    mtp, mts = mod_rows(0, TM_TAIL)
    wo0 = w_out_ab[0].astype(BF16)
    win, wout = w_ffn_in.astype(BF16), w_ffn_out.astype(BF16)
    xp = _tail(xp, a_p, mod, mtp, wo0, row2(g_ffn[0]), win, wout, 0, None, "tail0_ctx")
    xs = _tail(xs, a_s, mod, mts, wo0, row2(g_ffn[0]), win, wout, 0, None, "tail0_lat")

    w1, wg, bg = _prep_odd(w_in_c[0], b_gate_c[0])
    mrp, mrs = mod_rows(1, TM_PROJ1)
    gn = g_mlstm[0].reshape(DV_C, 1)
    q_p, k_p, v_p, o_p, gc_p, gr_p = _proj1(xp, mod, mrp, row2(g_mix[1]), w1, wg, bg)
    q_s, k_s, v_s, o_s, gc_s, gr_s = _proj1(xs, mod, mrs, row2(g_mix[1]), w1, wg, bg)
    hs_p, c_new, nst, mst = _mlstm(q_p, k_p, v_p, o_p, gc_p, gr_p, gn, None, seq_p, True)
    state = (state_mlstm_C.reshape(nbs, 2, H_C, DK_C, DV_C),
             state_mlstm_n.reshape(nbs, 2, H_C // 2, 1, LANES),
             jnp.broadcast_to(state_mlstm_m.reshape(nbs, 2, H_C, 1, 1), (nbs, 2, H_C, 1, LANES)))
    (hs_s,) = _mlstm(q_s, k_s, v_s, o_s, gc_s, gr_s, gn, state, seq_s, False)
    mtp, mts = mod_rows(1, TM_TAIL)
    wo1 = w_out_c[0].astype(BF16)
    gfin = row2(g_final)
    yp = _tail(xp, hs_p, mod, mtp, wo1, row2(g_ffn[1]), win, wout, 1, gfin, "tail1_ctx")
    ys = _tail(xs, hs_s, mod, mts, wo1, row2(g_ffn[1]), win, wout, 1, gfin, "tail1_lat")

    return (yp.reshape(nbp, seq_p, D_MODEL), ys.reshape(nbs, seq_s, D_MODEL),
            ckv_new.reshape(nbp, 1, seq_p, KV_LORA), kr_new.reshape(nbp, 1, seq_p, QK_ROPE),
            dk_new.reshape(nbp, 1, H_B, seq_p, 2 * DIFF_HD), dv_new.reshape(nbp, 1, H_B, seq_p, DIFF_VD),
            c_new, nst.reshape(nbp, 1, 2, H_C, DK_C), mst[:, :, 0].reshape(nbp, 1, 2, H_C))
```

```python
import functools
import math

import jax
import jax.numpy as jnp
import numpy as np
from jax import lax
from jax.experimental import pallas as pl
from jax.experimental.pallas import tpu as pltpu

F32 = jnp.float32
BF16 = jnp.bfloat16

D_MODEL = 1024
DEPTH = 2
GRID_W = 64
ROPE_BASE = 10000.0
RMS_EPS = 1e-6
H_A = 8
QK_NOPE = 64
QK_ROPE = 32
V_HD_A = 64
Q_LORA = 256
KV_LORA = 128
H_B = 4
DIFF_HD = 64
DIFF_VD = 2 * DIFF_HD
H_C = 8
DK_C = 64
DV_C = D_MODEL // H_C
D_FF = -(-8 * D_MODEL // (3 * 256)) * 256

LANES = 128
HEAD_PAD = LANES
ROPE_LANE0 = QK_NOPE
N_MOD = 6
COND_ROWS = 16
N_GATE = 2 * H_C
N_ROWS_N = 16

TM_PROJ = 1024
TM_PROJ_ROPE = 512
TM_PROJ1 = 1024
TQ_ATTN = 1024
TM_TAIL = 1024
FF_CHUNK = 256
MLSTM_L = 128
ADA_TN = 1024
VMEM_LIMIT = 56 * 1024 * 1024

assert D_FF % LANES == 0 and FF_CHUNK % LANES == 0


def _dot(a, b):
    return jnp.dot(a, b, preferred_element_type=F32)


def _dot_nt(a, b):
    return lax.dot_general(a, b, (((1,), (1,)), ((), ())), preferred_element_type=F32)


def _dot_tn(a, b):
    return lax.dot_general(a, b, (((0,), (0,)), ((), ())), preferred_element_type=F32)


def _rms(x):
    return x * lax.rsqrt(jnp.mean(x * x, axis=-1, keepdims=True) + RMS_EPS)


def _modulate(x, g, shift, scale):
    return _rms(x) * (g * (1.0 + scale)) + shift


def _params(semantics):
    return pltpu.CompilerParams(dimension_semantics=semantics, vmem_limit_bytes=VMEM_LIMIT)


def _ada_kernel(cond_ref, w_ref, b_ref, o_ref):
    c = cond_ref[...]
    s = (c * jax.nn.sigmoid(c)).astype(BF16)
    o_ref[0] = _dot(s, w_ref[0].astype(BF16)) + b_ref[0]


def _ada(cond, w_ada, b_ada):
    n_out = w_ada.shape[-1]
    return pl.pallas_call(
        _ada_kernel,
        grid=(DEPTH, n_out // ADA_TN),
        in_specs=[
            pl.BlockSpec((COND_ROWS, D_MODEL), lambda l, n: (0, 0)),
            pl.BlockSpec((1, D_MODEL, ADA_TN), lambda l, n: (l, 0, n)),
            pl.BlockSpec((1, 1, ADA_TN), lambda l, n: (l, 0, n)),
        ],
        out_specs=pl.BlockSpec((1, COND_ROWS, ADA_TN), lambda l, n: (l, 0, n)),
        out_shape=jax.ShapeDtypeStruct((DEPTH, COND_ROWS, n_out), F32),
        compiler_params=_params(("arbitrary", "arbitrary")),
        name="ada",
    )(cond, w_ada, b_ada.reshape(DEPTH, 1, n_out))


def _rope(x, c, sa, sb, off):
    return x * c + pltpu.roll(x, LANES - off, 1) * sa + pltpu.roll(x, off, 1) * sb


def _proj0_kernel(*refs, rope, emit_cache):
    x_ref, mod_ref, g_ref, wa_ref, wd_ref, gq_ref, gkv_ref, wq_ref, wukv_ref = refs[:9]
    refs = refs[9:]
    if rope:
        cm_ref, sam_ref, sbm_ref, cd_ref, sad_ref, sbd_ref = refs[:6]
        refs = refs[6:]
    qm_ref, km_ref, vm_ref, dq_ref, dk_ref, dv_ref = refs[:6]
    refs = refs[6:]
    if emit_cache:
        ckvf_ref, krf_ref, dkf_ref, dvf_ref = refs

    h = _modulate(x_ref[...], g_ref[...], mod_ref[0, 0:1, :], mod_ref[0, 1:2, :]).astype(BF16)
    pa = _dot(h, wa_ref[...])
    pd = _dot(h, wd_ref[...])
    lane = lax.broadcasted_iota(jnp.int32, (pa.shape[0], LANES), 1)
    krb = jnp.where((lane >= ROPE_LANE0) & (lane < ROPE_LANE0 + QK_ROPE),
                    pltpu.roll(pa[:, Q_LORA + KV_LORA:Q_LORA + KV_LORA + LANES], ROPE_LANE0, 1), 0.0)
    cq = pa[:, 0:Q_LORA]
    ckv = _rms(pa[:, Q_LORA:Q_LORA + KV_LORA]) * gkv_ref[...]
    qa = _dot((_rms(cq) * gq_ref[...]).astype(BF16), wq_ref[...])
    kv = _dot(ckv.astype(BF16), wukv_ref[...])
    if emit_cache:
        ckvf_ref[...] = ckv
        krf_ref[...] = krb[:, ROPE_LANE0:ROPE_LANE0 + QK_ROPE]
    if rope:
        cm, sam, sbm = cm_ref[...], sam_ref[...], sbm_ref[...]
        cd, sad, sbd = cd_ref[...], sad_ref[...], sbd_ref[...]
        krb = _rope(krb, cm, sam, sbm, QK_ROPE // 4)
    for hd in range(H_A):
        sl = slice(hd * HEAD_PAD, (hd + 1) * HEAD_PAD)
        qh = qa[:, sl]
        if rope:
            qh = _rope(qh, cm, sam, sbm, QK_ROPE // 4)
        qm_ref[:, sl] = qh.astype(BF16)
        km_ref[:, sl] = (kv[:, sl] + krb).astype(BF16)
    vm_ref[...] = kv[:, H_A * HEAD_PAD:].astype(BF16)
    for hd in range(H_B):
        sl = slice(hd * LANES, (hd + 1) * LANES)
        dq = pd[:, hd * LANES:(hd + 1) * LANES]
        dk = pd[:, (H_B + hd) * LANES:(H_B + hd + 1) * LANES]
        dv = pd[:, (2 * H_B + hd) * LANES:(2 * H_B + hd + 1) * LANES]
        if emit_cache:
            seq = dkf_ref.shape[2]
            for bi in range(dkf_ref.shape[0]):
                dkf_ref[bi, hd] = dk[bi * seq:(bi + 1) * seq]
                dvf_ref[bi, hd] = dv[bi * seq:(bi + 1) * seq]
        if rope:
            dq = _rope(dq, cd, sad, sbd, DIFF_HD // 4)
            dk = _rope(dk, cd, sad, sbd, DIFF_HD // 4)
        dq_ref[:, sl] = (dq * (DIFF_HD ** -0.5)).astype(BF16)
        dk_ref[:, sl] = dk.astype(BF16)
        dv_ref[:, sl] = dv.astype(BF16)


def _proj0(x, mod, mod_row, g, w_ab, wd, gq, gkv, wq, wukv, rope_tabs, seq):
    n = x.shape[0]
    rope = rope_tabs is not None
    emit_cache = not rope
    tm = TM_PROJ_ROPE if rope else TM_PROJ
    tiles_per_seq = max(seq // tm, 1)
    full = lambda shape: pl.BlockSpec(shape, lambda i: (0,) * len(shape))
    in_specs = [
        pl.BlockSpec((tm, D_MODEL), lambda i: (i, 0)),
        pl.BlockSpec((1, N_MOD, D_MODEL), lambda i: (mod_row(i), 0, 0)),
        full((1, D_MODEL)), full((D_MODEL, Q_LORA + KV_LORA + LANES)), full(wd.shape),
        full((1, Q_LORA)), full((1, KV_LORA)), full(wq.shape), full(wukv.shape),
    ]
    args = [x, mod, g, w_ab, wd, gq, gkv, wq, wukv]
    if rope:
        in_specs += [pl.BlockSpec((tm, LANES), lambda i: (i % tiles_per_seq, 0))] * 6
        args += list(rope_tabs)
    row = lambda w: pl.BlockSpec((tm, w), lambda i: (i, 0))
    out_specs = [row(1024), row(1024), row(1024), row(512), row(512), row(512)]
    out_shape = [jax.ShapeDtypeStruct((n, w), BF16) for w in (1024, 1024, 1024, 512, 512, 512)]
    if emit_cache:
        assert tm % seq == 0
        cache = pl.BlockSpec((tm // seq, H_B, seq, LANES), lambda i: (i, 0, 0, 0))
        out_specs += [row(KV_LORA), row(QK_ROPE), cache, cache]
        out_shape += [jax.ShapeDtypeStruct((n, KV_LORA), F32), jax.ShapeDtypeStruct((n, QK_ROPE), F32),
                      jax.ShapeDtypeStruct((n // seq, H_B, seq, LANES), F32),
                      jax.ShapeDtypeStruct((n // seq, H_B, seq, LANES), F32)]
    return pl.pallas_call(
        functools.partial(_proj0_kernel, rope=rope, emit_cache=emit_cache),
        grid=(n // tm,),
        in_specs=in_specs, out_specs=out_specs, out_shape=out_shape,
        compiler_params=_params(("arbitrary",)),
        name="proj0_rope" if rope else "proj0_ctx",
    )(*args)


def _attn_kernel(*refs, has_ctx, lam_init):
    q_ref, dq_ref, k_ref, v_ref, dk_ref, dv_ref, lamv_ref, gsub_ref = refs[:8]
    refs = refs[8:]
    if has_ctx:
        ckvc_ref, krc_ref, dkc_ref, dvc_ref, wukv_ref = refs[:5]
        refs = refs[5:]
    o_ref, kx_s, vx_s, dkx_s, dvx_s = refs
    seq = k_ref.shape[0]
    ktot = kx_s.shape[0]

    @pl.when(pl.program_id(1) == 0)
    def _():
        kx_s[0:seq, :] = k_ref[...]
        dkx_s[0:seq, :] = dk_ref[...]
        ones_own = jnp.ones((seq, LANES), BF16)
        for hd in range(H_A):
            vx_s[0:seq, 2 * hd * LANES:(2 * hd + 1) * LANES] = v_ref[:, hd * LANES:(hd + 1) * LANES]
            vx_s[0:seq, (2 * hd + 1) * LANES:(2 * hd + 2) * LANES] = ones_own
        for hd in range(H_B):
            dvx_s[0:seq, 2 * hd * LANES:(2 * hd + 1) * LANES] = dv_ref[:, hd * LANES:(hd + 1) * LANES]
            dvx_s[0:seq, (2 * hd + 1) * LANES:(2 * hd + 2) * LANES] = ones_own
        if has_ctx:
            kv = _dot(ckvc_ref[0].astype(BF16), wukv_ref[...])
            krb = krc_ref[0]
            ones_ctx = jnp.ones((ktot - seq, LANES), BF16)
            for hd in range(H_A):
                sl = slice(hd * HEAD_PAD, (hd + 1) * HEAD_PAD)
                kx_s[seq:ktot, sl] = (kv[:, sl] + krb).astype(BF16)
                vx_s[seq:ktot, 2 * hd * LANES:(2 * hd + 1) * LANES] = (
                    kv[:, (H_A + hd) * HEAD_PAD:(H_A + hd + 1) * HEAD_PAD].astype(BF16))
                vx_s[seq:ktot, (2 * hd + 1) * LANES:(2 * hd + 2) * LANES] = ones_ctx
            for hd in range(H_B):
                sl = slice(hd * LANES, (hd + 1) * LANES)
                dkx_s[seq:ktot, sl] = dkc_ref[0, hd].astype(BF16)
                dvx_s[seq:ktot, 2 * hd * LANES:(2 * hd + 1) * LANES] = dvc_ref[0, hd].astype(BF16)
                dvx_s[seq:ktot, (2 * hd + 1) * LANES:(2 * hd + 2) * LANES] = ones_ctx

    log2e = 1.0 / math.log(2.0)
    tq = dq_ref.shape[0]
    lo = lax.broadcasted_iota(jnp.int32, (tq, LANES), 1) < DIFF_HD

    jobs = []
    for hd in range(H_A):
        sl = slice(hd * HEAD_PAD, (hd + 1) * HEAD_PAD)
        jobs.append((lambda sl=sl: q_ref[:, sl], (kx_s, sl), (vx_s, hd), (QK_NOPE + QK_ROPE) ** -0.5 * log2e))
    for hd in range(H_B):
        sl = slice(hd * LANES, (hd + 1) * LANES)
        for part in range(2):
            def qfn(sl=sl, part=part):
                dq = dq_ref[:, sl].astype(F32)
                return (jnp.where(lo, dq, 0.0) if part == 0 else jnp.where(lo, 0.0, dq)).astype(BF16)
            jobs.append((qfn, (dkx_s, sl), (dvx_s, hd), log2e))

    def scores(job):
        qfn, (kref, sl), _, _ = job
        return _dot_nt(qfn(), kref[:, sl])

    def finish(s, job):
        _, _, (vref, hd), c = job
        m = jnp.max(s, axis=-1, keepdims=True)
        e = jnp.exp2((s - m) * c).astype(BF16)
        res = _dot(e, vref[:, 2 * hd * LANES:(2 * hd + 2) * LANES])
        return res[:, 0:LANES] * (1.0 / res[:, LANES:2 * LANES])

    outs = []
    s_next = scores(jobs[0])
    for i, job in enumerate(jobs):
        s_cur = s_next
        if i + 1 < len(jobs):
            s_next = scores(jobs[i + 1])
        outs.append(finish(s_cur, job))

    for j in range(H_A // 2):
        o_ref[:, j * LANES:(j + 1) * LANES] = (outs[2 * j] + outs[2 * j + 1]).astype(BF16)
    lv = lamv_ref[...]
    lam = (jnp.exp(jnp.sum(lv[0:1] * lv[1:2], axis=-1, keepdims=True))
           - jnp.exp(jnp.sum(lv[2:3] * lv[3:4], axis=-1, keepdims=True)) + lam_init)
    for hd in range(H_B):
        acc = outs[H_A + 2 * hd] - lam * outs[H_A + 2 * hd + 1]
        ob = (_rms(acc) * gsub_ref[...]) * (1.0 - lam_init)
        o_ref[:, H_A * V_HD_A + hd * LANES:H_A * V_HD_A + (hd + 1) * LANES] = ob.astype(BF16)


def _attn0(qm, km, vm, dq, dk, dv, lamv, gsub, ctx, wukv, seq, lam_init):
    n = qm.shape[0]
    nb = n // seq
    tq = min(TQ_ATTN, seq)
    has_ctx = ctx is not None
    full = lambda shape: pl.BlockSpec(shape, lambda b, t: (0,) * len(shape))
    qrow = lambda w: pl.BlockSpec((tq, w), lambda b, t: (b * (seq // tq) + t, 0))
    krow = lambda w: pl.BlockSpec((seq, w), lambda b, t: (b, 0))
    in_specs = [qrow(1024), qrow(512), krow(1024), krow(1024), krow(512), krow(512),
                full((4, DIFF_HD)), full((1, DIFF_VD))]
    args = [qm, dq, km, vm, dk, dv, lamv, gsub]
    past = 0
    if has_ctx:
        ckv_c, krb_c, dk_c, dv_c = ctx
        past = ckv_c.shape[1]
        in_specs += [pl.BlockSpec((1, past, LANES), lambda b, t: (b, 0, 0)),
                     pl.BlockSpec((1, past, LANES), lambda b, t: (b, 0, 0)),
                     pl.BlockSpec((1, H_B, past, LANES), lambda b, t: (b, 0, 0, 0)),
                     pl.BlockSpec((1, H_B, past, LANES), lambda b, t: (b, 0, 0, 0)),
                     full(wukv.shape)]
        args += [ckv_c, krb_c, dk_c, dv_c, wukv]
    ktot = seq + past
    scratch = [pltpu.VMEM((ktot, H_A * HEAD_PAD), BF16), pltpu.VMEM((ktot, 2 * H_A * LANES), BF16),
               pltpu.VMEM((ktot, H_B * LANES), BF16), pltpu.VMEM((ktot, 2 * H_B * LANES), BF16)]
    return pl.pallas_call(
        functools.partial(_attn_kernel, has_ctx=has_ctx, lam_init=lam_init),
        grid=(nb, seq // tq),
        in_specs=in_specs,
        out_specs=pl.BlockSpec((tq, D_MODEL), lambda b, t: (b * (seq // tq) + t, 0)),
        out_shape=jax.ShapeDtypeStruct((n, D_MODEL), BF16),
        scratch_shapes=scratch,
        compiler_params=_params(("arbitrary", "arbitrary")),
        name="attn0_lat" if has_ctx else "attn0_ctx",
    )(*args)


def _tail_kernel(*refs, final):
    x_ref, a_ref, mod_ref, wo_ref, gf_ref, win_ref, wout_ref = refs[:7]
    if final:
        gfin_ref, o_ref = refs[7:]
    else:
        (o_ref,) = refs[7:]
    x1 = x_ref[...] + mod_ref[0, 2:3, :] * _dot(a_ref[...], wo_ref[...])
    h = _modulate(x1, gf_ref[...], mod_ref[0, 3:4, :], mod_ref[0, 4:5, :]).astype(BF16)
    bounds = list(range(0, D_FF, FF_CHUNK)) + [D_FF]
    acc = None
    for lo, hi in zip(bounds[:-1], bounds[1:]):
        a = _dot(h, win_ref[0, :, lo:hi])
        b = _dot(h, win_ref[0, :, D_FF + lo:D_FF + hi])
        act = ((a * jax.nn.sigmoid(a)) * b).astype(BF16)
        part = _dot(act, wout_ref[0, lo:hi, :])
        acc = part if acc is None else acc + part
    x2 = x1 + mod_ref[0, 5:6, :] * acc
    if final:
        x2 = _rms(x2) * gfin_ref[...]
    o_ref[...] = x2


def _tail(x, a, mod, mod_row, wo, gf, win, wout, layer, gfin, name):
    n = x.shape[0]
    tm = TM_TAIL
    final = gfin is not None
    full = lambda shape: pl.BlockSpec(shape, lambda i: (0,) * len(shape))
    resident = lambda shape: pl.BlockSpec((1,) + shape[1:], lambda i: (layer, 0, 0),
                                          pipeline_mode=pl.Buffered(1))
    in_specs = [
        pl.BlockSpec((tm, D_MODEL), lambda i: (i, 0)),
        pl.BlockSpec((tm, D_MODEL), lambda i: (i, 0)),
        pl.BlockSpec((1, N_MOD, D_MODEL), lambda i: (mod_row(i), 0, 0)),
        full(wo.shape), full((1, D_MODEL)), resident(win.shape), resident(wout.shape),
    ]
    args = [x, a, mod, wo, gf, win, wout]
    if final:
        in_specs.append(full((1, D_MODEL)))
        args.append(gfin)
    return pl.pallas_call(
        functools.partial(_tail_kernel, final=final),
        grid=(n // tm,),
        in_specs=in_specs,
        out_specs=pl.BlockSpec((tm, D_MODEL), lambda i: (i, 0)),
        out_shape=jax.ShapeDtypeStruct((n, D_MODEL), F32),
        compiler_params=_params(("arbitrary",)),
        name=name,
    )(*args)


def _split3(x):
    hi = x.astype(BF16)
    r1 = x - hi.astype(F32)
    mid = r1.astype(BF16)
    lo = (r1 - mid.astype(F32)).astype(BF16)
    return hi, mid, lo


def _proj1_kernel(x_ref, mod_ref, g_ref, w1_ref, wg_ref, bg_ref, q_ref, k_ref, v_ref, o_ref, gc_ref, gr_ref):
    tm = x_ref.shape[0]
    L = MLSTM_L
    hk = H_C * DK_C
    hv = H_C * DV_C
    h = _modulate(x_ref[...], g_ref[...], mod_ref[0, 0:1, :], mod_ref[0, 1:2, :]).astype(BF16)
    gates = _dot(h, wg_ref[...]) + bg_ref[...]
    proj = _dot(h, w1_ref[...])
    for blk in range(hk // LANES):
        sl = slice(blk * LANES, (blk + 1) * LANES)
        q_ref[sl, :] = jnp.transpose(proj[:, sl]).astype(BF16)
    for blk in range(hv // LANES):
        sl = slice(blk * LANES, (blk + 1) * LANES)
        v_ref[sl, :] = jnp.transpose(proj[:, 2 * hk + blk * LANES:2 * hk + (blk + 1) * LANES]).astype(BF16)
    k_ref[...] = proj[:, hk:2 * hk] * (DK_C ** -0.5)
    o_ref[...] = proj[:, 2 * hk + hv:2 * hk + 2 * hv]
    lf = jnp.minimum(gates, 0.0) - jnp.log1p(jnp.exp(-jnp.abs(gates)))
    r = lax.broadcasted_iota(jnp.int32, (L, L), 0)
    c = lax.broadcasted_iota(jnp.int32, (L, L), 1)
    pre = jnp.where(c <= r, 1.0, 0.0).astype(BF16)
    suf = jnp.where(c >= r, 1.0, 0.0).astype(BF16)
    parts = jnp.concatenate(_split3(lf), axis=1)
    fold = lambda t: t[:, 0:LANES] + t[:, LANES:2 * LANES] + t[:, 2 * LANES:3 * LANES]
    chunks = [parts[ck * L:(ck + 1) * L] for ck in range(tm // L)]
    lane = lax.broadcasted_iota(jnp.int32, (tm, LANES), 1)
    b_sum = jnp.where((lane % 4) >= 2,
                      jnp.concatenate([fold(_dot(suf, p)) for p in chunks], axis=0),
                      jnp.concatenate([fold(_dot(pre, p)) for p in chunks], axis=0))
    b = pltpu.roll(b_sum, LANES - N_GATE, 1)
    u = gates - b
    pos = lax.broadcasted_iota(jnp.int32, (tm, LANES), 0) % L
    bwd = (lane % 4) >= 2
    cm = u
    step = 1
    while step < L:
        below = jnp.where(pos >= step, pltpu.roll(cm, step, 0), -jnp.inf)
        above = jnp.where(pos < L - step, pltpu.roll(cm, tm - step, 0), -jnp.inf)
        cm = jnp.maximum(cm, jnp.where(bwd, above, below))
        step *= 2
    low = lane < N_GATE
    packed = (jnp.where(low, b, 0.0) + pltpu.roll(jnp.where(low, cm, 0.0), N_GATE, 1)
              + pltpu.roll(jnp.where(low, u, 0.0), 2 * N_GATE, 1))
    gc_ref[...] = packed
    gr_ref[...] = jnp.transpose(packed)[0:3 * N_GATE]


def _proj1(x, mod, mod_row, g, w1, wg, bg):
    n = x.shape[0]
    tm = TM_PROJ1
    assert tm % MLSTM_L == 0
    full = lambda shape: pl.BlockSpec(shape, lambda i: (0,) * len(shape))
    row = lambda w: pl.BlockSpec((tm, w), lambda i: (i, 0))
    col = lambda w: pl.BlockSpec((w, tm), lambda i: (0, i))
    hk, hv = H_C * DK_C, H_C * DV_C
    return pl.pallas_call(
        _proj1_kernel,
        grid=(n // tm,),
        in_specs=[row(D_MODEL), pl.BlockSpec((1, N_MOD, D_MODEL), lambda i: (mod_row(i), 0, 0)),
                  full((1, D_MODEL)), full((D_MODEL, 2 * hk + 2 * hv)), full(wg.shape), full((1, LANES))],
        out_specs=[col(hk), row(hk), col(hv), row(hv), row(LANES),
                   pl.BlockSpec((3 * N_GATE, tm), lambda i: (0, i))],
        out_shape=[jax.ShapeDtypeStruct((hk, n), BF16), jax.ShapeDtypeStruct((n, hk), F32),
                   jax.ShapeDtypeStruct((hv, n), BF16), jax.ShapeDtypeStruct((n, hv), F32),
                   jax.ShapeDtypeStruct((n, LANES), F32), jax.ShapeDtypeStruct((3 * N_GATE, n), F32)],
        compiler_params=_params(("arbitrary",)),
        name="proj1",
    )(x, mod, g, w1, wg, bg)


def _chain(j, d, hh):
    return (2 * j + d) * 2 + hh


def _mlstm_kernel(*refs, has_state, emit_state):
    q_ref, k_ref, v_ref, o_ref, gc_ref, gr_ref, gn_ref = refs[:7]
    refs = refs[7:]
    if has_state:
        c0_ref, n0_ref, m0_ref = refs[:3]
        refs = refs[3:]
    hs_ref = refs[0]
    refs = refs[1:]
    if emit_state:
        cf_ref, nst_ref, mst_ref = refs[:3]
        refs = refs[3:]
    cx_s, m_s, h_s = refs

    L = MLSTM_L
    seq = k_ref.shape[0]
    nc = seq // L
    npair = H_C // 2
    lane = lax.broadcasted_iota(jnp.int32, (1, LANES), 1)
    head_mask = [lane < DK_C, lane >= DK_C]
    ri = lax.broadcasted_iota(jnp.int32, (L, L), 0)
    ci = lax.broadcasted_iota(jnp.int32, (L, L), 1)
    causal = [ri <= ci, ri >= ci]
    ones_blk = jnp.ones((N_ROWS_N, L), BF16)
    chains = [(j, d, hh) for d in range(2) for j in range(npair) for hh in range(2)]

    h_s[...] = jnp.zeros_like(h_s)
    for j, d, hh in chains:
        ch = _chain(j, d, hh)
        if has_state:
            zpad = jnp.zeros((DK_C, DV_C), F32)
            c0 = c0_ref[0, d, 2 * j + hh]
            c0 = jnp.concatenate([c0, zpad] if hh == 0 else [zpad, c0], axis=0)
            cx_s[ch, 0:DV_C, :] = jnp.transpose(c0)
            n_row = jnp.where(head_mask[hh], n0_ref[0, d, j], 0.0)
            cx_s[ch, DV_C:DV_C + N_ROWS_N, :] = jnp.broadcast_to(n_row, (N_ROWS_N, LANES))
            m_s[ch] = m0_ref[0, d, 2 * j + hh]
        else:
            cx_s[ch] = jnp.zeros((DV_C + N_ROWS_N, LANES), F32)
            m_s[ch] = jnp.zeros((1, LANES), F32)

    def chunk_step(i, carry):
        sl = [pl.ds(pl.multiple_of(i * L, L), L), pl.ds(pl.multiple_of((nc - 1 - i) * L, L), L)]
        gcol = [gc_ref[sl[d], :] for d in range(2)]
        grow = [gr_ref[:, sl[d]] for d in range(2)]
        kpair = {(d, j): k_ref[sl[d], j * LANES:(j + 1) * LANES] for d in range(2) for j in range(npair)}
        kbf = {key: kk.astype(BF16) for key, kk in kpair.items()}
        zq = jnp.zeros((DK_C, L), BF16)
        qt = {}
        for d in range(2):
            for j in range(npair):
                qt[(j, d, 0)] = jnp.concatenate([q_ref[j * LANES:j * LANES + DK_C, sl[d]], zq], axis=0)
                qt[(j, d, 1)] = jnp.concatenate([zq, q_ref[j * LANES + DK_C:(j + 1) * LANES, sl[d]]], axis=0)
        st, u_bc, vt, row = {}, {}, {}, {}
        for d in range(2):
            for j in range(npair):
                pair_scores = _dot(kbf[(d, j)], jnp.concatenate([qt[(j, d, 0)], qt[(j, d, 1)]], axis=1))
                st[(j, d, 0)], st[(j, d, 1)] = pair_scores[:, 0:L], pair_scores[:, L:2 * L]
        for j, d, hh in chains:
            key = (j, d, hh)
            idx = 4 * j + 2 * d + hh
            edge = L - 1 if d == 0 else 0
            b_row = grow[d][idx:idx + 1, :]
            cm_row = grow[d][N_GATE + idx:N_GATE + idx + 1, :]
            u_col = gcol[d][:, 2 * N_GATE + idx:2 * N_GATE + idx + 1]
            m_prev = m_s[_chain(j, d, hh)][:, 0:1]
            row[key] = (b_row, cm_row, b_row[:, edge:edge + 1], cm_row[:, edge:edge + 1], m_prev)
            u_bc[key] = jnp.broadcast_to(u_col, (L, L))
            vt[key] = v_ref[(2 * j + hh) * DV_C:(2 * j + hh + 1) * DV_C, sl[d]]
        for j, d, hh in chains:
            key = (j, d, hh)
            ch = _chain(j, d, hh)
            b_row, cm_row, g_tot, cm_last, m_prev = row[key]
            head = slice((2 * j + hh) * DV_C, (2 * j + hh + 1) * DV_C)
            cx = cx_s[ch]
            big_m = jnp.maximum(m_prev, cm_row)
            s = st[key] * jnp.exp(jnp.where(causal[d], u_bc[key] - big_m, -jnp.inf))
            inter = jnp.exp(m_prev - big_m)
            vx = jnp.concatenate([vt[key], ones_blk], axis=0)
            lhs = jnp.concatenate([vx, cx.astype(BF16)], axis=1)
            rhs = jnp.concatenate([s.astype(BF16), (inter * qt[key].astype(F32)).astype(BF16)], axis=0)
            res = _dot(lhs, rhs)
            den = res[DV_C:DV_C + 1, :]
            hval = res[0:DV_C, :] * (1.0 / jnp.maximum(jnp.abs(den), jnp.exp(-(b_row + big_m))))
            h_s[head, sl[d]] = h_s[head, sl[d]] + hval

            m_top = jnp.maximum(m_prev, cm_last)
            kw = kpair[(d, j)] * jnp.exp(u_bc[key] - m_top)
            cx_s[ch] = jnp.exp(m_prev - m_top) * cx + _dot(vx, kw.astype(BF16))
            m_s[ch] = jnp.broadcast_to(g_tot + m_top, (1, LANES))
        return carry

    lax.fori_loop(0, nc, chunk_step, 0)

    gn_col = jnp.broadcast_to(gn_ref[...], (DV_C, LANES))
    gn_col = jnp.concatenate([gn_col] * (seq // LANES), axis=1)
    for hd in range(H_C):
        sl = slice(hd * DV_C, (hd + 1) * DV_C)
        ht = h_s[sl, :]
        ms = jnp.mean(ht * ht, axis=0, keepdims=True)
        y = jnp.transpose((ht * lax.rsqrt(ms + RMS_EPS)) * gn_col) * jax.nn.sigmoid(o_ref[:, sl])
        hs_ref[:, sl] = y.astype(BF16)
    if emit_state:
        for j, d, hh in chains:
            ch = _chain(j, d, hh)
            cf_ref[0, 0, d, 2 * j + hh] = jnp.transpose(cx_s[ch, 0:DV_C, :])[hh * DK_C:(hh + 1) * DK_C, :]
            mst_ref[0, d * H_C + 2 * j + hh:d * H_C + 2 * j + hh + 1, :] = m_s[ch]
        for j in range(npair):
            for d in range(2):
                nst_ref[0, d * npair + j:d * npair + j + 1, :] = jnp.where(
                    head_mask[0], cx_s[_chain(j, d, 0), DV_C:DV_C + 1, :], cx_s[_chain(j, d, 1), DV_C:DV_C + 1, :])


def _mlstm(q, k, v, o, gc, gr, gn, state, seq, emit_state):
    n = k.shape[0]
    nb = n // seq
    npair = H_C // 2
    has_state = state is not None
    hk, hv = H_C * DK_C, H_C * DV_C
    in_specs = [
        pl.BlockSpec((hk, seq), lambda b: (0, b)),
        pl.BlockSpec((seq, hk), lambda b: (b, 0)),
        pl.BlockSpec((hv, seq), lambda b: (0, b)),
        pl.BlockSpec((seq, hv), lambda b: (b, 0)),
        pl.BlockSpec((seq, LANES), lambda b: (b, 0)),
        pl.BlockSpec((3 * N_GATE, seq), lambda b: (0, b)),
        pl.BlockSpec((DV_C, 1), lambda b: (0, 0)),
    ]
    args = [q, k, v, o, gc, gr, gn]
    if has_state:
        c0, n0, m0 = state
        in_specs += [pl.BlockSpec((1, 2, H_C, DK_C, DV_C), lambda b: (b, 0, 0, 0, 0)),
                     pl.BlockSpec((1, 2, npair, 1, LANES), lambda b: (b, 0, 0, 0, 0)),
                     pl.BlockSpec((1, 2, H_C, 1, LANES), lambda b: (b, 0, 0, 0, 0))]
        args += [c0, n0, m0]
    out_specs = [pl.BlockSpec((seq, hv), lambda b: (b, 0))]
    out_shape = [jax.ShapeDtypeStruct((n, hv), BF16)]
    if emit_state:
        out_specs += [pl.BlockSpec((1, 1, 2, H_C, DK_C, DV_C), lambda b: (b, 0, 0, 0, 0, 0)),
                      pl.BlockSpec((1, 2 * npair, LANES), lambda b: (b, 0, 0)),
                      pl.BlockSpec((1, 2 * H_C, LANES), lambda b: (b, 0, 0))]
        out_shape += [jax.ShapeDtypeStruct((nb, 1, 2, H_C, DK_C, DV_C), F32),
                      jax.ShapeDtypeStruct((nb, 2 * npair, LANES), F32),
                      jax.ShapeDtypeStruct((nb, 2 * H_C, LANES), F32)]
    n_chain = 2 * H_C
    return pl.pallas_call(
        functools.partial(_mlstm_kernel, has_state=has_state, emit_state=emit_state),
        grid=(nb,),
        in_specs=in_specs, out_specs=out_specs, out_shape=out_shape,
        scratch_shapes=[pltpu.VMEM((n_chain, DV_C + N_ROWS_N, LANES), F32), pltpu.VMEM((n_chain, 1, LANES), F32),
                        pltpu.VMEM((hv, seq), F32)],
        compiler_params=_params(("arbitrary",)),
        name="mlstm_lat" if has_state else "mlstm_ctx",
    )(*args)


def _rope_tables(n_tok):
    t = np.arange(n_tok)
    rows = (t // GRID_W).astype(np.float64)
    cols = (t % GRID_W).astype(np.float64)

    def axis_tabs(width, lane0):
        half = width // 2
        quarter = half // 2
        freqs = np.power(ROPE_BASE, -np.arange(quarter, dtype=np.float64) / quarter)
        c = np.ones((n_tok, LANES))
        sa = np.zeros((n_tok, LANES))
        sb = np.zeros((n_tok, LANES))
        for g, pos in enumerate((rows, cols)):
            ang = pos[:, None] * freqs[None, :]
            a0 = lane0 + g * half
            c[:, a0:a0 + quarter] = np.cos(ang)
            c[:, a0 + quarter:a0 + half] = np.cos(ang)
            sa[:, a0:a0 + quarter] = -np.sin(ang)
            sb[:, a0 + quarter:a0 + half] = np.sin(ang)
        return c, sa, sb

    cm, sam, sbm = axis_tabs(QK_ROPE, ROPE_LANE0)
    c0, sa0, sb0 = axis_tabs(DIFF_HD, 0)
    c1, sa1, sb1 = axis_tabs(DIFF_HD, DIFF_HD)
    cd = np.where(np.arange(LANES)[None, :] < DIFF_HD, c0, c1)
    return tuple(jnp.asarray(a, F32) for a in (cm, sam, sbm, cd, sa0 + sa1, sb0 + sb1))


def _prep_even(w_in_ab, w_uq, w_ukv):
    c2 = Q_LORA + KV_LORA
    w_ab = w_in_ab.astype(BF16)
    wd = w_ab[:, c2 + QK_ROPE:]
    wq = jnp.pad(w_uq.reshape(Q_LORA, H_A, QK_NOPE + QK_ROPE),
                 ((0, 0), (0, 0), (0, HEAD_PAD - QK_NOPE - QK_ROPE))).reshape(Q_LORA, H_A * HEAD_PAD)
    kvw = w_ukv.reshape(KV_LORA, H_A, QK_NOPE + V_HD_A)
    kpad = jnp.pad(kvw[..., :QK_NOPE], ((0, 0), (0, 0), (0, HEAD_PAD - QK_NOPE)))
    vw = kvw[..., QK_NOPE:]
    zv = jnp.zeros_like(vw)
    odd = (jnp.arange(H_A) % 2 == 1)[None, :, None]
    vpad = jnp.where(odd, jnp.concatenate([zv, vw], -1), jnp.concatenate([vw, zv], -1))
    wukv = jnp.concatenate([kpad.reshape(KV_LORA, -1), vpad.reshape(KV_LORA, -1)], axis=1)
    return (w_ab, wd), wq.astype(BF16), wukv.astype(BF16)


def _gate_order(g):
    lead = g.shape[:-1]
    g = g.reshape(lead + (2, 2, H_C // 2, 2))
    perm = tuple(range(len(lead))) + tuple(len(lead) + a for a in (1, 2, 0, 3))
    return g.transpose(perm).reshape(lead + (4 * H_C,))


def _prep_odd(w_in_c, b_gate_c):
    ng = 4 * H_C
    base = w_in_c.shape[1] - ng
    w1 = w_in_c.astype(BF16)
    wg = jnp.pad(_gate_order(w1[:, base:]), ((0, 0), (0, LANES - ng)))
    bg = jnp.pad(_gate_order(b_gate_c), (0, LANES - ng)).reshape(1, LANES)
    return w1, wg, bg


def kernel(x_prompt, x_sample, cache_mla_ckv, cache_mla_krope, cache_diff_k, cache_diff_v,
           state_mlstm_C, state_mlstm_n, state_mlstm_m, c, c_ctx,
           w_ada, b_ada, g_mix, g_ffn, w_ffn_in, w_ffn_out,
           w_in_ab, g_q_lora, g_kv_lora, w_uq, w_ukv, diff_lambda, g_diff_subln, w_out_ab,
           w_in_c, b_gate_c, g_mlstm, w_out_c, g_final):
    nbp, seq_p, _ = x_prompt.shape
    nbs, seq_s, _ = x_sample.shape
    past = cache_mla_ckv.shape[2]
    assert DEPTH == 2 and 1 + nbs <= COND_ROWS
    assert cache_mla_ckv.shape[1] == 1 and state_mlstm_C.shape[1] == 1

    cond = jnp.concatenate([c_ctx[None], c, jnp.zeros((COND_ROWS - 1 - nbs, D_MODEL), F32)], axis=0)
    mod = _ada(cond, w_ada, b_ada).reshape(DEPTH * COND_ROWS, N_MOD, D_MODEL)

    xp = x_prompt.reshape(nbp * seq_p, D_MODEL)
    xs = x_sample.reshape(nbs * seq_s, D_MODEL)
    row2 = lambda v: v.reshape(1, -1)

    def mod_rows(layer, tm):
        prompt = lambda i: layer * COND_ROWS
        sample = lambda i: layer * COND_ROWS + 1 + i // (seq_s // tm)
        return prompt, sample

    lam_init = 0.8 - 0.6 * math.exp(-0.3 * 0)
    w0, wq, wukv = _prep_even(w_in_ab[0], w_uq[0], w_ukv[0])
    rope_tabs = _rope_tables(seq_s)
    mrp, _ = mod_rows(0, TM_PROJ)
    _, mrs = mod_rows(0, TM_PROJ_ROPE)
    gq, gkv = row2(g_q_lora[0]), row2(g_kv_lora[0])
    outs_p = _proj0(xp, mod, mrp, row2(g_mix[0]), *w0, gq, gkv, wq, wukv, None, seq_p)
    outs_s = _proj0(xs, mod, mrs, row2(g_mix[0]), *w0, gq, gkv, wq, wukv, rope_tabs, seq_s)
    qm_p, km_p, vm_p, dq_p, dk_p, dv_p, ckv_new, kr_new, dk_new, dv_new = outs_p
    qm_s, km_s, vm_s, dq_s, dk_s, dv_s = outs_s
    gsub = row2(g_diff_subln[0])
    krb_c = jnp.pad(cache_mla_krope.reshape(nbs, past, QK_ROPE),
                    ((0, 0), (0, 0), (ROPE_LANE0, LANES - ROPE_LANE0 - QK_ROPE)))
    ctx = (cache_mla_ckv.reshape(nbs, past, KV_LORA), krb_c,
           cache_diff_k.reshape(nbs, H_B, past, 2 * DIFF_HD), cache_diff_v.reshape(nbs, H_B, past, DIFF_VD))
    a_p = _attn0(qm_p, km_p, vm_p, dq_p, dk_p, dv_p, diff_lambda[0], gsub, None, None, seq_p, lam_init)
    a_s = _attn0(qm_s, km_s, vm_s, dq_s, dk_s, dv_s, diff_lambda[0], gsub, ctx, wukv, seq_s, lam_init)
    mtp, mts = mod_rows(0, TM_TAIL)
    wo0 = w_out_ab[0].astype(BF16)
    win, wout = w_ffn_in.astype(BF16), w_ffn_out.astype(BF16)
    xp = _tail(xp, a_p, mod, mtp, wo0, row2(g_ffn[0]), win, wout, 0, None, "tail0_ctx")
    xs = _tail(xs, a_s, mod, mts, wo0, row2(g_ffn[0]), win, wout, 0, None, "tail0_lat")

    w1, wg, bg = _prep_odd(w_in_c[0], b_gate_c[0])
    mrp, mrs = mod_rows(1, TM_PROJ1)
    gn = g_mlstm[0].reshape(DV_C, 1)
    q_p, k_p, v_p, o_p, gc_p, gr_p = _proj1(xp, mod, mrp, row2(g_mix[1]), w1, wg, bg)
    q_s, k_s, v_s, o_s, gc_s, gr_s = _proj1(xs, mod, mrs, row2(g_mix[1]), w1, wg, bg)
    hs_p, c_new, nst, mst = _mlstm(q_p, k_p, v_p, o_p, gc_p, gr_p, gn, None, seq_p, True)
    state = (state_mlstm_C.reshape(nbs, 2, H_C, DK_C, DV_C),
             state_mlstm_n.reshape(nbs, 2, H_C // 2, 1, LANES),
             jnp.broadcast_to(state_mlstm_m.reshape(nbs, 2, H_C, 1, 1), (nbs, 2, H_C, 1, LANES)))
    (hs_s,) = _mlstm(q_s, k_s, v_s, o_s, gc_s, gr_s, gn, state, seq_s, False)
    mtp, mts = mod_rows(1, TM_TAIL)
    wo1 = w_out_c[0].astype(BF16)
    gfin = row2(g_final)
    yp = _tail(xp, hs_p, mod, mtp, wo1, row2(g_ffn[1]), win, wout, 1, gfin, "tail1_ctx")
    ys = _tail(xs, hs_s, mod, mts, wo1, row2(g_ffn[1]), win, wout, 1, gfin, "tail1_lat")

    return (yp.reshape(nbp, seq_p, D_MODEL), ys.reshape(nbs, seq_s, D_MODEL),
            ckv_new.reshape(nbp, 1, seq_p, KV_LORA), kr_new.reshape(nbp, 1, seq_p, QK_ROPE),
            dk_new.reshape(nbp, 1, H_B, seq_p, 2 * DIFF_HD), dv_new.reshape(nbp, 1, H_B, seq_p, DIFF_VD),
            c_new, nst.reshape(nbp, 1, 2, H_C, DK_C), mst[:, :, 0].reshape(nbp, 1, 2, H_C))
```

```python
import functools
import math

import jax
import jax.numpy as jnp
import numpy as np
from jax import lax
from jax.experimental import pallas as pl
from jax.experimental.pallas import tpu as pltpu

F32 = jnp.float32
BF16 = jnp.bfloat16

D_MODEL = 1024
DEPTH = 2
GRID_W = 64
ROPE_BASE = 10000.0
RMS_EPS = 1e-6
H_A = 8
QK_NOPE = 64
QK_ROPE = 32
V_HD_A = 64
Q_LORA = 256
KV_LORA = 128
H_B = 4
DIFF_HD = 64
DIFF_VD = 2 * DIFF_HD
H_C = 8
DK_C = 64
DV_C = D_MODEL // H_C
D_FF = -(-8 * D_MODEL // (3 * 256)) * 256

LANES = 128
HEAD_PAD = LANES
ROPE_LANE0 = QK_NOPE
N_MOD = 6
COND_ROWS = 16
N_GATE = 2 * H_C
N_ROWS_N = 16

TM_PROJ = 1024
TM_PROJ_ROPE = 512
TM_PROJ1 = 1024
TQ_ATTN = 512
TM_TAIL = 1024
FF_CHUNK = 256
MLSTM_L = 128
ADA_TN = 1024
VMEM_LIMIT = 56 * 1024 * 1024

assert D_FF % LANES == 0 and FF_CHUNK % LANES == 0


def _dot(a, b):
    return jnp.dot(a, b, preferred_element_type=F32)


def _dot_nt(a, b):
    return lax.dot_general(a, b, (((1,), (1,)), ((), ())), preferred_element_type=F32)


def _dot_tn(a, b):
    return lax.dot_general(a, b, (((0,), (0,)), ((), ())), preferred_element_type=F32)


def _rms(x):
    return x * lax.rsqrt(jnp.mean(x * x, axis=-1, keepdims=True) + RMS_EPS)


def _modulate(x, g, shift, scale):
    return _rms(x) * (g * (1.0 + scale)) + shift


def _params(semantics):
    return pltpu.CompilerParams(dimension_semantics=semantics, vmem_limit_bytes=VMEM_LIMIT)


def _ada_kernel(cond_ref, w_ref, b_ref, o_ref):
    c = cond_ref[...]
    s = (c * jax.nn.sigmoid(c)).astype(BF16)
    o_ref[0] = _dot(s, w_ref[0].astype(BF16)) + b_ref[0]


def _ada(cond, w_ada, b_ada):
    n_out = w_ada.shape[-1]
    return pl.pallas_call(
        _ada_kernel,
        grid=(DEPTH, n_out // ADA_TN),
        in_specs=[
            pl.BlockSpec((COND_ROWS, D_MODEL), lambda l, n: (0, 0)),
            pl.BlockSpec((1, D_MODEL, ADA_TN), lambda l, n: (l, 0, n)),
            pl.BlockSpec((1, 1, ADA_TN), lambda l, n: (l, 0, n)),
        ],
        out_specs=pl.BlockSpec((1, COND_ROWS, ADA_TN), lambda l, n: (l, 0, n)),
        out_shape=jax.ShapeDtypeStruct((DEPTH, COND_ROWS, n_out), F32),
        compiler_params=_params(("arbitrary", "arbitrary")),
        name="ada",
    )(cond, w_ada, b_ada.reshape(DEPTH, 1, n_out))


def _rope(x, c, sa, sb, off):
    return x * c + pltpu.roll(x, LANES - off, 1) * sa + pltpu.roll(x, off, 1) * sb


def _proj0_kernel(*refs, rope, emit_cache):
    x_ref, mod_ref, g_ref, wa_ref, wd_ref, gq_ref, gkv_ref, wq_ref, wukv_ref = refs[:9]
    refs = refs[9:]
    if rope:
        cm_ref, sam_ref, sbm_ref, cd_ref, sad_ref, sbd_ref = refs[:6]
        refs = refs[6:]
    qm_ref, km_ref, vm_ref, dq_ref, dk_ref, dv_ref = refs[:6]
    refs = refs[6:]
    if emit_cache:
        ckvf_ref, krf_ref, dkf_ref, dvf_ref = refs

    h = _modulate(x_ref[...], g_ref[...], mod_ref[0, 0:1, :], mod_ref[0, 1:2, :]).astype(BF16)
    pa = _dot(h, wa_ref[...])
    pd = _dot(h, wd_ref[...])
    lane = lax.broadcasted_iota(jnp.int32, (pa.shape[0], LANES), 1)
    krb = jnp.where((lane >= ROPE_LANE0) & (lane < ROPE_LANE0 + QK_ROPE),
                    pltpu.roll(pa[:, Q_LORA + KV_LORA:Q_LORA + KV_LORA + LANES], ROPE_LANE0, 1), 0.0)
    cq = pa[:, 0:Q_LORA]
    ckv = _rms(pa[:, Q_LORA:Q_LORA + KV_LORA]) * gkv_ref[...]
    qa = _dot((_rms(cq) * gq_ref[...]).astype(BF16), wq_ref[...])
    kv = _dot(ckv.astype(BF16), wukv_ref[...])
    if emit_cache:
        ckvf_ref[...] = ckv
        krf_ref[...] = krb[:, ROPE_LANE0:ROPE_LANE0 + QK_ROPE]
    if rope:
        cm, sam, sbm = cm_ref[...], sam_ref[...], sbm_ref[...]
        cd, sad, sbd = cd_ref[...], sad_ref[...], sbd_ref[...]
        krb = _rope(krb, cm, sam, sbm, QK_ROPE // 4)
    for hd in range(H_A):
        sl = slice(hd * HEAD_PAD, (hd + 1) * HEAD_PAD)
        qh = qa[:, sl]
        if rope:
            qh = _rope(qh, cm, sam, sbm, QK_ROPE // 4)
        qm_ref[:, sl] = qh.astype(BF16)
        km_ref[:, sl] = (kv[:, sl] + krb).astype(BF16)
    vm_ref[...] = kv[:, H_A * HEAD_PAD:].astype(BF16)
    for hd in range(H_B):
        sl = slice(hd * LANES, (hd + 1) * LANES)
        dq = pd[:, hd * LANES:(hd + 1) * LANES]
        dk = pd[:, (H_B + hd) * LANES:(H_B + hd + 1) * LANES]
        dv = pd[:, (2 * H_B + hd) * LANES:(2 * H_B + hd + 1) * LANES]
        if emit_cache:
            seq = dkf_ref.shape[2]
            for bi in range(dkf_ref.shape[0]):
                dkf_ref[bi, hd] = dk[bi * seq:(bi + 1) * seq]
                dvf_ref[bi, hd] = dv[bi * seq:(bi + 1) * seq]
        if rope:
            dq = _rope(dq, cd, sad, sbd, DIFF_HD // 4)
            dk = _rope(dk, cd, sad, sbd, DIFF_HD // 4)
        dq_ref[:, sl] = (dq * (DIFF_HD ** -0.5)).astype(BF16)
        dk_ref[:, sl] = dk.astype(BF16)
        dv_ref[:, sl] = dv.astype(BF16)


def _proj0(x, mod, mod_row, g, w_ab, wd, gq, gkv, wq, wukv, rope_tabs, seq):
    n = x.shape[0]
    rope = rope_tabs is not None
    emit_cache = not rope
    tm = TM_PROJ_ROPE if rope else TM_PROJ
    tiles_per_seq = max(seq // tm, 1)
    full = lambda shape: pl.BlockSpec(shape, lambda i: (0,) * len(shape))
    in_specs = [
        pl.BlockSpec((tm, D_MODEL), lambda i: (i, 0)),
        pl.BlockSpec((1, N_MOD, D_MODEL), lambda i: (mod_row(i), 0, 0)),
        full((1, D_MODEL)), full((D_MODEL, Q_LORA + KV_LORA + LANES)), full(wd.shape),
        full((1, Q_LORA)), full((1, KV_LORA)), full(wq.shape), full(wukv.shape),
    ]
    args = [x, mod, g, w_ab, wd, gq, gkv, wq, wukv]
    if rope:
        in_specs += [pl.BlockSpec((tm, LANES), lambda i: (i % tiles_per_seq, 0))] * 6
        args += list(rope_tabs)
    row = lambda w: pl.BlockSpec((tm, w), lambda i: (i, 0))
    out_specs = [row(1024), row(1024), row(1024), row(512), row(512), row(512)]
    out_shape = [jax.ShapeDtypeStruct((n, w), BF16) for w in (1024, 1024, 1024, 512, 512, 512)]
    if emit_cache:
        assert tm % seq == 0
        cache = pl.BlockSpec((tm // seq, H_B, seq, LANES), lambda i: (i, 0, 0, 0))
        out_specs += [row(KV_LORA), row(QK_ROPE), cache, cache]
        out_shape += [jax.ShapeDtypeStruct((n, KV_LORA), F32), jax.ShapeDtypeStruct((n, QK_ROPE), F32),
                      jax.ShapeDtypeStruct((n // seq, H_B, seq, LANES), F32),
                      jax.ShapeDtypeStruct((n // seq, H_B, seq, LANES), F32)]
    return pl.pallas_call(
        functools.partial(_proj0_kernel, rope=rope, emit_cache=emit_cache),
        grid=(n // tm,),
        in_specs=in_specs, out_specs=out_specs, out_shape=out_shape,
        compiler_params=_params(("arbitrary",)),
        name="proj0_rope" if rope else "proj0_ctx",
    )(*args)


def _attn_kernel(*refs, has_ctx, lam_init):
    q_ref, dq_ref, k_ref, v_ref, dk_ref, dv_ref, lamv_ref, gsub_ref = refs[:8]
    refs = refs[8:]
    if has_ctx:
        ckvc_ref, krc_ref, dkc_ref, dvc_ref, wukv_ref = refs[:5]
        refs = refs[5:]
    o_ref, kx_s, vx_s, dkx_s, dvx_s = refs
    seq = k_ref.shape[0]
    ktot = kx_s.shape[0]

    @pl.when(pl.program_id(1) == 0)
    def _():
        kx_s[0:seq, :] = k_ref[...]
        dkx_s[0:seq, :] = dk_ref[...]
        ones_own = jnp.ones((seq, LANES), BF16)
        for hd in range(H_A):
            vx_s[0:seq, 2 * hd * LANES:(2 * hd + 1) * LANES] = v_ref[:, hd * LANES:(hd + 1) * LANES]
            vx_s[0:seq, (2 * hd + 1) * LANES:(2 * hd + 2) * LANES] = ones_own
        for hd in range(H_B):
            dvx_s[0:seq, 2 * hd * LANES:(2 * hd + 1) * LANES] = dv_ref[:, hd * LANES:(hd + 1) * LANES]
            dvx_s[0:seq, (2 * hd + 1) * LANES:(2 * hd + 2) * LANES] = ones_own
        if has_ctx:
            kv = _dot(ckvc_ref[0].astype(BF16), wukv_ref[...])
            krb = krc_ref[0]
            ones_ctx = jnp.ones((ktot - seq, LANES), BF16)
            for hd in range(H_A):
                sl = slice(hd * HEAD_PAD, (hd + 1) * HEAD_PAD)
                kx_s[seq:ktot, sl] = (kv[:, sl] + krb).astype(BF16)
                vx_s[seq:ktot, 2 * hd * LANES:(2 * hd + 1) * LANES] = (
                    kv[:, (H_A + hd) * HEAD_PAD:(H_A + hd + 1) * HEAD_PAD].astype(BF16))
                vx_s[seq:ktot, (2 * hd + 1) * LANES:(2 * hd + 2) * LANES] = ones_ctx
            for hd in range(H_B):
                sl = slice(hd * LANES, (hd + 1) * LANES)
                dkx_s[seq:ktot, sl] = dkc_ref[0, hd].astype(BF16)
                dvx_s[seq:ktot, 2 * hd * LANES:(2 * hd + 1) * LANES] = dvc_ref[0, hd].astype(BF16)
                dvx_s[seq:ktot, (2 * hd + 1) * LANES:(2 * hd + 2) * LANES] = ones_ctx

    log2e = 1.0 / math.log(2.0)
    tq = dq_ref.shape[0]
    lo = lax.broadcasted_iota(jnp.int32, (tq, LANES), 1) < DIFF_HD

    jobs = []
    for hd in range(H_A):
        sl = slice(hd * HEAD_PAD, (hd + 1) * HEAD_PAD)
        jobs.append((lambda sl=sl: q_ref[:, sl], (kx_s, sl), (vx_s, hd), (QK_NOPE + QK_ROPE) ** -0.5 * log2e))
    for hd in range(H_B):
        sl = slice(hd * LANES, (hd + 1) * LANES)
        for part in range(2):
            def qfn(sl=sl, part=part):
                dq = dq_ref[:, sl].astype(F32)
                return (jnp.where(lo, dq, 0.0) if part == 0 else jnp.where(lo, 0.0, dq)).astype(BF16)
            jobs.append((qfn, (dkx_s, sl), (dvx_s, hd), log2e))

    def scores(job):
        qfn, (kref, sl), _, _ = job
        return _dot_nt(qfn(), kref[:, sl])

    def finish(s, job):
        _, _, (vref, hd), c = job
        m = jnp.max(s, axis=-1, keepdims=True)
        e = jnp.exp2((s - m) * c).astype(BF16)
        res = _dot(e, vref[:, 2 * hd * LANES:(2 * hd + 2) * LANES])
        return res[:, 0:LANES] * (1.0 / res[:, LANES:2 * LANES])

    outs = []
    s_next = scores(jobs[0])
    for i, job in enumerate(jobs):
        s_cur = s_next
        if i + 1 < len(jobs):
            s_next = scores(jobs[i + 1])
        outs.append(finish(s_cur, job))

    for j in range(H_A // 2):
        o_ref[:, j * LANES:(j + 1) * LANES] = (outs[2 * j] + outs[2 * j + 1]).astype(BF16)
    lv = lamv_ref[...]
    lam = (jnp.exp(jnp.sum(lv[0:1] * lv[1:2], axis=-1, keepdims=True))
           - jnp.exp(jnp.sum(lv[2:3] * lv[3:4], axis=-1, keepdims=True)) + lam_init)
    for hd in range(H_B):
        acc = outs[H_A + 2 * hd] - lam * outs[H_A + 2 * hd + 1]
        ob = (_rms(acc) * gsub_ref[...]) * (1.0 - lam_init)
        o_ref[:, H_A * V_HD_A + hd * LANES:H_A * V_HD_A + (hd + 1) * LANES] = ob.astype(BF16)


def _attn0(qm, km, vm, dq, dk, dv, lamv, gsub, ctx, wukv, seq, lam_init):
    n = qm.shape[0]
    nb = n // seq
    tq = min(TQ_ATTN, seq)
    has_ctx = ctx is not None
    full = lambda shape: pl.BlockSpec(shape, lambda b, t: (0,) * len(shape))
    qrow = lambda w: pl.BlockSpec((tq, w), lambda b, t: (b * (seq // tq) + t, 0))
    krow = lambda w: pl.BlockSpec((seq, w), lambda b, t: (b, 0))
    in_specs = [qrow(1024), qrow(512), krow(1024), krow(1024), krow(512), krow(512),
                full((4, DIFF_HD)), full((1, DIFF_VD))]
    args = [qm, dq, km, vm, dk, dv, lamv, gsub]
    past = 0
    if has_ctx:
        ckv_c, krb_c, dk_c, dv_c = ctx
        past = ckv_c.shape[1]
        in_specs += [pl.BlockSpec((1, past, LANES), lambda b, t: (b, 0, 0)),
                     pl.BlockSpec((1, past, LANES), lambda b, t: (b, 0, 0)),
                     pl.BlockSpec((1, H_B, past, LANES), lambda b, t: (b, 0, 0, 0)),
                     pl.BlockSpec((1, H_B, past, LANES), lambda b, t: (b, 0, 0, 0)),
                     full(wukv.shape)]
        args += [ckv_c, krb_c, dk_c, dv_c, wukv]
    ktot = seq + past
    scratch = [pltpu.VMEM((ktot, H_A * HEAD_PAD), BF16), pltpu.VMEM((ktot, 2 * H_A * LANES), BF16),
               pltpu.VMEM((ktot, H_B * LANES), BF16), pltpu.VMEM((ktot, 2 * H_B * LANES), BF16)]
    return pl.pallas_call(
        functools.partial(_attn_kernel, has_ctx=has_ctx, lam_init=lam_init),
        grid=(nb, seq // tq),
        in_specs=in_specs,
        out_specs=pl.BlockSpec((tq, D_MODEL), lambda b, t: (b * (seq // tq) + t, 0)),
        out_shape=jax.ShapeDtypeStruct((n, D_MODEL), BF16),
        scratch_shapes=scratch,
        compiler_params=_params(("arbitrary", "arbitrary")),
        name="attn0_lat" if has_ctx else "attn0_ctx",
    )(*args)


def _tail_kernel(*refs, final):
    x_ref, a_ref, mod_ref, wo_ref, gf_ref, win_ref, wout_ref = refs[:7]
    if final:
        gfin_ref, o_ref = refs[7:]
    else:
        (o_ref,) = refs[7:]
    x1 = x_ref[...] + mod_ref[0, 2:3, :] * _dot(a_ref[...], wo_ref[...])
    h = _modulate(x1, gf_ref[...], mod_ref[0, 3:4, :], mod_ref[0, 4:5, :]).astype(BF16)
    bounds = list(range(0, D_FF, FF_CHUNK)) + [D_FF]
    acc = None
    for lo, hi in zip(bounds[:-1], bounds[1:]):
        a = _dot(h, win_ref[0, :, lo:hi])
        b = _dot(h, win_ref[0, :, D_FF + lo:D_FF + hi])
        act = ((a * jax.nn.sigmoid(a)) * b).astype(BF16)
        part = _dot(act, wout_ref[0, lo:hi, :])
        acc = part if acc is None else acc + part
    x2 = x1 + mod_ref[0, 5:6, :] * acc
    if final:
        x2 = _rms(x2) * gfin_ref[...]
    o_ref[...] = x2


def _tail(x, a, mod, mod_row, wo, gf, win, wout, layer, gfin, name):
    n = x.shape[0]
    tm = TM_TAIL
    final = gfin is not None
    full = lambda shape: pl.BlockSpec(shape, lambda i: (0,) * len(shape))
    resident = lambda shape: pl.BlockSpec((1,) + shape[1:], lambda i: (layer, 0, 0),
                                          pipeline_mode=pl.Buffered(1))
    in_specs = [
        pl.BlockSpec((tm, D_MODEL), lambda i: (i, 0)),
        pl.BlockSpec((tm, D_MODEL), lambda i: (i, 0)),
        pl.BlockSpec((1, N_MOD, D_MODEL), lambda i: (mod_row(i), 0, 0)),
        full(wo.shape), full((1, D_MODEL)), resident(win.shape), resident(wout.shape),
    ]
    args = [x, a, mod, wo, gf, win, wout]
    if final:
        in_specs.append(full((1, D_MODEL)))
        args.append(gfin)
    return pl.pallas_call(
        functools.partial(_tail_kernel, final=final),
        grid=(n // tm,),
        in_specs=in_specs,
        out_specs=pl.BlockSpec((tm, D_MODEL), lambda i: (i, 0)),
        out_shape=jax.ShapeDtypeStruct((n, D_MODEL), F32),
        compiler_params=_params(("arbitrary",)),
        name=name,
    )(*args)


def _split3(x):
    hi = x.astype(BF16)
    r1 = x - hi.astype(F32)
    mid = r1.astype(BF16)
    lo = (r1 - mid.astype(F32)).astype(BF16)
    return hi, mid, lo


def _proj1_kernel(x_ref, mod_ref, g_ref, w1_ref, wg_ref, bg_ref, q_ref, k_ref, v_ref, o_ref, gc_ref, gr_ref):
    tm = x_ref.shape[0]
    L = MLSTM_L
    hk = H_C * DK_C
    hv = H_C * DV_C
    h = _modulate(x_ref[...], g_ref[...], mod_ref[0, 0:1, :], mod_ref[0, 1:2, :]).astype(BF16)
    gates = _dot(h, wg_ref[...]) + bg_ref[...]
    proj = _dot(h, w1_ref[...])
    for blk in range(hk // LANES):
        sl = slice(blk * LANES, (blk + 1) * LANES)
        q_ref[sl, :] = jnp.transpose(proj[:, sl]).astype(BF16)
    for blk in range(hv // LANES):
        sl = slice(blk * LANES, (blk + 1) * LANES)
        v_ref[sl, :] = jnp.transpose(proj[:, 2 * hk + blk * LANES:2 * hk + (blk + 1) * LANES]).astype(BF16)
    k_ref[...] = proj[:, hk:2 * hk] * (DK_C ** -0.5)
    o_ref[...] = proj[:, 2 * hk + hv:2 * hk + 2 * hv]
    lf = jnp.minimum(gates, 0.0) - jnp.log1p(jnp.exp(-jnp.abs(gates)))
    r = lax.broadcasted_iota(jnp.int32, (L, L), 0)
    c = lax.broadcasted_iota(jnp.int32, (L, L), 1)
    pre = jnp.where(c <= r, 1.0, 0.0).astype(BF16)
    suf = jnp.where(c >= r, 1.0, 0.0).astype(BF16)
    parts = jnp.concatenate(_split3(lf), axis=1)
    fold = lambda t: t[:, 0:LANES] + t[:, LANES:2 * LANES] + t[:, 2 * LANES:3 * LANES]
    chunks = [parts[ck * L:(ck + 1) * L] for ck in range(tm // L)]
    lane = lax.broadcasted_iota(jnp.int32, (tm, LANES), 1)
    b_sum = jnp.where((lane % 4) >= 2,
                      jnp.concatenate([fold(_dot(suf, p)) for p in chunks], axis=0),
                      jnp.concatenate([fold(_dot(pre, p)) for p in chunks], axis=0))
    b = pltpu.roll(b_sum, LANES - N_GATE, 1)
    u = gates - b
    pos = lax.broadcasted_iota(jnp.int32, (tm, LANES), 0) % L
    bwd = (lane % 4) >= 2
    cm = u
    step = 1
    while step < L:
        below = jnp.where(pos >= step, pltpu.roll(cm, step, 0), -jnp.inf)
        above = jnp.where(pos < L - step, pltpu.roll(cm, tm - step, 0), -jnp.inf)
        cm = jnp.maximum(cm, jnp.where(bwd, above, below))
        step *= 2
    low = lane < N_GATE
    packed = (jnp.where(low, b, 0.0) + pltpu.roll(jnp.where(low, cm, 0.0), N_GATE, 1)
              + pltpu.roll(jnp.where(low, u, 0.0), 2 * N_GATE, 1))
    gc_ref[...] = packed
    gr_ref[...] = jnp.transpose(packed)[0:3 * N_GATE]


def _proj1(x, mod, mod_row, g, w1, wg, bg):
    n = x.shape[0]
    tm = TM_PROJ1
    assert tm % MLSTM_L == 0
    full = lambda shape: pl.BlockSpec(shape, lambda i: (0,) * len(shape))
    row = lambda w: pl.BlockSpec((tm, w), lambda i: (i, 0))
    col = lambda w: pl.BlockSpec((w, tm), lambda i: (0, i))
    hk, hv = H_C * DK_C, H_C * DV_C
    return pl.pallas_call(
        _proj1_kernel,
        grid=(n // tm,),
        in_specs=[row(D_MODEL), pl.BlockSpec((1, N_MOD, D_MODEL), lambda i: (mod_row(i), 0, 0)),
                  full((1, D_MODEL)), full((D_MODEL, 2 * hk + 2 * hv)), full(wg.shape), full((1, LANES))],
        out_specs=[col(hk), row(hk), col(hv), row(hv), row(LANES),
                   pl.BlockSpec((3 * N_GATE, tm), lambda i: (0, i))],
        out_shape=[jax.ShapeDtypeStruct((hk, n), BF16), jax.ShapeDtypeStruct((n, hk), F32),
                   jax.ShapeDtypeStruct((hv, n), BF16), jax.ShapeDtypeStruct((n, hv), F32),
                   jax.ShapeDtypeStruct((n, LANES), F32), jax.ShapeDtypeStruct((3 * N_GATE, n), F32)],
        compiler_params=_params(("arbitrary",)),
        name="proj1",
    )(x, mod, g, w1, wg, bg)


def _chain(j, d, hh):
    return (2 * j + d) * 2 + hh


def _mlstm_kernel(*refs, has_state, emit_state):
    q_ref, k_ref, v_ref, o_ref, gc_ref, gr_ref, gn_ref = refs[:7]
    refs = refs[7:]
    if has_state:
        c0_ref, n0_ref, m0_ref = refs[:3]
        refs = refs[3:]
    hs_ref = refs[0]
    refs = refs[1:]
    if emit_state:
        cf_ref, nst_ref, mst_ref = refs[:3]
        refs = refs[3:]
    cx_s, m_s, h_s = refs

    L = MLSTM_L
    seq = k_ref.shape[0]
    nc = seq // L
    npair = H_C // 2
    lane = lax.broadcasted_iota(jnp.int32, (1, LANES), 1)
    head_mask = [lane < DK_C, lane >= DK_C]
    ri = lax.broadcasted_iota(jnp.int32, (L, L), 0)
    ci = lax.broadcasted_iota(jnp.int32, (L, L), 1)
    causal = [ri <= ci, ri >= ci]
    ones_blk = jnp.ones((N_ROWS_N, L), BF16)
    chains = [(j, d, hh) for d in range(2) for j in range(npair) for hh in range(2)]

    h_s[...] = jnp.zeros_like(h_s)
    for j, d, hh in chains:
        ch = _chain(j, d, hh)
        if has_state:
            zpad = jnp.zeros((DK_C, DV_C), F32)
            c0 = c0_ref[0, d, 2 * j + hh]
            c0 = jnp.concatenate([c0, zpad] if hh == 0 else [zpad, c0], axis=0)
            cx_s[ch, 0:DV_C, :] = jnp.transpose(c0)
            n_row = jnp.where(head_mask[hh], n0_ref[0, d, j], 0.0)
            cx_s[ch, DV_C:DV_C + N_ROWS_N, :] = jnp.broadcast_to(n_row, (N_ROWS_N, LANES))
            m_s[ch] = m0_ref[0, d, 2 * j + hh]
        else:
            cx_s[ch] = jnp.zeros((DV_C + N_ROWS_N, LANES), F32)
            m_s[ch] = jnp.zeros((1, LANES), F32)

    def chunk_step(i, carry):
        sl = [pl.ds(pl.multiple_of(i * L, L), L), pl.ds(pl.multiple_of((nc - 1 - i) * L, L), L)]
        gcol = [gc_ref[sl[d], :] for d in range(2)]
        grow = [gr_ref[:, sl[d]] for d in range(2)]
        kpair = {(d, j): k_ref[sl[d], j * LANES:(j + 1) * LANES] for d in range(2) for j in range(npair)}
        kbf = {key: kk.astype(BF16) for key, kk in kpair.items()}
        zq = jnp.zeros((DK_C, L), BF16)
        qt = {}
        for d in range(2):
            for j in range(npair):
                qt[(j, d, 0)] = jnp.concatenate([q_ref[j * LANES:j * LANES + DK_C, sl[d]], zq], axis=0)
                qt[(j, d, 1)] = jnp.concatenate([zq, q_ref[j * LANES + DK_C:(j + 1) * LANES, sl[d]]], axis=0)
        st, u_bc, vt, row = {}, {}, {}, {}
        for d in range(2):
            for j in range(npair):
                pair_scores = _dot(kbf[(d, j)], jnp.concatenate([qt[(j, d, 0)], qt[(j, d, 1)]], axis=1))
                st[(j, d, 0)], st[(j, d, 1)] = pair_scores[:, 0:L], pair_scores[:, L:2 * L]
        for j, d, hh in chains:
            key = (j, d, hh)
            idx = 4 * j + 2 * d + hh
            edge = L - 1 if d == 0 else 0
            b_row = grow[d][idx:idx + 1, :]
            cm_row = grow[d][N_GATE + idx:N_GATE + idx + 1, :]
            u_col = gcol[d][:, 2 * N_GATE + idx:2 * N_GATE + idx + 1]
            m_prev = m_s[_chain(j, d, hh)][:, 0:1]
            row[key] = (b_row, cm_row, b_row[:, edge:edge + 1], cm_row[:, edge:edge + 1], m_prev)
            u_bc[key] = jnp.broadcast_to(u_col, (L, L))
            vt[key] = v_ref[(2 * j + hh) * DV_C:(2 * j + hh + 1) * DV_C, sl[d]]
        for j, d, hh in chains:
            key = (j, d, hh)
            ch = _chain(j, d, hh)
            b_row, cm_row, g_tot, cm_last, m_prev = row[key]
            head = slice((2 * j + hh) * DV_C, (2 * j + hh + 1) * DV_C)
            cx = cx_s[ch]
            big_m = jnp.maximum(m_prev, cm_row)
            s = st[key] * jnp.exp(jnp.where(causal[d], u_bc[key] - big_m, -jnp.inf))
            inter = jnp.exp(m_prev - big_m)
            vx = jnp.concatenate([vt[key], ones_blk], axis=0)
            lhs = jnp.concatenate([vx, cx.astype(BF16)], axis=1)
            rhs = jnp.concatenate([s.astype(BF16), (inter * qt[key].astype(F32)).astype(BF16)], axis=0)
            res = _dot(lhs, rhs)
            den = res[DV_C:DV_C + 1, :]
            hval = res[0:DV_C, :] * (1.0 / jnp.maximum(jnp.abs(den), jnp.exp(-(b_row + big_m))))
            h_s[head, sl[d]] = h_s[head, sl[d]] + hval

            m_top = jnp.maximum(m_prev, cm_last)
            kw = kpair[(d, j)] * jnp.exp(u_bc[key] - m_top)
            cx_s[ch] = jnp.exp(m_prev - m_top) * cx + _dot(vx, kw.astype(BF16))
            m_s[ch] = jnp.broadcast_to(g_tot + m_top, (1, LANES))
        return carry

    lax.fori_loop(0, nc, chunk_step, 0)

    gn_col = jnp.broadcast_to(gn_ref[...], (DV_C, LANES))
    gn_col = jnp.concatenate([gn_col] * (seq // LANES), axis=1)
    for hd in range(H_C):
        sl = slice(hd * DV_C, (hd + 1) * DV_C)
        ht = h_s[sl, :]
        ms = jnp.mean(ht * ht, axis=0, keepdims=True)
        y = jnp.transpose((ht * lax.rsqrt(ms + RMS_EPS)) * gn_col) * jax.nn.sigmoid(o_ref[:, sl])
        hs_ref[:, sl] = y.astype(BF16)
    if emit_state:
        for j, d, hh in chains:
            ch = _chain(j, d, hh)
            cf_ref[0, 0, d, 2 * j + hh] = jnp.transpose(cx_s[ch, 0:DV_C, :])[hh * DK_C:(hh + 1) * DK_C, :]
            mst_ref[0, d * H_C + 2 * j + hh:d * H_C + 2 * j + hh + 1, :] = m_s[ch]
        for j in range(npair):
            for d in range(2):
                nst_ref[0, d * npair + j:d * npair + j + 1, :] = jnp.where(
                    head_mask[0], cx_s[_chain(j, d, 0), DV_C:DV_C + 1, :], cx_s[_chain(j, d, 1), DV_C:DV_C + 1, :])


def _mlstm(q, k, v, o, gc, gr, gn, state, seq, emit_state):
    n = k.shape[0]
    nb = n // seq
    npair = H_C // 2
    has_state = state is not None
    hk, hv = H_C * DK_C, H_C * DV_C
    in_specs = [
        pl.BlockSpec((hk, seq), lambda b: (0, b)),
        pl.BlockSpec((seq, hk), lambda b: (b, 0)),
        pl.BlockSpec((hv, seq), lambda b: (0, b)),
        pl.BlockSpec((seq, hv), lambda b: (b, 0)),
        pl.BlockSpec((seq, LANES), lambda b: (b, 0)),
        pl.BlockSpec((3 * N_GATE, seq), lambda b: (0, b)),
        pl.BlockSpec((DV_C, 1), lambda b: (0, 0)),
    ]
    args = [q, k, v, o, gc, gr, gn]
    if has_state:
        c0, n0, m0 = state
        in_specs += [pl.BlockSpec((1, 2, H_C, DK_C, DV_C), lambda b: (b, 0, 0, 0, 0)),
                     pl.BlockSpec((1, 2, npair, 1, LANES), lambda b: (b, 0, 0, 0, 0)),
                     pl.BlockSpec((1, 2, H_C, 1, LANES), lambda b: (b, 0, 0, 0, 0))]
        args += [c0, n0, m0]
    out_specs = [pl.BlockSpec((seq, hv), lambda b: (b, 0))]
    out_shape = [jax.ShapeDtypeStruct((n, hv), BF16)]
    if emit_state:
        out_specs += [pl.BlockSpec((1, 1, 2, H_C, DK_C, DV_C), lambda b: (b, 0, 0, 0, 0, 0)),
                      pl.BlockSpec((1, 2 * npair, LANES), lambda b: (b, 0, 0)),
                      pl.BlockSpec((1, 2 * H_C, LANES), lambda b: (b, 0, 0))]
        out_shape += [jax.ShapeDtypeStruct((nb, 1, 2, H_C, DK_C, DV_C), F32),
                      jax.ShapeDtypeStruct((nb, 2 * npair, LANES), F32),
                      jax.ShapeDtypeStruct((nb, 2 * H_C, LANES), F32)]
    n_chain = 2 * H_C
    return pl.pallas_call(
        functools.partial(_mlstm_kernel, has_state=has_state, emit_state=emit_state),
        grid=(nb,),
        in_specs=in_specs, out_specs=out_specs, out_shape=out_shape,
        scratch_shapes=[pltpu.VMEM((n_chain, DV_C + N_ROWS_N, LANES), F32), pltpu.VMEM((n_chain, 1, LANES), F32),
                        pltpu.VMEM((hv, seq), F32)],
        compiler_params=_params(("arbitrary",)),
        name="mlstm_lat" if has_state else "mlstm_ctx",
    )(*args)


def _rope_tables(n_tok):
    t = np.arange(n_tok)
    rows = (t // GRID_W).astype(np.float64)
    cols = (t % GRID_W).astype(np.float64)

    def axis_tabs(width, lane0):
        half = width // 2
        quarter = half // 2
        freqs = np.power(ROPE_BASE, -np.arange(quarter, dtype=np.float64) / quarter)
        c = np.ones((n_tok, LANES))
        sa = np.zeros((n_tok, LANES))
        sb = np.zeros((n_tok, LANES))
        for g, pos in enumerate((rows, cols)):
            ang = pos[:, None] * freqs[None, :]
            a0 = lane0 + g * half
            c[:, a0:a0 + quarter] = np.cos(ang)
            c[:, a0 + quarter:a0 + half] = np.cos(ang)
            sa[:, a0:a0 + quarter] = -np.sin(ang)
            sb[:, a0 + quarter:a0 + half] = np.sin(ang)
        return c, sa, sb

    cm, sam, sbm = axis_tabs(QK_ROPE, ROPE_LANE0)
    c0, sa0, sb0 = axis_tabs(DIFF_HD, 0)
    c1, sa1, sb1 = axis_tabs(DIFF_HD, DIFF_HD)
    cd = np.where(np.arange(LANES)[None, :] < DIFF_HD, c0, c1)
    return tuple(jnp.asarray(a, F32) for a in (cm, sam, sbm, cd, sa0 + sa1, sb0 + sb1))


def _prep_even(w_in_ab, w_uq, w_ukv):
    c2 = Q_LORA + KV_LORA
    w_ab = w_in_ab.astype(BF16)
    wd = w_ab[:, c2 + QK_ROPE:]
    wq = jnp.pad(w_uq.reshape(Q_LORA, H_A, QK_NOPE + QK_ROPE),
                 ((0, 0), (0, 0), (0, HEAD_PAD - QK_NOPE - QK_ROPE))).reshape(Q_LORA, H_A * HEAD_PAD)
    kvw = w_ukv.reshape(KV_LORA, H_A, QK_NOPE + V_HD_A)
    kpad = jnp.pad(kvw[..., :QK_NOPE], ((0, 0), (0, 0), (0, HEAD_PAD - QK_NOPE)))
    vw = kvw[..., QK_NOPE:]
    zv = jnp.zeros_like(vw)
    odd = (jnp.arange(H_A) % 2 == 1)[None, :, None]
    vpad = jnp.where(odd, jnp.concatenate([zv, vw], -1), jnp.concatenate([vw, zv], -1))
    wukv = jnp.concatenate([kpad.reshape(KV_LORA, -1), vpad.reshape(KV_LORA, -1)], axis=1)
    return (w_ab, wd), wq.astype(BF16), wukv.astype(BF16)


def _gate_order(g):
    lead = g.shape[:-1]
    g = g.reshape(lead + (2, 2, H_C // 2, 2))
    perm = tuple(range(len(lead))) + tuple(len(lead) + a for a in (1, 2, 0, 3))
    return g.transpose(perm).reshape(lead + (4 * H_C,))


def _prep_odd(w_in_c, b_gate_c):
    ng = 4 * H_C
    base = w_in_c.shape[1] - ng
    w1 = w_in_c.astype(BF16)
    wg = jnp.pad(_gate_order(w1[:, base:]), ((0, 0), (0, LANES - ng)))
    bg = jnp.pad(_gate_order(b_gate_c), (0, LANES - ng)).reshape(1, LANES)
    return w1, wg, bg


def kernel(x_prompt, x_sample, cache_mla_ckv, cache_mla_krope, cache_diff_k, cache_diff_v,
           state_mlstm_C, state_mlstm_n, state_mlstm_m, c, c_ctx,
           w_ada, b_ada, g_mix, g_ffn, w_ffn_in, w_ffn_out,
           w_in_ab, g_q_lora, g_kv_lora, w_uq, w_ukv, diff_lambda, g_diff_subln, w_out_ab,
           w_in_c, b_gate_c, g_mlstm, w_out_c, g_final):
    nbp, seq_p, _ = x_prompt.shape
    nbs, seq_s, _ = x_sample.shape
    past = cache_mla_ckv.shape[2]
    assert DEPTH == 2 and 1 + nbs <= COND_ROWS
    assert cache_mla_ckv.shape[1] == 1 and state_mlstm_C.shape[1] == 1

    cond = jnp.concatenate([c_ctx[None], c, jnp.zeros((COND_ROWS - 1 - nbs, D_MODEL), F32)], axis=0)
    mod = _ada(cond, w_ada, b_ada).reshape(DEPTH * COND_ROWS, N_MOD, D_MODEL)

    xp = x_prompt.reshape(nbp * seq_p, D_MODEL)
    xs = x_sample.reshape(nbs * seq_s, D_MODEL)
    row2 = lambda v: v.reshape(1, -1)

    def mod_rows(layer, tm):
        prompt = lambda i: layer * COND_ROWS
        sample = lambda i: layer * COND_ROWS + 1 + i // (seq_s // tm)
        return prompt, sample

    lam_init = 0.8 - 0.6 * math.exp(-0.3 * 0)
    w0, wq, wukv = _prep_even(w_in_ab[0], w_uq[0], w_ukv[0])
    rope_tabs = _rope_tables(seq_s)
    mrp, _ = mod_rows(0, TM_PROJ)
    _, mrs = mod_rows(0, TM_PROJ_ROPE)
    gq, gkv = row2(g_q_lora[0]), row2(g_kv_lora[0])
    outs_p = _proj0(xp, mod, mrp, row2(g_mix[0]), *w0, gq, gkv, wq, wukv, None, seq_p)
    outs_s = _proj0(xs, mod, mrs, row2(g_mix[0]), *w0, gq, gkv, wq, wukv, rope_tabs, seq_s)
    qm_p, km_p, vm_p, dq_p, dk_p, dv_p, ckv_new, kr_new, dk_new, dv_new = outs_p
    qm_s, km_s, vm_s, dq_s, dk_s, dv_s = outs_s
    gsub = row2(g_diff_subln[0])
    krb_c = jnp.pad(cache_mla_krope.reshape(nbs, past, QK_ROPE),
                    ((0, 0), (0, 0), (ROPE_LANE0, LANES - ROPE_LANE0 - QK_ROPE)))
    ctx = (cache_mla_ckv.reshape(nbs, past, KV_LORA), krb_c,
           cache_diff_k.reshape(nbs, H_B, past, 2 * DIFF_HD), cache_diff_v.reshape(nbs, H_B, past, DIFF_VD))
    a_p = _attn0(qm_p, km_p, vm_p, dq_p, dk_p, dv_p, diff_lambda[0], gsub, None, None, seq_p, lam_init)
    a_s = _attn0(qm_s, km_s, vm_s, dq_s, dk_s, dv_s, diff_lambda[0], gsub, ctx, wukv, seq_s, lam_init)
    mtp, mts = mod_rows(0, TM_TAIL)
    wo0 = w_out_ab[0].astype(BF16)
    win, wout = w_ffn_in.astype(BF16), w_ffn_out.astype(BF16)
    xp = _tail(xp, a_p, mod, mtp, wo0, row2(g_ffn[0]), win, wout, 0, None, "tail0_ctx")
    xs = _tail(xs, a_s, mod, mts, wo0, row2(g_ffn[0]), win, wout, 0, None, "tail0_lat")

    w1, wg, bg = _prep_odd(w_in_c[0], b_gate_c[0])
    mrp, mrs = mod_rows(1, TM_PROJ1)
    gn = g_mlstm[0].reshape(DV_C, 1)
    q_p, k_p, v_p, o_p, gc_p, gr_p = _proj1(xp, mod, mrp, row2(g_mix[1]), w1, wg, bg)
    q_s, k_s, v_s, o_s, gc_s, gr_s = _proj1(xs, mod, mrs, row2(g_mix[1]), w1, wg, bg)
    hs_p, c_new, nst, mst = _mlstm(q_p, k_p, v_p, o_p, gc_p, gr_p, gn, None, seq_p, True)
    state = (state_mlstm_C.reshape(nbs, 2, H_C, DK_C, DV_C),
             state_mlstm_n.reshape(nbs, 2, H_C // 2, 1, LANES),
             jnp.broadcast_to(state_mlstm_m.reshape(nbs, 2, H_C, 1, 1), (nbs, 2, H_C, 1, LANES)))
    (hs_s,) = _mlstm(q_s, k_s, v_s, o_s, gc_s, gr_s, gn, state, seq_s, False)
    mtp, mts = mod_rows(1, TM_TAIL)
    wo1 = w_out_c[0].astype(BF16)
    gfin = row2(g_final)
    yp = _tail(xp, hs_p, mod, mtp, wo1, row2(g_ffn[1]), win, wout, 1, gfin, "tail1_ctx")
    ys = _tail(xs, hs_s, mod, mts, wo1, row2(g_ffn[1]), win, wout, 1, gfin, "tail1_lat")

    return (yp.reshape(nbp, seq_p, D_MODEL), ys.reshape(nbs, seq_s, D_MODEL),
            ckv_new.reshape(nbp, 1, seq_p, KV_LORA), kr_new.reshape(nbp, 1, seq_p, QK_ROPE),
            dk_new.reshape(nbp, 1, H_B, seq_p, 2 * DIFF_HD), dv_new.reshape(nbp, 1, H_B, seq_p, DIFF_VD),
            c_new, nst.reshape(nbp, 1, 2, H_C, DK_C), mst[:, :, 0].reshape(nbp, 1, 2, H_C))
```

```python
import functools
import math

import jax
import jax.numpy as jnp
import numpy as np
from jax import lax
from jax.experimental import pallas as pl
from jax.experimental.pallas import tpu as pltpu

F32 = jnp.float32
BF16 = jnp.bfloat16

D_MODEL = 1024
DEPTH = 2
GRID_W = 64
ROPE_BASE = 10000.0
RMS_EPS = 1e-6
H_A = 8
QK_NOPE = 64
QK_ROPE = 32
V_HD_A = 64
Q_LORA = 256
KV_LORA = 128
H_B = 4
DIFF_HD = 64
DIFF_VD = 2 * DIFF_HD
H_C = 8
DK_C = 64
DV_C = D_MODEL // H_C
D_FF = -(-8 * D_MODEL // (3 * 256)) * 256

LANES = 128
HEAD_PAD = LANES
ROPE_LANE0 = QK_NOPE
N_MOD = 6
COND_ROWS = 16
N_GATE = 2 * H_C
N_ROWS_N = 16

TM_PROJ = 1024
TM_PROJ_ROPE = 512
TM_PROJ1 = 1024
TQ_ATTN = 512
TM_TAIL = 1024
FF_CHUNK = 256
STAGE_ROWS_WIDE = 64
STAGE_ROWS_NARROW = 128
MLSTM_L = 128
ADA_TN = 1024
VMEM_LIMIT = 56 * 1024 * 1024

assert D_FF % LANES == 0 and FF_CHUNK % LANES == 0


def _dot(a, b):
    return jnp.dot(a, b, preferred_element_type=F32)


def _dot_nt(a, b):
    return lax.dot_general(a, b, (((1,), (1,)), ((), ())), preferred_element_type=F32)


def _dot_tn(a, b):
    return lax.dot_general(a, b, (((0,), (0,)), ((), ())), preferred_element_type=F32)


def _rms(x):
    return x * lax.rsqrt(jnp.mean(x * x, axis=-1, keepdims=True) + RMS_EPS)


def _modulate(x, g, shift, scale):
    return _rms(x) * (g * (1.0 + scale)) + shift


def _params(semantics):
    return pltpu.CompilerParams(dimension_semantics=semantics, vmem_limit_bytes=VMEM_LIMIT)


def _ada_kernel(cond_ref, w_ref, b_ref, o_ref):
    c = cond_ref[...]
    s = (c * jax.nn.sigmoid(c)).astype(BF16)
    o_ref[0] = _dot(s, w_ref[0].astype(BF16)) + b_ref[0]


def _ada(cond, w_ada, b_ada):
    n_out = w_ada.shape[-1]
    return pl.pallas_call(
        _ada_kernel,
        grid=(DEPTH, n_out // ADA_TN),
        in_specs=[
            pl.BlockSpec((COND_ROWS, D_MODEL), lambda l, n: (0, 0)),
            pl.BlockSpec((1, D_MODEL, ADA_TN), lambda l, n: (l, 0, n)),
            pl.BlockSpec((1, 1, ADA_TN), lambda l, n: (l, 0, n)),
        ],
        out_specs=pl.BlockSpec((1, COND_ROWS, ADA_TN), lambda l, n: (l, 0, n)),
        out_shape=jax.ShapeDtypeStruct((DEPTH, COND_ROWS, n_out), F32),
        compiler_params=_params(("arbitrary", "arbitrary")),
        name="ada",
    )(cond, w_ada, b_ada.reshape(DEPTH, 1, n_out))


def _rope(x, c, sa, sb, off):
    return x * c + pltpu.roll(x, LANES - off, 1) * sa + pltpu.roll(x, off, 1) * sb


def _proj0_kernel(*refs, rope, emit_cache):
    x_ref, mod_ref, g_ref, wa_ref, wd_ref, gq_ref, gkv_ref, wq_ref, wukv_ref = refs[:9]
    refs = refs[9:]
    if rope:
        cm_ref, sam_ref, sbm_ref, cd_ref, sad_ref, sbd_ref = refs[:6]
        refs = refs[6:]
    qm_ref, km_ref, vm_ref, dq_ref, dk_ref, dv_ref = refs[:6]
    refs = refs[6:]
    if emit_cache:
        ckvf_ref, krf_ref, dkf_ref, dvf_ref = refs

    h = _modulate(x_ref[...], g_ref[...], mod_ref[0, 0:1, :], mod_ref[0, 1:2, :]).astype(BF16)
    pa = _dot(h, wa_ref[...])
    pd = _dot(h, wd_ref[...])
    lane = lax.broadcasted_iota(jnp.int32, (pa.shape[0], LANES), 1)
    krb = jnp.where((lane >= ROPE_LANE0) & (lane < ROPE_LANE0 + QK_ROPE),
                    pltpu.roll(pa[:, Q_LORA + KV_LORA:Q_LORA + KV_LORA + LANES], ROPE_LANE0, 1), 0.0)
    cq = pa[:, 0:Q_LORA]
    ckv = _rms(pa[:, Q_LORA:Q_LORA + KV_LORA]) * gkv_ref[...]
    qa = _dot((_rms(cq) * gq_ref[...]).astype(BF16), wq_ref[...])
    kv = _dot(ckv.astype(BF16), wukv_ref[...])
    if emit_cache:
        ckvf_ref[...] = ckv
        krf_ref[...] = krb[:, ROPE_LANE0:ROPE_LANE0 + QK_ROPE]
    if rope:
        cm, sam, sbm = cm_ref[...], sam_ref[...], sbm_ref[...]
        cd, sad, sbd = cd_ref[...], sad_ref[...], sbd_ref[...]
        krb = _rope(krb, cm, sam, sbm, QK_ROPE // 4)
    for hd in range(H_A):
        sl = slice(hd * HEAD_PAD, (hd + 1) * HEAD_PAD)
        qh = qa[:, sl]
        if rope:
            qh = _rope(qh, cm, sam, sbm, QK_ROPE // 4)
        qm_ref[:, sl] = qh.astype(BF16)
        km_ref[:, sl] = (kv[:, sl] + krb).astype(BF16)
    vm_ref[...] = kv[:, H_A * HEAD_PAD:].astype(BF16)
    for hd in range(H_B):
        sl = slice(hd * LANES, (hd + 1) * LANES)
        dq = pd[:, hd * LANES:(hd + 1) * LANES]
        dk = pd[:, (H_B + hd) * LANES:(H_B + hd + 1) * LANES]
        dv = pd[:, (2 * H_B + hd) * LANES:(2 * H_B + hd + 1) * LANES]
        if emit_cache:
            seq = dkf_ref.shape[2]
            for bi in range(dkf_ref.shape[0]):
                dkf_ref[bi, hd] = dk[bi * seq:(bi + 1) * seq]
                dvf_ref[bi, hd] = dv[bi * seq:(bi + 1) * seq]
        if rope:
            dq = _rope(dq, cd, sad, sbd, DIFF_HD // 4)
            dk = _rope(dk, cd, sad, sbd, DIFF_HD // 4)
        dq_ref[:, sl] = (dq * (DIFF_HD ** -0.5)).astype(BF16)
        dk_ref[:, sl] = dk.astype(BF16)
        dv_ref[:, sl] = dv.astype(BF16)


def _proj0(x, mod, mod_row, g, w_ab, wd, gq, gkv, wq, wukv, rope_tabs, seq):
    n = x.shape[0]
    rope = rope_tabs is not None
    emit_cache = not rope
    tm = TM_PROJ_ROPE if rope else TM_PROJ
    tiles_per_seq = max(seq // tm, 1)
    full = lambda shape: pl.BlockSpec(shape, lambda i: (0,) * len(shape))
    in_specs = [
        pl.BlockSpec((tm, D_MODEL), lambda i: (i, 0)),
        pl.BlockSpec((1, N_MOD, D_MODEL), lambda i: (mod_row(i), 0, 0)),
        full((1, D_MODEL)), full((D_MODEL, Q_LORA + KV_LORA + LANES)), full(wd.shape),
        full((1, Q_LORA)), full((1, KV_LORA)), full(wq.shape), full(wukv.shape),
    ]
    args = [x, mod, g, w_ab, wd, gq, gkv, wq, wukv]
    if rope:
        in_specs += [pl.BlockSpec((tm, LANES), lambda i: (i % tiles_per_seq, 0))] * 6
        args += list(rope_tabs)
    row = lambda w: pl.BlockSpec((tm, w), lambda i: (i, 0))
    out_specs = [row(1024), row(1024), row(1024), row(512), row(512), row(512)]
    out_shape = [jax.ShapeDtypeStruct((n, w), BF16) for w in (1024, 1024, 1024, 512, 512, 512)]
    if emit_cache:
        assert tm % seq == 0
        cache = pl.BlockSpec((tm // seq, H_B, seq, LANES), lambda i: (i, 0, 0, 0))
        out_specs += [row(KV_LORA), row(QK_ROPE), cache, cache]
        out_shape += [jax.ShapeDtypeStruct((n, KV_LORA), F32), jax.ShapeDtypeStruct((n, QK_ROPE), F32),
                      jax.ShapeDtypeStruct((n // seq, H_B, seq, LANES), F32),
                      jax.ShapeDtypeStruct((n // seq, H_B, seq, LANES), F32)]
    return pl.pallas_call(
        functools.partial(_proj0_kernel, rope=rope, emit_cache=emit_cache),
        grid=(n // tm,),
        in_specs=in_specs, out_specs=out_specs, out_shape=out_shape,
        compiler_params=_params(("arbitrary",)),
        name="proj0_rope" if rope else "proj0_ctx",
    )(*args)


def _attn_kernel(*refs, has_ctx, lam_init):
    q_ref, dq_ref, k_ref, v_ref, dk_ref, dv_ref, lamv_ref, gsub_ref = refs[:8]
    refs = refs[8:]
    if has_ctx:
        ckvc_ref, krc_ref, dkc_ref, dvc_ref, wukv_ref = refs[:5]
        refs = refs[5:]
    o_ref, kx_s, vx_s, dkx_s, dvx_s = refs
    seq = k_ref.shape[0]
    ktot = kx_s.shape[0]

    @pl.when(pl.program_id(1) == 0)
    def _():
        kx_s[0:seq, :] = k_ref[...]
        dkx_s[0:seq, :] = dk_ref[...]
        ones_own = jnp.ones((seq, LANES), BF16)
        for hd in range(H_A):
            vx_s[0:seq, 2 * hd * LANES:(2 * hd + 1) * LANES] = v_ref[:, hd * LANES:(hd + 1) * LANES]
            vx_s[0:seq, (2 * hd + 1) * LANES:(2 * hd + 2) * LANES] = ones_own
        for hd in range(H_B):
            dvx_s[0:seq, 2 * hd * LANES:(2 * hd + 1) * LANES] = dv_ref[:, hd * LANES:(hd + 1) * LANES]
            dvx_s[0:seq, (2 * hd + 1) * LANES:(2 * hd + 2) * LANES] = ones_own
        if has_ctx:
            kv = _dot(ckvc_ref[0].astype(BF16), wukv_ref[...])
            krb = krc_ref[0]
            ones_ctx = jnp.ones((ktot - seq, LANES), BF16)
            for hd in range(H_A):
                sl = slice(hd * HEAD_PAD, (hd + 1) * HEAD_PAD)
                kx_s[seq:ktot, sl] = (kv[:, sl] + krb).astype(BF16)
                vx_s[seq:ktot, 2 * hd * LANES:(2 * hd + 1) * LANES] = (
                    kv[:, (H_A + hd) * HEAD_PAD:(H_A + hd + 1) * HEAD_PAD].astype(BF16))
                vx_s[seq:ktot, (2 * hd + 1) * LANES:(2 * hd + 2) * LANES] = ones_ctx
            for hd in range(H_B):
                sl = slice(hd * LANES, (hd + 1) * LANES)
                dkx_s[seq:ktot, sl] = dkc_ref[0, hd].astype(BF16)
                dvx_s[seq:ktot, 2 * hd * LANES:(2 * hd + 1) * LANES] = dvc_ref[0, hd].astype(BF16)
                dvx_s[seq:ktot, (2 * hd + 1) * LANES:(2 * hd + 2) * LANES] = ones_ctx

    log2e = 1.0 / math.log(2.0)
    tq = dq_ref.shape[0]
    lo = lax.broadcasted_iota(jnp.int32, (tq, LANES), 1) < DIFF_HD

    jobs = []
    for hd in range(H_A):
        sl = slice(hd * HEAD_PAD, (hd + 1) * HEAD_PAD)
        jobs.append((lambda sl=sl: q_ref[:, sl], (kx_s, sl), (vx_s, hd), (QK_NOPE + QK_ROPE) ** -0.5 * log2e))
    for hd in range(H_B):
        sl = slice(hd * LANES, (hd + 1) * LANES)
        for part in range(2):
            def qfn(sl=sl, part=part):
                dq = dq_ref[:, sl].astype(F32)
                return (jnp.where(lo, dq, 0.0) if part == 0 else jnp.where(lo, 0.0, dq)).astype(BF16)
            jobs.append((qfn, (dkx_s, sl), (dvx_s, hd), log2e))

    def scores(job):
        qfn, (kref, sl), _, _ = job
        return _dot_nt(qfn(), kref[:, sl])

    def finish(s, job):
        _, _, (vref, hd), c = job
        m = jnp.max(s, axis=-1, keepdims=True)
        e = jnp.exp2((s - m) * c).astype(BF16)
        res = _dot(e, vref[:, 2 * hd * LANES:(2 * hd + 2) * LANES])
        return res[:, 0:LANES] * (1.0 / res[:, LANES:2 * LANES])

    outs = []
    s_next = scores(jobs[0])
    for i, job in enumerate(jobs):
        s_cur = s_next
        if i + 1 < len(jobs):
            s_next = scores(jobs[i + 1])
        outs.append(finish(s_cur, job))

    for j in range(H_A // 2):
        o_ref[:, j * LANES:(j + 1) * LANES] = (outs[2 * j] + outs[2 * j + 1]).astype(BF16)
    lv = lamv_ref[...]
    lam = (jnp.exp(jnp.sum(lv[0:1] * lv[1:2], axis=-1, keepdims=True))
           - jnp.exp(jnp.sum(lv[2:3] * lv[3:4], axis=-1, keepdims=True)) + lam_init)
    for hd in range(H_B):
        acc = outs[H_A + 2 * hd] - lam * outs[H_A + 2 * hd + 1]
        ob = (_rms(acc) * gsub_ref[...]) * (1.0 - lam_init)
        o_ref[:, H_A * V_HD_A + hd * LANES:H_A * V_HD_A + (hd + 1) * LANES] = ob.astype(BF16)


def _attn0(qm, km, vm, dq, dk, dv, lamv, gsub, ctx, wukv, seq, lam_init):
    n = qm.shape[0]
    nb = n // seq
    tq = min(TQ_ATTN, seq)
    has_ctx = ctx is not None
    full = lambda shape: pl.BlockSpec(shape, lambda b, t: (0,) * len(shape))
    qrow = lambda w: pl.BlockSpec((tq, w), lambda b, t: (b * (seq // tq) + t, 0))
    krow = lambda w: pl.BlockSpec((seq, w), lambda b, t: (b, 0))
    in_specs = [qrow(1024), qrow(512), krow(1024), krow(1024), krow(512), krow(512),
                full((4, DIFF_HD)), full((1, DIFF_VD))]
    args = [qm, dq, km, vm, dk, dv, lamv, gsub]
    past = 0
    if has_ctx:
        ckv_c, krb_c, dk_c, dv_c = ctx
        past = ckv_c.shape[1]
        in_specs += [pl.BlockSpec((1, past, LANES), lambda b, t: (b, 0, 0)),
                     pl.BlockSpec((1, past, LANES), lambda b, t: (b, 0, 0)),
                     pl.BlockSpec((1, H_B, past, LANES), lambda b, t: (b, 0, 0, 0)),
                     pl.BlockSpec((1, H_B, past, LANES), lambda b, t: (b, 0, 0, 0)),
                     full(wukv.shape)]
        args += [ckv_c, krb_c, dk_c, dv_c, wukv]
    ktot = seq + past
    scratch = [pltpu.VMEM((ktot, H_A * HEAD_PAD), BF16), pltpu.VMEM((ktot, 2 * H_A * LANES), BF16),
               pltpu.VMEM((ktot, H_B * LANES), BF16), pltpu.VMEM((ktot, 2 * H_B * LANES), BF16)]
    return pl.pallas_call(
        functools.partial(_attn_kernel, has_ctx=has_ctx, lam_init=lam_init),
        grid=(nb, seq // tq),
        in_specs=in_specs,
        out_specs=pl.BlockSpec((tq, D_MODEL), lambda b, t: (b * (seq // tq) + t, 0)),
        out_shape=jax.ShapeDtypeStruct((n, D_MODEL), BF16),
        scratch_shapes=scratch,
        compiler_params=_params(("arbitrary", "arbitrary")),
        name="attn0_lat" if has_ctx else "attn0_ctx",
    )(*args)


def _load_cast(src, dst, stage, sem):
    rows_per_chunk = stage.shape[1]
    n_chunks = dst.shape[0] // rows_per_chunk
    assert n_chunks * rows_per_chunk == dst.shape[0]

    def chunk_copy(c):
        return pltpu.make_async_copy(src.at[pl.ds(c * rows_per_chunk, rows_per_chunk), :],
                                     stage.at[c % 2], sem.at[c % 2])

    chunk_copy(0).start()
    for c in range(n_chunks):
        if c + 1 < n_chunks:
            chunk_copy(c + 1).start()
        chunk_copy(c).wait()
        dst[c * rows_per_chunk:(c + 1) * rows_per_chunk, :] = stage[c % 2].astype(BF16)


def _tail_kernel(*refs, final, layer):
    x_ref, a_ref, mod_ref, wo_hbm, gf_ref, win_hbm, wout_hbm = refs[:7]
    refs = refs[7:]
    if final:
        gfin_ref = refs[0]
        refs = refs[1:]
    o_ref, wo_s, win_s, wout_s, stage_wide, stage_narrow, sem = refs

    @pl.when(pl.program_id(0) == 0)
    def _():
        _load_cast(wo_hbm.at[0], wo_s, stage_narrow, sem)
        _load_cast(win_hbm.at[layer], win_s, stage_wide, sem)
        _load_cast(wout_hbm.at[layer], wout_s, stage_narrow, sem)

    x1 = x_ref[...] + mod_ref[0, 2:3, :] * _dot(a_ref[...], wo_s[...])
    h = _modulate(x1, gf_ref[...], mod_ref[0, 3:4, :], mod_ref[0, 4:5, :]).astype(BF16)
    bounds = list(range(0, D_FF, FF_CHUNK)) + [D_FF]
    acc = None
    for lo, hi in zip(bounds[:-1], bounds[1:]):
        a = _dot(h, win_s[:, lo:hi])
        b = _dot(h, win_s[:, D_FF + lo:D_FF + hi])
        act = ((a * jax.nn.sigmoid(a)) * b).astype(BF16)
        part = _dot(act, wout_s[lo:hi, :])
        acc = part if acc is None else acc + part
    x2 = x1 + mod_ref[0, 5:6, :] * acc
    if final:
        x2 = _rms(x2) * gfin_ref[...]
    o_ref[...] = x2


def _tail(x, a, mod, mod_row, wo, gf, win, wout, layer, gfin, name):
    n = x.shape[0]
    tm = TM_TAIL
    final = gfin is not None
    full = lambda shape: pl.BlockSpec(shape, lambda i: (0,) * len(shape))
    hbm = pl.BlockSpec(memory_space=pl.ANY)
    in_specs = [
        pl.BlockSpec((tm, D_MODEL), lambda i: (i, 0)),
        pl.BlockSpec((tm, D_MODEL), lambda i: (i, 0)),
        pl.BlockSpec((1, N_MOD, D_MODEL), lambda i: (mod_row(i), 0, 0)),
        hbm, full((1, D_MODEL)), hbm, hbm,
    ]
    args = [x, a, mod, wo, gf, win, wout]
    if final:
        in_specs.append(full((1, D_MODEL)))
        args.append(gfin)
    return pl.pallas_call(
        functools.partial(_tail_kernel, final=final, layer=layer),
        grid=(n // tm,),
        in_specs=in_specs,
        out_specs=pl.BlockSpec((tm, D_MODEL), lambda i: (i, 0)),
        out_shape=jax.ShapeDtypeStruct((n, D_MODEL), F32),
        scratch_shapes=[pltpu.VMEM(wo.shape[1:], BF16), pltpu.VMEM(win.shape[1:], BF16),
                        pltpu.VMEM(wout.shape[1:], BF16),
                        pltpu.VMEM((2, STAGE_ROWS_WIDE, win.shape[2]), F32),
                        pltpu.VMEM((2, STAGE_ROWS_NARROW, D_MODEL), F32),
                        pltpu.SemaphoreType.DMA((2,))],
        compiler_params=_params(("arbitrary",)),
        name=name,
    )(*args)


def _split3(x):
    hi = x.astype(BF16)
    r1 = x - hi.astype(F32)
    mid = r1.astype(BF16)
    lo = (r1 - mid.astype(F32)).astype(BF16)
    return hi, mid, lo


def _proj1_kernel(x_ref, mod_ref, g_ref, w1_ref, wg_ref, bg_ref, q_ref, k_ref, v_ref, o_ref, gc_ref, gr_ref):
    tm = x_ref.shape[0]
    L = MLSTM_L
    hk = H_C * DK_C
    hv = H_C * DV_C
    h = _modulate(x_ref[...], g_ref[...], mod_ref[0, 0:1, :], mod_ref[0, 1:2, :]).astype(BF16)
    gates = _dot(h, wg_ref[...]) + bg_ref[...]
    proj = _dot(h, w1_ref[...])
    for blk in range(hk // LANES):
        sl = slice(blk * LANES, (blk + 1) * LANES)
        q_ref[sl, :] = jnp.transpose(proj[:, sl]).astype(BF16)
    for blk in range(hv // LANES):
        sl = slice(blk * LANES, (blk + 1) * LANES)
        v_ref[sl, :] = jnp.transpose(proj[:, 2 * hk + blk * LANES:2 * hk + (blk + 1) * LANES]).astype(BF16)
    k_ref[...] = proj[:, hk:2 * hk] * (DK_C ** -0.5)
    o_ref[...] = proj[:, 2 * hk + hv:2 * hk + 2 * hv]
    lf = jnp.minimum(gates, 0.0) - jnp.log1p(jnp.exp(-jnp.abs(gates)))
    r = lax.broadcasted_iota(jnp.int32, (L, L), 0)
    c = lax.broadcasted_iota(jnp.int32, (L, L), 1)
    pre = jnp.where(c <= r, 1.0, 0.0).astype(BF16)
    suf = jnp.where(c >= r, 1.0, 0.0).astype(BF16)
    parts = jnp.concatenate(_split3(lf), axis=1)
    fold = lambda t: t[:, 0:LANES] + t[:, LANES:2 * LANES] + t[:, 2 * LANES:3 * LANES]
    chunks = [parts[ck * L:(ck + 1) * L] for ck in range(tm // L)]
    lane = lax.broadcasted_iota(jnp.int32, (tm, LANES), 1)
    b_sum = jnp.where((lane % 4) >= 2,
                      jnp.concatenate([fold(_dot(suf, p)) for p in chunks], axis=0),
                      jnp.concatenate([fold(_dot(pre, p)) for p in chunks], axis=0))
    b = pltpu.roll(b_sum, LANES - N_GATE, 1)
    u = gates - b
    pos = lax.broadcasted_iota(jnp.int32, (tm, LANES), 0) % L
    bwd = (lane % 4) >= 2
    cm = u
    step = 1
    while step < L:
        below = jnp.where(pos >= step, pltpu.roll(cm, step, 0), -jnp.inf)
        above = jnp.where(pos < L - step, pltpu.roll(cm, tm - step, 0), -jnp.inf)
        cm = jnp.maximum(cm, jnp.where(bwd, above, below))
        step *= 2
    low = lane < N_GATE
    packed = (jnp.where(low, b, 0.0) + pltpu.roll(jnp.where(low, cm, 0.0), N_GATE, 1)
              + pltpu.roll(jnp.where(low, u, 0.0), 2 * N_GATE, 1))
    gc_ref[...] = packed
    gr_ref[...] = jnp.transpose(packed)[0:3 * N_GATE]


def _proj1(x, mod, mod_row, g, w1, wg, bg):
    n = x.shape[0]
    tm = TM_PROJ1
    assert tm % MLSTM_L == 0
    full = lambda shape: pl.BlockSpec(shape, lambda i: (0,) * len(shape))
    row = lambda w: pl.BlockSpec((tm, w), lambda i: (i, 0))
    col = lambda w: pl.BlockSpec((w, tm), lambda i: (0, i))
    hk, hv = H_C * DK_C, H_C * DV_C
    return pl.pallas_call(
        _proj1_kernel,
        grid=(n // tm,),
        in_specs=[row(D_MODEL), pl.BlockSpec((1, N_MOD, D_MODEL), lambda i: (mod_row(i), 0, 0)),
                  full((1, D_MODEL)), full((D_MODEL, 2 * hk + 2 * hv)), full(wg.shape), full((1, LANES))],
        out_specs=[col(hk), row(hk), col(hv), row(hv), row(LANES),
                   pl.BlockSpec((3 * N_GATE, tm), lambda i: (0, i))],
        out_shape=[jax.ShapeDtypeStruct((hk, n), BF16), jax.ShapeDtypeStruct((n, hk), F32),
                   jax.ShapeDtypeStruct((hv, n), BF16), jax.ShapeDtypeStruct((n, hv), F32),
                   jax.ShapeDtypeStruct((n, LANES), F32), jax.ShapeDtypeStruct((3 * N_GATE, n), F32)],
        compiler_params=_params(("arbitrary",)),
        name="proj1",
    )(x, mod, g, w1, wg, bg)


def _chain(j, d, hh):
    return (2 * j + d) * 2 + hh


def _mlstm_kernel(*refs, has_state, emit_state):
    q_ref, k_ref, v_ref, o_ref, gc_ref, gr_ref, gn_ref = refs[:7]
    refs = refs[7:]
    if has_state:
        c0_ref, n0_ref, m0_ref = refs[:3]
        refs = refs[3:]
    hs_ref = refs[0]
    refs = refs[1:]
    if emit_state:
        cf_ref, nst_ref, mst_ref = refs[:3]
        refs = refs[3:]
    cx_s, m_s, h_s = refs

    L = MLSTM_L
    seq = k_ref.shape[0]
    nc = seq // L
    npair = H_C // 2
    lane = lax.broadcasted_iota(jnp.int32, (1, LANES), 1)
    head_mask = [lane < DK_C, lane >= DK_C]
    ri = lax.broadcasted_iota(jnp.int32, (L, L), 0)
    ci = lax.broadcasted_iota(jnp.int32, (L, L), 1)
    causal = [ri <= ci, ri >= ci]
    ones_blk = jnp.ones((N_ROWS_N, L), BF16)
    chains = [(j, d, hh) for d in range(2) for j in range(npair) for hh in range(2)]

    h_s[...] = jnp.zeros_like(h_s)
    for j, d, hh in chains:
        ch = _chain(j, d, hh)
        if has_state:
            zpad = jnp.zeros((DK_C, DV_C), F32)
            c0 = c0_ref[0, d, 2 * j + hh]
            c0 = jnp.concatenate([c0, zpad] if hh == 0 else [zpad, c0], axis=0)
            cx_s[ch, 0:DV_C, :] = jnp.transpose(c0)
            n_row = jnp.where(head_mask[hh], n0_ref[0, d, j], 0.0)
            cx_s[ch, DV_C:DV_C + N_ROWS_N, :] = jnp.broadcast_to(n_row, (N_ROWS_N, LANES))
            m_s[ch] = m0_ref[0, d, 2 * j + hh]
        else:
            cx_s[ch] = jnp.zeros((DV_C + N_ROWS_N, LANES), F32)
            m_s[ch] = jnp.zeros((1, LANES), F32)

    def chunk_step(i, carry):
        sl = [pl.ds(pl.multiple_of(i * L, L), L), pl.ds(pl.multiple_of((nc - 1 - i) * L, L), L)]
        gcol = [gc_ref[sl[d], :] for d in range(2)]
        grow = [gr_ref[:, sl[d]] for d in range(2)]
        kpair = {(d, j): k_ref[sl[d], j * LANES:(j + 1) * LANES] for d in range(2) for j in range(npair)}
        kbf = {key: kk.astype(BF16) for key, kk in kpair.items()}
        zq = jnp.zeros((DK_C, L), BF16)
        qt = {}
        for d in range(2):
            for j in range(npair):
                qt[(j, d, 0)] = jnp.concatenate([q_ref[j * LANES:j * LANES + DK_C, sl[d]], zq], axis=0)
                qt[(j, d, 1)] = jnp.concatenate([zq, q_ref[j * LANES + DK_C:(j + 1) * LANES, sl[d]]], axis=0)
        st, u_bc, vt, row = {}, {}, {}, {}
        for d in range(2):
            for j in range(npair):
                pair_scores = _dot(kbf[(d, j)], jnp.concatenate([qt[(j, d, 0)], qt[(j, d, 1)]], axis=1))
                st[(j, d, 0)], st[(j, d, 1)] = pair_scores[:, 0:L], pair_scores[:, L:2 * L]
        for j, d, hh in chains:
            key = (j, d, hh)
            idx = 4 * j + 2 * d + hh
            edge = L - 1 if d == 0 else 0
            b_row = grow[d][idx:idx + 1, :]
            cm_row = grow[d][N_GATE + idx:N_GATE + idx + 1, :]
            u_col = gcol[d][:, 2 * N_GATE + idx:2 * N_GATE + idx + 1]
            m_prev = m_s[_chain(j, d, hh)][:, 0:1]
            row[key] = (b_row, cm_row, b_row[:, edge:edge + 1], cm_row[:, edge:edge + 1], m_prev)
            u_bc[key] = jnp.broadcast_to(u_col, (L, L))
            vt[key] = v_ref[(2 * j + hh) * DV_C:(2 * j + hh + 1) * DV_C, sl[d]]
        for j, d, hh in chains:
            key = (j, d, hh)
            ch = _chain(j, d, hh)
            b_row, cm_row, g_tot, cm_last, m_prev = row[key]
            head = slice((2 * j + hh) * DV_C, (2 * j + hh + 1) * DV_C)
            cx = cx_s[ch]
            big_m = jnp.maximum(m_prev, cm_row)
            s = st[key] * jnp.exp(jnp.where(causal[d], u_bc[key] - big_m, -jnp.inf))
            inter = jnp.exp(m_prev - big_m)
            vx = jnp.concatenate([vt[key], ones_blk], axis=0)
            lhs = jnp.concatenate([vx, cx.astype(BF16)], axis=1)
            rhs = jnp.concatenate([s.astype(BF16), (inter * qt[key].astype(F32)).astype(BF16)], axis=0)
            res = _dot(lhs, rhs)
            den = res[DV_C:DV_C + 1, :]
            hval = res[0:DV_C, :] * (1.0 / jnp.maximum(jnp.abs(den), jnp.exp(-(b_row + big_m))))
            h_s[head, sl[d]] = h_s[head, sl[d]] + hval

            m_top = jnp.maximum(m_prev, cm_last)
            kw = kpair[(d, j)] * jnp.exp(u_bc[key] - m_top)
            cx_s[ch] = jnp.exp(m_prev - m_top) * cx + _dot(vx, kw.astype(BF16))
            m_s[ch] = jnp.broadcast_to(g_tot + m_top, (1, LANES))
        return carry

    lax.fori_loop(0, nc, chunk_step, 0)

    gn_col = jnp.broadcast_to(gn_ref[...], (DV_C, LANES))
    gn_col = jnp.concatenate([gn_col] * (seq // LANES), axis=1)
    for hd in range(H_C):
        sl = slice(hd * DV_C, (hd + 1) * DV_C)
        ht = h_s[sl, :]
        ms = jnp.mean(ht * ht, axis=0, keepdims=True)
        y = jnp.transpose((ht * lax.rsqrt(ms + RMS_EPS)) * gn_col) * jax.nn.sigmoid(o_ref[:, sl])
        hs_ref[:, sl] = y.astype(BF16)
    if emit_state:
        for j, d, hh in chains:
            ch = _chain(j, d, hh)
            cf_ref[0, 0, d, 2 * j + hh] = jnp.transpose(cx_s[ch, 0:DV_C, :])[hh * DK_C:(hh + 1) * DK_C, :]
            mst_ref[0, d * H_C + 2 * j + hh:d * H_C + 2 * j + hh + 1, :] = m_s[ch]
        for j in range(npair):
            for d in range(2):
                nst_ref[0, d * npair + j:d * npair + j + 1, :] = jnp.where(
                    head_mask[0], cx_s[_chain(j, d, 0), DV_C:DV_C + 1, :], cx_s[_chain(j, d, 1), DV_C:DV_C + 1, :])


def _mlstm(q, k, v, o, gc, gr, gn, state, seq, emit_state):
    n = k.shape[0]
    nb = n // seq
    npair = H_C // 2
    has_state = state is not None
    hk, hv = H_C * DK_C, H_C * DV_C
    in_specs = [
        pl.BlockSpec((hk, seq), lambda b: (0, b)),
        pl.BlockSpec((seq, hk), lambda b: (b, 0)),
        pl.BlockSpec((hv, seq), lambda b: (0, b)),
        pl.BlockSpec((seq, hv), lambda b: (b, 0)),
        pl.BlockSpec((seq, LANES), lambda b: (b, 0)),
        pl.BlockSpec((3 * N_GATE, seq), lambda b: (0, b)),
        pl.BlockSpec((DV_C, 1), lambda b: (0, 0)),
    ]
    args = [q, k, v, o, gc, gr, gn]
    if has_state:
        c0, n0, m0 = state
        in_specs += [pl.BlockSpec((1, 2, H_C, DK_C, DV_C), lambda b: (b, 0, 0, 0, 0)),
                     pl.BlockSpec((1, 2, npair, 1, LANES), lambda b: (b, 0, 0, 0, 0)),
                     pl.BlockSpec((1, 2, H_C, 1, LANES), lambda b: (b, 0, 0, 0, 0))]
        args += [c0, n0, m0]
    out_specs = [pl.BlockSpec((seq, hv), lambda b: (b, 0))]
    out_shape = [jax.ShapeDtypeStruct((n, hv), BF16)]
    if emit_state:
        out_specs += [pl.BlockSpec((1, 1, 2, H_C, DK_C, DV_C), lambda b: (b, 0, 0, 0, 0, 0)),
                      pl.BlockSpec((1, 2 * npair, LANES), lambda b: (b, 0, 0)),
                      pl.BlockSpec((1, 2 * H_C, LANES), lambda b: (b, 0, 0))]
        out_shape += [jax.ShapeDtypeStruct((nb, 1, 2, H_C, DK_C, DV_C), F32),
                      jax.ShapeDtypeStruct((nb, 2 * npair, LANES), F32),
                      jax.ShapeDtypeStruct((nb, 2 * H_C, LANES), F32)]
    n_chain = 2 * H_C
    return pl.pallas_call(
        functools.partial(_mlstm_kernel, has_state=has_state, emit_state=emit_state),
        grid=(nb,),
        in_specs=in_specs, out_specs=out_specs, out_shape=out_shape,
        scratch_shapes=[pltpu.VMEM((n_chain, DV_C + N_ROWS_N, LANES), F32), pltpu.VMEM((n_chain, 1, LANES), F32),
                        pltpu.VMEM((hv, seq), F32)],
        compiler_params=_params(("arbitrary",)),
        name="mlstm_lat" if has_state else "mlstm_ctx",
    )(*args)


def _rope_tables(n_tok):
    t = np.arange(n_tok)
    rows = (t // GRID_W).astype(np.float64)
    cols = (t % GRID_W).astype(np.float64)

    def axis_tabs(width, lane0):
        half = width // 2
        quarter = half // 2
        freqs = np.power(ROPE_BASE, -np.arange(quarter, dtype=np.float64) / quarter)
        c = np.ones((n_tok, LANES))
        sa = np.zeros((n_tok, LANES))
        sb = np.zeros((n_tok, LANES))
        for g, pos in enumerate((rows, cols)):
            ang = pos[:, None] * freqs[None, :]
            a0 = lane0 + g * half
            c[:, a0:a0 + quarter] = np.cos(ang)
            c[:, a0 + quarter:a0 + half] = np.cos(ang)
            sa[:, a0:a0 + quarter] = -np.sin(ang)
            sb[:, a0 + quarter:a0 + half] = np.sin(ang)
        return c, sa, sb

    cm, sam, sbm = axis_tabs(QK_ROPE, ROPE_LANE0)
    c0, sa0, sb0 = axis_tabs(DIFF_HD, 0)
    c1, sa1, sb1 = axis_tabs(DIFF_HD, DIFF_HD)
    cd = np.where(np.arange(LANES)[None, :] < DIFF_HD, c0, c1)
    return tuple(jnp.asarray(a, F32) for a in (cm, sam, sbm, cd, sa0 + sa1, sb0 + sb1))


def _prep_even(w_in_ab, w_uq, w_ukv):
    c2 = Q_LORA + KV_LORA
    w_ab = w_in_ab.astype(BF16)
    wd = w_ab[:, c2 + QK_ROPE:]
    wq = jnp.pad(w_uq.reshape(Q_LORA, H_A, QK_NOPE + QK_ROPE),
                 ((0, 0), (0, 0), (0, HEAD_PAD - QK_NOPE - QK_ROPE))).reshape(Q_LORA, H_A * HEAD_PAD)
    kvw = w_ukv.reshape(KV_LORA, H_A, QK_NOPE + V_HD_A)
    kpad = jnp.pad(kvw[..., :QK_NOPE], ((0, 0), (0, 0), (0, HEAD_PAD - QK_NOPE)))
    vw = kvw[..., QK_NOPE:]
    zv = jnp.zeros_like(vw)
    odd = (jnp.arange(H_A) % 2 == 1)[None, :, None]
    vpad = jnp.where(odd, jnp.concatenate([zv, vw], -1), jnp.concatenate([vw, zv], -1))
    wukv = jnp.concatenate([kpad.reshape(KV_LORA, -1), vpad.reshape(KV_LORA, -1)], axis=1)
    return (w_ab, wd), wq.astype(BF16), wukv.astype(BF16)


def _gate_order(g):
    lead = g.shape[:-1]
    g = g.reshape(lead + (2, 2, H_C // 2, 2))
    perm = tuple(range(len(lead))) + tuple(len(lead) + a for a in (1, 2, 0, 3))
    return g.transpose(perm).reshape(lead + (4 * H_C,))


def _prep_odd(w_in_c, b_gate_c):
    ng = 4 * H_C
    base = w_in_c.shape[1] - ng
    w1 = w_in_c.astype(BF16)
    wg = jnp.pad(_gate_order(w1[:, base:]), ((0, 0), (0, LANES - ng)))
    bg = jnp.pad(_gate_order(b_gate_c), (0, LANES - ng)).reshape(1, LANES)
    return w1, wg, bg


def kernel(x_prompt, x_sample, cache_mla_ckv, cache_mla_krope, cache_diff_k, cache_diff_v,
           state_mlstm_C, state_mlstm_n, state_mlstm_m, c, c_ctx,
           w_ada, b_ada, g_mix, g_ffn, w_ffn_in, w_ffn_out,
           w_in_ab, g_q_lora, g_kv_lora, w_uq, w_ukv, diff_lambda, g_diff_subln, w_out_ab,
           w_in_c, b_gate_c, g_mlstm, w_out_c, g_final):
    nbp, seq_p, _ = x_prompt.shape
    nbs, seq_s, _ = x_sample.shape
    past = cache_mla_ckv.shape[2]
    assert DEPTH == 2 and 1 + nbs <= COND_ROWS
    assert cache_mla_ckv.shape[1] == 1 and state_mlstm_C.shape[1] == 1

    cond = jnp.concatenate([c_ctx[None], c, jnp.zeros((COND_ROWS - 1 - nbs, D_MODEL), F32)], axis=0)
    mod = _ada(cond, w_ada, b_ada).reshape(DEPTH * COND_ROWS, N_MOD, D_MODEL)

    xp = x_prompt.reshape(nbp * seq_p, D_MODEL)
    xs = x_sample.reshape(nbs * seq_s, D_MODEL)
    row2 = lambda v: v.reshape(1, -1)

    def mod_rows(layer, tm):
        prompt = lambda i: layer * COND_ROWS
        sample = lambda i: layer * COND_ROWS + 1 + i // (seq_s // tm)
        return prompt, sample

    lam_init = 0.8 - 0.6 * math.exp(-0.3 * 0)
    w0, wq, wukv = _prep_even(w_in_ab[0], w_uq[0], w_ukv[0])
    rope_tabs = _rope_tables(seq_s)
    mrp, _ = mod_rows(0, TM_PROJ)
    _, mrs = mod_rows(0, TM_PROJ_ROPE)
    gq, gkv = row2(g_q_lora[0]), row2(g_kv_lora[0])
    outs_p = _proj0(xp, mod, mrp, row2(g_mix[0]), *w0, gq, gkv, wq, wukv, None, seq_p)
    outs_s = _proj0(xs, mod, mrs, row2(g_mix[0]), *w0, gq, gkv, wq, wukv, rope_tabs, seq_s)
    qm_p, km_p, vm_p, dq_p, dk_p, dv_p, ckv_new, kr_new, dk_new, dv_new = outs_p
    qm_s, km_s, vm_s, dq_s, dk_s, dv_s = outs_s
    gsub = row2(g_diff_subln[0])
    krb_c = jnp.pad(cache_mla_krope.reshape(nbs, past, QK_ROPE),
                    ((0, 0), (0, 0), (ROPE_LANE0, LANES - ROPE_LANE0 - QK_ROPE)))
    ctx = (cache_mla_ckv.reshape(nbs, past, KV_LORA), krb_c,
           cache_diff_k.reshape(nbs, H_B, past, 2 * DIFF_HD), cache_diff_v.reshape(nbs, H_B, past, DIFF_VD))
    a_p = _attn0(qm_p, km_p, vm_p, dq_p, dk_p, dv_p, diff_lambda[0], gsub, None, None, seq_p, lam_init)
    a_s = _attn0(qm_s, km_s, vm_s, dq_s, dk_s, dv_s, diff_lambda[0], gsub, ctx, wukv, seq_s, lam_init)
    mtp, mts = mod_rows(0, TM_TAIL)
    xp = _tail(xp, a_p, mod, mtp, w_out_ab, row2(g_ffn[0]), w_ffn_in, w_ffn_out, 0, None, "tail0_ctx")
    xs = _tail(xs, a_s, mod, mts, w_out_ab, row2(g_ffn[0]), w_ffn_in, w_ffn_out, 0, None, "tail0_lat")

    w1, wg, bg = _prep_odd(w_in_c[0], b_gate_c[0])
    mrp, mrs = mod_rows(1, TM_PROJ1)
    gn = g_mlstm[0].reshape(DV_C, 1)
    q_p, k_p, v_p, o_p, gc_p, gr_p = _proj1(xp, mod, mrp, row2(g_mix[1]), w1, wg, bg)
    q_s, k_s, v_s, o_s, gc_s, gr_s = _proj1(xs, mod, mrs, row2(g_mix[1]), w1, wg, bg)
    hs_p, c_new, nst, mst = _mlstm(q_p, k_p, v_p, o_p, gc_p, gr_p, gn, None, seq_p, True)
    state = (state_mlstm_C.reshape(nbs, 2, H_C, DK_C, DV_C),
             state_mlstm_n.reshape(nbs, 2, H_C // 2, 1, LANES),
             jnp.broadcast_to(state_mlstm_m.reshape(nbs, 2, H_C, 1, 1), (nbs, 2, H_C, 1, LANES)))
    (hs_s,) = _mlstm(q_s, k_s, v_s, o_s, gc_s, gr_s, gn, state, seq_s, False)
    mtp, mts = mod_rows(1, TM_TAIL)
    gfin = row2(g_final)
    yp = _tail(xp, hs_p, mod, mtp, w_out_c, row2(g_ffn[1]), w_ffn_in, w_ffn_out, 1, gfin, "tail1_ctx")
    ys = _tail(xs, hs_s, mod, mts, w_out_c, row2(g_ffn[1]), w_ffn_in, w_ffn_out, 1, gfin, "tail1_lat")

    return (yp.reshape(nbp, seq_p, D_MODEL), ys.reshape(nbs, seq_s, D_MODEL),
            ckv_new.reshape(nbp, 1, seq_p, KV_LORA), kr_new.reshape(nbp, 1, seq_p, QK_ROPE),
            dk_new.reshape(nbp, 1, H_B, seq_p, 2 * DIFF_HD), dv_new.reshape(nbp, 1, H_B, seq_p, DIFF_VD),
            c_new, nst.reshape(nbp, 1, 2, H_C, DK_C), mst[:, :, 0].reshape(nbp, 1, 2, H_C))
```

```python
import functools
import math

import jax
import jax.numpy as jnp
import numpy as np
from jax import lax
from jax.experimental import pallas as pl
from jax.experimental.pallas import tpu as pltpu

F32 = jnp.float32
BF16 = jnp.bfloat16

D_MODEL = 1024
DEPTH = 2
GRID_W = 64
ROPE_BASE = 10000.0
RMS_EPS = 1e-6
H_A = 8
QK_NOPE = 64
QK_ROPE = 32
V_HD_A = 64
Q_LORA = 256
KV_LORA = 128
H_B = 4
DIFF_HD = 64
DIFF_VD = 2 * DIFF_HD
H_C = 8
DK_C = 64
DV_C = D_MODEL // H_C
D_FF = -(-8 * D_MODEL // (3 * 256)) * 256

LANES = 128
HEAD_PAD = LANES
ROPE_LANE0 = QK_NOPE
W_MLA = H_A * HEAD_PAD
W_V = H_A * V_HD_A
W_DIFF = H_B * LANES
N_MOD = 6
COND_ROWS = 16
N_GATE = 2 * H_C
N_ROWS_N = 16

TM_PROJ = 1024
TM_PROJ_ROPE = 512
TM_PROJ1 = 1024
TQ_ATTN = 512
TM_TAIL = 1024
FF_CHUNK = 256
MLSTM_L = 128
ADA_TN = 1024
VMEM_LIMIT = 56 * 1024 * 1024

assert D_FF % LANES == 0 and FF_CHUNK % LANES == 0


def _dot(a, b):
    return jnp.dot(a, b, preferred_element_type=F32)


def _dot_nt(a, b):
    return lax.dot_general(a, b, (((1,), (1,)), ((), ())), preferred_element_type=F32)


def _rms(x):
    return x * lax.rsqrt(jnp.mean(x * x, axis=-1, keepdims=True) + RMS_EPS)


def _modulate(x, g, shift, scale):
    return _rms(x) * (g * (1.0 + scale)) + shift


def _params(semantics):
    return pltpu.CompilerParams(dimension_semantics=semantics, vmem_limit_bytes=VMEM_LIMIT)


def _ada_kernel(cond_ref, w_ref, b_ref, o_ref):
    c = cond_ref[...]
    s = (c * jax.nn.sigmoid(c)).astype(BF16)
    o_ref[0] = _dot(s, w_ref[0].astype(BF16)) + b_ref[0]


def _ada(cond, w_ada, b_ada):
    n_out = w_ada.shape[-1]
    return pl.pallas_call(
        _ada_kernel,
        grid=(DEPTH, n_out // ADA_TN),
        in_specs=[
            pl.BlockSpec((COND_ROWS, D_MODEL), lambda l, n: (0, 0)),
            pl.BlockSpec((1, D_MODEL, ADA_TN), lambda l, n: (l, 0, n)),
            pl.BlockSpec((1, 1, ADA_TN), lambda l, n: (l, 0, n)),
        ],
        out_specs=pl.BlockSpec((1, COND_ROWS, ADA_TN), lambda l, n: (l, 0, n)),
        out_shape=jax.ShapeDtypeStruct((DEPTH, COND_ROWS, n_out), F32),
        compiler_params=_params(("arbitrary", "arbitrary")),
        name="ada",
    )(cond, w_ada, b_ada.reshape(DEPTH, 1, n_out))


def _rope(x, c, sa, sb, off):
    return x * c + pltpu.roll(x, LANES - off, 1) * sa + pltpu.roll(x, off, 1) * sb


def _proj0_kernel(*refs, rope, emit_cache):
    x_ref, mod_ref, g_ref, wa_ref, wd_ref, gq_ref, gkv_ref, wq_ref, wukv_ref = refs[:9]
    refs = refs[9:]
    if rope:
        cm_ref, sam_ref, sbm_ref, cd_ref, sad_ref, sbd_ref = refs[:6]
        refs = refs[6:]
    qm_ref, km_ref, vm_ref, dq_ref, dk_ref, dv_ref = refs[:6]
    refs = refs[6:]
    if emit_cache:
        ckvf_ref, krf_ref, dkf_ref, dvf_ref = refs

    h = _modulate(x_ref[...], g_ref[...], mod_ref[0, 0:1, :], mod_ref[0, 1:2, :]).astype(BF16)
    pa = _dot(h, wa_ref[...])
    pd = _dot(h, wd_ref[...])
    lane = lax.broadcasted_iota(jnp.int32, (pa.shape[0], LANES), 1)
    krb = jnp.where((lane >= ROPE_LANE0) & (lane < ROPE_LANE0 + QK_ROPE),
                    pltpu.roll(pa[:, Q_LORA + KV_LORA:Q_LORA + KV_LORA + LANES], ROPE_LANE0, 1), 0.0)
    cq = pa[:, 0:Q_LORA]
    ckv = _rms(pa[:, Q_LORA:Q_LORA + KV_LORA]) * gkv_ref[...]
    qa = _dot((_rms(cq) * gq_ref[...]).astype(BF16), wq_ref[...])
    kv = _dot(ckv.astype(BF16), wukv_ref[...])
    if emit_cache:
        ckvf_ref[...] = ckv
        krf_ref[...] = krb[:, ROPE_LANE0:ROPE_LANE0 + QK_ROPE]
    if rope:
        cm, sam, sbm = cm_ref[...], sam_ref[...], sbm_ref[...]
        cd, sad, sbd = cd_ref[...], sad_ref[...], sbd_ref[...]
        krb = _rope(krb, cm, sam, sbm, QK_ROPE // 4)
    for hd in range(H_A):
        sl = slice(hd * HEAD_PAD, (hd + 1) * HEAD_PAD)
        qh = qa[:, sl]
        if rope:
            qh = _rope(qh, cm, sam, sbm, QK_ROPE // 4)
        qm_ref[:, sl] = qh.astype(BF16)
        km_ref[:, sl] = (kv[:, sl] + krb).astype(BF16)
    vm_ref[...] = kv[:, H_A * HEAD_PAD:].astype(BF16)
    for hd in range(H_B):
        sl = slice(hd * LANES, (hd + 1) * LANES)
        dq = pd[:, hd * LANES:(hd + 1) * LANES]
        dk = pd[:, (H_B + hd) * LANES:(H_B + hd + 1) * LANES]
        dv = pd[:, (2 * H_B + hd) * LANES:(2 * H_B + hd + 1) * LANES]
        if emit_cache:
            seq = dkf_ref.shape[2]
            for bi in range(dkf_ref.shape[0]):
                dkf_ref[bi, hd] = dk[bi * seq:(bi + 1) * seq]
                dvf_ref[bi, hd] = dv[bi * seq:(bi + 1) * seq]
        if rope:
            dq = _rope(dq, cd, sad, sbd, DIFF_HD // 4)
            dk = _rope(dk, cd, sad, sbd, DIFF_HD // 4)
        dq_ref[:, sl] = (dq * (DIFF_HD ** -0.5)).astype(BF16)
        dk_ref[:, sl] = dk.astype(BF16)
        dv_ref[:, sl] = dv.astype(BF16)


def _proj0(x, mod, mod_row, g, w_ab, wd, gq, gkv, wq, wukv, rope_tabs, seq):
    n = x.shape[0]
    rope = rope_tabs is not None
    emit_cache = not rope
    tm = TM_PROJ_ROPE if rope else TM_PROJ
    tiles_per_seq = max(seq // tm, 1)
    full = lambda shape: pl.BlockSpec(shape, lambda i: (0,) * len(shape))
    in_specs = [
        pl.BlockSpec((tm, D_MODEL), lambda i: (i, 0)),
        pl.BlockSpec((1, N_MOD, D_MODEL), lambda i: (mod_row(i), 0, 0)),
        full((1, D_MODEL)), full((D_MODEL, Q_LORA + KV_LORA + LANES)), full(wd.shape),
        full((1, Q_LORA)), full((1, KV_LORA)), full(wq.shape), full(wukv.shape),
    ]
    args = [x, mod, g, w_ab, wd, gq, gkv, wq, wukv]
    if rope:
        in_specs += [pl.BlockSpec((tm, LANES), lambda i: (i % tiles_per_seq, 0))] * 6
        args += list(rope_tabs)
    row = lambda w: pl.BlockSpec((tm, w), lambda i: (i, 0))
    widths = (W_MLA, W_MLA, W_V, W_DIFF, W_DIFF, W_DIFF)
    out_specs = [row(w) for w in widths]
    out_shape = [jax.ShapeDtypeStruct((n, w), BF16) for w in widths]
    if emit_cache:
        assert tm % seq == 0
        cache = pl.BlockSpec((tm // seq, H_B, seq, LANES), lambda i: (i, 0, 0, 0))
        out_specs += [row(KV_LORA), row(QK_ROPE), cache, cache]
        out_shape += [jax.ShapeDtypeStruct((n, KV_LORA), F32), jax.ShapeDtypeStruct((n, QK_ROPE), F32),
                      jax.ShapeDtypeStruct((n // seq, H_B, seq, LANES), F32),
                      jax.ShapeDtypeStruct((n // seq, H_B, seq, LANES), F32)]
    return pl.pallas_call(
        functools.partial(_proj0_kernel, rope=rope, emit_cache=emit_cache),
        grid=(n // tm,),
        in_specs=in_specs, out_specs=out_specs, out_shape=out_shape,
        compiler_params=_params(("arbitrary",)),
        name="proj0_rope" if rope else "proj0_ctx",
    )(*args)


def _attn_kernel(*refs, has_ctx, lam_init):
    q_ref, dq_ref, k_ref, v_ref, dk_ref, dv_ref, lamv_ref, gsub_ref = refs[:8]
    refs = refs[8:]
    if has_ctx:
        ckvc_ref, krc_ref, dkc_ref, dvc_ref, wukv_ref = refs[:5]
        refs = refs[5:]
    o_ref, kx_s, vx_s, dkx_s, dvx_s = refs
    seq = k_ref.shape[0]
    ktot = kx_s.shape[0]

    @pl.when(pl.program_id(1) == 0)
    def _():
        kx_s[0:seq, :] = k_ref[...]
        dkx_s[0:seq, :] = dk_ref[...]
        ones_own = jnp.ones((seq, LANES), BF16)
        low_own = lax.broadcasted_iota(jnp.int32, (seq, LANES), 1) < V_HD_A
        for hd in range(H_A):
            pair = v_ref[:, (hd // 2) * LANES:(hd // 2 + 1) * LANES].astype(F32)
            vx_s[0:seq, 2 * hd * LANES:(2 * hd + 1) * LANES] = jnp.where(
                low_own if hd % 2 == 0 else jnp.logical_not(low_own), pair, 0.0).astype(BF16)
            vx_s[0:seq, (2 * hd + 1) * LANES:(2 * hd + 2) * LANES] = ones_own
        for hd in range(H_B):
            dvx_s[0:seq, 2 * hd * LANES:(2 * hd + 1) * LANES] = dv_ref[:, hd * LANES:(hd + 1) * LANES]
            dvx_s[0:seq, (2 * hd + 1) * LANES:(2 * hd + 2) * LANES] = ones_own
        if has_ctx:
            kv = _dot(ckvc_ref[0].astype(BF16), wukv_ref[...])
            krb = krc_ref[0]
            ones_ctx = jnp.ones((ktot - seq, LANES), BF16)
            low_ctx = lax.broadcasted_iota(jnp.int32, (ktot - seq, LANES), 1) < V_HD_A
            for hd in range(H_A):
                sl = slice(hd * HEAD_PAD, (hd + 1) * HEAD_PAD)
                kx_s[seq:ktot, sl] = (kv[:, sl] + krb).astype(BF16)
                pair = kv[:, W_MLA + (hd // 2) * LANES:W_MLA + (hd // 2 + 1) * LANES]
                vx_s[seq:ktot, 2 * hd * LANES:(2 * hd + 1) * LANES] = jnp.where(
                    low_ctx if hd % 2 == 0 else jnp.logical_not(low_ctx), pair, 0.0).astype(BF16)
                vx_s[seq:ktot, (2 * hd + 1) * LANES:(2 * hd + 2) * LANES] = ones_ctx
            for hd in range(H_B):
                sl = slice(hd * LANES, (hd + 1) * LANES)
                dkx_s[seq:ktot, sl] = dkc_ref[0, hd].astype(BF16)
                dvx_s[seq:ktot, 2 * hd * LANES:(2 * hd + 1) * LANES] = dvc_ref[0, hd].astype(BF16)
                dvx_s[seq:ktot, (2 * hd + 1) * LANES:(2 * hd + 2) * LANES] = ones_ctx

    log2e = 1.0 / math.log(2.0)
    tq = dq_ref.shape[0]
    lo = lax.broadcasted_iota(jnp.int32, (tq, LANES), 1) < DIFF_HD

    jobs = []
    for hd in range(H_A):
        sl = slice(hd * HEAD_PAD, (hd + 1) * HEAD_PAD)
        jobs.append((lambda sl=sl: q_ref[:, sl], (kx_s, sl), (vx_s, hd), (QK_NOPE + QK_ROPE) ** -0.5 * log2e))
    for hd in range(H_B):
        sl = slice(hd * LANES, (hd + 1) * LANES)
        for part in range(2):
            def qfn(sl=sl, part=part):
                dq = dq_ref[:, sl].astype(F32)
                return (jnp.where(lo, dq, 0.0) if part == 0 else jnp.where(lo, 0.0, dq)).astype(BF16)
            jobs.append((qfn, (dkx_s, sl), (dvx_s, hd), log2e))

    def scores(job):
        qfn, (kref, sl), _, _ = job
        return _dot_nt(qfn(), kref[:, sl])

    def finish(s, job):
        _, _, (vref, hd), c = job
        m = jnp.max(s, axis=-1, keepdims=True)
        e = jnp.exp2((s - m) * c).astype(BF16)
        res = _dot(e, vref[:, 2 * hd * LANES:(2 * hd + 2) * LANES])
        return res[:, 0:LANES] * (1.0 / res[:, LANES:2 * LANES])

    outs = []
    s_next = scores(jobs[0])
    for i, job in enumerate(jobs):
        s_cur = s_next
        if i + 1 < len(jobs):
            s_next = scores(jobs[i + 1])
        outs.append(finish(s_cur, job))

    for j in range(H_A // 2):
        o_ref[:, j * LANES:(j + 1) * LANES] = (outs[2 * j] + outs[2 * j + 1]).astype(BF16)
    lv = lamv_ref[...]
    lam = (jnp.exp(jnp.sum(lv[0:1] * lv[1:2], axis=-1, keepdims=True))
           - jnp.exp(jnp.sum(lv[2:3] * lv[3:4], axis=-1, keepdims=True)) + lam_init)
    for hd in range(H_B):
        acc = outs[H_A + 2 * hd] - lam * outs[H_A + 2 * hd + 1]
        ob = (_rms(acc) * gsub_ref[...]) * (1.0 - lam_init)
        o_ref[:, H_A * V_HD_A + hd * LANES:H_A * V_HD_A + (hd + 1) * LANES] = ob.astype(BF16)


def _attn0(qm, km, vm, dq, dk, dv, lamv, gsub, ctx, wukv, seq, lam_init):
    n = qm.shape[0]
    nb = n // seq
    tq = min(TQ_ATTN, seq)
    has_ctx = ctx is not None
    full = lambda shape: pl.BlockSpec(shape, lambda b, t: (0,) * len(shape))
    qrow = lambda w: pl.BlockSpec((tq, w), lambda b, t: (b * (seq // tq) + t, 0))
    krow = lambda w: pl.BlockSpec((seq, w), lambda b, t: (b, 0))
    in_specs = [qrow(W_MLA), qrow(W_DIFF), krow(W_MLA), krow(W_V), krow(W_DIFF), krow(W_DIFF),
                full((4, DIFF_HD)), full((1, DIFF_VD))]
    args = [qm, dq, km, vm, dk, dv, lamv, gsub]
    past = 0
    if has_ctx:
        ckv_c, krb_c, dk_c, dv_c = ctx
        past = ckv_c.shape[1]
        in_specs += [pl.BlockSpec((1, past, LANES), lambda b, t: (b, 0, 0)),
                     pl.BlockSpec((1, past, LANES), lambda b, t: (b, 0, 0)),
                     pl.BlockSpec((1, H_B, past, LANES), lambda b, t: (b, 0, 0, 0)),
                     pl.BlockSpec((1, H_B, past, LANES), lambda b, t: (b, 0, 0, 0)),
                     full(wukv.shape)]
        args += [ckv_c, krb_c, dk_c, dv_c, wukv]
    ktot = seq + past
    scratch = [pltpu.VMEM((ktot, H_A * HEAD_PAD), BF16), pltpu.VMEM((ktot, 2 * H_A * LANES), BF16),
               pltpu.VMEM((ktot, H_B * LANES), BF16), pltpu.VMEM((ktot, 2 * H_B * LANES), BF16)]
    return pl.pallas_call(
        functools.partial(_attn_kernel, has_ctx=has_ctx, lam_init=lam_init),
        grid=(nb, seq // tq),
        in_specs=in_specs,
        out_specs=pl.BlockSpec((tq, D_MODEL), lambda b, t: (b * (seq // tq) + t, 0)),
        out_shape=jax.ShapeDtypeStruct((n, D_MODEL), BF16),
        scratch_shapes=scratch,
        compiler_params=_params(("arbitrary", "arbitrary")),
        name="attn0_lat" if has_ctx else "attn0_ctx",
    )(*args)


def _tail_kernel(*refs, final):
    x_ref, a_ref, mod_ref, wo_ref, gf_ref, win_ref, wout_ref = refs[:7]
    if final:
        gfin_ref, o_ref = refs[7:]
    else:
        (o_ref,) = refs[7:]
    x1 = x_ref[...] + mod_ref[0, 2:3, :] * _dot(a_ref[...], wo_ref[...])
    h = _modulate(x1, gf_ref[...], mod_ref[0, 3:4, :], mod_ref[0, 4:5, :]).astype(BF16)
    bounds = list(range(0, D_FF, FF_CHUNK)) + [D_FF]
    acc = None
    for lo, hi in zip(bounds[:-1], bounds[1:]):
        a = _dot(h, win_ref[0, :, lo:hi])
        b = _dot(h, win_ref[0, :, D_FF + lo:D_FF + hi])
        act = ((a * jax.nn.sigmoid(a)) * b).astype(BF16)
        part = _dot(act, wout_ref[0, lo:hi, :])
        acc = part if acc is None else acc + part
    x2 = x1 + mod_ref[0, 5:6, :] * acc
    if final:
        x2 = _rms(x2) * gfin_ref[...]
    o_ref[...] = x2


def _tail(x, a, mod, mod_row, wo, gf, win, wout, layer, gfin, name):
    n = x.shape[0]
    tm = TM_TAIL
    final = gfin is not None
    full = lambda shape: pl.BlockSpec(shape, lambda i: (0,) * len(shape))
    resident = lambda shape: pl.BlockSpec((1,) + shape[1:], lambda i: (layer, 0, 0),
                                          pipeline_mode=pl.Buffered(1))
    in_specs = [
        pl.BlockSpec((tm, D_MODEL), lambda i: (i, 0)),
        pl.BlockSpec((tm, D_MODEL), lambda i: (i, 0)),
        pl.BlockSpec((1, N_MOD, D_MODEL), lambda i: (mod_row(i), 0, 0)),
        full(wo.shape), full((1, D_MODEL)), resident(win.shape), resident(wout.shape),
    ]
    args = [x, a, mod, wo, gf, win, wout]
    if final:
        in_specs.append(full((1, D_MODEL)))
        args.append(gfin)
    return pl.pallas_call(
        functools.partial(_tail_kernel, final=final),
        grid=(n // tm,),
        in_specs=in_specs,
        out_specs=pl.BlockSpec((tm, D_MODEL), lambda i: (i, 0)),
        out_shape=jax.ShapeDtypeStruct((n, D_MODEL), F32),
        compiler_params=_params(("arbitrary",)),
        name=name,
    )(*args)


def _split3(x):
    hi = x.astype(BF16)
    r1 = x - hi.astype(F32)
    mid = r1.astype(BF16)
    lo = (r1 - mid.astype(F32)).astype(BF16)
    return hi, mid, lo


def _proj1_kernel(x_ref, mod_ref, g_ref, w1_ref, wg_ref, bg_ref, q_ref, k_ref, v_ref, o_ref, gc_ref, gr_ref):
    tm = x_ref.shape[0]
    L = MLSTM_L
    hk = H_C * DK_C
    hv = H_C * DV_C
    h = _modulate(x_ref[...], g_ref[...], mod_ref[0, 0:1, :], mod_ref[0, 1:2, :]).astype(BF16)
    gates = _dot(h, wg_ref[...]) + bg_ref[...]
    proj = _dot(h, w1_ref[...])
    for blk in range(hk // LANES):
        sl = slice(blk * LANES, (blk + 1) * LANES)
        q_ref[sl, :] = jnp.transpose(proj[:, sl]).astype(BF16)
    for blk in range(hv // LANES):
        sl = slice(blk * LANES, (blk + 1) * LANES)
        v_ref[sl, :] = jnp.transpose(proj[:, 2 * hk + blk * LANES:2 * hk + (blk + 1) * LANES]).astype(BF16)
    k_ref[...] = proj[:, hk:2 * hk] * (DK_C ** -0.5)
    o_ref[...] = proj[:, 2 * hk + hv:2 * hk + 2 * hv]
    lf = jnp.minimum(gates, 0.0) - jnp.log1p(jnp.exp(-jnp.abs(gates)))
    r = lax.broadcasted_iota(jnp.int32, (L, L), 0)
    c = lax.broadcasted_iota(jnp.int32, (L, L), 1)
    pre = jnp.where(c <= r, 1.0, 0.0).astype(BF16)
    suf = jnp.where(c >= r, 1.0, 0.0).astype(BF16)
    parts = jnp.concatenate(_split3(lf), axis=1)
    fold = lambda t: t[:, 0:LANES] + t[:, LANES:2 * LANES] + t[:, 2 * LANES:3 * LANES]
    chunks = [parts[ck * L:(ck + 1) * L] for ck in range(tm // L)]
    lane = lax.broadcasted_iota(jnp.int32, (tm, LANES), 1)
    b_sum = jnp.where((lane % 4) >= 2,
                      jnp.concatenate([fold(_dot(suf, p)) for p in chunks], axis=0),
                      jnp.concatenate([fold(_dot(pre, p)) for p in chunks], axis=0))
    b = pltpu.roll(b_sum, LANES - N_GATE, 1)
    u = gates - b
    pos = lax.broadcasted_iota(jnp.int32, (tm, LANES), 0) % L
    bwd = (lane % 4) >= 2
    cm = u
    step = 1
    while step < L:
        below = jnp.where(pos >= step, pltpu.roll(cm, step, 0), -jnp.inf)
        above = jnp.where(pos < L - step, pltpu.roll(cm, tm - step, 0), -jnp.inf)
        cm = jnp.maximum(cm, jnp.where(bwd, above, below))
        step *= 2
    low = lane < N_GATE
    packed = (jnp.where(low, b, 0.0) + pltpu.roll(jnp.where(low, cm, 0.0), N_GATE, 1)
              + pltpu.roll(jnp.where(low, u, 0.0), 2 * N_GATE, 1))
    gc_ref[...] = packed
    gr_ref[...] = jnp.transpose(packed)[0:3 * N_GATE]


def _proj1(x, mod, mod_row, g, w1, wg, bg):
    n = x.shape[0]
    tm = TM_PROJ1
    assert tm % MLSTM_L == 0
    full = lambda shape: pl.BlockSpec(shape, lambda i: (0,) * len(shape))
    row = lambda w: pl.BlockSpec((tm, w), lambda i: (i, 0))
    col = lambda w: pl.BlockSpec((w, tm), lambda i: (0, i))
    hk, hv = H_C * DK_C, H_C * DV_C
    return pl.pallas_call(
        _proj1_kernel,
        grid=(n // tm,),
        in_specs=[row(D_MODEL), pl.BlockSpec((1, N_MOD, D_MODEL), lambda i: (mod_row(i), 0, 0)),
                  full((1, D_MODEL)), full((D_MODEL, 2 * hk + 2 * hv)), full(wg.shape), full((1, LANES))],
        out_specs=[col(hk), row(hk), col(hv), row(hv), row(LANES),
                   pl.BlockSpec((3 * N_GATE, tm), lambda i: (0, i))],
        out_shape=[jax.ShapeDtypeStruct((hk, n), BF16), jax.ShapeDtypeStruct((n, hk), F32),
                   jax.ShapeDtypeStruct((hv, n), BF16), jax.ShapeDtypeStruct((n, hv), F32),
                   jax.ShapeDtypeStruct((n, LANES), F32), jax.ShapeDtypeStruct((3 * N_GATE, n), F32)],
        compiler_params=_params(("arbitrary",)),
        name="proj1",
    )(x, mod, g, w1, wg, bg)


def _chain(j, d, hh):
    return (2 * j + d) * 2 + hh


def _mlstm_kernel(*refs, has_state, emit_state):
    q_ref, k_ref, v_ref, o_ref, gc_ref, gr_ref, gn_ref = refs[:7]
    refs = refs[7:]
    if has_state:
        c0_ref, n0_ref, m0_ref = refs[:3]
        refs = refs[3:]
    hs_ref = refs[0]
    refs = refs[1:]
    if emit_state:
        cf_ref, nst_ref, mst_ref = refs[:3]
        refs = refs[3:]
    cx_s, m_s, h_s = refs

    L = MLSTM_L
    seq = k_ref.shape[0]
    nc = seq // L
    npair = H_C // 2
    lane = lax.broadcasted_iota(jnp.int32, (1, LANES), 1)
    head_mask = [lane < DK_C, lane >= DK_C]
    ri = lax.broadcasted_iota(jnp.int32, (L, L), 0)
    ci = lax.broadcasted_iota(jnp.int32, (L, L), 1)
    causal = [ri <= ci, ri >= ci]
    ones_blk = jnp.ones((N_ROWS_N, L), BF16)
    chains = [(j, d, hh) for d in range(2) for j in range(npair) for hh in range(2)]

    h_s[...] = jnp.zeros_like(h_s)
    for j, d, hh in chains:
        ch = _chain(j, d, hh)
        if has_state:
            zpad = jnp.zeros((DK_C, DV_C), F32)
            c0 = c0_ref[0, d, 2 * j + hh]
            c0 = jnp.concatenate([c0, zpad] if hh == 0 else [zpad, c0], axis=0)
            cx_s[ch, 0:DV_C, :] = jnp.transpose(c0)
            n_row = jnp.where(head_mask[hh], n0_ref[0, d, j], 0.0)
            cx_s[ch, DV_C:DV_C + N_ROWS_N, :] = jnp.broadcast_to(n_row, (N_ROWS_N, LANES))
            m_s[ch] = m0_ref[0, d, 2 * j + hh]
        else:
            cx_s[ch] = jnp.zeros((DV_C + N_ROWS_N, LANES), F32)
            m_s[ch] = jnp.zeros((1, LANES), F32)

    def chunk_step(i, carry):
        sl = [pl.ds(pl.multiple_of(i * L, L), L), pl.ds(pl.multiple_of((nc - 1 - i) * L, L), L)]
        gcol = [gc_ref[sl[d], :] for d in range(2)]
        grow = [gr_ref[:, sl[d]] for d in range(2)]
        kpair = {(d, j): k_ref[sl[d], j * LANES:(j + 1) * LANES] for d in range(2) for j in range(npair)}
        kbf = {key: kk.astype(BF16) for key, kk in kpair.items()}
        zq = jnp.zeros((DK_C, L), BF16)
        qt = {}
        for d in range(2):
            for j in range(npair):
                qt[(j, d, 0)] = jnp.concatenate([q_ref[j * LANES:j * LANES + DK_C, sl[d]], zq], axis=0)
                qt[(j, d, 1)] = jnp.concatenate([zq, q_ref[j * LANES + DK_C:(j + 1) * LANES, sl[d]]], axis=0)
        st, u_bc, vt, row = {}, {}, {}, {}
        for d in range(2):
            for j in range(npair):
                pair_scores = _dot(kbf[(d, j)], jnp.concatenate([qt[(j, d, 0)], qt[(j, d, 1)]], axis=1))
                st[(j, d, 0)], st[(j, d, 1)] = pair_scores[:, 0:L], pair_scores[:, L:2 * L]
        for j, d, hh in chains:
            key = (j, d, hh)
            idx = 4 * j + 2 * d + hh
            edge = L - 1 if d == 0 else 0
            b_row = grow[d][idx:idx + 1, :]
            cm_row = grow[d][N_GATE + idx:N_GATE + idx + 1, :]
            u_col = gcol[d][:, 2 * N_GATE + idx:2 * N_GATE + idx + 1]
            m_prev = m_s[_chain(j, d, hh)][:, 0:1]
            row[key] = (b_row, cm_row, b_row[:, edge:edge + 1], cm_row[:, edge:edge + 1], m_prev)
            u_bc[key] = jnp.broadcast_to(u_col, (L, L))
            vt[key] = v_ref[(2 * j + hh) * DV_C:(2 * j + hh + 1) * DV_C, sl[d]]
        for j, d, hh in chains:
            key = (j, d, hh)
            ch = _chain(j, d, hh)
            b_row, cm_row, g_tot, cm_last, m_prev = row[key]
            head = slice((2 * j + hh) * DV_C, (2 * j + hh + 1) * DV_C)
            cx = cx_s[ch]
            big_m = jnp.maximum(m_prev, cm_row)
            s = st[key] * jnp.exp(jnp.where(causal[d], u_bc[key] - big_m, -jnp.inf))
            inter = jnp.exp(m_prev - big_m)
            vx = jnp.concatenate([vt[key], ones_blk], axis=0)
            lhs = jnp.concatenate([vx, cx.astype(BF16)], axis=1)
            rhs = jnp.concatenate([s.astype(BF16), (inter * qt[key].astype(F32)).astype(BF16)], axis=0)
            res = _dot(lhs, rhs)
            den = res[DV_C:DV_C + 1, :]
            hval = res[0:DV_C, :] * (1.0 / jnp.maximum(jnp.abs(den), jnp.exp(-(b_row + big_m))))
            h_s[head, sl[d]] = h_s[head, sl[d]] + hval

            m_top = jnp.maximum(m_prev, cm_last)
            kw = kpair[(d, j)] * jnp.exp(u_bc[key] - m_top)
            cx_s[ch] = jnp.exp(m_prev - m_top) * cx + _dot(vx, kw.astype(BF16))
            m_s[ch] = jnp.broadcast_to(g_tot + m_top, (1, LANES))
        return carry

    lax.fori_loop(0, nc, chunk_step, 0)

    gn_col = jnp.broadcast_to(gn_ref[...], (DV_C, LANES))
    gn_col = jnp.concatenate([gn_col] * (seq // LANES), axis=1)
    for hd in range(H_C):
        sl = slice(hd * DV_C, (hd + 1) * DV_C)
        ht = h_s[sl, :]
        ms = jnp.mean(ht * ht, axis=0, keepdims=True)
        y = jnp.transpose((ht * lax.rsqrt(ms + RMS_EPS)) * gn_col) * jax.nn.sigmoid(o_ref[:, sl])
        hs_ref[:, sl] = y.astype(BF16)
    if emit_state:
        for j, d, hh in chains:
            ch = _chain(j, d, hh)
            cf_ref[0, 0, d, 2 * j + hh] = jnp.transpose(cx_s[ch, 0:DV_C, :])[hh * DK_C:(hh + 1) * DK_C, :]
            mst_ref[0, d * H_C + 2 * j + hh:d * H_C + 2 * j + hh + 1, :] = m_s[ch]
        for j in range(npair):
            for d in range(2):
                nst_ref[0, d * npair + j:d * npair + j + 1, :] = jnp.where(
                    head_mask[0], cx_s[_chain(j, d, 0), DV_C:DV_C + 1, :], cx_s[_chain(j, d, 1), DV_C:DV_C + 1, :])


def _mlstm(q, k, v, o, gc, gr, gn, state, seq, emit_state):
    n = k.shape[0]
    nb = n // seq
    npair = H_C // 2
    has_state = state is not None
    hk, hv = H_C * DK_C, H_C * DV_C
    in_specs = [
        pl.BlockSpec((hk, seq), lambda b: (0, b)),
        pl.BlockSpec((seq, hk), lambda b: (b, 0)),
        pl.BlockSpec((hv, seq), lambda b: (0, b)),
        pl.BlockSpec((seq, hv), lambda b: (b, 0)),
        pl.BlockSpec((seq, LANES), lambda b: (b, 0)),
        pl.BlockSpec((3 * N_GATE, seq), lambda b: (0, b)),
        pl.BlockSpec((DV_C, 1), lambda b: (0, 0)),
    ]
    args = [q, k, v, o, gc, gr, gn]
    if has_state:
        c0, n0, m0 = state
        in_specs += [pl.BlockSpec((1, 2, H_C, DK_C, DV_C), lambda b: (b, 0, 0, 0, 0)),
                     pl.BlockSpec((1, 2, npair, 1, LANES), lambda b: (b, 0, 0, 0, 0)),
                     pl.BlockSpec((1, 2, H_C, 1, LANES), lambda b: (b, 0, 0, 0, 0))]
        args += [c0, n0, m0]
    out_specs = [pl.BlockSpec((seq, hv), lambda b: (b, 0))]
    out_shape = [jax.ShapeDtypeStruct((n, hv), BF16)]
    if emit_state:
        out_specs += [pl.BlockSpec((1, 1, 2, H_C, DK_C, DV_C), lambda b: (b, 0, 0, 0, 0, 0)),
                      pl.BlockSpec((1, 2 * npair, LANES), lambda b: (b, 0, 0)),
                      pl.BlockSpec((1, 2 * H_C, LANES), lambda b: (b, 0, 0))]
        out_shape += [jax.ShapeDtypeStruct((nb, 1, 2, H_C, DK_C, DV_C), F32),
                      jax.ShapeDtypeStruct((nb, 2 * npair, LANES), F32),
                      jax.ShapeDtypeStruct((nb, 2 * H_C, LANES), F32)]
    n_chain = 2 * H_C
    return pl.pallas_call(
        functools.partial(_mlstm_kernel, has_state=has_state, emit_state=emit_state),
        grid=(nb,),
        in_specs=in_specs, out_specs=out_specs, out_shape=out_shape,
        scratch_shapes=[pltpu.VMEM((n_chain, DV_C + N_ROWS_N, LANES), F32), pltpu.VMEM((n_chain, 1, LANES), F32),
                        pltpu.VMEM((hv, seq), F32)],
        compiler_params=_params(("arbitrary",)),
        name="mlstm_lat" if has_state else "mlstm_ctx",
    )(*args)


def _rope_tables(n_tok):
    t = np.arange(n_tok)
    rows = (t // GRID_W).astype(np.float64)
    cols = (t % GRID_W).astype(np.float64)

    def axis_tabs(width, lane0):
        half = width // 2
        quarter = half // 2
        freqs = np.power(ROPE_BASE, -np.arange(quarter, dtype=np.float64) / quarter)
        c = np.ones((n_tok, LANES))
        sa = np.zeros((n_tok, LANES))
        sb = np.zeros((n_tok, LANES))
        for g, pos in enumerate((rows, cols)):
            ang = pos[:, None] * freqs[None, :]
            a0 = lane0 + g * half
            c[:, a0:a0 + quarter] = np.cos(ang)
            c[:, a0 + quarter:a0 + half] = np.cos(ang)
            sa[:, a0:a0 + quarter] = -np.sin(ang)
            sb[:, a0 + quarter:a0 + half] = np.sin(ang)
        return c, sa, sb

    cm, sam, sbm = axis_tabs(QK_ROPE, ROPE_LANE0)
    c0, sa0, sb0 = axis_tabs(DIFF_HD, 0)
    c1, sa1, sb1 = axis_tabs(DIFF_HD, DIFF_HD)
    cd = np.where(np.arange(LANES)[None, :] < DIFF_HD, c0, c1)
    return tuple(jnp.asarray(a, F32) for a in (cm, sam, sbm, cd, sa0 + sa1, sb0 + sb1))


def _prep_even(w_in_ab, w_uq, w_ukv):
    c2 = Q_LORA + KV_LORA
    w_ab = w_in_ab.astype(BF16)
    wd = w_ab[:, c2 + QK_ROPE:]
    wq = jnp.pad(w_uq.reshape(Q_LORA, H_A, QK_NOPE + QK_ROPE),
                 ((0, 0), (0, 0), (0, HEAD_PAD - QK_NOPE - QK_ROPE))).reshape(Q_LORA, H_A * HEAD_PAD)
    kvw = w_ukv.reshape(KV_LORA, H_A, QK_NOPE + V_HD_A)
    kpad = jnp.pad(kvw[..., :QK_NOPE], ((0, 0), (0, 0), (0, HEAD_PAD - QK_NOPE)))
    vw = kvw[..., QK_NOPE:]
    wukv = jnp.concatenate([kpad.reshape(KV_LORA, -1), vw.reshape(KV_LORA, -1)], axis=1)
    return (w_ab, wd), wq.astype(BF16), wukv.astype(BF16)


def _gate_order(g):
    lead = g.shape[:-1]
    g = g.reshape(lead + (2, 2, H_C // 2, 2))
    perm = tuple(range(len(lead))) + tuple(len(lead) + a for a in (1, 2, 0, 3))
    return g.transpose(perm).reshape(lead + (4 * H_C,))


def _prep_odd(w_in_c, b_gate_c):
    ng = 4 * H_C
    base = w_in_c.shape[1] - ng
    w1 = w_in_c.astype(BF16)
    wg = jnp.pad(_gate_order(w1[:, base:]), ((0, 0), (0, LANES - ng)))
    bg = jnp.pad(_gate_order(b_gate_c), (0, LANES - ng)).reshape(1, LANES)
    return w1, wg, bg


def kernel(x_prompt, x_sample, cache_mla_ckv, cache_mla_krope, cache_diff_k, cache_diff_v,
           state_mlstm_C, state_mlstm_n, state_mlstm_m, c, c_ctx,
           w_ada, b_ada, g_mix, g_ffn, w_ffn_in, w_ffn_out,
           w_in_ab, g_q_lora, g_kv_lora, w_uq, w_ukv, diff_lambda, g_diff_subln, w_out_ab,
           w_in_c, b_gate_c, g_mlstm, w_out_c, g_final):
    nbp, seq_p, _ = x_prompt.shape
    nbs, seq_s, _ = x_sample.shape
    past = cache_mla_ckv.shape[2]
    assert DEPTH == 2 and 1 + nbs <= COND_ROWS
    assert cache_mla_ckv.shape[1] == 1 and state_mlstm_C.shape[1] == 1

    cond = jnp.concatenate([c_ctx[None], c, jnp.zeros((COND_ROWS - 1 - nbs, D_MODEL), F32)], axis=0)
    mod = _ada(cond, w_ada, b_ada).reshape(DEPTH * COND_ROWS, N_MOD, D_MODEL)

    xp = x_prompt.reshape(nbp * seq_p, D_MODEL)
    xs = x_sample.reshape(nbs * seq_s, D_MODEL)
    row2 = lambda v: v.reshape(1, -1)

    def mod_rows(layer, tm):
        prompt = lambda i: layer * COND_ROWS
        sample = lambda i: layer * COND_ROWS + 1 + i // (seq_s // tm)
        return prompt, sample

    lam_init = 0.8 - 0.6 * math.exp(-0.3 * 0)
    w0, wq, wukv = _prep_even(w_in_ab[0], w_uq[0], w_ukv[0])
    rope_tabs = _rope_tables(seq_s)
    mrp, _ = mod_rows(0, TM_PROJ)
    _, mrs = mod_rows(0, TM_PROJ_ROPE)
    gq, gkv = row2(g_q_lora[0]), row2(g_kv_lora[0])
    outs_p = _proj0(xp, mod, mrp, row2(g_mix[0]), *w0, gq, gkv, wq, wukv, None, seq_p)
    outs_s = _proj0(xs, mod, mrs, row2(g_mix[0]), *w0, gq, gkv, wq, wukv, rope_tabs, seq_s)
    qm_p, km_p, vm_p, dq_p, dk_p, dv_p, ckv_new, kr_new, dk_new, dv_new = outs_p
    qm_s, km_s, vm_s, dq_s, dk_s, dv_s = outs_s
    gsub = row2(g_diff_subln[0])
    krb_c = jnp.pad(cache_mla_krope.reshape(nbs, past, QK_ROPE),
                    ((0, 0), (0, 0), (ROPE_LANE0, LANES - ROPE_LANE0 - QK_ROPE)))
    ctx = (cache_mla_ckv.reshape(nbs, past, KV_LORA), krb_c,
           cache_diff_k.reshape(nbs, H_B, past, 2 * DIFF_HD), cache_diff_v.reshape(nbs, H_B, past, DIFF_VD))
    a_p = _attn0(qm_p, km_p, vm_p, dq_p, dk_p, dv_p, diff_lambda[0], gsub, None, None, seq_p, lam_init)
    a_s = _attn0(qm_s, km_s, vm_s, dq_s, dk_s, dv_s, diff_lambda[0], gsub, ctx, wukv, seq_s, lam_init)
    mtp, mts = mod_rows(0, TM_TAIL)
    wo0 = w_out_ab[0].astype(BF16)
    win, wout = w_ffn_in.astype(BF16), w_ffn_out.astype(BF16)
    xp = _tail(xp, a_p, mod, mtp, wo0, row2(g_ffn[0]), win, wout, 0, None, "tail0_ctx")
    xs = _tail(xs, a_s, mod, mts, wo0, row2(g_ffn[0]), win, wout, 0, None, "tail0_lat")

    w1, wg, bg = _prep_odd(w_in_c[0], b_gate_c[0])
    mrp, mrs = mod_rows(1, TM_PROJ1)
    gn = g_mlstm[0].reshape(DV_C, 1)
    q_p, k_p, v_p, o_p, gc_p, gr_p = _proj1(xp, mod, mrp, row2(g_mix[1]), w1, wg, bg)
    q_s, k_s, v_s, o_s, gc_s, gr_s = _proj1(xs, mod, mrs, row2(g_mix[1]), w1, wg, bg)
    hs_p, c_new, nst, mst = _mlstm(q_p, k_p, v_p, o_p, gc_p, gr_p, gn, None, seq_p, True)
    state = (state_mlstm_C.reshape(nbs, 2, H_C, DK_C, DV_C),
             state_mlstm_n.reshape(nbs, 2, H_C // 2, 1, LANES),
             jnp.broadcast_to(state_mlstm_m.reshape(nbs, 2, H_C, 1, 1), (nbs, 2, H_C, 1, LANES)))
    (hs_s,) = _mlstm(q_s, k_s, v_s, o_s, gc_s, gr_s, gn, state, seq_s, False)
    mtp, mts = mod_rows(1, TM_TAIL)
    wo1 = w_out_c[0].astype(BF16)
    gfin = row2(g_final)
    yp = _tail(xp, hs_p, mod, mtp, wo1, row2(g_ffn[1]), win, wout, 1, gfin, "tail1_ctx")
    ys = _tail(xs, hs_s, mod, mts, wo1, row2(g_ffn[1]), win, wout, 1, gfin, "tail1_lat")

    return (yp.reshape(nbp, seq_p, D_MODEL), ys.reshape(nbs, seq_s, D_MODEL),
            ckv_new.reshape(nbp, 1, seq_p, KV_LORA), kr_new.reshape(nbp, 1, seq_p, QK_ROPE),
            dk_new.reshape(nbp, 1, H_B, seq_p, 2 * DIFF_HD), dv_new.reshape(nbp, 1, H_B, seq_p, DIFF_VD),
            c_new, nst.reshape(nbp, 1, 2, H_C, DK_C), mst[:, :, 0].reshape(nbp, 1, 2, H_C))
```

```python
import functools
import math

import jax
import jax.numpy as jnp
import numpy as np
from jax import lax
from jax.experimental import pallas as pl
from jax.experimental.pallas import tpu as pltpu

F32 = jnp.float32
BF16 = jnp.bfloat16

D_MODEL = 1024
DEPTH = 2
GRID_W = 64
ROPE_BASE = 10000.0
RMS_EPS = 1e-6
H_A = 8
QK_NOPE = 64
QK_ROPE = 32
V_HD_A = 64
Q_LORA = 256
KV_LORA = 128
H_B = 4
DIFF_HD = 64
DIFF_VD = 2 * DIFF_HD
H_C = 8
DK_C = 64
DV_C = D_MODEL // H_C
D_FF = -(-8 * D_MODEL // (3 * 256)) * 256

LANES = 128
HEAD_PAD = LANES
ROPE_LANE0 = QK_NOPE
W_MLA = H_A * HEAD_PAD
W_DIFF = H_B * LANES
N_MOD = 6
COND_ROWS = 16
N_GATE = 2 * H_C
N_ROWS_N = 16

TM_PROJ = 1024
TM_PROJ_ROPE = 512
TM_PROJ1 = 1024
TQ_ATTN = 512
TM_TAIL = 1024
FF_CHUNK = 256
MLSTM_L = 128
ADA_TN = 1024
VMEM_LIMIT = 56 * 1024 * 1024

assert D_FF % LANES == 0 and FF_CHUNK % LANES == 0


def _dot(a, b):
    return jnp.dot(a, b, preferred_element_type=F32)


def _dot_nt(a, b):
    return lax.dot_general(a, b, (((1,), (1,)), ((), ())), preferred_element_type=F32)


def _rms(x):
    return x * lax.rsqrt(jnp.mean(x * x, axis=-1, keepdims=True) + RMS_EPS)


def _modulate(x, g, shift, scale):
    return _rms(x) * (g * (1.0 + scale)) + shift


def _params(semantics):
    return pltpu.CompilerParams(dimension_semantics=semantics, vmem_limit_bytes=VMEM_LIMIT)


def _ada_kernel(cond_ref, w_ref, b_ref, o_ref):
    c = cond_ref[...]
    s = (c * jax.nn.sigmoid(c)).astype(BF16)
    o_ref[0] = _dot(s, w_ref[0].astype(BF16)) + b_ref[0]


def _ada(cond, w_ada, b_ada):
    n_out = w_ada.shape[-1]
    return pl.pallas_call(
        _ada_kernel,
        grid=(DEPTH, n_out // ADA_TN),
        in_specs=[
            pl.BlockSpec((COND_ROWS, D_MODEL), lambda l, n: (0, 0)),
            pl.BlockSpec((1, D_MODEL, ADA_TN), lambda l, n: (l, 0, n)),
            pl.BlockSpec((1, 1, ADA_TN), lambda l, n: (l, 0, n)),
        ],
        out_specs=pl.BlockSpec((1, COND_ROWS, ADA_TN), lambda l, n: (l, 0, n)),
        out_shape=jax.ShapeDtypeStruct((DEPTH, COND_ROWS, n_out), F32),
        compiler_params=_params(("arbitrary", "arbitrary")),
        name="ada",
    )(cond, w_ada, b_ada.reshape(DEPTH, 1, n_out))


def _rope(x, c, sa, sb, off):
    return x * c + pltpu.roll(x, LANES - off, 1) * sa + pltpu.roll(x, off, 1) * sb


def _proj0_kernel(*refs, rope, emit_cache):
    x_ref, mod_ref, g_ref, wa_ref, wd_ref, gq_ref, gkv_ref, wq_ref, wukv_ref = refs[:9]
    refs = refs[9:]
    if rope:
        cm_ref, sam_ref, sbm_ref, cd_ref, sad_ref, sbd_ref = refs[:6]
        refs = refs[6:]
    qm_ref, km_ref, vm_ref, dq_ref, dk_ref, dv_ref = refs[:6]
    refs = refs[6:]
    if emit_cache:
        ckvf_ref, krf_ref, dkf_ref, dvf_ref = refs

    h = _modulate(x_ref[...], g_ref[...], mod_ref[0, 0:1, :], mod_ref[0, 1:2, :]).astype(BF16)
    pa = _dot(h, wa_ref[...])
    pd = _dot(h, wd_ref[...])
    lane = lax.broadcasted_iota(jnp.int32, (pa.shape[0], LANES), 1)
    krb = jnp.where((lane >= ROPE_LANE0) & (lane < ROPE_LANE0 + QK_ROPE),
                    pltpu.roll(pa[:, Q_LORA + KV_LORA:Q_LORA + KV_LORA + LANES], ROPE_LANE0, 1), 0.0)
    cq = pa[:, 0:Q_LORA]
    ckv = _rms(pa[:, Q_LORA:Q_LORA + KV_LORA]) * gkv_ref[...]
    qa = _dot((_rms(cq) * gq_ref[...]).astype(BF16), wq_ref[...])
    kv = _dot(ckv.astype(BF16), wukv_ref[...])
    if emit_cache:
        ckvf_ref[...] = ckv
        krf_ref[...] = krb[:, ROPE_LANE0:ROPE_LANE0 + QK_ROPE]
    if rope:
        cm, sam, sbm = cm_ref[...], sam_ref[...], sbm_ref[...]
        cd, sad, sbd = cd_ref[...], sad_ref[...], sbd_ref[...]
        krb = _rope(krb, cm, sam, sbm, QK_ROPE // 4)
    for hd in range(H_A):
        sl = slice(hd * HEAD_PAD, (hd + 1) * HEAD_PAD)
        qh = qa[:, sl]
        if rope:
            qh = _rope(qh, cm, sam, sbm, QK_ROPE // 4)
        qm_ref[:, sl] = qh.astype(BF16)
        km_ref[:, sl] = (kv[:, sl] + krb).astype(BF16)
    vm_ref[...] = kv[:, H_A * HEAD_PAD:].astype(BF16)
    for hd in range(H_B):
        sl = slice(hd * LANES, (hd + 1) * LANES)
        dq = pd[:, hd * LANES:(hd + 1) * LANES]
        dk = pd[:, (H_B + hd) * LANES:(H_B + hd + 1) * LANES]
        dv = pd[:, (2 * H_B + hd) * LANES:(2 * H_B + hd + 1) * LANES]
        if emit_cache:
            seq = dkf_ref.shape[2]
            for bi in range(dkf_ref.shape[0]):
                dkf_ref[bi, hd] = dk[bi * seq:(bi + 1) * seq]
                dvf_ref[bi, hd] = dv[bi * seq:(bi + 1) * seq]
        if rope:
            dq = _rope(dq, cd, sad, sbd, DIFF_HD // 4)
            dk = _rope(dk, cd, sad, sbd, DIFF_HD // 4)
        dq_ref[:, sl] = (dq * (DIFF_HD ** -0.5)).astype(BF16)
        dk_ref[:, sl] = dk.astype(BF16)
        dv_ref[:, sl] = dv.astype(BF16)


def _proj0(x, mod, mod_row, g, w_ab, wd, gq, gkv, wq, wukv, rope_tabs, seq):
    n = x.shape[0]
    rope = rope_tabs is not None
    emit_cache = not rope
    tm = TM_PROJ_ROPE if rope else TM_PROJ
    tiles_per_seq = max(seq // tm, 1)
    full = lambda shape: pl.BlockSpec(shape, lambda i: (0,) * len(shape))
    in_specs = [
        pl.BlockSpec((tm, D_MODEL), lambda i: (i, 0)),
        pl.BlockSpec((1, N_MOD, D_MODEL), lambda i: (mod_row(i), 0, 0)),
        full((1, D_MODEL)), full((D_MODEL, Q_LORA + KV_LORA + LANES)), full(wd.shape),
        full((1, Q_LORA)), full((1, KV_LORA)), full(wq.shape), full(wukv.shape),
    ]
    args = [x, mod, g, w_ab, wd, gq, gkv, wq, wukv]
    if rope:
        in_specs += [pl.BlockSpec((tm, LANES), lambda i: (i % tiles_per_seq, 0))] * 6
        args += list(rope_tabs)
    row = lambda w: pl.BlockSpec((tm, w), lambda i: (i, 0))
    widths = (W_MLA, W_MLA, W_MLA, W_DIFF, W_DIFF, W_DIFF)
    out_specs = [row(w) for w in widths]
    out_shape = [jax.ShapeDtypeStruct((n, w), BF16) for w in widths]
    if emit_cache:
        assert tm % seq == 0
        cache = pl.BlockSpec((tm // seq, H_B, seq, LANES), lambda i: (i, 0, 0, 0))
        out_specs += [row(KV_LORA), row(QK_ROPE), cache, cache]
        out_shape += [jax.ShapeDtypeStruct((n, KV_LORA), F32), jax.ShapeDtypeStruct((n, QK_ROPE), F32),
                      jax.ShapeDtypeStruct((n // seq, H_B, seq, LANES), F32),
                      jax.ShapeDtypeStruct((n // seq, H_B, seq, LANES), F32)]
    return pl.pallas_call(
        functools.partial(_proj0_kernel, rope=rope, emit_cache=emit_cache),
        grid=(n // tm,),
        in_specs=in_specs, out_specs=out_specs, out_shape=out_shape,
        compiler_params=_params(("arbitrary",)),
        name="proj0_rope" if rope else "proj0_ctx",
    )(*args)


def _attn_kernel(*refs, has_ctx, lam_init):
    q_ref, dq_ref, k_ref, v_ref, dk_ref, dv_ref, lamv_ref, gsub_ref = refs[:8]
    refs = refs[8:]
    if has_ctx:
        ckvc_ref, krc_ref, dkc_ref, dvc_ref, wukv_ref = refs[:5]
        o_ref, vx_s, dvx_s, kx_s, dkx_s = refs[5:]
    else:
        o_ref, vx_s, dvx_s = refs
    seq = k_ref.shape[0]
    ktot = vx_s.shape[0]

    @pl.when(pl.program_id(1) == 0)
    def _():
        if has_ctx:
            kx_s[0:seq, :] = k_ref[...]
            dkx_s[0:seq, :] = dk_ref[...]
        ones_own = jnp.ones((seq, LANES), BF16)
        for hd in range(H_A):
            vx_s[0:seq, 2 * hd * LANES:(2 * hd + 1) * LANES] = v_ref[:, hd * LANES:(hd + 1) * LANES]
            vx_s[0:seq, (2 * hd + 1) * LANES:(2 * hd + 2) * LANES] = ones_own
        for hd in range(H_B):
            dvx_s[0:seq, 2 * hd * LANES:(2 * hd + 1) * LANES] = dv_ref[:, hd * LANES:(hd + 1) * LANES]
            dvx_s[0:seq, (2 * hd + 1) * LANES:(2 * hd + 2) * LANES] = ones_own
        if has_ctx:
            kv = _dot(ckvc_ref[0].astype(BF16), wukv_ref[...])
            krb = krc_ref[0]
            ones_ctx = jnp.ones((ktot - seq, LANES), BF16)
            for hd in range(H_A):
                sl = slice(hd * HEAD_PAD, (hd + 1) * HEAD_PAD)
                kx_s[seq:ktot, sl] = (kv[:, sl] + krb).astype(BF16)
                vx_s[seq:ktot, 2 * hd * LANES:(2 * hd + 1) * LANES] = (
                    kv[:, (H_A + hd) * HEAD_PAD:(H_A + hd + 1) * HEAD_PAD].astype(BF16))
                vx_s[seq:ktot, (2 * hd + 1) * LANES:(2 * hd + 2) * LANES] = ones_ctx
            for hd in range(H_B):
                sl = slice(hd * LANES, (hd + 1) * LANES)
                dkx_s[seq:ktot, sl] = dkc_ref[0, hd].astype(BF16)
                dvx_s[seq:ktot, 2 * hd * LANES:(2 * hd + 1) * LANES] = dvc_ref[0, hd].astype(BF16)
                dvx_s[seq:ktot, (2 * hd + 1) * LANES:(2 * hd + 2) * LANES] = ones_ctx

    log2e = 1.0 / math.log(2.0)
    tq = dq_ref.shape[0]
    lo = lax.broadcasted_iota(jnp.int32, (tq, LANES), 1) < DIFF_HD

    keys_a, keys_b = (kx_s, dkx_s) if has_ctx else (k_ref, dk_ref)
    jobs = []
    for hd in range(H_A):
        sl = slice(hd * HEAD_PAD, (hd + 1) * HEAD_PAD)
        jobs.append((lambda sl=sl: q_ref[:, sl], (keys_a, sl), (vx_s, hd), (QK_NOPE + QK_ROPE) ** -0.5 * log2e))
    for hd in range(H_B):
        sl = slice(hd * LANES, (hd + 1) * LANES)
        for part in range(2):
            def qfn(sl=sl, part=part):
                dq = dq_ref[:, sl].astype(F32)
                return (jnp.where(lo, dq, 0.0) if part == 0 else jnp.where(lo, 0.0, dq)).astype(BF16)
            jobs.append((qfn, (keys_b, sl), (dvx_s, hd), log2e))

    def scores(job):
        qfn, (kref, sl), _, _ = job
        return _dot_nt(qfn(), kref[:, sl])

    def finish(s, job):
        _, _, (vref, hd), c = job
        m = jnp.max(s, axis=-1, keepdims=True)
        e = jnp.exp2((s - m) * c).astype(BF16)
        res = _dot(e, vref[:, 2 * hd * LANES:(2 * hd + 2) * LANES])
        return res[:, 0:LANES] * (1.0 / res[:, LANES:2 * LANES])

    outs = []
    s_next = scores(jobs[0])
    for i, job in enumerate(jobs):
        s_cur = s_next
        if i + 1 < len(jobs):
            s_next = scores(jobs[i + 1])
        outs.append(finish(s_cur, job))

    for j in range(H_A // 2):
        o_ref[:, j * LANES:(j + 1) * LANES] = (outs[2 * j] + outs[2 * j + 1]).astype(BF16)
    lv = lamv_ref[...]
    lam = (jnp.exp(jnp.sum(lv[0:1] * lv[1:2], axis=-1, keepdims=True))
           - jnp.exp(jnp.sum(lv[2:3] * lv[3:4], axis=-1, keepdims=True)) + lam_init)
    for hd in range(H_B):
        acc = outs[H_A + 2 * hd] - lam * outs[H_A + 2 * hd + 1]
        ob = (_rms(acc) * gsub_ref[...]) * (1.0 - lam_init)
        o_ref[:, H_A * V_HD_A + hd * LANES:H_A * V_HD_A + (hd + 1) * LANES] = ob.astype(BF16)


def _attn0(qm, km, vm, dq, dk, dv, lamv, gsub, ctx, wukv, seq, lam_init):
    n = qm.shape[0]
    nb = n // seq
    tq = min(TQ_ATTN, seq)
    has_ctx = ctx is not None
    full = lambda shape: pl.BlockSpec(shape, lambda b, t: (0,) * len(shape))
    qrow = lambda w: pl.BlockSpec((tq, w), lambda b, t: (b * (seq // tq) + t, 0))
    krow = lambda w: pl.BlockSpec((seq, w), lambda b, t: (b, 0))
    in_specs = [qrow(W_MLA), qrow(W_DIFF), krow(W_MLA), krow(W_MLA), krow(W_DIFF), krow(W_DIFF),
                full((4, DIFF_HD)), full((1, DIFF_VD))]
    args = [qm, dq, km, vm, dk, dv, lamv, gsub]
    past = 0
    if has_ctx:
        ckv_c, krb_c, dk_c, dv_c = ctx
        past = ckv_c.shape[1]
        in_specs += [pl.BlockSpec((1, past, LANES), lambda b, t: (b, 0, 0)),
                     pl.BlockSpec((1, past, LANES), lambda b, t: (b, 0, 0)),
                     pl.BlockSpec((1, H_B, past, LANES), lambda b, t: (b, 0, 0, 0)),
                     pl.BlockSpec((1, H_B, past, LANES), lambda b, t: (b, 0, 0, 0)),
                     full(wukv.shape)]
        args += [ckv_c, krb_c, dk_c, dv_c, wukv]
    ktot = seq + past
    scratch = [pltpu.VMEM((ktot, 2 * W_MLA), BF16), pltpu.VMEM((ktot, 2 * W_DIFF), BF16)]
    if has_ctx:
        scratch += [pltpu.VMEM((ktot, W_MLA), BF16), pltpu.VMEM((ktot, W_DIFF), BF16)]
    return pl.pallas_call(
        functools.partial(_attn_kernel, has_ctx=has_ctx, lam_init=lam_init),
        grid=(nb, seq // tq),
        in_specs=in_specs,
        out_specs=pl.BlockSpec((tq, D_MODEL), lambda b, t: (b * (seq // tq) + t, 0)),
        out_shape=jax.ShapeDtypeStruct((n, D_MODEL), BF16),
        scratch_shapes=scratch,
        compiler_params=_params(("arbitrary", "arbitrary")),
        name="attn0_lat" if has_ctx else "attn0_ctx",
    )(*args)


def _tail_kernel(*refs, final):
    x_ref, a_ref, mod_ref, wo_ref, gf_ref, win_ref, wout_ref = refs[:7]
    if final:
        gfin_ref, o_ref = refs[7:]
    else:
        (o_ref,) = refs[7:]
    x1 = x_ref[...] + mod_ref[0, 2:3, :] * _dot(a_ref[...], wo_ref[...])
    h = _modulate(x1, gf_ref[...], mod_ref[0, 3:4, :], mod_ref[0, 4:5, :]).astype(BF16)
    bounds = list(range(0, D_FF, FF_CHUNK)) + [D_FF]
    acc = None
    for lo, hi in zip(bounds[:-1], bounds[1:]):
        a = _dot(h, win_ref[0, :, lo:hi])
        b = _dot(h, win_ref[0, :, D_FF + lo:D_FF + hi])
        act = ((a * jax.nn.sigmoid(a)) * b).astype(BF16)
        part = _dot(act, wout_ref[0, lo:hi, :])
        acc = part if acc is None else acc + part
    x2 = x1 + mod_ref[0, 5:6, :] * acc
    if final:
        x2 = _rms(x2) * gfin_ref[...]
    o_ref[...] = x2


def _tail(x, a, mod, mod_row, wo, gf, win, wout, layer, gfin, name):
    n = x.shape[0]
    tm = TM_TAIL
    final = gfin is not None
    full = lambda shape: pl.BlockSpec(shape, lambda i: (0,) * len(shape))
    resident = lambda shape: pl.BlockSpec((1,) + shape[1:], lambda i: (layer, 0, 0),
                                          pipeline_mode=pl.Buffered(1))
    in_specs = [
        pl.BlockSpec((tm, D_MODEL), lambda i: (i, 0)),
        pl.BlockSpec((tm, D_MODEL), lambda i: (i, 0)),
        pl.BlockSpec((1, N_MOD, D_MODEL), lambda i: (mod_row(i), 0, 0)),
        full(wo.shape), full((1, D_MODEL)), resident(win.shape), resident(wout.shape),
    ]
    args = [x, a, mod, wo, gf, win, wout]
    if final:
        in_specs.append(full((1, D_MODEL)))
        args.append(gfin)
    return pl.pallas_call(
        functools.partial(_tail_kernel, final=final),
        grid=(n // tm,),
        in_specs=in_specs,
        out_specs=pl.BlockSpec((tm, D_MODEL), lambda i: (i, 0)),
        out_shape=jax.ShapeDtypeStruct((n, D_MODEL), F32),
        compiler_params=_params(("arbitrary",)),
        name=name,
    )(*args)


def _split3(x):
    hi = x.astype(BF16)
    r1 = x - hi.astype(F32)
    mid = r1.astype(BF16)
    lo = (r1 - mid.astype(F32)).astype(BF16)
    return hi, mid, lo


def _proj1_kernel(x_ref, mod_ref, g_ref, w1_ref, wg_ref, bg_ref, q_ref, k_ref, v_ref, o_ref, gc_ref, gr_ref):
    tm = x_ref.shape[0]
    L = MLSTM_L
    hk = H_C * DK_C
    hv = H_C * DV_C
    h = _modulate(x_ref[...], g_ref[...], mod_ref[0, 0:1, :], mod_ref[0, 1:2, :]).astype(BF16)
    gates = _dot(h, wg_ref[...]) + bg_ref[...]
    proj = _dot(h, w1_ref[...])
    for blk in range(hk // LANES):
        sl = slice(blk * LANES, (blk + 1) * LANES)
        q_ref[sl, :] = jnp.transpose(proj[:, sl]).astype(BF16)
    for blk in range(hv // LANES):
        sl = slice(blk * LANES, (blk + 1) * LANES)
        v_ref[sl, :] = jnp.transpose(proj[:, 2 * hk + blk * LANES:2 * hk + (blk + 1) * LANES]).astype(BF16)
    k_ref[...] = proj[:, hk:2 * hk] * (DK_C ** -0.5)
    o_ref[...] = proj[:, 2 * hk + hv:2 * hk + 2 * hv]
    lf = jnp.minimum(gates, 0.0) - jnp.log1p(jnp.exp(-jnp.abs(gates)))
    r = lax.broadcasted_iota(jnp.int32, (L, L), 0)
    c = lax.broadcasted_iota(jnp.int32, (L, L), 1)
    pre = jnp.where(c <= r, 1.0, 0.0).astype(BF16)
    suf = jnp.where(c >= r, 1.0, 0.0).astype(BF16)
    parts = jnp.concatenate(_split3(lf), axis=1)
    fold = lambda t: t[:, 0:LANES] + t[:, LANES:2 * LANES] + t[:, 2 * LANES:3 * LANES]
    chunks = [parts[ck * L:(ck + 1) * L] for ck in range(tm // L)]
    lane = lax.broadcasted_iota(jnp.int32, (tm, LANES), 1)
    b_sum = jnp.where((lane % 4) >= 2,
                      jnp.concatenate([fold(_dot(suf, p)) for p in chunks], axis=0),
                      jnp.concatenate([fold(_dot(pre, p)) for p in chunks], axis=0))
    b = pltpu.roll(b_sum, LANES - N_GATE, 1)
    u = gates - b
    pos = lax.broadcasted_iota(jnp.int32, (tm, LANES), 0) % L
    bwd = (lane % 4) >= 2
    cm = u
    step = 1
    while step < L:
        below = jnp.where(pos >= step, pltpu.roll(cm, step, 0), -jnp.inf)
        above = jnp.where(pos < L - step, pltpu.roll(cm, tm - step, 0), -jnp.inf)
        cm = jnp.maximum(cm, jnp.where(bwd, above, below))
        step *= 2
    low = lane < N_GATE
    packed = (jnp.where(low, b, 0.0) + pltpu.roll(jnp.where(low, cm, 0.0), N_GATE, 1)
              + pltpu.roll(jnp.where(low, u, 0.0), 2 * N_GATE, 1))
    gc_ref[...] = packed
    gr_ref[...] = jnp.transpose(packed)[0:3 * N_GATE]


def _proj1(x, mod, mod_row, g, w1, wg, bg):
    n = x.shape[0]
    tm = TM_PROJ1
    assert tm % MLSTM_L == 0
    full = lambda shape: pl.BlockSpec(shape, lambda i: (0,) * len(shape))
    row = lambda w: pl.BlockSpec((tm, w), lambda i: (i, 0))
    col = lambda w: pl.BlockSpec((w, tm), lambda i: (0, i))
    hk, hv = H_C * DK_C, H_C * DV_C
    return pl.pallas_call(
        _proj1_kernel,
        grid=(n // tm,),
        in_specs=[row(D_MODEL), pl.BlockSpec((1, N_MOD, D_MODEL), lambda i: (mod_row(i), 0, 0)),
                  full((1, D_MODEL)), full((D_MODEL, 2 * hk + 2 * hv)), full(wg.shape), full((1, LANES))],
        out_specs=[col(hk), row(hk), col(hv), row(hv), row(LANES),
                   pl.BlockSpec((3 * N_GATE, tm), lambda i: (0, i))],
        out_shape=[jax.ShapeDtypeStruct((hk, n), BF16), jax.ShapeDtypeStruct((n, hk), F32),
                   jax.ShapeDtypeStruct((hv, n), BF16), jax.ShapeDtypeStruct((n, hv), F32),
                   jax.ShapeDtypeStruct((n, LANES), F32), jax.ShapeDtypeStruct((3 * N_GATE, n), F32)],
        compiler_params=_params(("arbitrary",)),
        name="proj1",
    )(x, mod, g, w1, wg, bg)


def _chain(j, d, hh):
    return (2 * j + d) * 2 + hh


def _mlstm_kernel(*refs, has_state, emit_state):
    q_ref, k_ref, v_ref, o_ref, gc_ref, gr_ref, gn_ref = refs[:7]
    refs = refs[7:]
    if has_state:
        c0_ref, n0_ref, m0_ref = refs[:3]
        refs = refs[3:]
    hs_ref = refs[0]
    refs = refs[1:]
    if emit_state:
        cf_ref, nst_ref, mst_ref = refs[:3]
        refs = refs[3:]
    cx_s, m_s, h_s = refs

    L = MLSTM_L
    seq = k_ref.shape[0]
    nc = seq // L
    npair = H_C // 2
    lane = lax.broadcasted_iota(jnp.int32, (1, LANES), 1)
    head_mask = [lane < DK_C, lane >= DK_C]
    ri = lax.broadcasted_iota(jnp.int32, (L, L), 0)
    ci = lax.broadcasted_iota(jnp.int32, (L, L), 1)
    causal = [ri <= ci, ri >= ci]
    ones_blk = jnp.ones((N_ROWS_N, L), BF16)
    chains = [(j, d, hh) for d in range(2) for j in range(npair) for hh in range(2)]

    h_s[...] = jnp.zeros_like(h_s)
    for j, d, hh in chains:
        ch = _chain(j, d, hh)
        if has_state:
            zpad = jnp.zeros((DK_C, DV_C), F32)
            c0 = c0_ref[0, d, 2 * j + hh]
            c0 = jnp.concatenate([c0, zpad] if hh == 0 else [zpad, c0], axis=0)
            cx_s[ch, 0:DV_C, :] = jnp.transpose(c0)
            n_row = jnp.where(head_mask[hh], n0_ref[0, d, j], 0.0)
            cx_s[ch, DV_C:DV_C + N_ROWS_N, :] = jnp.broadcast_to(n_row, (N_ROWS_N, LANES))
            m_s[ch] = m0_ref[0, d, 2 * j + hh]
        else:
            cx_s[ch] = jnp.zeros((DV_C + N_ROWS_N, LANES), F32)
            m_s[ch] = jnp.zeros((1, LANES), F32)

    def chunk_step(i, carry):
        sl = [pl.ds(pl.multiple_of(i * L, L), L), pl.ds(pl.multiple_of((nc - 1 - i) * L, L), L)]
        gcol = [gc_ref[sl[d], :] for d in range(2)]
        grow = [gr_ref[:, sl[d]] for d in range(2)]
        kpair = {(d, j): k_ref[sl[d], j * LANES:(j + 1) * LANES] for d in range(2) for j in range(npair)}
        kbf = {key: kk.astype(BF16) for key, kk in kpair.items()}
        zq = jnp.zeros((DK_C, L), BF16)
        qt = {}
        for d in range(2):
            for j in range(npair):
                qt[(j, d, 0)] = jnp.concatenate([q_ref[j * LANES:j * LANES + DK_C, sl[d]], zq], axis=0)
                qt[(j, d, 1)] = jnp.concatenate([zq, q_ref[j * LANES + DK_C:(j + 1) * LANES, sl[d]]], axis=0)
        st, u_bc, vt, row = {}, {}, {}, {}
        for d in range(2):
            for j in range(npair):
                pair_scores = _dot(kbf[(d, j)], jnp.concatenate([qt[(j, d, 0)], qt[(j, d, 1)]], axis=1))
                st[(j, d, 0)], st[(j, d, 1)] = pair_scores[:, 0:L], pair_scores[:, L:2 * L]
        for j, d, hh in chains:
            key = (j, d, hh)
            idx = 4 * j + 2 * d + hh
            edge = L - 1 if d == 0 else 0
            b_row = grow[d][idx:idx + 1, :]
            cm_row = grow[d][N_GATE + idx:N_GATE + idx + 1, :]
            u_col = gcol[d][:, 2 * N_GATE + idx:2 * N_GATE + idx + 1]
            m_prev = m_s[_chain(j, d, hh)][:, 0:1]
            row[key] = (b_row, cm_row, b_row[:, edge:edge + 1], cm_row[:, edge:edge + 1], m_prev)
            u_bc[key] = jnp.broadcast_to(u_col, (L, L))
            vt[key] = v_ref[(2 * j + hh) * DV_C:(2 * j + hh + 1) * DV_C, sl[d]]
        for j, d, hh in chains:
            key = (j, d, hh)
            ch = _chain(j, d, hh)
            b_row, cm_row, g_tot, cm_last, m_prev = row[key]
            head = slice((2 * j + hh) * DV_C, (2 * j + hh + 1) * DV_C)
            cx = cx_s[ch]
            big_m = jnp.maximum(m_prev, cm_row)
            s = st[key] * jnp.exp(jnp.where(causal[d], u_bc[key] - big_m, -jnp.inf))
            inter = jnp.exp(m_prev - big_m)
            vx = jnp.concatenate([vt[key], ones_blk], axis=0)
            lhs = jnp.concatenate([vx, cx.astype(BF16)], axis=1)
            rhs = jnp.concatenate([s.astype(BF16), (inter * qt[key].astype(F32)).astype(BF16)], axis=0)
            res = _dot(lhs, rhs)
            den = res[DV_C:DV_C + 1, :]
            hval = res[0:DV_C, :] * (1.0 / jnp.maximum(jnp.abs(den), jnp.exp(-(b_row + big_m))))
            h_s[head, sl[d]] = h_s[head, sl[d]] + hval

            m_top = jnp.maximum(m_prev, cm_last)
            kw = kpair[(d, j)] * jnp.exp(u_bc[key] - m_top)
            cx_s[ch] = jnp.exp(m_prev - m_top) * cx + _dot(vx, kw.astype(BF16))
            m_s[ch] = jnp.broadcast_to(g_tot + m_top, (1, LANES))
        return carry

    lax.fori_loop(0, nc, chunk_step, 0)

    gn_col = jnp.broadcast_to(gn_ref[...], (DV_C, LANES))
    gn_col = jnp.concatenate([gn_col] * (seq // LANES), axis=1)
    for hd in range(H_C):
        sl = slice(hd * DV_C, (hd + 1) * DV_C)
        ht = h_s[sl, :]
        ms = jnp.mean(ht * ht, axis=0, keepdims=True)
        y = jnp.transpose((ht * lax.rsqrt(ms + RMS_EPS)) * gn_col) * jax.nn.sigmoid(o_ref[:, sl])
        hs_ref[:, sl] = y.astype(BF16)
    if emit_state:
        for j, d, hh in chains:
            ch = _chain(j, d, hh)
            cf_ref[0, 0, d, 2 * j + hh] = jnp.transpose(cx_s[ch, 0:DV_C, :])[hh * DK_C:(hh + 1) * DK_C, :]
            mst_ref[0, d * H_C + 2 * j + hh:d * H_C + 2 * j + hh + 1, :] = m_s[ch]
        for j in range(npair):
            for d in range(2):
                nst_ref[0, d * npair + j:d * npair + j + 1, :] = jnp.where(
                    head_mask[0], cx_s[_chain(j, d, 0), DV_C:DV_C + 1, :], cx_s[_chain(j, d, 1), DV_C:DV_C + 1, :])


def _mlstm(q, k, v, o, gc, gr, gn, state, seq, emit_state):
    n = k.shape[0]
    nb = n // seq
    npair = H_C // 2
    has_state = state is not None
    hk, hv = H_C * DK_C, H_C * DV_C
    in_specs = [
        pl.BlockSpec((hk, seq), lambda b: (0, b)),
        pl.BlockSpec((seq, hk), lambda b: (b, 0)),
        pl.BlockSpec((hv, seq), lambda b: (0, b)),
        pl.BlockSpec((seq, hv), lambda b: (b, 0)),
        pl.BlockSpec((seq, LANES), lambda b: (b, 0)),
        pl.BlockSpec((3 * N_GATE, seq), lambda b: (0, b)),
        pl.BlockSpec((DV_C, 1), lambda b: (0, 0)),
    ]
    args = [q, k, v, o, gc, gr, gn]
    if has_state:
        c0, n0, m0 = state
        in_specs += [pl.BlockSpec((1, 2, H_C, DK_C, DV_C), lambda b: (b, 0, 0, 0, 0)),
                     pl.BlockSpec((1, 2, npair, 1, LANES), lambda b: (b, 0, 0, 0, 0)),
                     pl.BlockSpec((1, 2, H_C, 1, LANES), lambda b: (b, 0, 0, 0, 0))]
        args += [c0, n0, m0]
    out_specs = [pl.BlockSpec((seq, hv), lambda b: (b, 0))]
    out_shape = [jax.ShapeDtypeStruct((n, hv), BF16)]
    if emit_state:
        out_specs += [pl.BlockSpec((1, 1, 2, H_C, DK_C, DV_C), lambda b: (b, 0, 0, 0, 0, 0)),
                      pl.BlockSpec((1, 2 * npair, LANES), lambda b: (b, 0, 0)),
                      pl.BlockSpec((1, 2 * H_C, LANES), lambda b: (b, 0, 0))]
        out_shape += [jax.ShapeDtypeStruct((nb, 1, 2, H_C, DK_C, DV_C), F32),
                      jax.ShapeDtypeStruct((nb, 2 * npair, LANES), F32),
                      jax.ShapeDtypeStruct((nb, 2 * H_C, LANES), F32)]
    n_chain = 2 * H_C
    return pl.pallas_call(
        functools.partial(_mlstm_kernel, has_state=has_state, emit_state=emit_state),
        grid=(nb,),
        in_specs=in_specs, out_specs=out_specs, out_shape=out_shape,
        scratch_shapes=[pltpu.VMEM((n_chain, DV_C + N_ROWS_N, LANES), F32), pltpu.VMEM((n_chain, 1, LANES), F32),
                        pltpu.VMEM((hv, seq), F32)],
        compiler_params=_params(("arbitrary",)),
        name="mlstm_lat" if has_state else "mlstm_ctx",
    )(*args)


def _rope_tables(n_tok):
    t = np.arange(n_tok)
    rows = (t // GRID_W).astype(np.float64)
    cols = (t % GRID_W).astype(np.float64)

    def axis_tabs(width, lane0):
        half = width // 2
        quarter = half // 2
        freqs = np.power(ROPE_BASE, -np.arange(quarter, dtype=np.float64) / quarter)
        c = np.ones((n_tok, LANES))
        sa = np.zeros((n_tok, LANES))
        sb = np.zeros((n_tok, LANES))
        for g, pos in enumerate((rows, cols)):
            ang = pos[:, None] * freqs[None, :]
            a0 = lane0 + g * half
            c[:, a0:a0 + quarter] = np.cos(ang)
            c[:, a0 + quarter:a0 + half] = np.cos(ang)
            sa[:, a0:a0 + quarter] = -np.sin(ang)
            sb[:, a0 + quarter:a0 + half] = np.sin(ang)
        return c, sa, sb

    cm, sam, sbm = axis_tabs(QK_ROPE, ROPE_LANE0)
    c0, sa0, sb0 = axis_tabs(DIFF_HD, 0)
    c1, sa1, sb1 = axis_tabs(DIFF_HD, DIFF_HD)
    cd = np.where(np.arange(LANES)[None, :] < DIFF_HD, c0, c1)
    return tuple(jnp.asarray(a, F32) for a in (cm, sam, sbm, cd, sa0 + sa1, sb0 + sb1))


def _prep_even(w_in_ab, w_uq, w_ukv):
    c2 = Q_LORA + KV_LORA
    w_ab = w_in_ab.astype(BF16)
    wd = w_ab[:, c2 + QK_ROPE:]
    wq = jnp.pad(w_uq.reshape(Q_LORA, H_A, QK_NOPE + QK_ROPE),
                 ((0, 0), (0, 0), (0, HEAD_PAD - QK_NOPE - QK_ROPE))).reshape(Q_LORA, H_A * HEAD_PAD)
    kvw = w_ukv.reshape(KV_LORA, H_A, QK_NOPE + V_HD_A)
    kpad = jnp.pad(kvw[..., :QK_NOPE], ((0, 0), (0, 0), (0, HEAD_PAD - QK_NOPE)))
    vw = kvw[..., QK_NOPE:]
    zv = jnp.zeros_like(vw)
    odd = (jnp.arange(H_A) % 2 == 1)[None, :, None]
    vpad = jnp.where(odd, jnp.concatenate([zv, vw], -1), jnp.concatenate([vw, zv], -1))
    wukv = jnp.concatenate([kpad.reshape(KV_LORA, -1), vpad.reshape(KV_LORA, -1)], axis=1)
    return (w_ab, wd), wq.astype(BF16), wukv.astype(BF16)


def _gate_order(g):
    lead = g.shape[:-1]
    g = g.reshape(lead + (2, 2, H_C // 2, 2))
    perm = tuple(range(len(lead))) + tuple(len(lead) + a for a in (1, 2, 0, 3))
    return g.transpose(perm).reshape(lead + (4 * H_C,))


def _prep_odd(w_in_c, b_gate_c):
    ng = 4 * H_C
    base = w_in_c.shape[1] - ng
    w1 = w_in_c.astype(BF16)
    wg = jnp.pad(_gate_order(w1[:, base:]), ((0, 0), (0, LANES - ng)))
    bg = jnp.pad(_gate_order(b_gate_c), (0, LANES - ng)).reshape(1, LANES)
    return w1, wg, bg


def kernel(x_prompt, x_sample, cache_mla_ckv, cache_mla_krope, cache_diff_k, cache_diff_v,
           state_mlstm_C, state_mlstm_n, state_mlstm_m, c, c_ctx,
           w_ada, b_ada, g_mix, g_ffn, w_ffn_in, w_ffn_out,
           w_in_ab, g_q_lora, g_kv_lora, w_uq, w_ukv, diff_lambda, g_diff_subln, w_out_ab,
           w_in_c, b_gate_c, g_mlstm, w_out_c, g_final):
    nbp, seq_p, _ = x_prompt.shape
    nbs, seq_s, _ = x_sample.shape
    past = cache_mla_ckv.shape[2]
    assert DEPTH == 2 and 1 + nbs <= COND_ROWS
    assert cache_mla_ckv.shape[1] == 1 and state_mlstm_C.shape[1] == 1

    cond = jnp.concatenate([c_ctx[None], c, jnp.zeros((COND_ROWS - 1 - nbs, D_MODEL), F32)], axis=0)
    mod = _ada(cond, w_ada, b_ada).reshape(DEPTH * COND_ROWS, N_MOD, D_MODEL)

    xp = x_prompt.reshape(nbp * seq_p, D_MODEL)
    xs = x_sample.reshape(nbs * seq_s, D_MODEL)
    row2 = lambda v: v.reshape(1, -1)

    def mod_rows(layer, tm):
        prompt = lambda i: layer * COND_ROWS
        sample = lambda i: layer * COND_ROWS + 1 + i // (seq_s // tm)
        return prompt, sample

    lam_init = 0.8 - 0.6 * math.exp(-0.3 * 0)
    w0, wq, wukv = _prep_even(w_in_ab[0], w_uq[0], w_ukv[0])
    rope_tabs = _rope_tables(seq_s)
    mrp, _ = mod_rows(0, TM_PROJ)
    _, mrs = mod_rows(0, TM_PROJ_ROPE)
    gq, gkv = row2(g_q_lora[0]), row2(g_kv_lora[0])
    outs_p = _proj0(xp, mod, mrp, row2(g_mix[0]), *w0, gq, gkv, wq, wukv, None, seq_p)
    outs_s = _proj0(xs, mod, mrs, row2(g_mix[0]), *w0, gq, gkv, wq, wukv, rope_tabs, seq_s)
    qm_p, km_p, vm_p, dq_p, dk_p, dv_p, ckv_new, kr_new, dk_new, dv_new = outs_p
    qm_s, km_s, vm_s, dq_s, dk_s, dv_s = outs_s
    gsub = row2(g_diff_subln[0])
    krb_c = jnp.pad(cache_mla_krope.reshape(nbs, past, QK_ROPE),
                    ((0, 0), (0, 0), (ROPE_LANE0, LANES - ROPE_LANE0 - QK_ROPE)))
    ctx = (cache_mla_ckv.reshape(nbs, past, KV_LORA), krb_c,
           cache_diff_k.reshape(nbs, H_B, past, 2 * DIFF_HD), cache_diff_v.reshape(nbs, H_B, past, DIFF_VD))
    a_p = _attn0(qm_p, km_p, vm_p, dq_p, dk_p, dv_p, diff_lambda[0], gsub, None, None, seq_p, lam_init)
    a_s = _attn0(qm_s, km_s, vm_s, dq_s, dk_s, dv_s, diff_lambda[0], gsub, ctx, wukv, seq_s, lam_init)
    mtp, mts = mod_rows(0, TM_TAIL)
    wo0 = w_out_ab[0].astype(BF16)
    win, wout = w_ffn_in.astype(BF16), w_ffn_out.astype(BF16)
    xp = _tail(xp, a_p, mod, mtp, wo0, row2(g_ffn[0]), win, wout, 0, None, "tail0_ctx")
    xs = _tail(xs, a_s, mod, mts, wo0, row2(g_ffn[0]), win, wout, 0, None, "tail0_lat")

    w1, wg, bg = _prep_odd(w_in_c[0], b_gate_c[0])
    mrp, mrs = mod_rows(1, TM_PROJ1)
    gn = g_mlstm[0].reshape(DV_C, 1)
    q_p, k_p, v_p, o_p, gc_p, gr_p = _proj1(xp, mod, mrp, row2(g_mix[1]), w1, wg, bg)
    q_s, k_s, v_s, o_s, gc_s, gr_s = _proj1(xs, mod, mrs, row2(g_mix[1]), w1, wg, bg)
    hs_p, c_new, nst, mst = _mlstm(q_p, k_p, v_p, o_p, gc_p, gr_p, gn, None, seq_p, True)
    state = (state_mlstm_C.reshape(nbs, 2, H_C, DK_C, DV_C),
             state_mlstm_n.reshape(nbs, 2, H_C // 2, 1, LANES),
             jnp.broadcast_to(state_mlstm_m.reshape(nbs, 2, H_C, 1, 1), (nbs, 2, H_C, 1, LANES)))
    (hs_s,) = _mlstm(q_s, k_s, v_s, o_s, gc_s, gr_s, gn, state, seq_s, False)
    mtp, mts = mod_rows(1, TM_TAIL)
    wo1 = w_out_c[0].astype(BF16)
    gfin = row2(g_final)
    yp = _tail(xp, hs_p, mod, mtp, wo1, row2(g_ffn[1]), win, wout, 1, gfin, "tail1_ctx")
    ys = _tail(xs, hs_s, mod, mts, wo1, row2(g_ffn[1]), win, wout, 1, gfin, "tail1_lat")

    return (yp.reshape(nbp, seq_p, D_MODEL), ys.reshape(nbs, seq_s, D_MODEL),
            ckv_new.reshape(nbp, 1, seq_p, KV_LORA), kr_new.reshape(nbp, 1, seq_p, QK_ROPE),
            dk_new.reshape(nbp, 1, H_B, seq_p, 2 * DIFF_HD), dv_new.reshape(nbp, 1, H_B, seq_p, DIFF_VD),
            c_new, nst.reshape(nbp, 1, 2, H_C, DK_C), mst[:, :, 0].reshape(nbp, 1, 2, H_C))
```

```python
import functools
import math

import jax
import jax.numpy as jnp
import numpy as np
from jax import lax
from jax.experimental import pallas as pl
from jax.experimental.pallas import tpu as pltpu

F32 = jnp.float32
BF16 = jnp.bfloat16

D_MODEL = 1024
DEPTH = 2
GRID_W = 64
ROPE_BASE = 10000.0
RMS_EPS = 1e-6
H_A = 8
QK_NOPE = 64
QK_ROPE = 32
V_HD_A = 64
Q_LORA = 256
KV_LORA = 128
H_B = 4
DIFF_HD = 64
DIFF_VD = 2 * DIFF_HD
H_C = 8
DK_C = 64
DV_C = D_MODEL // H_C
D_FF = -(-8 * D_MODEL // (3 * 256)) * 256

LANES = 128
HEAD_PAD = LANES
ROPE_LANE0 = QK_NOPE
W_MLA = H_A * HEAD_PAD
W_DIFF = H_B * LANES
N_MOD = 6
COND_ROWS = 16
N_GATE = 2 * H_C
N_ROWS_N = 16

TM_PROJ = 1024
TM_PROJ_ROPE = 512
TM_PROJ1 = 1024
TQ_ATTN = 512
TM_TAIL = 1024
FF_CHUNK = 256
MLSTM_L = 128
MLSTM_UNROLL = 4
ADA_TN = 1024
VMEM_LIMIT = 56 * 1024 * 1024

assert D_FF % LANES == 0 and FF_CHUNK % LANES == 0


def _dot(a, b):
    return jnp.dot(a, b, preferred_element_type=F32)


def _dot_nt(a, b):
    return lax.dot_general(a, b, (((1,), (1,)), ((), ())), preferred_element_type=F32)


def _rms(x):
    return x * lax.rsqrt(jnp.mean(x * x, axis=-1, keepdims=True) + RMS_EPS)


def _modulate(x, g, shift, scale):
    return _rms(x) * (g * (1.0 + scale)) + shift


def _params(semantics):
    return pltpu.CompilerParams(dimension_semantics=semantics, vmem_limit_bytes=VMEM_LIMIT)


def _ada_kernel(cond_ref, w_ref, b_ref, o_ref):
    c = cond_ref[...]
    s = (c * jax.nn.sigmoid(c)).astype(BF16)
    o_ref[0] = _dot(s, w_ref[0].astype(BF16)) + b_ref[0]


def _ada(cond, w_ada, b_ada):
    n_out = w_ada.shape[-1]
    return pl.pallas_call(
        _ada_kernel,
        grid=(DEPTH, n_out // ADA_TN),
        in_specs=[
            pl.BlockSpec((COND_ROWS, D_MODEL), lambda l, n: (0, 0)),
            pl.BlockSpec((1, D_MODEL, ADA_TN), lambda l, n: (l, 0, n)),
            pl.BlockSpec((1, 1, ADA_TN), lambda l, n: (l, 0, n)),
        ],
        out_specs=pl.BlockSpec((1, COND_ROWS, ADA_TN), lambda l, n: (l, 0, n)),
        out_shape=jax.ShapeDtypeStruct((DEPTH, COND_ROWS, n_out), F32),
        compiler_params=_params(("arbitrary", "arbitrary")),
        name="ada",
    )(cond, w_ada, b_ada.reshape(DEPTH, 1, n_out))


def _rope(x, c, sa, sb, off):
    return x * c + pltpu.roll(x, LANES - off, 1) * sa + pltpu.roll(x, off, 1) * sb


def _proj0_kernel(*refs, rope, emit_cache):
    x_ref, mod_ref, g_ref, wa_ref, wd_ref, gq_ref, gkv_ref, wq_ref, wukv_ref = refs[:9]
    refs = refs[9:]
    if rope:
        cm_ref, sam_ref, sbm_ref, cd_ref, sad_ref, sbd_ref = refs[:6]
        refs = refs[6:]
    qm_ref, km_ref, vm_ref, dq_ref, dk_ref, dv_ref = refs[:6]
    refs = refs[6:]
    if emit_cache:
        ckvf_ref, krf_ref, dkf_ref, dvf_ref = refs

    h = _modulate(x_ref[...], g_ref[...], mod_ref[0, 0:1, :], mod_ref[0, 1:2, :]).astype(BF16)
    pa = _dot(h, wa_ref[...])
    pd = _dot(h, wd_ref[...])
    lane = lax.broadcasted_iota(jnp.int32, (pa.shape[0], LANES), 1)
    krb = jnp.where((lane >= ROPE_LANE0) & (lane < ROPE_LANE0 + QK_ROPE),
                    pltpu.roll(pa[:, Q_LORA + KV_LORA:Q_LORA + KV_LORA + LANES], ROPE_LANE0, 1), 0.0)
    cq = pa[:, 0:Q_LORA]
    ckv = _rms(pa[:, Q_LORA:Q_LORA + KV_LORA]) * gkv_ref[...]
    qa = _dot((_rms(cq) * gq_ref[...]).astype(BF16), wq_ref[...])
    kv = _dot(ckv.astype(BF16), wukv_ref[...])
    if emit_cache:
        ckvf_ref[...] = ckv
        krf_ref[...] = krb[:, ROPE_LANE0:ROPE_LANE0 + QK_ROPE]
    if rope:
        cm, sam, sbm = cm_ref[...], sam_ref[...], sbm_ref[...]
        cd, sad, sbd = cd_ref[...], sad_ref[...], sbd_ref[...]
        krb = _rope(krb, cm, sam, sbm, QK_ROPE // 4)
    for hd in range(H_A):
        sl = slice(hd * HEAD_PAD, (hd + 1) * HEAD_PAD)
        qh = qa[:, sl]
        if rope:
            qh = _rope(qh, cm, sam, sbm, QK_ROPE // 4)
        qm_ref[:, sl] = qh.astype(BF16)
        km_ref[:, sl] = (kv[:, sl] + krb).astype(BF16)
    vm_ref[...] = kv[:, H_A * HEAD_PAD:].astype(BF16)
    for hd in range(H_B):
        sl = slice(hd * LANES, (hd + 1) * LANES)
        dq = pd[:, hd * LANES:(hd + 1) * LANES]
        dk = pd[:, (H_B + hd) * LANES:(H_B + hd + 1) * LANES]
        dv = pd[:, (2 * H_B + hd) * LANES:(2 * H_B + hd + 1) * LANES]
        if emit_cache:
            seq = dkf_ref.shape[2]
            for bi in range(dkf_ref.shape[0]):
                dkf_ref[bi, hd] = dk[bi * seq:(bi + 1) * seq]
                dvf_ref[bi, hd] = dv[bi * seq:(bi + 1) * seq]
        if rope:
            dq = _rope(dq, cd, sad, sbd, DIFF_HD // 4)
            dk = _rope(dk, cd, sad, sbd, DIFF_HD // 4)
        dq_ref[:, sl] = (dq * (DIFF_HD ** -0.5)).astype(BF16)
        dk_ref[:, sl] = dk.astype(BF16)
        dv_ref[:, sl] = dv.astype(BF16)


def _proj0(x, mod, mod_row, g, w_ab, wd, gq, gkv, wq, wukv, rope_tabs, seq):
    n = x.shape[0]
    rope = rope_tabs is not None
    emit_cache = not rope
    tm = TM_PROJ_ROPE if rope else TM_PROJ
    tiles_per_seq = max(seq // tm, 1)
    full = lambda shape: pl.BlockSpec(shape, lambda i: (0,) * len(shape))
    in_specs = [
        pl.BlockSpec((tm, D_MODEL), lambda i: (i, 0)),
        pl.BlockSpec((1, N_MOD, D_MODEL), lambda i: (mod_row(i), 0, 0)),
        full((1, D_MODEL)), full((D_MODEL, Q_LORA + KV_LORA + LANES)), full(wd.shape),
        full((1, Q_LORA)), full((1, KV_LORA)), full(wq.shape), full(wukv.shape),
    ]
    args = [x, mod, g, w_ab, wd, gq, gkv, wq, wukv]
    if rope:
        in_specs += [pl.BlockSpec((tm, LANES), lambda i: (i % tiles_per_seq, 0))] * 6
        args += list(rope_tabs)
    row = lambda w: pl.BlockSpec((tm, w), lambda i: (i, 0))
    widths = (W_MLA, W_MLA, W_MLA, W_DIFF, W_DIFF, W_DIFF)
    out_specs = [row(w) for w in widths]
    out_shape = [jax.ShapeDtypeStruct((n, w), BF16) for w in widths]
    if emit_cache:
        assert tm % seq == 0
        cache = pl.BlockSpec((tm // seq, H_B, seq, LANES), lambda i: (i, 0, 0, 0))
        out_specs += [row(KV_LORA), row(QK_ROPE), cache, cache]
        out_shape += [jax.ShapeDtypeStruct((n, KV_LORA), F32), jax.ShapeDtypeStruct((n, QK_ROPE), F32),
                      jax.ShapeDtypeStruct((n // seq, H_B, seq, LANES), F32),
                      jax.ShapeDtypeStruct((n // seq, H_B, seq, LANES), F32)]
    return pl.pallas_call(
        functools.partial(_proj0_kernel, rope=rope, emit_cache=emit_cache),
        grid=(n // tm,),
        in_specs=in_specs, out_specs=out_specs, out_shape=out_shape,
        compiler_params=_params(("arbitrary",)),
        name="proj0_rope" if rope else "proj0_ctx",
    )(*args)


def _attn_kernel(*refs, has_ctx, lam_init):
    q_ref, dq_ref, k_ref, v_ref, dk_ref, dv_ref, lamv_ref, gsub_ref = refs[:8]
    refs = refs[8:]
    if has_ctx:
        ckvc_ref, krc_ref, dkc_ref, dvc_ref, wukv_ref = refs[:5]
        o_ref, vx_s, dvx_s, kx_s, dkx_s = refs[5:]
    else:
        o_ref, vx_s, dvx_s = refs
    seq = k_ref.shape[0]
    ktot = vx_s.shape[0]

    @pl.when(pl.program_id(1) == 0)
    def _():
        if has_ctx:
            kx_s[0:seq, :] = k_ref[...]
            dkx_s[0:seq, :] = dk_ref[...]
        ones_own = jnp.ones((seq, LANES), BF16)
        for hd in range(H_A):
            vx_s[0:seq, 2 * hd * LANES:(2 * hd + 1) * LANES] = v_ref[:, hd * LANES:(hd + 1) * LANES]
            vx_s[0:seq, (2 * hd + 1) * LANES:(2 * hd + 2) * LANES] = ones_own
        for hd in range(H_B):
            dvx_s[0:seq, 2 * hd * LANES:(2 * hd + 1) * LANES] = dv_ref[:, hd * LANES:(hd + 1) * LANES]
            dvx_s[0:seq, (2 * hd + 1) * LANES:(2 * hd + 2) * LANES] = ones_own
        if has_ctx:
            kv = _dot(ckvc_ref[0].astype(BF16), wukv_ref[...])
            krb = krc_ref[0]
            ones_ctx = jnp.ones((ktot - seq, LANES), BF16)
            for hd in range(H_A):
                sl = slice(hd * HEAD_PAD, (hd + 1) * HEAD_PAD)
                kx_s[seq:ktot, sl] = (kv[:, sl] + krb).astype(BF16)
                vx_s[seq:ktot, 2 * hd * LANES:(2 * hd + 1) * LANES] = (
                    kv[:, (H_A + hd) * HEAD_PAD:(H_A + hd + 1) * HEAD_PAD].astype(BF16))
                vx_s[seq:ktot, (2 * hd + 1) * LANES:(2 * hd + 2) * LANES] = ones_ctx
            for hd in range(H_B):
                sl = slice(hd * LANES, (hd + 1) * LANES)
                dkx_s[seq:ktot, sl] = dkc_ref[0, hd].astype(BF16)
                dvx_s[seq:ktot, 2 * hd * LANES:(2 * hd + 1) * LANES] = dvc_ref[0, hd].astype(BF16)
                dvx_s[seq:ktot, (2 * hd + 1) * LANES:(2 * hd + 2) * LANES] = ones_ctx

    log2e = 1.0 / math.log(2.0)
    tq = dq_ref.shape[0]
    lo = lax.broadcasted_iota(jnp.int32, (tq, LANES), 1) < DIFF_HD

    keys_a, keys_b = (kx_s, dkx_s) if has_ctx else (k_ref, dk_ref)
    jobs = []
    for hd in range(H_A):
        sl = slice(hd * HEAD_PAD, (hd + 1) * HEAD_PAD)
        jobs.append((lambda sl=sl: q_ref[:, sl], (keys_a, sl), (vx_s, hd), (QK_NOPE + QK_ROPE) ** -0.5 * log2e))
    for hd in range(H_B):
        sl = slice(hd * LANES, (hd + 1) * LANES)
        for part in range(2):
            def qfn(sl=sl, part=part):
                dq = dq_ref[:, sl].astype(F32)
                return (jnp.where(lo, dq, 0.0) if part == 0 else jnp.where(lo, 0.0, dq)).astype(BF16)
            jobs.append((qfn, (keys_b, sl), (dvx_s, hd), log2e))

    def scores(job):
        qfn, (kref, sl), _, _ = job
        return _dot_nt(qfn(), kref[:, sl])

    def finish(s, job):
        _, _, (vref, hd), c = job
        m = jnp.max(s, axis=-1, keepdims=True)
        e = jnp.exp2((s - m) * c).astype(BF16)
        res = _dot(e, vref[:, 2 * hd * LANES:(2 * hd + 2) * LANES])
        return res[:, 0:LANES] * (1.0 / res[:, LANES:2 * LANES])

    outs = []
    s_next = scores(jobs[0])
    for i, job in enumerate(jobs):
        s_cur = s_next
        if i + 1 < len(jobs):
            s_next = scores(jobs[i + 1])
        outs.append(finish(s_cur, job))

    for j in range(H_A // 2):
        o_ref[:, j * LANES:(j + 1) * LANES] = (outs[2 * j] + outs[2 * j + 1]).astype(BF16)
    lv = lamv_ref[...]
    lam = (jnp.exp(jnp.sum(lv[0:1] * lv[1:2], axis=-1, keepdims=True))
           - jnp.exp(jnp.sum(lv[2:3] * lv[3:4], axis=-1, keepdims=True)) + lam_init)
    for hd in range(H_B):
        acc = outs[H_A + 2 * hd] - lam * outs[H_A + 2 * hd + 1]
        ob = (_rms(acc) * gsub_ref[...]) * (1.0 - lam_init)
        o_ref[:, H_A * V_HD_A + hd * LANES:H_A * V_HD_A + (hd + 1) * LANES] = ob.astype(BF16)


def _attn0(qm, km, vm, dq, dk, dv, lamv, gsub, ctx, wukv, seq, lam_init):
    n = qm.shape[0]
    nb = n // seq
    tq = min(TQ_ATTN, seq)
    has_ctx = ctx is not None
    full = lambda shape: pl.BlockSpec(shape, lambda b, t: (0,) * len(shape))
    qrow = lambda w: pl.BlockSpec((tq, w), lambda b, t: (b * (seq // tq) + t, 0))
    krow = lambda w: pl.BlockSpec((seq, w), lambda b, t: (b, 0))
    in_specs = [qrow(W_MLA), qrow(W_DIFF), krow(W_MLA), krow(W_MLA), krow(W_DIFF), krow(W_DIFF),
                full((4, DIFF_HD)), full((1, DIFF_VD))]
    args = [qm, dq, km, vm, dk, dv, lamv, gsub]
    past = 0
    if has_ctx:
        ckv_c, krb_c, dk_c, dv_c = ctx
        past = ckv_c.shape[1]
        in_specs += [pl.BlockSpec((1, past, LANES), lambda b, t: (b, 0, 0)),
                     pl.BlockSpec((1, past, LANES), lambda b, t: (b, 0, 0)),
                     pl.BlockSpec((1, H_B, past, LANES), lambda b, t: (b, 0, 0, 0)),
                     pl.BlockSpec((1, H_B, past, LANES), lambda b, t: (b, 0, 0, 0)),
                     full(wukv.shape)]
        args += [ckv_c, krb_c, dk_c, dv_c, wukv]
    ktot = seq + past
    scratch = [pltpu.VMEM((ktot, 2 * W_MLA), BF16), pltpu.VMEM((ktot, 2 * W_DIFF), BF16)]
    if has_ctx:
        scratch += [pltpu.VMEM((ktot, W_MLA), BF16), pltpu.VMEM((ktot, W_DIFF), BF16)]
    return pl.pallas_call(
        functools.partial(_attn_kernel, has_ctx=has_ctx, lam_init=lam_init),
        grid=(nb, seq // tq),
        in_specs=in_specs,
        out_specs=pl.BlockSpec((tq, D_MODEL), lambda b, t: (b * (seq // tq) + t, 0)),
        out_shape=jax.ShapeDtypeStruct((n, D_MODEL), BF16),
        scratch_shapes=scratch,
        compiler_params=_params(("arbitrary", "arbitrary")),
        name="attn0_lat" if has_ctx else "attn0_ctx",
    )(*args)


def _tail_kernel(*refs, final):
    x_ref, a_ref, mod_ref, wo_ref, gf_ref, win_ref, wout_ref = refs[:7]
    if final:
        gfin_ref, o_ref = refs[7:]
    else:
        (o_ref,) = refs[7:]
    x1 = x_ref[...] + mod_ref[0, 2:3, :] * _dot(a_ref[...], wo_ref[...])
    h = _modulate(x1, gf_ref[...], mod_ref[0, 3:4, :], mod_ref[0, 4:5, :]).astype(BF16)
    bounds = list(range(0, D_FF, FF_CHUNK)) + [D_FF]
    acc = None
    for lo, hi in zip(bounds[:-1], bounds[1:]):
        a = _dot(h, win_ref[0, :, lo:hi])
        b = _dot(h, win_ref[0, :, D_FF + lo:D_FF + hi])
        act = ((a * jax.nn.sigmoid(a)) * b).astype(BF16)
        part = _dot(act, wout_ref[0, lo:hi, :])
        acc = part if acc is None else acc + part
    x2 = x1 + mod_ref[0, 5:6, :] * acc
    if final:
        x2 = _rms(x2) * gfin_ref[...]
    o_ref[...] = x2


def _tail(x, a, mod, mod_row, wo, gf, win, wout, layer, gfin, name):
    n = x.shape[0]
    tm = TM_TAIL
    final = gfin is not None
    full = lambda shape: pl.BlockSpec(shape, lambda i: (0,) * len(shape))
    resident = lambda shape: pl.BlockSpec((1,) + shape[1:], lambda i: (layer, 0, 0),
                                          pipeline_mode=pl.Buffered(1))
    in_specs = [
        pl.BlockSpec((tm, D_MODEL), lambda i: (i, 0)),
        pl.BlockSpec((tm, D_MODEL), lambda i: (i, 0)),
        pl.BlockSpec((1, N_MOD, D_MODEL), lambda i: (mod_row(i), 0, 0)),
        full(wo.shape), full((1, D_MODEL)), resident(win.shape), resident(wout.shape),
    ]
    args = [x, a, mod, wo, gf, win, wout]
    if final:
        in_specs.append(full((1, D_MODEL)))
        args.append(gfin)
    return pl.pallas_call(
        functools.partial(_tail_kernel, final=final),
        grid=(n // tm,),
        in_specs=in_specs,
        out_specs=pl.BlockSpec((tm, D_MODEL), lambda i: (i, 0)),
        out_shape=jax.ShapeDtypeStruct((n, D_MODEL), F32),
        compiler_params=_params(("arbitrary",)),
        name=name,
    )(*args)


def _split3(x):
    hi = x.astype(BF16)
    r1 = x - hi.astype(F32)
    mid = r1.astype(BF16)
    lo = (r1 - mid.astype(F32)).astype(BF16)
    return hi, mid, lo


def _proj1_kernel(x_ref, mod_ref, g_ref, w1_ref, wg_ref, bg_ref, q_ref, k_ref, v_ref, o_ref, gc_ref, gr_ref):
    tm = x_ref.shape[0]
    L = MLSTM_L
    hk = H_C * DK_C
    hv = H_C * DV_C
    h = _modulate(x_ref[...], g_ref[...], mod_ref[0, 0:1, :], mod_ref[0, 1:2, :]).astype(BF16)
    gates = _dot(h, wg_ref[...]) + bg_ref[...]
    proj = _dot(h, w1_ref[...])
    for blk in range(hk // LANES):
        sl = slice(blk * LANES, (blk + 1) * LANES)
        q_ref[sl, :] = jnp.transpose(proj[:, sl]).astype(BF16)
    for blk in range(hv // LANES):
        sl = slice(blk * LANES, (blk + 1) * LANES)
        v_ref[sl, :] = jnp.transpose(proj[:, 2 * hk + blk * LANES:2 * hk + (blk + 1) * LANES]).astype(BF16)
    k_ref[...] = proj[:, hk:2 * hk] * (DK_C ** -0.5)
    o_ref[...] = proj[:, 2 * hk + hv:2 * hk + 2 * hv]
    lf = jnp.minimum(gates, 0.0) - jnp.log1p(jnp.exp(-jnp.abs(gates)))
    r = lax.broadcasted_iota(jnp.int32, (L, L), 0)
    c = lax.broadcasted_iota(jnp.int32, (L, L), 1)
    pre = jnp.where(c <= r, 1.0, 0.0).astype(BF16)
    suf = jnp.where(c >= r, 1.0, 0.0).astype(BF16)
    parts = jnp.concatenate(_split3(lf), axis=1)
    fold = lambda t: t[:, 0:LANES] + t[:, LANES:2 * LANES] + t[:, 2 * LANES:3 * LANES]
    chunks = [parts[ck * L:(ck + 1) * L] for ck in range(tm // L)]
    lane = lax.broadcasted_iota(jnp.int32, (tm, LANES), 1)
    b_sum = jnp.where((lane % 4) >= 2,
                      jnp.concatenate([fold(_dot(suf, p)) for p in chunks], axis=0),
                      jnp.concatenate([fold(_dot(pre, p)) for p in chunks], axis=0))
    b = pltpu.roll(b_sum, LANES - N_GATE, 1)
    u = gates - b
    pos = lax.broadcasted_iota(jnp.int32, (tm, LANES), 0) % L
    bwd = (lane % 4) >= 2
    cm = u
    step = 1
    while step < L:
        below = jnp.where(pos >= step, pltpu.roll(cm, step, 0), -jnp.inf)
        above = jnp.where(pos < L - step, pltpu.roll(cm, tm - step, 0), -jnp.inf)
        cm = jnp.maximum(cm, jnp.where(bwd, above, below))
        step *= 2
    low = lane < N_GATE
    packed = (jnp.where(low, b, 0.0) + pltpu.roll(jnp.where(low, cm, 0.0), N_GATE, 1)
              + pltpu.roll(jnp.where(low, u, 0.0), 2 * N_GATE, 1))
    gc_ref[...] = packed
    gr_ref[...] = jnp.transpose(packed)[0:3 * N_GATE]


def _proj1(x, mod, mod_row, g, w1, wg, bg):
    n = x.shape[0]
    tm = TM_PROJ1
    assert tm % MLSTM_L == 0
    full = lambda shape: pl.BlockSpec(shape, lambda i: (0,) * len(shape))
    row = lambda w: pl.BlockSpec((tm, w), lambda i: (i, 0))
    col = lambda w: pl.BlockSpec((w, tm), lambda i: (0, i))
    hk, hv = H_C * DK_C, H_C * DV_C
    return pl.pallas_call(
        _proj1_kernel,
        grid=(n // tm,),
        in_specs=[row(D_MODEL), pl.BlockSpec((1, N_MOD, D_MODEL), lambda i: (mod_row(i), 0, 0)),
                  full((1, D_MODEL)), full((D_MODEL, 2 * hk + 2 * hv)), full(wg.shape), full((1, LANES))],
        out_specs=[col(hk), row(hk), col(hv), row(hv), row(LANES),
                   pl.BlockSpec((3 * N_GATE, tm), lambda i: (0, i))],
        out_shape=[jax.ShapeDtypeStruct((hk, n), BF16), jax.ShapeDtypeStruct((n, hk), F32),
                   jax.ShapeDtypeStruct((hv, n), BF16), jax.ShapeDtypeStruct((n, hv), F32),
                   jax.ShapeDtypeStruct((n, LANES), F32), jax.ShapeDtypeStruct((3 * N_GATE, n), F32)],
        compiler_params=_params(("arbitrary",)),
        name="proj1",
    )(x, mod, g, w1, wg, bg)


def _chain(j, d, hh):
    return (2 * j + d) * 2 + hh


def _mlstm_kernel(*refs, has_state, emit_state):
    q_ref, k_ref, v_ref, o_ref, gc_ref, gr_ref, gn_ref = refs[:7]
    refs = refs[7:]
    if has_state:
        c0_ref, n0_ref, m0_ref = refs[:3]
        refs = refs[3:]
    hs_ref = refs[0]
    refs = refs[1:]
    if emit_state:
        cf_ref, nst_ref, mst_ref = refs[:3]
        refs = refs[3:]
    cx_s, m_s, h_s = refs

    L = MLSTM_L
    seq = k_ref.shape[0]
    nc = seq // L
    npair = H_C // 2
    lane = lax.broadcasted_iota(jnp.int32, (1, LANES), 1)
    head_mask = [lane < DK_C, lane >= DK_C]
    ri = lax.broadcasted_iota(jnp.int32, (L, L), 0)
    ci = lax.broadcasted_iota(jnp.int32, (L, L), 1)
    causal = [ri <= ci, ri >= ci]
    ones_blk = jnp.ones((N_ROWS_N, L), BF16)
    chains = [(j, d, hh) for d in range(2) for j in range(npair) for hh in range(2)]

    h_s[...] = jnp.zeros_like(h_s)
    for j, d, hh in chains:
        ch = _chain(j, d, hh)
        if has_state:
            zpad = jnp.zeros((DK_C, DV_C), F32)
            c0 = c0_ref[0, d, 2 * j + hh]
            c0 = jnp.concatenate([c0, zpad] if hh == 0 else [zpad, c0], axis=0)
            cx_s[ch, 0:DV_C, :] = jnp.transpose(c0)
            n_row = jnp.where(head_mask[hh], n0_ref[0, d, j], 0.0)
            cx_s[ch, DV_C:DV_C + N_ROWS_N, :] = jnp.broadcast_to(n_row, (N_ROWS_N, LANES))
            m_s[ch] = m0_ref[0, d, 2 * j + hh]
        else:
            cx_s[ch] = jnp.zeros((DV_C + N_ROWS_N, LANES), F32)
            m_s[ch] = jnp.zeros((1, LANES), F32)

    def chunk_step(i, carry):
        sl = [pl.ds(pl.multiple_of(i * L, L), L), pl.ds(pl.multiple_of((nc - 1 - i) * L, L), L)]
        gcol = [gc_ref[sl[d], :] for d in range(2)]
        grow = [gr_ref[:, sl[d]] for d in range(2)]
        kpair = {(d, j): k_ref[sl[d], j * LANES:(j + 1) * LANES] for d in range(2) for j in range(npair)}
        kbf = {key: kk.astype(BF16) for key, kk in kpair.items()}
        zq = jnp.zeros((DK_C, L), BF16)
        qt = {}
        for d in range(2):
            for j in range(npair):
                qt[(j, d, 0)] = jnp.concatenate([q_ref[j * LANES:j * LANES + DK_C, sl[d]], zq], axis=0)
                qt[(j, d, 1)] = jnp.concatenate([zq, q_ref[j * LANES + DK_C:(j + 1) * LANES, sl[d]]], axis=0)
        st, u_bc, vt, row = {}, {}, {}, {}
        for d in range(2):
            for j in range(npair):
                pair_scores = _dot(kbf[(d, j)], jnp.concatenate([qt[(j, d, 0)], qt[(j, d, 1)]], axis=1))
                st[(j, d, 0)], st[(j, d, 1)] = pair_scores[:, 0:L], pair_scores[:, L:2 * L]
        for j, d, hh in chains:
            key = (j, d, hh)
            idx = 4 * j + 2 * d + hh
            edge = L - 1 if d == 0 else 0
            b_row = grow[d][idx:idx + 1, :]
            cm_row = grow[d][N_GATE + idx:N_GATE + idx + 1, :]
            u_col = gcol[d][:, 2 * N_GATE + idx:2 * N_GATE + idx + 1]
            m_prev = m_s[_chain(j, d, hh)][:, 0:1]
            row[key] = (b_row, cm_row, b_row[:, edge:edge + 1], cm_row[:, edge:edge + 1], m_prev)
            u_bc[key] = jnp.broadcast_to(u_col, (L, L))
            vt[key] = v_ref[(2 * j + hh) * DV_C:(2 * j + hh + 1) * DV_C, sl[d]]
        for j, d, hh in chains:
            key = (j, d, hh)
            ch = _chain(j, d, hh)
            b_row, cm_row, g_tot, cm_last, m_prev = row[key]
            head = slice((2 * j + hh) * DV_C, (2 * j + hh + 1) * DV_C)
            cx = cx_s[ch]
            big_m = jnp.maximum(m_prev, cm_row)
            s = st[key] * jnp.exp(jnp.where(causal[d], u_bc[key] - big_m, -jnp.inf))
            inter = jnp.exp(m_prev - big_m)
            vx = jnp.concatenate([vt[key], ones_blk], axis=0)
            lhs = jnp.concatenate([vx, cx.astype(BF16)], axis=1)
            rhs = jnp.concatenate([s.astype(BF16), (inter * qt[key].astype(F32)).astype(BF16)], axis=0)
            res = _dot(lhs, rhs)
            den = res[DV_C:DV_C + 1, :]
            hval = res[0:DV_C, :] * (1.0 / jnp.maximum(jnp.abs(den), jnp.exp(-(b_row + big_m))))
            h_s[head, sl[d]] = h_s[head, sl[d]] + hval

            m_top = jnp.maximum(m_prev, cm_last)
            kw = kpair[(d, j)] * jnp.exp(u_bc[key] - m_top)
            cx_s[ch] = jnp.exp(m_prev - m_top) * cx + _dot(vx, kw.astype(BF16))
            m_s[ch] = jnp.broadcast_to(g_tot + m_top, (1, LANES))
        return carry

    lax.fori_loop(0, nc, chunk_step, 0, unroll=min(nc, MLSTM_UNROLL))

    gn_col = jnp.broadcast_to(gn_ref[...], (DV_C, LANES))
    gn_col = jnp.concatenate([gn_col] * (seq // LANES), axis=1)
    for hd in range(H_C):
        sl = slice(hd * DV_C, (hd + 1) * DV_C)
        ht = h_s[sl, :]
        ms = jnp.mean(ht * ht, axis=0, keepdims=True)
        y = jnp.transpose((ht * lax.rsqrt(ms + RMS_EPS)) * gn_col) * jax.nn.sigmoid(o_ref[:, sl])
        hs_ref[:, sl] = y.astype(BF16)
    if emit_state:
        for j, d, hh in chains:
            ch = _chain(j, d, hh)
            cf_ref[0, 0, d, 2 * j + hh] = jnp.transpose(cx_s[ch, 0:DV_C, :])[hh * DK_C:(hh + 1) * DK_C, :]
            mst_ref[0, d * H_C + 2 * j + hh:d * H_C + 2 * j + hh + 1, :] = m_s[ch]
        for j in range(npair):
            for d in range(2):
                nst_ref[0, d * npair + j:d * npair + j + 1, :] = jnp.where(
                    head_mask[0], cx_s[_chain(j, d, 0), DV_C:DV_C + 1, :], cx_s[_chain(j, d, 1), DV_C:DV_C + 1, :])


def _mlstm(q, k, v, o, gc, gr, gn, state, seq, emit_state):
    n = k.shape[0]
    nb = n // seq
    npair = H_C // 2
    has_state = state is not None
    hk, hv = H_C * DK_C, H_C * DV_C
    in_specs = [
        pl.BlockSpec((hk, seq), lambda b: (0, b)),
        pl.BlockSpec((seq, hk), lambda b: (b, 0)),
        pl.BlockSpec((hv, seq), lambda b: (0, b)),
        pl.BlockSpec((seq, hv), lambda b: (b, 0)),
        pl.BlockSpec((seq, LANES), lambda b: (b, 0)),
        pl.BlockSpec((3 * N_GATE, seq), lambda b: (0, b)),
        pl.BlockSpec((DV_C, 1), lambda b: (0, 0)),
    ]
    args = [q, k, v, o, gc, gr, gn]
    if has_state:
        c0, n0, m0 = state
        in_specs += [pl.BlockSpec((1, 2, H_C, DK_C, DV_C), lambda b: (b, 0, 0, 0, 0)),
                     pl.BlockSpec((1, 2, npair, 1, LANES), lambda b: (b, 0, 0, 0, 0)),
                     pl.BlockSpec((1, 2, H_C, 1, LANES), lambda b: (b, 0, 0, 0, 0))]
        args += [c0, n0, m0]
    out_specs = [pl.BlockSpec((seq, hv), lambda b: (b, 0))]
    out_shape = [jax.ShapeDtypeStruct((n, hv), BF16)]
    if emit_state:
        out_specs += [pl.BlockSpec((1, 1, 2, H_C, DK_C, DV_C), lambda b: (b, 0, 0, 0, 0, 0)),
                      pl.BlockSpec((1, 2 * npair, LANES), lambda b: (b, 0, 0)),
                      pl.BlockSpec((1, 2 * H_C, LANES), lambda b: (b, 0, 0))]
        out_shape += [jax.ShapeDtypeStruct((nb, 1, 2, H_C, DK_C, DV_C), F32),
                      jax.ShapeDtypeStruct((nb, 2 * npair, LANES), F32),
                      jax.ShapeDtypeStruct((nb, 2 * H_C, LANES), F32)]
    n_chain = 2 * H_C
    return pl.pallas_call(
        functools.partial(_mlstm_kernel, has_state=has_state, emit_state=emit_state),
        grid=(nb,),
        in_specs=in_specs, out_specs=out_specs, out_shape=out_shape,
        scratch_shapes=[pltpu.VMEM((n_chain, DV_C + N_ROWS_N, LANES), F32), pltpu.VMEM((n_chain, 1, LANES), F32),
                        pltpu.VMEM((hv, seq), F32)],
        compiler_params=_params(("arbitrary",)),
        name="mlstm_lat" if has_state else "mlstm_ctx",
    )(*args)


def _rope_tables(n_tok):
    t = np.arange(n_tok)
    rows = (t // GRID_W).astype(np.float64)
    cols = (t % GRID_W).astype(np.float64)

    def axis_tabs(width, lane0):
        half = width // 2
        quarter = half // 2
        freqs = np.power(ROPE_BASE, -np.arange(quarter, dtype=np.float64) / quarter)
        c = np.ones((n_tok, LANES))
        sa = np.zeros((n_tok, LANES))
        sb = np.zeros((n_tok, LANES))
        for g, pos in enumerate((rows, cols)):
            ang = pos[:, None] * freqs[None, :]
            a0 = lane0 + g * half
            c[:, a0:a0 + quarter] = np.cos(ang)
            c[:, a0 + quarter:a0 + half] = np.cos(ang)
            sa[:, a0:a0 + quarter] = -np.sin(ang)
            sb[:, a0 + quarter:a0 + half] = np.sin(ang)
        return c, sa, sb

    cm, sam, sbm = axis_tabs(QK_ROPE, ROPE_LANE0)
    c0, sa0, sb0 = axis_tabs(DIFF_HD, 0)
    c1, sa1, sb1 = axis_tabs(DIFF_HD, DIFF_HD)
    cd = np.where(np.arange(LANES)[None, :] < DIFF_HD, c0, c1)
    return tuple(jnp.asarray(a, F32) for a in (cm, sam, sbm, cd, sa0 + sa1, sb0 + sb1))


def _prep_even(w_in_ab, w_uq, w_ukv):
    c2 = Q_LORA + KV_LORA
    w_ab = w_in_ab.astype(BF16)
    wd = w_ab[:, c2 + QK_ROPE:]
    wq = jnp.pad(w_uq.reshape(Q_LORA, H_A, QK_NOPE + QK_ROPE),
                 ((0, 0), (0, 0), (0, HEAD_PAD - QK_NOPE - QK_ROPE))).reshape(Q_LORA, H_A * HEAD_PAD)
    kvw = w_ukv.reshape(KV_LORA, H_A, QK_NOPE + V_HD_A)
    kpad = jnp.pad(kvw[..., :QK_NOPE], ((0, 0), (0, 0), (0, HEAD_PAD - QK_NOPE)))
    vw = kvw[..., QK_NOPE:]
    zv = jnp.zeros_like(vw)
    odd = (jnp.arange(H_A) % 2 == 1)[None, :, None]
    vpad = jnp.where(odd, jnp.concatenate([zv, vw], -1), jnp.concatenate([vw, zv], -1))
    wukv = jnp.concatenate([kpad.reshape(KV_LORA, -1), vpad.reshape(KV_LORA, -1)], axis=1)
    return (w_ab, wd), wq.astype(BF16), wukv.astype(BF16)


def _gate_order(g):
    lead = g.shape[:-1]
    g = g.reshape(lead + (2, 2, H_C // 2, 2))
    perm = tuple(range(len(lead))) + tuple(len(lead) + a for a in (1, 2, 0, 3))
    return g.transpose(perm).reshape(lead + (4 * H_C,))


def _prep_odd(w_in_c, b_gate_c):
    ng = 4 * H_C
    base = w_in_c.shape[1] - ng
    w1 = w_in_c.astype(BF16)
    wg = jnp.pad(_gate_order(w1[:, base:]), ((0, 0), (0, LANES - ng)))
    bg = jnp.pad(_gate_order(b_gate_c), (0, LANES - ng)).reshape(1, LANES)
    return w1, wg, bg


def kernel(x_prompt, x_sample, cache_mla_ckv, cache_mla_krope, cache_diff_k, cache_diff_v,
           state_mlstm_C, state_mlstm_n, state_mlstm_m, c, c_ctx,
           w_ada, b_ada, g_mix, g_ffn, w_ffn_in, w_ffn_out,
           w_in_ab, g_q_lora, g_kv_lora, w_uq, w_ukv, diff_lambda, g_diff_subln, w_out_ab,
           w_in_c, b_gate_c, g_mlstm, w_out_c, g_final):
    nbp, seq_p, _ = x_prompt.shape
    nbs, seq_s, _ = x_sample.shape
    past = cache_mla_ckv.shape[2]
    assert DEPTH == 2 and 1 + nbs <= COND_ROWS
    assert cache_mla_ckv.shape[1] == 1 and state_mlstm_C.shape[1] == 1

    cond = jnp.concatenate([c_ctx[None], c, jnp.zeros((COND_ROWS - 1 - nbs, D_MODEL), F32)], axis=0)
    mod = _ada(cond, w_ada, b_ada).reshape(DEPTH * COND_ROWS, N_MOD, D_MODEL)

    xp = x_prompt.reshape(nbp * seq_p, D_MODEL)
    xs = x_sample.reshape(nbs * seq_s, D_MODEL)
    row2 = lambda v: v.reshape(1, -1)

    def mod_rows(layer, tm):
        prompt = lambda i: layer * COND_ROWS
        sample = lambda i: layer * COND_ROWS + 1 + i // (seq_s // tm)
        return prompt, sample

    lam_init = 0.8 - 0.6 * math.exp(-0.3 * 0)
    w0, wq, wukv = _prep_even(w_in_ab[0], w_uq[0], w_ukv[0])
    rope_tabs = _rope_tables(seq_s)
    mrp, _ = mod_rows(0, TM_PROJ)
    _, mrs = mod_rows(0, TM_PROJ_ROPE)
    gq, gkv = row2(g_q_lora[0]), row2(g_kv_lora[0])
    outs_p = _proj0(xp, mod, mrp, row2(g_mix[0]), *w0, gq, gkv, wq, wukv, None, seq_p)
    outs_s = _proj0(xs, mod, mrs, row2(g_mix[0]), *w0, gq, gkv, wq, wukv, rope_tabs, seq_s)
    qm_p, km_p, vm_p, dq_p, dk_p, dv_p, ckv_new, kr_new, dk_new, dv_new = outs_p
    qm_s, km_s, vm_s, dq_s, dk_s, dv_s = outs_s
    gsub = row2(g_diff_subln[0])
    krb_c = jnp.pad(cache_mla_krope.reshape(nbs, past, QK_ROPE),
                    ((0, 0), (0, 0), (ROPE_LANE0, LANES - ROPE_LANE0 - QK_ROPE)))
    ctx = (cache_mla_ckv.reshape(nbs, past, KV_LORA), krb_c,
           cache_diff_k.reshape(nbs, H_B, past, 2 * DIFF_HD), cache_diff_v.reshape(nbs, H_B, past, DIFF_VD))
    a_p = _attn0(qm_p, km_p, vm_p, dq_p, dk_p, dv_p, diff_lambda[0], gsub, None, None, seq_p, lam_init)
    a_s = _attn0(qm_s, km_s, vm_s, dq_s, dk_s, dv_s, diff_lambda[0], gsub, ctx, wukv, seq_s, lam_init)
    mtp, mts = mod_rows(0, TM_TAIL)
    wo0 = w_out_ab[0].astype(BF16)
    win, wout = w_ffn_in.astype(BF16), w_ffn_out.astype(BF16)
    xp = _tail(xp, a_p, mod, mtp, wo0, row2(g_ffn[0]), win, wout, 0, None, "tail0_ctx")
    xs = _tail(xs, a_s, mod, mts, wo0, row2(g_ffn[0]), win, wout, 0, None, "tail0_lat")

    w1, wg, bg = _prep_odd(w_in_c[0], b_gate_c[0])
    mrp, mrs = mod_rows(1, TM_PROJ1)
    gn = g_mlstm[0].reshape(DV_C, 1)
    q_p, k_p, v_p, o_p, gc_p, gr_p = _proj1(xp, mod, mrp, row2(g_mix[1]), w1, wg, bg)
    q_s, k_s, v_s, o_s, gc_s, gr_s = _proj1(xs, mod, mrs, row2(g_mix[1]), w1, wg, bg)
    hs_p, c_new, nst, mst = _mlstm(q_p, k_p, v_p, o_p, gc_p, gr_p, gn, None, seq_p, True)
    state = (state_mlstm_C.reshape(nbs, 2, H_C, DK_C, DV_C),
             state_mlstm_n.reshape(nbs, 2, H_C // 2, 1, LANES),
             jnp.broadcast_to(state_mlstm_m.reshape(nbs, 2, H_C, 1, 1), (nbs, 2, H_C, 1, LANES)))
    (hs_s,) = _mlstm(q_s, k_s, v_s, o_s, gc_s, gr_s, gn, state, seq_s, False)
    mtp, mts = mod_rows(1, TM_TAIL)
    wo1 = w_out_c[0].astype(BF16)
    gfin = row2(g_final)
    yp = _tail(xp, hs_p, mod, mtp, wo1, row2(g_ffn[1]), win, wout, 1, gfin, "tail1_ctx")
    ys = _tail(xs, hs_s, mod, mts, wo1, row2(g_ffn[1]), win, wout, 1, gfin, "tail1_lat")

    return (yp.reshape(nbp, seq_p, D_MODEL), ys.reshape(nbs, seq_s, D_MODEL),
            ckv_new.reshape(nbp, 1, seq_p, KV_LORA), kr_new.reshape(nbp, 1, seq_p, QK_ROPE),
            dk_new.reshape(nbp, 1, H_B, seq_p, 2 * DIFF_HD), dv_new.reshape(nbp, 1, H_B, seq_p, DIFF_VD),
            c_new, nst.reshape(nbp, 1, 2, H_C, DK_C), mst[:, :, 0].reshape(nbp, 1, 2, H_C))
```

```python
import functools
import math

import jax
import jax.numpy as jnp
import numpy as np
from jax import lax
from jax.experimental import pallas as pl
from jax.experimental.pallas import tpu as pltpu

F32 = jnp.float32
BF16 = jnp.bfloat16

D_MODEL = 1024
DEPTH = 2
GRID_W = 64
ROPE_BASE = 10000.0
RMS_EPS = 1e-6
H_A = 8
QK_NOPE = 64
QK_ROPE = 32
V_HD_A = 64
Q_LORA = 256
KV_LORA = 128
H_B = 4
DIFF_HD = 64
DIFF_VD = 2 * DIFF_HD
H_C = 8
DK_C = 64
DV_C = D_MODEL // H_C
D_FF = -(-8 * D_MODEL // (3 * 256)) * 256

LANES = 128
HEAD_PAD = LANES
ROPE_LANE0 = QK_NOPE
W_MLA = H_A * HEAD_PAD
W_DIFF = H_B * LANES
N_MOD = 6
COND_ROWS = 16
N_GATE = 2 * H_C
N_ROWS_N = 16

TM_PROJ = 1024
TM_PROJ_ROPE = 512
TM_PROJ1 = 1024
TQ_ATTN = 512
ATTN_SEQS_PER_STEP = 4
TM_TAIL = 1024
FF_CHUNK = 256
MLSTM_L = 128
MLSTM_UNROLL = 4
ADA_TN = 1024
VMEM_LIMIT = 56 * 1024 * 1024

assert D_FF % LANES == 0 and FF_CHUNK % LANES == 0


def _dot(a, b):
    return jnp.dot(a, b, preferred_element_type=F32)


def _dot_nt(a, b):
    return lax.dot_general(a, b, (((1,), (1,)), ((), ())), preferred_element_type=F32)


def _rms(x):
    return x * lax.rsqrt(jnp.mean(x * x, axis=-1, keepdims=True) + RMS_EPS)


def _modulate(x, g, shift, scale):
    return _rms(x) * (g * (1.0 + scale)) + shift


def _params(semantics):
    return pltpu.CompilerParams(dimension_semantics=semantics, vmem_limit_bytes=VMEM_LIMIT)


def _ada_kernel(cond_ref, w_ref, b_ref, o_ref):
    c = cond_ref[...]
    s = (c * jax.nn.sigmoid(c)).astype(BF16)
    o_ref[0] = _dot(s, w_ref[0].astype(BF16)) + b_ref[0]


def _ada(cond, w_ada, b_ada):
    n_out = w_ada.shape[-1]
    return pl.pallas_call(
        _ada_kernel,
        grid=(DEPTH, n_out // ADA_TN),
        in_specs=[
            pl.BlockSpec((COND_ROWS, D_MODEL), lambda l, n: (0, 0)),
            pl.BlockSpec((1, D_MODEL, ADA_TN), lambda l, n: (l, 0, n)),
            pl.BlockSpec((1, 1, ADA_TN), lambda l, n: (l, 0, n)),
        ],
        out_specs=pl.BlockSpec((1, COND_ROWS, ADA_TN), lambda l, n: (l, 0, n)),
        out_shape=jax.ShapeDtypeStruct((DEPTH, COND_ROWS, n_out), F32),
        compiler_params=_params(("arbitrary", "arbitrary")),
        name="ada",
    )(cond, w_ada, b_ada.reshape(DEPTH, 1, n_out))


def _rope(x, c, sa, sb, off):
    return x * c + pltpu.roll(x, LANES - off, 1) * sa + pltpu.roll(x, off, 1) * sb


def _proj0_kernel(*refs, rope, emit_cache):
    x_ref, mod_ref, g_ref, wa_ref, wd_ref, gq_ref, gkv_ref, wq_ref, wukv_ref = refs[:9]
    refs = refs[9:]
    if rope:
        cm_ref, sam_ref, sbm_ref, cd_ref, sad_ref, sbd_ref = refs[:6]
        refs = refs[6:]
    qm_ref, km_ref, vm_ref, dq_ref, dk_ref, dv_ref = refs[:6]
    refs = refs[6:]
    if emit_cache:
        ckvf_ref, krf_ref, dkf_ref, dvf_ref = refs

    h = _modulate(x_ref[...], g_ref[...], mod_ref[0, 0:1, :], mod_ref[0, 1:2, :]).astype(BF16)
    pa = _dot(h, wa_ref[...])
    pd = _dot(h, wd_ref[...])
    lane = lax.broadcasted_iota(jnp.int32, (pa.shape[0], LANES), 1)
    krb = jnp.where((lane >= ROPE_LANE0) & (lane < ROPE_LANE0 + QK_ROPE),
                    pltpu.roll(pa[:, Q_LORA + KV_LORA:Q_LORA + KV_LORA + LANES], ROPE_LANE0, 1), 0.0)
    cq = pa[:, 0:Q_LORA]
    ckv = _rms(pa[:, Q_LORA:Q_LORA + KV_LORA]) * gkv_ref[...]
    qa = _dot((_rms(cq) * gq_ref[...]).astype(BF16), wq_ref[...])
    kv = _dot(ckv.astype(BF16), wukv_ref[...])
    if emit_cache:
        ckvf_ref[...] = ckv
        krf_ref[...] = krb[:, ROPE_LANE0:ROPE_LANE0 + QK_ROPE]
    if rope:
        cm, sam, sbm = cm_ref[...], sam_ref[...], sbm_ref[...]
        cd, sad, sbd = cd_ref[...], sad_ref[...], sbd_ref[...]
        krb = _rope(krb, cm, sam, sbm, QK_ROPE // 4)
    for hd in range(H_A):
        sl = slice(hd * HEAD_PAD, (hd + 1) * HEAD_PAD)
        qh = qa[:, sl]
        if rope:
            qh = _rope(qh, cm, sam, sbm, QK_ROPE // 4)
        qm_ref[:, sl] = qh.astype(BF16)
        km_ref[:, sl] = (kv[:, sl] + krb).astype(BF16)
    vm_ref[...] = kv[:, H_A * HEAD_PAD:].astype(BF16)
    for hd in range(H_B):
        sl = slice(hd * LANES, (hd + 1) * LANES)
        dq = pd[:, hd * LANES:(hd + 1) * LANES]
        dk = pd[:, (H_B + hd) * LANES:(H_B + hd + 1) * LANES]
        dv = pd[:, (2 * H_B + hd) * LANES:(2 * H_B + hd + 1) * LANES]
        if emit_cache:
            seq = dkf_ref.shape[2]
            for bi in range(dkf_ref.shape[0]):
                dkf_ref[bi, hd] = dk[bi * seq:(bi + 1) * seq]
                dvf_ref[bi, hd] = dv[bi * seq:(bi + 1) * seq]
        if rope:
            dq = _rope(dq, cd, sad, sbd, DIFF_HD // 4)
            dk = _rope(dk, cd, sad, sbd, DIFF_HD // 4)
        dq_ref[:, sl] = (dq * (DIFF_HD ** -0.5)).astype(BF16)
        dk_ref[:, sl] = dk.astype(BF16)
        dv_ref[:, sl] = dv.astype(BF16)


def _proj0(x, mod, mod_row, g, w_ab, wd, gq, gkv, wq, wukv, rope_tabs, seq):
    n = x.shape[0]
    rope = rope_tabs is not None
    emit_cache = not rope
    tm = TM_PROJ_ROPE if rope else TM_PROJ
    tiles_per_seq = max(seq // tm, 1)
    full = lambda shape: pl.BlockSpec(shape, lambda i: (0,) * len(shape))
    in_specs = [
        pl.BlockSpec((tm, D_MODEL), lambda i: (i, 0)),
        pl.BlockSpec((1, N_MOD, D_MODEL), lambda i: (mod_row(i), 0, 0)),
        full((1, D_MODEL)), full((D_MODEL, Q_LORA + KV_LORA + LANES)), full(wd.shape),
        full((1, Q_LORA)), full((1, KV_LORA)), full(wq.shape), full(wukv.shape),
    ]
    args = [x, mod, g, w_ab, wd, gq, gkv, wq, wukv]
    if rope:
        in_specs += [pl.BlockSpec((tm, LANES), lambda i: (i % tiles_per_seq, 0))] * 6
        args += list(rope_tabs)
    row = lambda w: pl.BlockSpec((tm, w), lambda i: (i, 0))
    widths = (W_MLA, W_MLA, W_MLA, W_DIFF, W_DIFF, W_DIFF)
    out_specs = [row(w) for w in widths]
    out_shape = [jax.ShapeDtypeStruct((n, w), BF16) for w in widths]
    if emit_cache:
        assert tm % seq == 0
        cache = pl.BlockSpec((tm // seq, H_B, seq, LANES), lambda i: (i, 0, 0, 0))
        out_specs += [row(KV_LORA), row(QK_ROPE), cache, cache]
        out_shape += [jax.ShapeDtypeStruct((n, KV_LORA), F32), jax.ShapeDtypeStruct((n, QK_ROPE), F32),
                      jax.ShapeDtypeStruct((n // seq, H_B, seq, LANES), F32),
                      jax.ShapeDtypeStruct((n // seq, H_B, seq, LANES), F32)]
    return pl.pallas_call(
        functools.partial(_proj0_kernel, rope=rope, emit_cache=emit_cache),
        grid=(n // tm,),
        in_specs=in_specs, out_specs=out_specs, out_shape=out_shape,
        compiler_params=_params(("arbitrary",)),
        name="proj0_rope" if rope else "proj0_ctx",
    )(*args)


def _attn_kernel(*refs, has_ctx, lam_init, n_seq):
    q_ref, dq_ref, k_ref, v_ref, dk_ref, dv_ref, lamv_ref, gsub_ref = refs[:8]
    refs = refs[8:]
    if has_ctx:
        ckvc_ref, krc_ref, dkc_ref, dvc_ref, wukv_ref = refs[:5]
        o_ref, vx_s, dvx_s, kx_s, dkx_s = refs[5:]
    else:
        o_ref, vx_s, dvx_s = refs
    seq = k_ref.shape[0] // n_seq
    ktot = vx_s.shape[0] // n_seq
    assert n_seq == 1 or not has_ctx

    @pl.when(pl.program_id(1) == 0)
    def _():
        if has_ctx:
            kx_s[0:seq, :] = k_ref[...]
            dkx_s[0:seq, :] = dk_ref[...]
        own = n_seq * seq
        ones_own = jnp.ones((own, LANES), BF16)
        for hd in range(H_A):
            vx_s[0:own, 2 * hd * LANES:(2 * hd + 1) * LANES] = v_ref[:, hd * LANES:(hd + 1) * LANES]
            vx_s[0:own, (2 * hd + 1) * LANES:(2 * hd + 2) * LANES] = ones_own
        for hd in range(H_B):
            dvx_s[0:own, 2 * hd * LANES:(2 * hd + 1) * LANES] = dv_ref[:, hd * LANES:(hd + 1) * LANES]
            dvx_s[0:own, (2 * hd + 1) * LANES:(2 * hd + 2) * LANES] = ones_own
        if has_ctx:
            kv = _dot(ckvc_ref[0].astype(BF16), wukv_ref[...])
            krb = krc_ref[0]
            ones_ctx = jnp.ones((ktot - seq, LANES), BF16)
            for hd in range(H_A):
                sl = slice(hd * HEAD_PAD, (hd + 1) * HEAD_PAD)
                kx_s[seq:ktot, sl] = (kv[:, sl] + krb).astype(BF16)
                vx_s[seq:ktot, 2 * hd * LANES:(2 * hd + 1) * LANES] = (
                    kv[:, (H_A + hd) * HEAD_PAD:(H_A + hd + 1) * HEAD_PAD].astype(BF16))
                vx_s[seq:ktot, (2 * hd + 1) * LANES:(2 * hd + 2) * LANES] = ones_ctx
            for hd in range(H_B):
                sl = slice(hd * LANES, (hd + 1) * LANES)
                dkx_s[seq:ktot, sl] = dkc_ref[0, hd].astype(BF16)
                dvx_s[seq:ktot, 2 * hd * LANES:(2 * hd + 1) * LANES] = dvc_ref[0, hd].astype(BF16)
                dvx_s[seq:ktot, (2 * hd + 1) * LANES:(2 * hd + 2) * LANES] = ones_ctx

    log2e = 1.0 / math.log(2.0)
    tq = dq_ref.shape[0] // n_seq
    lo = lax.broadcasted_iota(jnp.int32, (tq, LANES), 1) < DIFF_HD

    keys_a, keys_b = (kx_s, dkx_s) if has_ctx else (k_ref, dk_ref)
    jobs = []
    for si in range(n_seq):
        qr = slice(si * tq, (si + 1) * tq)
        kr = slice(si * ktot, (si + 1) * ktot)
        for hd in range(H_A):
            sl = slice(hd * HEAD_PAD, (hd + 1) * HEAD_PAD)
            jobs.append((lambda qr=qr, sl=sl: q_ref[qr, sl], (keys_a, kr, sl), (vx_s, hd),
                         (QK_NOPE + QK_ROPE) ** -0.5 * log2e))
        for hd in range(H_B):
            sl = slice(hd * LANES, (hd + 1) * LANES)
            for part in range(2):
                def qfn(qr=qr, sl=sl, part=part):
                    dq = dq_ref[qr, sl].astype(F32)
                    return (jnp.where(lo, dq, 0.0) if part == 0 else jnp.where(lo, 0.0, dq)).astype(BF16)
                jobs.append((qfn, (keys_b, kr, sl), (dvx_s, hd), log2e))

    def scores(job):
        qfn, (kref, kr, sl), _, _ = job
        return _dot_nt(qfn(), kref[kr, sl])

    def finish(s, job):
        _, (_, kr, _), (vref, hd), c = job
        m = jnp.max(s, axis=-1, keepdims=True)
        e = jnp.exp2((s - m) * c).astype(BF16)
        res = _dot(e, vref[kr, 2 * hd * LANES:(2 * hd + 2) * LANES])
        return res[:, 0:LANES] * (1.0 / res[:, LANES:2 * LANES])

    outs = []
    s_next = scores(jobs[0])
    for i, job in enumerate(jobs):
        s_cur = s_next
        if i + 1 < len(jobs):
            s_next = scores(jobs[i + 1])
        outs.append(finish(s_cur, job))

    lv = lamv_ref[...]
    lam = (jnp.exp(jnp.sum(lv[0:1] * lv[1:2], axis=-1, keepdims=True))
           - jnp.exp(jnp.sum(lv[2:3] * lv[3:4], axis=-1, keepdims=True)) + lam_init)
    per_seq = H_A + 2 * H_B
    for si in range(n_seq):
        qr = slice(si * tq, (si + 1) * tq)
        o = outs[si * per_seq:(si + 1) * per_seq]
        for j in range(H_A // 2):
            o_ref[qr, j * LANES:(j + 1) * LANES] = (o[2 * j] + o[2 * j + 1]).astype(BF16)
        for hd in range(H_B):
            acc = o[H_A + 2 * hd] - lam * o[H_A + 2 * hd + 1]
            ob = (_rms(acc) * gsub_ref[...]) * (1.0 - lam_init)
            o_ref[qr, H_A * V_HD_A + hd * LANES:H_A * V_HD_A + (hd + 1) * LANES] = ob.astype(BF16)


def _attn0(qm, km, vm, dq, dk, dv, lamv, gsub, ctx, wukv, seq, lam_init):
    n = qm.shape[0]
    nb = n // seq
    tq = min(TQ_ATTN, seq)
    has_ctx = ctx is not None
    n_seq = 1 if has_ctx or tq != seq else ATTN_SEQS_PER_STEP
    assert nb % n_seq == 0
    full = lambda shape: pl.BlockSpec(shape, lambda b, t: (0,) * len(shape))
    qrow = lambda w: pl.BlockSpec((n_seq * tq, w), lambda b, t: (b * (seq // tq) + t, 0))
    krow = lambda w: pl.BlockSpec((n_seq * seq, w), lambda b, t: (b, 0))
    in_specs = [qrow(W_MLA), qrow(W_DIFF), krow(W_MLA), krow(W_MLA), krow(W_DIFF), krow(W_DIFF),
                full((4, DIFF_HD)), full((1, DIFF_VD))]
    args = [qm, dq, km, vm, dk, dv, lamv, gsub]
    past = 0
    if has_ctx:
        ckv_c, krb_c, dk_c, dv_c = ctx
        past = ckv_c.shape[1]
        in_specs += [pl.BlockSpec((1, past, LANES), lambda b, t: (b, 0, 0)),
                     pl.BlockSpec((1, past, LANES), lambda b, t: (b, 0, 0)),
                     pl.BlockSpec((1, H_B, past, LANES), lambda b, t: (b, 0, 0, 0)),
                     pl.BlockSpec((1, H_B, past, LANES), lambda b, t: (b, 0, 0, 0)),
                     full(wukv.shape)]
        args += [ckv_c, krb_c, dk_c, dv_c, wukv]
    ktot = n_seq * (seq + past)
    scratch = [pltpu.VMEM((ktot, 2 * W_MLA), BF16), pltpu.VMEM((ktot, 2 * W_DIFF), BF16)]
    if has_ctx:
        scratch += [pltpu.VMEM((ktot, W_MLA), BF16), pltpu.VMEM((ktot, W_DIFF), BF16)]
    return pl.pallas_call(
        functools.partial(_attn_kernel, has_ctx=has_ctx, lam_init=lam_init, n_seq=n_seq),
        grid=(nb // n_seq, seq // tq),
        in_specs=in_specs,
        out_specs=pl.BlockSpec((n_seq * tq, D_MODEL), lambda b, t: (b * (seq // tq) + t, 0)),
        out_shape=jax.ShapeDtypeStruct((n, D_MODEL), BF16),
        scratch_shapes=scratch,
        compiler_params=_params(("arbitrary", "arbitrary")),
        name="attn0_lat" if has_ctx else "attn0_ctx",
    )(*args)


def _tail_kernel(*refs, final):
    x_ref, a_ref, mod_ref, wo_ref, gf_ref, win_ref, wout_ref = refs[:7]
    if final:
        gfin_ref, o_ref = refs[7:]
    else:
        (o_ref,) = refs[7:]
    x1 = x_ref[...] + mod_ref[0, 2:3, :] * _dot(a_ref[...], wo_ref[...])
    h = _modulate(x1, gf_ref[...], mod_ref[0, 3:4, :], mod_ref[0, 4:5, :]).astype(BF16)
    bounds = list(range(0, D_FF, FF_CHUNK)) + [D_FF]
    acc = None
    for lo, hi in zip(bounds[:-1], bounds[1:]):
        a = _dot(h, win_ref[0, :, lo:hi])
        b = _dot(h, win_ref[0, :, D_FF + lo:D_FF + hi])
        act = ((a * jax.nn.sigmoid(a)) * b).astype(BF16)
        part = _dot(act, wout_ref[0, lo:hi, :])
        acc = part if acc is None else acc + part
    x2 = x1 + mod_ref[0, 5:6, :] * acc
    if final:
        x2 = _rms(x2) * gfin_ref[...]
    o_ref[...] = x2


def _tail(x, a, mod, mod_row, wo, gf, win, wout, layer, gfin, name):
    n = x.shape[0]
    tm = TM_TAIL
    final = gfin is not None
    full = lambda shape: pl.BlockSpec(shape, lambda i: (0,) * len(shape))
    resident = lambda shape: pl.BlockSpec((1,) + shape[1:], lambda i: (layer, 0, 0),
                                          pipeline_mode=pl.Buffered(1))
    in_specs = [
        pl.BlockSpec((tm, D_MODEL), lambda i: (i, 0)),
        pl.BlockSpec((tm, D_MODEL), lambda i: (i, 0)),
        pl.BlockSpec((1, N_MOD, D_MODEL), lambda i: (mod_row(i), 0, 0)),
        full(wo.shape), full((1, D_MODEL)), resident(win.shape), resident(wout.shape),
    ]
    args = [x, a, mod, wo, gf, win, wout]
    if final:
        in_specs.append(full((1, D_MODEL)))
        args.append(gfin)
    return pl.pallas_call(
        functools.partial(_tail_kernel, final=final),
        grid=(n // tm,),
        in_specs=in_specs,
        out_specs=pl.BlockSpec((tm, D_MODEL), lambda i: (i, 0)),
        out_shape=jax.ShapeDtypeStruct((n, D_MODEL), F32),
        compiler_params=_params(("arbitrary",)),
        name=name,
    )(*args)


def _split3(x):
    hi = x.astype(BF16)
    r1 = x - hi.astype(F32)
    mid = r1.astype(BF16)
    lo = (r1 - mid.astype(F32)).astype(BF16)
    return hi, mid, lo


def _proj1_kernel(x_ref, mod_ref, g_ref, w1_ref, wg_ref, bg_ref, q_ref, k_ref, v_ref, o_ref, gc_ref, gr_ref):
    tm = x_ref.shape[0]
    L = MLSTM_L
    hk = H_C * DK_C
    hv = H_C * DV_C
    h = _modulate(x_ref[...], g_ref[...], mod_ref[0, 0:1, :], mod_ref[0, 1:2, :]).astype(BF16)
    gates = _dot(h, wg_ref[...]) + bg_ref[...]
    proj = _dot(h, w1_ref[...])
    for blk in range(hk // LANES):
        sl = slice(blk * LANES, (blk + 1) * LANES)
        q_ref[sl, :] = jnp.transpose(proj[:, sl]).astype(BF16)
    for blk in range(hv // LANES):
        sl = slice(blk * LANES, (blk + 1) * LANES)
        v_ref[sl, :] = jnp.transpose(proj[:, 2 * hk + blk * LANES:2 * hk + (blk + 1) * LANES]).astype(BF16)
    k_ref[...] = proj[:, hk:2 * hk] * (DK_C ** -0.5)
    o_ref[...] = proj[:, 2 * hk + hv:2 * hk + 2 * hv]
    lf = jnp.minimum(gates, 0.0) - jnp.log1p(jnp.exp(-jnp.abs(gates)))
    r = lax.broadcasted_iota(jnp.int32, (L, L), 0)
    c = lax.broadcasted_iota(jnp.int32, (L, L), 1)
    pre = jnp.where(c <= r, 1.0, 0.0).astype(BF16)
    suf = jnp.where(c >= r, 1.0, 0.0).astype(BF16)
    parts = jnp.concatenate(_split3(lf), axis=1)
    fold = lambda t: t[:, 0:LANES] + t[:, LANES:2 * LANES] + t[:, 2 * LANES:3 * LANES]
    chunks = [parts[ck * L:(ck + 1) * L] for ck in range(tm // L)]
    lane = lax.broadcasted_iota(jnp.int32, (tm, LANES), 1)
    b_sum = jnp.where((lane % 4) >= 2,
                      jnp.concatenate([fold(_dot(suf, p)) for p in chunks], axis=0),
                      jnp.concatenate([fold(_dot(pre, p)) for p in chunks], axis=0))
    b = pltpu.roll(b_sum, LANES - N_GATE, 1)
    u = gates - b
    pos = lax.broadcasted_iota(jnp.int32, (tm, LANES), 0) % L
    bwd = (lane % 4) >= 2
    cm = u
    step = 1
    while step < L:
        below = jnp.where(pos >= step, pltpu.roll(cm, step, 0), -jnp.inf)
        above = jnp.where(pos < L - step, pltpu.roll(cm, tm - step, 0), -jnp.inf)
        cm = jnp.maximum(cm, jnp.where(bwd, above, below))
        step *= 2
    low = lane < N_GATE
    packed = (jnp.where(low, b, 0.0) + pltpu.roll(jnp.where(low, cm, 0.0), N_GATE, 1)
              + pltpu.roll(jnp.where(low, u, 0.0), 2 * N_GATE, 1))
    gc_ref[...] = packed
    gr_ref[...] = jnp.transpose(packed)[0:3 * N_GATE]


def _proj1(x, mod, mod_row, g, w1, wg, bg):
    n = x.shape[0]
    tm = TM_PROJ1
    assert tm % MLSTM_L == 0
    full = lambda shape: pl.BlockSpec(shape, lambda i: (0,) * len(shape))
    row = lambda w: pl.BlockSpec((tm, w), lambda i: (i, 0))
    col = lambda w: pl.BlockSpec((w, tm), lambda i: (0, i))
    hk, hv = H_C * DK_C, H_C * DV_C
    return pl.pallas_call(
        _proj1_kernel,
        grid=(n // tm,),
        in_specs=[row(D_MODEL), pl.BlockSpec((1, N_MOD, D_MODEL), lambda i: (mod_row(i), 0, 0)),
                  full((1, D_MODEL)), full((D_MODEL, 2 * hk + 2 * hv)), full(wg.shape), full((1, LANES))],
        out_specs=[col(hk), row(hk), col(hv), row(hv), row(LANES),
                   pl.BlockSpec((3 * N_GATE, tm), lambda i: (0, i))],
        out_shape=[jax.ShapeDtypeStruct((hk, n), BF16), jax.ShapeDtypeStruct((n, hk), F32),
                   jax.ShapeDtypeStruct((hv, n), BF16), jax.ShapeDtypeStruct((n, hv), F32),
                   jax.ShapeDtypeStruct((n, LANES), F32), jax.ShapeDtypeStruct((3 * N_GATE, n), F32)],
        compiler_params=_params(("arbitrary",)),
        name="proj1",
    )(x, mod, g, w1, wg, bg)


def _chain(j, d, hh):
    return (2 * j + d) * 2 + hh


def _mlstm_kernel(*refs, has_state, emit_state):
    q_ref, k_ref, v_ref, o_ref, gc_ref, gr_ref, gn_ref = refs[:7]
    refs = refs[7:]
    if has_state:
        c0_ref, n0_ref, m0_ref = refs[:3]
        refs = refs[3:]
    hs_ref = refs[0]
    refs = refs[1:]
    if emit_state:
        cf_ref, nst_ref, mst_ref = refs[:3]
        refs = refs[3:]
    cx_s, m_s, h_s = refs

    L = MLSTM_L
    seq = k_ref.shape[0]
    nc = seq // L
    npair = H_C // 2
    lane = lax.broadcasted_iota(jnp.int32, (1, LANES), 1)
    head_mask = [lane < DK_C, lane >= DK_C]
    ri = lax.broadcasted_iota(jnp.int32, (L, L), 0)
    ci = lax.broadcasted_iota(jnp.int32, (L, L), 1)
    causal = [ri <= ci, ri >= ci]
    ones_blk = jnp.ones((N_ROWS_N, L), BF16)
    chains = [(j, d, hh) for d in range(2) for j in range(npair) for hh in range(2)]

    h_s[...] = jnp.zeros_like(h_s)
    for j, d, hh in chains:
        ch = _chain(j, d, hh)
        if has_state:
            zpad = jnp.zeros((DK_C, DV_C), F32)
            c0 = c0_ref[0, d, 2 * j + hh]
            c0 = jnp.concatenate([c0, zpad] if hh == 0 else [zpad, c0], axis=0)
            cx_s[ch, 0:DV_C, :] = jnp.transpose(c0)
            n_row = jnp.where(head_mask[hh], n0_ref[0, d, j], 0.0)
            cx_s[ch, DV_C:DV_C + N_ROWS_N, :] = jnp.broadcast_to(n_row, (N_ROWS_N, LANES))
            m_s[ch] = m0_ref[0, d, 2 * j + hh]
        else:
            cx_s[ch] = jnp.zeros((DV_C + N_ROWS_N, LANES), F32)
            m_s[ch] = jnp.zeros((1, LANES), F32)

    def chunk_step(i, carry):
        sl = [pl.ds(pl.multiple_of(i * L, L), L), pl.ds(pl.multiple_of((nc - 1 - i) * L, L), L)]
        gcol = [gc_ref[sl[d], :] for d in range(2)]
        grow = [gr_ref[:, sl[d]] for d in range(2)]
        kpair = {(d, j): k_ref[sl[d], j * LANES:(j + 1) * LANES] for d in range(2) for j in range(npair)}
        kbf = {key: kk.astype(BF16) for key, kk in kpair.items()}
        zq = jnp.zeros((DK_C, L), BF16)
        qt = {}
        for d in range(2):
            for j in range(npair):
                qt[(j, d, 0)] = jnp.concatenate([q_ref[j * LANES:j * LANES + DK_C, sl[d]], zq], axis=0)
                qt[(j, d, 1)] = jnp.concatenate([zq, q_ref[j * LANES + DK_C:(j + 1) * LANES, sl[d]]], axis=0)
        st, u_bc, vt, row = {}, {}, {}, {}
        for d in range(2):
            for j in range(npair):
                pair_scores = _dot(kbf[(d, j)], jnp.concatenate([qt[(j, d, 0)], qt[(j, d, 1)]], axis=1))
                st[(j, d, 0)], st[(j, d, 1)] = pair_scores[:, 0:L], pair_scores[:, L:2 * L]
        for j, d, hh in chains:
            key = (j, d, hh)
            idx = 4 * j + 2 * d + hh
            edge = L - 1 if d == 0 else 0
            b_row = grow[d][idx:idx + 1, :]
            cm_row = grow[d][N_GATE + idx:N_GATE + idx + 1, :]
            u_col = gcol[d][:, 2 * N_GATE + idx:2 * N_GATE + idx + 1]
            m_prev = m_s[_chain(j, d, hh)][:, 0:1]
            row[key] = (b_row, cm_row, b_row[:, edge:edge + 1], cm_row[:, edge:edge + 1], m_prev)
            u_bc[key] = jnp.broadcast_to(u_col, (L, L))
            vt[key] = v_ref[(2 * j + hh) * DV_C:(2 * j + hh + 1) * DV_C, sl[d]]
        for j, d, hh in chains:
            key = (j, d, hh)
            ch = _chain(j, d, hh)
            b_row, cm_row, g_tot, cm_last, m_prev = row[key]
            head = slice((2 * j + hh) * DV_C, (2 * j + hh + 1) * DV_C)
            cx = cx_s[ch]
            big_m = jnp.maximum(m_prev, cm_row)
            s = st[key] * jnp.exp(jnp.where(causal[d], u_bc[key] - big_m, -jnp.inf))
            inter = jnp.exp(m_prev - big_m)
            vx = jnp.concatenate([vt[key], ones_blk], axis=0)
            lhs = jnp.concatenate([vx, cx.astype(BF16)], axis=1)
            rhs = jnp.concatenate([s.astype(BF16), (inter * qt[key].astype(F32)).astype(BF16)], axis=0)
            res = _dot(lhs, rhs)
            den = res[DV_C:DV_C + 1, :]
            hval = res[0:DV_C, :] * (1.0 / jnp.maximum(jnp.abs(den), jnp.exp(-(b_row + big_m))))
            h_s[head, sl[d]] = h_s[head, sl[d]] + hval

            m_top = jnp.maximum(m_prev, cm_last)
            kw = kpair[(d, j)] * jnp.exp(u_bc[key] - m_top)
            cx_s[ch] = jnp.exp(m_prev - m_top) * cx + _dot(vx, kw.astype(BF16))
            m_s[ch] = jnp.broadcast_to(g_tot + m_top, (1, LANES))
        return carry

    lax.fori_loop(0, nc, chunk_step, 0, unroll=min(nc, MLSTM_UNROLL))

    gn_col = jnp.broadcast_to(gn_ref[...], (DV_C, LANES))
    gn_col = jnp.concatenate([gn_col] * (seq // LANES), axis=1)
    for hd in range(H_C):
        sl = slice(hd * DV_C, (hd + 1) * DV_C)
        ht = h_s[sl, :]
        ms = jnp.mean(ht * ht, axis=0, keepdims=True)
        y = jnp.transpose((ht * lax.rsqrt(ms + RMS_EPS)) * gn_col) * jax.nn.sigmoid(o_ref[:, sl])
        hs_ref[:, sl] = y.astype(BF16)
    if emit_state:
        for j, d, hh in chains:
            ch = _chain(j, d, hh)
            cf_ref[0, 0, d, 2 * j + hh] = jnp.transpose(cx_s[ch, 0:DV_C, :])[hh * DK_C:(hh + 1) * DK_C, :]
            mst_ref[0, d * H_C + 2 * j + hh:d * H_C + 2 * j + hh + 1, :] = m_s[ch]
        for j in range(npair):
            for d in range(2):
                nst_ref[0, d * npair + j:d * npair + j + 1, :] = jnp.where(
                    head_mask[0], cx_s[_chain(j, d, 0), DV_C:DV_C + 1, :], cx_s[_chain(j, d, 1), DV_C:DV_C + 1, :])


def _mlstm(q, k, v, o, gc, gr, gn, state, seq, emit_state):
    n = k.shape[0]
    nb = n // seq
    npair = H_C // 2
    has_state = state is not None
    hk, hv = H_C * DK_C, H_C * DV_C
    in_specs = [
        pl.BlockSpec((hk, seq), lambda b: (0, b)),
        pl.BlockSpec((seq, hk), lambda b: (b, 0)),
        pl.BlockSpec((hv, seq), lambda b: (0, b)),
        pl.BlockSpec((seq, hv), lambda b: (b, 0)),
        pl.BlockSpec((seq, LANES), lambda b: (b, 0)),
        pl.BlockSpec((3 * N_GATE, seq), lambda b: (0, b)),
        pl.BlockSpec((DV_C, 1), lambda b: (0, 0)),
    ]
    args = [q, k, v, o, gc, gr, gn]
    if has_state:
        c0, n0, m0 = state
        in_specs += [pl.BlockSpec((1, 2, H_C, DK_C, DV_C), lambda b: (b, 0, 0, 0, 0)),
                     pl.BlockSpec((1, 2, npair, 1, LANES), lambda b: (b, 0, 0, 0, 0)),
                     pl.BlockSpec((1, 2, H_C, 1, LANES), lambda b: (b, 0, 0, 0, 0))]
        args += [c0, n0, m0]
    out_specs = [pl.BlockSpec((seq, hv), lambda b: (b, 0))]
    out_shape = [jax.ShapeDtypeStruct((n, hv), BF16)]
    if emit_state:
        out_specs += [pl.BlockSpec((1, 1, 2, H_C, DK_C, DV_C), lambda b: (b, 0, 0, 0, 0, 0)),
                      pl.BlockSpec((1, 2 * npair, LANES), lambda b: (b, 0, 0)),
                      pl.BlockSpec((1, 2 * H_C, LANES), lambda b: (b, 0, 0))]
        out_shape += [jax.ShapeDtypeStruct((nb, 1, 2, H_C, DK_C, DV_C), F32),
                      jax.ShapeDtypeStruct((nb, 2 * npair, LANES), F32),
                      jax.ShapeDtypeStruct((nb, 2 * H_C, LANES), F32)]
    n_chain = 2 * H_C
    return pl.pallas_call(
        functools.partial(_mlstm_kernel, has_state=has_state, emit_state=emit_state),
        grid=(nb,),
        in_specs=in_specs, out_specs=out_specs, out_shape=out_shape,
        scratch_shapes=[pltpu.VMEM((n_chain, DV_C + N_ROWS_N, LANES), F32), pltpu.VMEM((n_chain, 1, LANES), F32),
                        pltpu.VMEM((hv, seq), F32)],
        compiler_params=_params(("arbitrary",)),
        name="mlstm_lat" if has_state else "mlstm_ctx",
    )(*args)


def _rope_tables(n_tok):
    t = np.arange(n_tok)
    rows = (t // GRID_W).astype(np.float64)
    cols = (t % GRID_W).astype(np.float64)

    def axis_tabs(width, lane0):
        half = width // 2
        quarter = half // 2
        freqs = np.power(ROPE_BASE, -np.arange(quarter, dtype=np.float64) / quarter)
        c = np.ones((n_tok, LANES))
        sa = np.zeros((n_tok, LANES))
        sb = np.zeros((n_tok, LANES))
        for g, pos in enumerate((rows, cols)):
            ang = pos[:, None] * freqs[None, :]
            a0 = lane0 + g * half
            c[:, a0:a0 + quarter] = np.cos(ang)
            c[:, a0 + quarter:a0 + half] = np.cos(ang)
            sa[:, a0:a0 + quarter] = -np.sin(ang)
            sb[:, a0 + quarter:a0 + half] = np.sin(ang)
        return c, sa, sb

    cm, sam, sbm = axis_tabs(QK_ROPE, ROPE_LANE0)
    c0, sa0, sb0 = axis_tabs(DIFF_HD, 0)
    c1, sa1, sb1 = axis_tabs(DIFF_HD, DIFF_HD)
    cd = np.where(np.arange(LANES)[None, :] < DIFF_HD, c0, c1)
    return tuple(jnp.asarray(a, F32) for a in (cm, sam, sbm, cd, sa0 + sa1, sb0 + sb1))


def _prep_even(w_in_ab, w_uq, w_ukv):
    c2 = Q_LORA + KV_LORA
    w_ab = w_in_ab.astype(BF16)
    wd = w_ab[:, c2 + QK_ROPE:]
    wq = jnp.pad(w_uq.reshape(Q_LORA, H_A, QK_NOPE + QK_ROPE),
                 ((0, 0), (0, 0), (0, HEAD_PAD - QK_NOPE - QK_ROPE))).reshape(Q_LORA, H_A * HEAD_PAD)
    kvw = w_ukv.reshape(KV_LORA, H_A, QK_NOPE + V_HD_A)
    kpad = jnp.pad(kvw[..., :QK_NOPE], ((0, 0), (0, 0), (0, HEAD_PAD - QK_NOPE)))
    vw = kvw[..., QK_NOPE:]
    zv = jnp.zeros_like(vw)
    odd = (jnp.arange(H_A) % 2 == 1)[None, :, None]
    vpad = jnp.where(odd, jnp.concatenate([zv, vw], -1), jnp.concatenate([vw, zv], -1))
    wukv = jnp.concatenate([kpad.reshape(KV_LORA, -1), vpad.reshape(KV_LORA, -1)], axis=1)
    return (w_ab, wd), wq.astype(BF16), wukv.astype(BF16)


def _gate_order(g):
    lead = g.shape[:-1]
    g = g.reshape(lead + (2, 2, H_C // 2, 2))
    perm = tuple(range(len(lead))) + tuple(len(lead) + a for a in (1, 2, 0, 3))
    return g.transpose(perm).reshape(lead + (4 * H_C,))


def _prep_odd(w_in_c, b_gate_c):
    ng = 4 * H_C
    base = w_in_c.shape[1] - ng
    w1 = w_in_c.astype(BF16)
    wg = jnp.pad(_gate_order(w1[:, base:]), ((0, 0), (0, LANES - ng)))
    bg = jnp.pad(_gate_order(b_gate_c), (0, LANES - ng)).reshape(1, LANES)
    return w1, wg, bg


def kernel(x_prompt, x_sample, cache_mla_ckv, cache_mla_krope, cache_diff_k, cache_diff_v,
           state_mlstm_C, state_mlstm_n, state_mlstm_m, c, c_ctx,
           w_ada, b_ada, g_mix, g_ffn, w_ffn_in, w_ffn_out,
           w_in_ab, g_q_lora, g_kv_lora, w_uq, w_ukv, diff_lambda, g_diff_subln, w_out_ab,
           w_in_c, b_gate_c, g_mlstm, w_out_c, g_final):
    nbp, seq_p, _ = x_prompt.shape
    nbs, seq_s, _ = x_sample.shape
    past = cache_mla_ckv.shape[2]
    assert DEPTH == 2 and 1 + nbs <= COND_ROWS
    assert cache_mla_ckv.shape[1] == 1 and state_mlstm_C.shape[1] == 1

    cond = jnp.concatenate([c_ctx[None], c, jnp.zeros((COND_ROWS - 1 - nbs, D_MODEL), F32)], axis=0)
    mod = _ada(cond, w_ada, b_ada).reshape(DEPTH * COND_ROWS, N_MOD, D_MODEL)

    xp = x_prompt.reshape(nbp * seq_p, D_MODEL)
    xs = x_sample.reshape(nbs * seq_s, D_MODEL)
    row2 = lambda v: v.reshape(1, -1)

    def mod_rows(layer, tm):
        prompt = lambda i: layer * COND_ROWS
        sample = lambda i: layer * COND_ROWS + 1 + i // (seq_s // tm)
        return prompt, sample

    lam_init = 0.8 - 0.6 * math.exp(-0.3 * 0)
    w0, wq, wukv = _prep_even(w_in_ab[0], w_uq[0], w_ukv[0])
    rope_tabs = _rope_tables(seq_s)
    mrp, _ = mod_rows(0, TM_PROJ)
    _, mrs = mod_rows(0, TM_PROJ_ROPE)
    gq, gkv = row2(g_q_lora[0]), row2(g_kv_lora[0])
    outs_p = _proj0(xp, mod, mrp, row2(g_mix[0]), *w0, gq, gkv, wq, wukv, None, seq_p)
    outs_s = _proj0(xs, mod, mrs, row2(g_mix[0]), *w0, gq, gkv, wq, wukv, rope_tabs, seq_s)
    qm_p, km_p, vm_p, dq_p, dk_p, dv_p, ckv_new, kr_new, dk_new, dv_new = outs_p
    qm_s, km_s, vm_s, dq_s, dk_s, dv_s = outs_s
    gsub = row2(g_diff_subln[0])
    krb_c = jnp.pad(cache_mla_krope.reshape(nbs, past, QK_ROPE),
                    ((0, 0), (0, 0), (ROPE_LANE0, LANES - ROPE_LANE0 - QK_ROPE)))
    ctx = (cache_mla_ckv.reshape(nbs, past, KV_LORA), krb_c,
           cache_diff_k.reshape(nbs, H_B, past, 2 * DIFF_HD), cache_diff_v.reshape(nbs, H_B, past, DIFF_VD))
    a_p = _attn0(qm_p, km_p, vm_p, dq_p, dk_p, dv_p, diff_lambda[0], gsub, None, None, seq_p, lam_init)
    a_s = _attn0(qm_s, km_s, vm_s, dq_s, dk_s, dv_s, diff_lambda[0], gsub, ctx, wukv, seq_s, lam_init)
    mtp, mts = mod_rows(0, TM_TAIL)
    wo0 = w_out_ab[0].astype(BF16)
    win, wout = w_ffn_in.astype(BF16), w_ffn_out.astype(BF16)
    xp = _tail(xp, a_p, mod, mtp, wo0, row2(g_ffn[0]), win, wout, 0, None, "tail0_ctx")
    xs = _tail(xs, a_s, mod, mts, wo0, row2(g_ffn[0]), win, wout, 0, None, "tail0_lat")

    w1, wg, bg = _prep_odd(w_in_c[0], b_gate_c[0])
    mrp, mrs = mod_rows(1, TM_PROJ1)
    gn = g_mlstm[0].reshape(DV_C, 1)
    q_p, k_p, v_p, o_p, gc_p, gr_p = _proj1(xp, mod, mrp, row2(g_mix[1]), w1, wg, bg)
    q_s, k_s, v_s, o_s, gc_s, gr_s = _proj1(xs, mod, mrs, row2(g_mix[1]), w1, wg, bg)
    hs_p, c_new, nst, mst = _mlstm(q_p, k_p, v_p, o_p, gc_p, gr_p, gn, None, seq_p, True)
    state = (state_mlstm_C.reshape(nbs, 2, H_C, DK_C, DV_C),
             state_mlstm_n.reshape(nbs, 2, H_C // 2, 1, LANES),
             jnp.broadcast_to(state_mlstm_m.reshape(nbs, 2, H_C, 1, 1), (nbs, 2, H_C, 1, LANES)))
    (hs_s,) = _mlstm(q_s, k_s, v_s, o_s, gc_s, gr_s, gn, state, seq_s, False)
    mtp, mts = mod_rows(1, TM_TAIL)
    wo1 = w_out_c[0].astype(BF16)
    gfin = row2(g_final)
    yp = _tail(xp, hs_p, mod, mtp, wo1, row2(g_ffn[1]), win, wout, 1, gfin, "tail1_ctx")
    ys = _tail(xs, hs_s, mod, mts, wo1, row2(g_ffn[1]), win, wout, 1, gfin, "tail1_lat")

    return (yp.reshape(nbp, seq_p, D_MODEL), ys.reshape(nbs, seq_s, D_MODEL),
            ckv_new.reshape(nbp, 1, seq_p, KV_LORA), kr_new.reshape(nbp, 1, seq_p, QK_ROPE),
            dk_new.reshape(nbp, 1, H_B, seq_p, 2 * DIFF_HD), dv_new.reshape(nbp, 1, H_B, seq_p, DIFF_VD),
            c_new, nst.reshape(nbp, 1, 2, H_C, DK_C), mst[:, :, 0].reshape(nbp, 1, 2, H_C))
```

```python
import functools
import math

import jax
import jax.numpy as jnp
import numpy as np
from jax import lax
from jax.experimental import pallas as pl
from jax.experimental.pallas import tpu as pltpu

F32 = jnp.float32
BF16 = jnp.bfloat16

D_MODEL = 1024
DEPTH = 2
GRID_W = 64
ROPE_BASE = 10000.0
RMS_EPS = 1e-6
H_A = 8
QK_NOPE = 64
QK_ROPE = 32
V_HD_A = 64
Q_LORA = 256
KV_LORA = 128
H_B = 4
DIFF_HD = 64
DIFF_VD = 2 * DIFF_HD
H_C = 8
DK_C = 64
DV_C = D_MODEL // H_C
D_FF = -(-8 * D_MODEL // (3 * 256)) * 256

LANES = 128
HEAD_PAD = LANES
ROPE_LANE0 = QK_NOPE
W_MLA = H_A * HEAD_PAD
W_DIFF = H_B * LANES
N_MOD = 6
COND_ROWS = 16
N_GATE = 2 * H_C
N_ROWS_N = 16

TM_PROJ = 1024
TM_PROJ_ROPE = 512
TM_PROJ1 = 1024
TQ_ATTN = 512
ATTN_SEQS_PER_STEP = 4
ATTN_KEY_BLOCK = 256
TM_TAIL = 1024
FF_CHUNK = 256
MLSTM_L = 128
MLSTM_UNROLL = 4
ADA_TN = 1024
VMEM_LIMIT = 56 * 1024 * 1024

assert D_FF % LANES == 0 and FF_CHUNK % LANES == 0


def _dot(a, b):
    return jnp.dot(a, b, preferred_element_type=F32)


def _dot_nt(a, b):
    return lax.dot_general(a, b, (((1,), (1,)), ((), ())), preferred_element_type=F32)


def _rms(x):
    return x * lax.rsqrt(jnp.mean(x * x, axis=-1, keepdims=True) + RMS_EPS)


def _modulate(x, g, shift, scale):
    return _rms(x) * (g * (1.0 + scale)) + shift


def _params(semantics):
    return pltpu.CompilerParams(dimension_semantics=semantics, vmem_limit_bytes=VMEM_LIMIT)


def _ada_kernel(cond_ref, w_ref, b_ref, o_ref):
    c = cond_ref[...]
    s = (c * jax.nn.sigmoid(c)).astype(BF16)
    o_ref[0] = _dot(s, w_ref[0].astype(BF16)) + b_ref[0]


def _ada(cond, w_ada, b_ada):
    n_out = w_ada.shape[-1]
    return pl.pallas_call(
        _ada_kernel,
        grid=(DEPTH, n_out // ADA_TN),
        in_specs=[
            pl.BlockSpec((COND_ROWS, D_MODEL), lambda l, n: (0, 0)),
            pl.BlockSpec((1, D_MODEL, ADA_TN), lambda l, n: (l, 0, n)),
            pl.BlockSpec((1, 1, ADA_TN), lambda l, n: (l, 0, n)),
        ],
        out_specs=pl.BlockSpec((1, COND_ROWS, ADA_TN), lambda l, n: (l, 0, n)),
        out_shape=jax.ShapeDtypeStruct((DEPTH, COND_ROWS, n_out), F32),
        compiler_params=_params(("arbitrary", "arbitrary")),
        name="ada",
    )(cond, w_ada, b_ada.reshape(DEPTH, 1, n_out))


def _rope(x, c, sa, sb, off):
    return x * c + pltpu.roll(x, LANES - off, 1) * sa + pltpu.roll(x, off, 1) * sb


def _proj0_kernel(*refs, rope, emit_cache):
    x_ref, mod_ref, g_ref, wa_ref, wd_ref, gq_ref, gkv_ref, wq_ref, wukv_ref = refs[:9]
    refs = refs[9:]
    if rope:
        cm_ref, sam_ref, sbm_ref, cd_ref, sad_ref, sbd_ref = refs[:6]
        refs = refs[6:]
    qm_ref, km_ref, vm_ref, dq_ref, dk_ref, dv_ref = refs[:6]
    refs = refs[6:]
    if emit_cache:
        ckvf_ref, krf_ref, dkf_ref, dvf_ref = refs

    h = _modulate(x_ref[...], g_ref[...], mod_ref[0, 0:1, :], mod_ref[0, 1:2, :]).astype(BF16)
    pa = _dot(h, wa_ref[...])
    pd = _dot(h, wd_ref[...])
    lane = lax.broadcasted_iota(jnp.int32, (pa.shape[0], LANES), 1)
    krb = jnp.where((lane >= ROPE_LANE0) & (lane < ROPE_LANE0 + QK_ROPE),
                    pltpu.roll(pa[:, Q_LORA + KV_LORA:Q_LORA + KV_LORA + LANES], ROPE_LANE0, 1), 0.0)
    cq = pa[:, 0:Q_LORA]
    ckv = _rms(pa[:, Q_LORA:Q_LORA + KV_LORA]) * gkv_ref[...]
    qa = _dot((_rms(cq) * gq_ref[...]).astype(BF16), wq_ref[...])
    kv = _dot(ckv.astype(BF16), wukv_ref[...])
    if emit_cache:
        ckvf_ref[...] = ckv
        krf_ref[...] = krb[:, ROPE_LANE0:ROPE_LANE0 + QK_ROPE]
    if rope:
        cm, sam, sbm = cm_ref[...], sam_ref[...], sbm_ref[...]
        cd, sad, sbd = cd_ref[...], sad_ref[...], sbd_ref[...]
        krb = _rope(krb, cm, sam, sbm, QK_ROPE // 4)
    for hd in range(H_A):
        sl = slice(hd * HEAD_PAD, (hd + 1) * HEAD_PAD)
        qh = qa[:, sl]
        if rope:
            qh = _rope(qh, cm, sam, sbm, QK_ROPE // 4)
        qm_ref[:, sl] = qh.astype(BF16)
        km_ref[:, sl] = (kv[:, sl] + krb).astype(BF16)
    vm_ref[...] = kv[:, H_A * HEAD_PAD:].astype(BF16)
    for hd in range(H_B):
        sl = slice(hd * LANES, (hd + 1) * LANES)
        dq = pd[:, hd * LANES:(hd + 1) * LANES]
        dk = pd[:, (H_B + hd) * LANES:(H_B + hd + 1) * LANES]
        dv = pd[:, (2 * H_B + hd) * LANES:(2 * H_B + hd + 1) * LANES]
        if emit_cache:
            seq = dkf_ref.shape[2]
            for bi in range(dkf_ref.shape[0]):
                dkf_ref[bi, hd] = dk[bi * seq:(bi + 1) * seq]
                dvf_ref[bi, hd] = dv[bi * seq:(bi + 1) * seq]
        if rope:
            dq = _rope(dq, cd, sad, sbd, DIFF_HD // 4)
            dk = _rope(dk, cd, sad, sbd, DIFF_HD // 4)
        dq_ref[:, sl] = (dq * (DIFF_HD ** -0.5)).astype(BF16)
        dk_ref[:, sl] = dk.astype(BF16)
        dv_ref[:, sl] = dv.astype(BF16)


def _proj0(x, mod, mod_row, g, w_ab, wd, gq, gkv, wq, wukv, rope_tabs, seq):
    n = x.shape[0]
    rope = rope_tabs is not None
    emit_cache = not rope
    tm = TM_PROJ_ROPE if rope else TM_PROJ
    tiles_per_seq = max(seq // tm, 1)
    full = lambda shape: pl.BlockSpec(shape, lambda i: (0,) * len(shape))
    in_specs = [
        pl.BlockSpec((tm, D_MODEL), lambda i: (i, 0)),
        pl.BlockSpec((1, N_MOD, D_MODEL), lambda i: (mod_row(i), 0, 0)),
        full((1, D_MODEL)), full((D_MODEL, Q_LORA + KV_LORA + LANES)), full(wd.shape),
        full((1, Q_LORA)), full((1, KV_LORA)), full(wq.shape), full(wukv.shape),
    ]
    args = [x, mod, g, w_ab, wd, gq, gkv, wq, wukv]
    if rope:
        in_specs += [pl.BlockSpec((tm, LANES), lambda i: (i % tiles_per_seq, 0))] * 6
        args += list(rope_tabs)
    row = lambda w: pl.BlockSpec((tm, w), lambda i: (i, 0))
    widths = (W_MLA, W_MLA, W_MLA, W_DIFF, W_DIFF, W_DIFF)
    out_specs = [row(w) for w in widths]
    out_shape = [jax.ShapeDtypeStruct((n, w), BF16) for w in widths]
    if emit_cache:
        assert tm % seq == 0
        cache = pl.BlockSpec((tm // seq, H_B, seq, LANES), lambda i: (i, 0, 0, 0))
        out_specs += [row(KV_LORA), row(QK_ROPE), cache, cache]
        out_shape += [jax.ShapeDtypeStruct((n, KV_LORA), F32), jax.ShapeDtypeStruct((n, QK_ROPE), F32),
                      jax.ShapeDtypeStruct((n // seq, H_B, seq, LANES), F32),
                      jax.ShapeDtypeStruct((n // seq, H_B, seq, LANES), F32)]
    return pl.pallas_call(
        functools.partial(_proj0_kernel, rope=rope, emit_cache=emit_cache),
        grid=(n // tm,),
        in_specs=in_specs, out_specs=out_specs, out_shape=out_shape,
        compiler_params=_params(("arbitrary",)),
        name="proj0_rope" if rope else "proj0_ctx",
    )(*args)


def _attn_kernel(*refs, has_ctx, lam_init, n_seq):
    q_ref, dq_ref, k_ref, v_ref, dk_ref, dv_ref, lamv_ref, gsub_ref = refs[:8]
    refs = refs[8:]
    if has_ctx:
        ckvc_ref, krc_ref, dkc_ref, dvc_ref, wukv_ref = refs[:5]
        o_ref, vx_s, dvx_s, kx_s, dkx_s = refs[5:]
    else:
        o_ref, vx_s, dvx_s = refs
    seq = k_ref.shape[0] // n_seq
    ktot = vx_s.shape[0] // n_seq
    assert n_seq == 1 or not has_ctx

    @pl.when(pl.program_id(1) == 0)
    def _():
        if has_ctx:
            kx_s[0:seq, :] = k_ref[...]
            dkx_s[0:seq, :] = dk_ref[...]
        own = n_seq * seq
        ones_own = jnp.ones((own, LANES), BF16)
        for hd in range(H_A):
            vx_s[0:own, 2 * hd * LANES:(2 * hd + 1) * LANES] = v_ref[:, hd * LANES:(hd + 1) * LANES]
            vx_s[0:own, (2 * hd + 1) * LANES:(2 * hd + 2) * LANES] = ones_own
        for hd in range(H_B):
            dvx_s[0:own, 2 * hd * LANES:(2 * hd + 1) * LANES] = dv_ref[:, hd * LANES:(hd + 1) * LANES]
            dvx_s[0:own, (2 * hd + 1) * LANES:(2 * hd + 2) * LANES] = ones_own
        if has_ctx:
            kv = _dot(ckvc_ref[0].astype(BF16), wukv_ref[...])
            krb = krc_ref[0]
            ones_ctx = jnp.ones((ktot - seq, LANES), BF16)
            for hd in range(H_A):
                sl = slice(hd * HEAD_PAD, (hd + 1) * HEAD_PAD)
                kx_s[seq:ktot, sl] = (kv[:, sl] + krb).astype(BF16)
                vx_s[seq:ktot, 2 * hd * LANES:(2 * hd + 1) * LANES] = (
                    kv[:, (H_A + hd) * HEAD_PAD:(H_A + hd + 1) * HEAD_PAD].astype(BF16))
                vx_s[seq:ktot, (2 * hd + 1) * LANES:(2 * hd + 2) * LANES] = ones_ctx
            for hd in range(H_B):
                sl = slice(hd * LANES, (hd + 1) * LANES)
                dkx_s[seq:ktot, sl] = dkc_ref[0, hd].astype(BF16)
                dvx_s[seq:ktot, 2 * hd * LANES:(2 * hd + 1) * LANES] = dvc_ref[0, hd].astype(BF16)
                dvx_s[seq:ktot, (2 * hd + 1) * LANES:(2 * hd + 2) * LANES] = ones_ctx

    log2e = 1.0 / math.log(2.0)
    tq = dq_ref.shape[0] // n_seq
    lo = lax.broadcasted_iota(jnp.int32, (tq, LANES), 1) < DIFF_HD

    keys_a, keys_b = (kx_s, dkx_s) if has_ctx else (k_ref, dk_ref)
    jobs = []
    for si in range(n_seq):
        qr = slice(si * tq, (si + 1) * tq)
        kr = slice(si * ktot, (si + 1) * ktot)
        for hd in range(H_A):
            sl = slice(hd * HEAD_PAD, (hd + 1) * HEAD_PAD)
            jobs.append((lambda qr=qr, sl=sl: q_ref[qr, sl], (keys_a, kr, sl), (vx_s, hd),
                         (QK_NOPE + QK_ROPE) ** -0.5 * log2e))
        for hd in range(H_B):
            sl = slice(hd * LANES, (hd + 1) * LANES)
            for part in range(2):
                def qfn(qr=qr, sl=sl, part=part):
                    dq = dq_ref[qr, sl].astype(F32)
                    return (jnp.where(lo, dq, 0.0) if part == 0 else jnp.where(lo, 0.0, dq)).astype(BF16)
                jobs.append((qfn, (keys_b, kr, sl), (dvx_s, hd), log2e))

    items = [(ji, k0) for ji in range(len(jobs)) for k0 in range(0, ktot, ATTN_KEY_BLOCK)]
    queries = {}

    def scores(item):
        ji, k0 = item
        qfn, (kref, kr, sl), _, _ = jobs[ji]
        if ji not in queries:
            queries[ji] = qfn()
        k1 = min(k0 + ATTN_KEY_BLOCK, ktot)
        return _dot_nt(queries[ji], kref[kr.start + k0:kr.start + k1, sl])

    def absorb(s, item, state):
        ji, k0 = item
        _, (_, kr, _), (vref, hd), c = jobs[ji]
        k1 = min(k0 + ATTN_KEY_BLOCK, ktot)
        vblk = vref[kr.start + k0:kr.start + k1, 2 * hd * LANES:(2 * hd + 2) * LANES]
        blk_max = jnp.max(s, axis=-1, keepdims=True)
        if state is None:
            m = blk_max
            return m, _dot(jnp.exp2((s - m) * c).astype(BF16), vblk)
        m_old, acc = state
        m = jnp.maximum(m_old, blk_max)
        e = jnp.exp2((s - m) * c).astype(BF16)
        return m, acc * jnp.exp2((m_old - m) * c) + _dot(e, vblk)

    outs = []
    state = None
    s_next = scores(items[0])
    for i, item in enumerate(items):
        s_cur = s_next
        if i + 1 < len(items):
            s_next = scores(items[i + 1])
        state = absorb(s_cur, item, state)
        if i + 1 == len(items) or items[i + 1][0] != item[0]:
            res = state[1]
            outs.append(res[:, 0:LANES] * (1.0 / res[:, LANES:2 * LANES]))
            state = None

    lv = lamv_ref[...]
    lam = (jnp.exp(jnp.sum(lv[0:1] * lv[1:2], axis=-1, keepdims=True))
           - jnp.exp(jnp.sum(lv[2:3] * lv[3:4], axis=-1, keepdims=True)) + lam_init)
    per_seq = H_A + 2 * H_B
    for si in range(n_seq):
        qr = slice(si * tq, (si + 1) * tq)
        o = outs[si * per_seq:(si + 1) * per_seq]
        for j in range(H_A // 2):
            o_ref[qr, j * LANES:(j + 1) * LANES] = (o[2 * j] + o[2 * j + 1]).astype(BF16)
        for hd in range(H_B):
            acc = o[H_A + 2 * hd] - lam * o[H_A + 2 * hd + 1]
            ob = (_rms(acc) * gsub_ref[...]) * (1.0 - lam_init)
            o_ref[qr, H_A * V_HD_A + hd * LANES:H_A * V_HD_A + (hd + 1) * LANES] = ob.astype(BF16)


def _attn0(qm, km, vm, dq, dk, dv, lamv, gsub, ctx, wukv, seq, lam_init):
    n = qm.shape[0]
    nb = n // seq
    tq = min(TQ_ATTN, seq)
    has_ctx = ctx is not None
    n_seq = 1 if has_ctx or tq != seq else ATTN_SEQS_PER_STEP
    assert nb % n_seq == 0
    full = lambda shape: pl.BlockSpec(shape, lambda b, t: (0,) * len(shape))
    qrow = lambda w: pl.BlockSpec((n_seq * tq, w), lambda b, t: (b * (seq // tq) + t, 0))
    krow = lambda w: pl.BlockSpec((n_seq * seq, w), lambda b, t: (b, 0))
    in_specs = [qrow(W_MLA), qrow(W_DIFF), krow(W_MLA), krow(W_MLA), krow(W_DIFF), krow(W_DIFF),
                full((4, DIFF_HD)), full((1, DIFF_VD))]
    args = [qm, dq, km, vm, dk, dv, lamv, gsub]
    past = 0
    if has_ctx:
        ckv_c, krb_c, dk_c, dv_c = ctx
        past = ckv_c.shape[1]
        in_specs += [pl.BlockSpec((1, past, LANES), lambda b, t: (b, 0, 0)),
                     pl.BlockSpec((1, past, LANES), lambda b, t: (b, 0, 0)),
                     pl.BlockSpec((1, H_B, past, LANES), lambda b, t: (b, 0, 0, 0)),
                     pl.BlockSpec((1, H_B, past, LANES), lambda b, t: (b, 0, 0, 0)),
                     full(wukv.shape)]
        args += [ckv_c, krb_c, dk_c, dv_c, wukv]
    ktot = n_seq * (seq + past)
    scratch = [pltpu.VMEM((ktot, 2 * W_MLA), BF16), pltpu.VMEM((ktot, 2 * W_DIFF), BF16)]
    if has_ctx:
        scratch += [pltpu.VMEM((ktot, W_MLA), BF16), pltpu.VMEM((ktot, W_DIFF), BF16)]
    return pl.pallas_call(
        functools.partial(_attn_kernel, has_ctx=has_ctx, lam_init=lam_init, n_seq=n_seq),
        grid=(nb // n_seq, seq // tq),
        in_specs=in_specs,
        out_specs=pl.BlockSpec((n_seq * tq, D_MODEL), lambda b, t: (b * (seq // tq) + t, 0)),
        out_shape=jax.ShapeDtypeStruct((n, D_MODEL), BF16),
        scratch_shapes=scratch,
        compiler_params=_params(("arbitrary", "arbitrary")),
        name="attn0_lat" if has_ctx else "attn0_ctx",
    )(*args)


def _tail_kernel(*refs, final):
    x_ref, a_ref, mod_ref, wo_ref, gf_ref, win_ref, wout_ref = refs[:7]
    if final:
        gfin_ref, o_ref = refs[7:]
    else:
        (o_ref,) = refs[7:]
    x1 = x_ref[...] + mod_ref[0, 2:3, :] * _dot(a_ref[...], wo_ref[...])
    h = _modulate(x1, gf_ref[...], mod_ref[0, 3:4, :], mod_ref[0, 4:5, :]).astype(BF16)
    bounds = list(range(0, D_FF, FF_CHUNK)) + [D_FF]
    acc = None
    for lo, hi in zip(bounds[:-1], bounds[1:]):
        a = _dot(h, win_ref[0, :, lo:hi])
        b = _dot(h, win_ref[0, :, D_FF + lo:D_FF + hi])
        act = ((a * jax.nn.sigmoid(a)) * b).astype(BF16)
        part = _dot(act, wout_ref[0, lo:hi, :])
        acc = part if acc is None else acc + part
    x2 = x1 + mod_ref[0, 5:6, :] * acc
    if final:
        x2 = _rms(x2) * gfin_ref[...]
    o_ref[...] = x2


def _tail(x, a, mod, mod_row, wo, gf, win, wout, layer, gfin, name):
    n = x.shape[0]
    tm = TM_TAIL
    final = gfin is not None
    full = lambda shape: pl.BlockSpec(shape, lambda i: (0,) * len(shape))
    resident = lambda shape: pl.BlockSpec((1,) + shape[1:], lambda i: (layer, 0, 0),
                                          pipeline_mode=pl.Buffered(1))
    in_specs = [
        pl.BlockSpec((tm, D_MODEL), lambda i: (i, 0)),
        pl.BlockSpec((tm, D_MODEL), lambda i: (i, 0)),
        pl.BlockSpec((1, N_MOD, D_MODEL), lambda i: (mod_row(i), 0, 0)),
        full(wo.shape), full((1, D_MODEL)), resident(win.shape), resident(wout.shape),
    ]
    args = [x, a, mod, wo, gf, win, wout]
    if final:
        in_specs.append(full((1, D_MODEL)))
        args.append(gfin)
    return pl.pallas_call(
        functools.partial(_tail_kernel, final=final),
        grid=(n // tm,),
        in_specs=in_specs,
        out_specs=pl.BlockSpec((tm, D_MODEL), lambda i: (i, 0)),
        out_shape=jax.ShapeDtypeStruct((n, D_MODEL), F32),
        compiler_params=_params(("arbitrary",)),
        name=name,
    )(*args)


def _split3(x):
    hi = x.astype(BF16)
    r1 = x - hi.astype(F32)
    mid = r1.astype(BF16)
    lo = (r1 - mid.astype(F32)).astype(BF16)
    return hi, mid, lo


def _proj1_kernel(x_ref, mod_ref, g_ref, w1_ref, wg_ref, bg_ref, q_ref, k_ref, v_ref, o_ref, gc_ref, gr_ref):
    tm = x_ref.shape[0]
    L = MLSTM_L
    hk = H_C * DK_C
    hv = H_C * DV_C
    h = _modulate(x_ref[...], g_ref[...], mod_ref[0, 0:1, :], mod_ref[0, 1:2, :]).astype(BF16)
    gates = _dot(h, wg_ref[...]) + bg_ref[...]
    proj = _dot(h, w1_ref[...])
    for blk in range(hk // LANES):
        sl = slice(blk * LANES, (blk + 1) * LANES)
        q_ref[sl, :] = jnp.transpose(proj[:, sl]).astype(BF16)
    for blk in range(hv // LANES):
        sl = slice(blk * LANES, (blk + 1) * LANES)
        v_ref[sl, :] = jnp.transpose(proj[:, 2 * hk + blk * LANES:2 * hk + (blk + 1) * LANES]).astype(BF16)
    k_ref[...] = proj[:, hk:2 * hk] * (DK_C ** -0.5)
    o_ref[...] = proj[:, 2 * hk + hv:2 * hk + 2 * hv]
    lf = jnp.minimum(gates, 0.0) - jnp.log1p(jnp.exp(-jnp.abs(gates)))
    r = lax.broadcasted_iota(jnp.int32, (L, L), 0)
    c = lax.broadcasted_iota(jnp.int32, (L, L), 1)
    pre = jnp.where(c <= r, 1.0, 0.0).astype(BF16)
    suf = jnp.where(c >= r, 1.0, 0.0).astype(BF16)
    parts = jnp.concatenate(_split3(lf), axis=1)
    fold = lambda t: t[:, 0:LANES] + t[:, LANES:2 * LANES] + t[:, 2 * LANES:3 * LANES]
    chunks = [parts[ck * L:(ck + 1) * L] for ck in range(tm // L)]
    lane = lax.broadcasted_iota(jnp.int32, (tm, LANES), 1)
    b_sum = jnp.where((lane % 4) >= 2,
                      jnp.concatenate([fold(_dot(suf, p)) for p in chunks], axis=0),
                      jnp.concatenate([fold(_dot(pre, p)) for p in chunks], axis=0))
    b = pltpu.roll(b_sum, LANES - N_GATE, 1)
    u = gates - b
    pos = lax.broadcasted_iota(jnp.int32, (tm, LANES), 0) % L
    bwd = (lane % 4) >= 2
    cm = u
    step = 1
    while step < L:
        below = jnp.where(pos >= step, pltpu.roll(cm, step, 0), -jnp.inf)
        above = jnp.where(pos < L - step, pltpu.roll(cm, tm - step, 0), -jnp.inf)
        cm = jnp.maximum(cm, jnp.where(bwd, above, below))
        step *= 2
    low = lane < N_GATE
    packed = (jnp.where(low, b, 0.0) + pltpu.roll(jnp.where(low, cm, 0.0), N_GATE, 1)
              + pltpu.roll(jnp.where(low, u, 0.0), 2 * N_GATE, 1))
    gc_ref[...] = packed
    gr_ref[...] = jnp.transpose(packed)[0:3 * N_GATE]


def _proj1(x, mod, mod_row, g, w1, wg, bg):
    n = x.shape[0]
    tm = TM_PROJ1
    assert tm % MLSTM_L == 0
    full = lambda shape: pl.BlockSpec(shape, lambda i: (0,) * len(shape))
    row = lambda w: pl.BlockSpec((tm, w), lambda i: (i, 0))
    col = lambda w: pl.BlockSpec((w, tm), lambda i: (0, i))
    hk, hv = H_C * DK_C, H_C * DV_C
    return pl.pallas_call(
        _proj1_kernel,
        grid=(n // tm,),
        in_specs=[row(D_MODEL), pl.BlockSpec((1, N_MOD, D_MODEL), lambda i: (mod_row(i), 0, 0)),
                  full((1, D_MODEL)), full((D_MODEL, 2 * hk + 2 * hv)), full(wg.shape), full((1, LANES))],
        out_specs=[col(hk), row(hk), col(hv), row(hv), row(LANES),
                   pl.BlockSpec((3 * N_GATE, tm), lambda i: (0, i))],
        out_shape=[jax.ShapeDtypeStruct((hk, n), BF16), jax.ShapeDtypeStruct((n, hk), F32),
                   jax.ShapeDtypeStruct((hv, n), BF16), jax.ShapeDtypeStruct((n, hv), F32),
                   jax.ShapeDtypeStruct((n, LANES), F32), jax.ShapeDtypeStruct((3 * N_GATE, n), F32)],
        compiler_params=_params(("arbitrary",)),
        name="proj1",
    )(x, mod, g, w1, wg, bg)


def _chain(j, d, hh):
    return (2 * j + d) * 2 + hh


def _mlstm_kernel(*refs, has_state, emit_state):
    q_ref, k_ref, v_ref, o_ref, gc_ref, gr_ref, gn_ref = refs[:7]
    refs = refs[7:]
    if has_state:
        c0_ref, n0_ref, m0_ref = refs[:3]
        refs = refs[3:]
    hs_ref = refs[0]
    refs = refs[1:]
    if emit_state:
        cf_ref, nst_ref, mst_ref = refs[:3]
        refs = refs[3:]
    cx_s, m_s, h_s = refs

    L = MLSTM_L
    seq = k_ref.shape[0]
    nc = seq // L
    npair = H_C // 2
    lane = lax.broadcasted_iota(jnp.int32, (1, LANES), 1)
    head_mask = [lane < DK_C, lane >= DK_C]
    ri = lax.broadcasted_iota(jnp.int32, (L, L), 0)
    ci = lax.broadcasted_iota(jnp.int32, (L, L), 1)
    causal = [ri <= ci, ri >= ci]
    ones_blk = jnp.ones((N_ROWS_N, L), BF16)
    chains = [(j, d, hh) for d in range(2) for j in range(npair) for hh in range(2)]

    h_s[...] = jnp.zeros_like(h_s)
    for j, d, hh in chains:
        ch = _chain(j, d, hh)
        if has_state:
            zpad = jnp.zeros((DK_C, DV_C), F32)
            c0 = c0_ref[0, d, 2 * j + hh]
            c0 = jnp.concatenate([c0, zpad] if hh == 0 else [zpad, c0], axis=0)
            cx_s[ch, 0:DV_C, :] = jnp.transpose(c0)
            n_row = jnp.where(head_mask[hh], n0_ref[0, d, j], 0.0)
            cx_s[ch, DV_C:DV_C + N_ROWS_N, :] = jnp.broadcast_to(n_row, (N_ROWS_N, LANES))
            m_s[ch] = m0_ref[0, d, 2 * j + hh]
        else:
            cx_s[ch] = jnp.zeros((DV_C + N_ROWS_N, LANES), F32)
            m_s[ch] = jnp.zeros((1, LANES), F32)

    def chunk_step(i, carry):
        sl = [pl.ds(pl.multiple_of(i * L, L), L), pl.ds(pl.multiple_of((nc - 1 - i) * L, L), L)]
        gcol = [gc_ref[sl[d], :] for d in range(2)]
        grow = [gr_ref[:, sl[d]] for d in range(2)]
        kpair = {(d, j): k_ref[sl[d], j * LANES:(j + 1) * LANES] for d in range(2) for j in range(npair)}
        kbf = {key: kk.astype(BF16) for key, kk in kpair.items()}
        zq = jnp.zeros((DK_C, L), BF16)
        qt = {}
        for d in range(2):
            for j in range(npair):
                qt[(j, d, 0)] = jnp.concatenate([q_ref[j * LANES:j * LANES + DK_C, sl[d]], zq], axis=0)
                qt[(j, d, 1)] = jnp.concatenate([zq, q_ref[j * LANES + DK_C:(j + 1) * LANES, sl[d]]], axis=0)
        st, u_bc, vt, row = {}, {}, {}, {}
        for d in range(2):
            for j in range(npair):
                pair_scores = _dot(kbf[(d, j)], jnp.concatenate([qt[(j, d, 0)], qt[(j, d, 1)]], axis=1))
                st[(j, d, 0)], st[(j, d, 1)] = pair_scores[:, 0:L], pair_scores[:, L:2 * L]
        for j, d, hh in chains:
            key = (j, d, hh)
            idx = 4 * j + 2 * d + hh
            edge = L - 1 if d == 0 else 0
            b_row = grow[d][idx:idx + 1, :]
            cm_row = grow[d][N_GATE + idx:N_GATE + idx + 1, :]
            u_col = gcol[d][:, 2 * N_GATE + idx:2 * N_GATE + idx + 1]
            m_prev = m_s[_chain(j, d, hh)][:, 0:1]
            row[key] = (b_row, cm_row, b_row[:, edge:edge + 1], cm_row[:, edge:edge + 1], m_prev)
            u_bc[key] = jnp.broadcast_to(u_col, (L, L))
            vt[key] = v_ref[(2 * j + hh) * DV_C:(2 * j + hh + 1) * DV_C, sl[d]]
        for j, d, hh in chains:
            key = (j, d, hh)
            ch = _chain(j, d, hh)
            b_row, cm_row, g_tot, cm_last, m_prev = row[key]
            head = slice((2 * j + hh) * DV_C, (2 * j + hh + 1) * DV_C)
            cx = cx_s[ch]
            big_m = jnp.maximum(m_prev, cm_row)
            s = st[key] * jnp.exp(jnp.where(causal[d], u_bc[key] - big_m, -jnp.inf))
            inter = jnp.exp(m_prev - big_m)
            vx = jnp.concatenate([vt[key], ones_blk], axis=0)
            lhs = jnp.concatenate([vx, cx.astype(BF16)], axis=1)
            rhs = jnp.concatenate([s.astype(BF16), (inter * qt[key].astype(F32)).astype(BF16)], axis=0)
            res = _dot(lhs, rhs)
            den = res[DV_C:DV_C + 1, :]
            hval = res[0:DV_C, :] * (1.0 / jnp.maximum(jnp.abs(den), jnp.exp(-(b_row + big_m))))
            h_s[head, sl[d]] = h_s[head, sl[d]] + hval

            m_top = jnp.maximum(m_prev, cm_last)
            kw = kpair[(d, j)] * jnp.exp(u_bc[key] - m_top)
            cx_s[ch] = jnp.exp(m_prev - m_top) * cx + _dot(vx, kw.astype(BF16))
            m_s[ch] = jnp.broadcast_to(g_tot + m_top, (1, LANES))
        return carry

    lax.fori_loop(0, nc, chunk_step, 0, unroll=min(nc, MLSTM_UNROLL))

    gn_col = jnp.broadcast_to(gn_ref[...], (DV_C, LANES))
    gn_col = jnp.concatenate([gn_col] * (seq // LANES), axis=1)
    for hd in range(H_C):
        sl = slice(hd * DV_C, (hd + 1) * DV_C)
        ht = h_s[sl, :]
        ms = jnp.mean(ht * ht, axis=0, keepdims=True)
        y = jnp.transpose((ht * lax.rsqrt(ms + RMS_EPS)) * gn_col) * jax.nn.sigmoid(o_ref[:, sl])
        hs_ref[:, sl] = y.astype(BF16)
    if emit_state:
        for j, d, hh in chains:
            ch = _chain(j, d, hh)
            cf_ref[0, 0, d, 2 * j + hh] = jnp.transpose(cx_s[ch, 0:DV_C, :])[hh * DK_C:(hh + 1) * DK_C, :]
            mst_ref[0, d * H_C + 2 * j + hh:d * H_C + 2 * j + hh + 1, :] = m_s[ch]
        for j in range(npair):
            for d in range(2):
                nst_ref[0, d * npair + j:d * npair + j + 1, :] = jnp.where(
                    head_mask[0], cx_s[_chain(j, d, 0), DV_C:DV_C + 1, :], cx_s[_chain(j, d, 1), DV_C:DV_C + 1, :])


def _mlstm(q, k, v, o, gc, gr, gn, state, seq, emit_state):
    n = k.shape[0]
    nb = n // seq
    npair = H_C // 2
    has_state = state is not None
    hk, hv = H_C * DK_C, H_C * DV_C
    in_specs = [
        pl.BlockSpec((hk, seq), lambda b: (0, b)),
        pl.BlockSpec((seq, hk), lambda b: (b, 0)),
        pl.BlockSpec((hv, seq), lambda b: (0, b)),
        pl.BlockSpec((seq, hv), lambda b: (b, 0)),
        pl.BlockSpec((seq, LANES), lambda b: (b, 0)),
        pl.BlockSpec((3 * N_GATE, seq), lambda b: (0, b)),
        pl.BlockSpec((DV_C, 1), lambda b: (0, 0)),
    ]
    args = [q, k, v, o, gc, gr, gn]
    if has_state:
        c0, n0, m0 = state
        in_specs += [pl.BlockSpec((1, 2, H_C, DK_C, DV_C), lambda b: (b, 0, 0, 0, 0)),
                     pl.BlockSpec((1, 2, npair, 1, LANES), lambda b: (b, 0, 0, 0, 0)),
                     pl.BlockSpec((1, 2, H_C, 1, LANES), lambda b: (b, 0, 0, 0, 0))]
        args += [c0, n0, m0]
    out_specs = [pl.BlockSpec((seq, hv), lambda b: (b, 0))]
    out_shape = [jax.ShapeDtypeStruct((n, hv), BF16)]
    if emit_state:
        out_specs += [pl.BlockSpec((1, 1, 2, H_C, DK_C, DV_C), lambda b: (b, 0, 0, 0, 0, 0)),
                      pl.BlockSpec((1, 2 * npair, LANES), lambda b: (b, 0, 0)),
                      pl.BlockSpec((1, 2 * H_C, LANES), lambda b: (b, 0, 0))]
        out_shape += [jax.ShapeDtypeStruct((nb, 1, 2, H_C, DK_C, DV_C), F32),
                      jax.ShapeDtypeStruct((nb, 2 * npair, LANES), F32),
                      jax.ShapeDtypeStruct((nb, 2 * H_C, LANES), F32)]
    n_chain = 2 * H_C
    return pl.pallas_call(
        functools.partial(_mlstm_kernel, has_state=has_state, emit_state=emit_state),
        grid=(nb,),
        in_specs=in_specs, out_specs=out_specs, out_shape=out_shape,
        scratch_shapes=[pltpu.VMEM((n_chain, DV_C + N_ROWS_N, LANES), F32), pltpu.VMEM((n_chain, 1, LANES), F32),
                        pltpu.VMEM((hv, seq), F32)],
        compiler_params=_params(("arbitrary",)),
        name="mlstm_lat" if has_state else "mlstm_ctx",
    )(*args)


def _rope_tables(n_tok):
    t = np.arange(n_tok)
    rows = (t // GRID_W).astype(np.float64)
    cols = (t % GRID_W).astype(np.float64)

    def axis_tabs(width, lane0):
        half = width // 2
        quarter = half // 2
        freqs = np.power(ROPE_BASE, -np.arange(quarter, dtype=np.float64) / quarter)
        c = np.ones((n_tok, LANES))
        sa = np.zeros((n_tok, LANES))
        sb = np.zeros((n_tok, LANES))
        for g, pos in enumerate((rows, cols)):
            ang = pos[:, None] * freqs[None, :]
            a0 = lane0 + g * half
            c[:, a0:a0 + quarter] = np.cos(ang)
            c[:, a0 + quarter:a0 + half] = np.cos(ang)
            sa[:, a0:a0 + quarter] = -np.sin(ang)
            sb[:, a0 + quarter:a0 + half] = np.sin(ang)
        return c, sa, sb

    cm, sam, sbm = axis_tabs(QK_ROPE, ROPE_LANE0)
    c0, sa0, sb0 = axis_tabs(DIFF_HD, 0)
    c1, sa1, sb1 = axis_tabs(DIFF_HD, DIFF_HD)
    cd = np.where(np.arange(LANES)[None, :] < DIFF_HD, c0, c1)
    return tuple(jnp.asarray(a, F32) for a in (cm, sam, sbm, cd, sa0 + sa1, sb0 + sb1))


def _prep_even(w_in_ab, w_uq, w_ukv):
    c2 = Q_LORA + KV_LORA
    w_ab = w_in_ab.astype(BF16)
    wd = w_ab[:, c2 + QK_ROPE:]
    wq = jnp.pad(w_uq.reshape(Q_LORA, H_A, QK_NOPE + QK_ROPE),
                 ((0, 0), (0, 0), (0, HEAD_PAD - QK_NOPE - QK_ROPE))).reshape(Q_LORA, H_A * HEAD_PAD)
    kvw = w_ukv.reshape(KV_LORA, H_A, QK_NOPE + V_HD_A)
    kpad = jnp.pad(kvw[..., :QK_NOPE], ((0, 0), (0, 0), (0, HEAD_PAD - QK_NOPE)))
    vw = kvw[..., QK_NOPE:]
    zv = jnp.zeros_like(vw)
    odd = (jnp.arange(H_A) % 2 == 1)[None, :, None]
    vpad = jnp.where(odd, jnp.concatenate([zv, vw], -1), jnp.concatenate([vw, zv], -1))
    wukv = jnp.concatenate([kpad.reshape(KV_LORA, -1), vpad.reshape(KV_LORA, -1)], axis=1)
    return (w_ab, wd), wq.astype(BF16), wukv.astype(BF16)


def _gate_order(g):
    lead = g.shape[:-1]
    g = g.reshape(lead + (2, 2, H_C // 2, 2))
    perm = tuple(range(len(lead))) + tuple(len(lead) + a for a in (1, 2, 0, 3))
    return g.transpose(perm).reshape(lead + (4 * H_C,))


def _prep_odd(w_in_c, b_gate_c):
    ng = 4 * H_C
    base = w_in_c.shape[1] - ng
    w1 = w_in_c.astype(BF16)
    wg = jnp.pad(_gate_order(w1[:, base:]), ((0, 0), (0, LANES - ng)))
    bg = jnp.pad(_gate_order(b_gate_c), (0, LANES - ng)).reshape(1, LANES)
    return w1, wg, bg


def kernel(x_prompt, x_sample, cache_mla_ckv, cache_mla_krope, cache_diff_k, cache_diff_v,
           state_mlstm_C, state_mlstm_n, state_mlstm_m, c, c_ctx,
           w_ada, b_ada, g_mix, g_ffn, w_ffn_in, w_ffn_out,
           w_in_ab, g_q_lora, g_kv_lora, w_uq, w_ukv, diff_lambda, g_diff_subln, w_out_ab,
           w_in_c, b_gate_c, g_mlstm, w_out_c, g_final):
    nbp, seq_p, _ = x_prompt.shape
    nbs, seq_s, _ = x_sample.shape
    past = cache_mla_ckv.shape[2]
    assert DEPTH == 2 and 1 + nbs <= COND_ROWS
    assert cache_mla_ckv.shape[1] == 1 and state_mlstm_C.shape[1] == 1

    cond = jnp.concatenate([c_ctx[None], c, jnp.zeros((COND_ROWS - 1 - nbs, D_MODEL), F32)], axis=0)
    mod = _ada(cond, w_ada, b_ada).reshape(DEPTH * COND_ROWS, N_MOD, D_MODEL)

    xp = x_prompt.reshape(nbp * seq_p, D_MODEL)
    xs = x_sample.reshape(nbs * seq_s, D_MODEL)
    row2 = lambda v: v.reshape(1, -1)

    def mod_rows(layer, tm):
        prompt = lambda i: layer * COND_ROWS
        sample = lambda i: layer * COND_ROWS + 1 + i // (seq_s // tm)
        return prompt, sample

    lam_init = 0.8 - 0.6 * math.exp(-0.3 * 0)
    w0, wq, wukv = _prep_even(w_in_ab[0], w_uq[0], w_ukv[0])
    rope_tabs = _rope_tables(seq_s)
    mrp, _ = mod_rows(0, TM_PROJ)
    _, mrs = mod_rows(0, TM_PROJ_ROPE)
    gq, gkv = row2(g_q_lora[0]), row2(g_kv_lora[0])
    outs_p = _proj0(xp, mod, mrp, row2(g_mix[0]), *w0, gq, gkv, wq, wukv, None, seq_p)
    outs_s = _proj0(xs, mod, mrs, row2(g_mix[0]), *w0, gq, gkv, wq, wukv, rope_tabs, seq_s)
    qm_p, km_p, vm_p, dq_p, dk_p, dv_p, ckv_new, kr_new, dk_new, dv_new = outs_p
    qm_s, km_s, vm_s, dq_s, dk_s, dv_s = outs_s
    gsub = row2(g_diff_subln[0])
    krb_c = jnp.pad(cache_mla_krope.reshape(nbs, past, QK_ROPE),
                    ((0, 0), (0, 0), (ROPE_LANE0, LANES - ROPE_LANE0 - QK_ROPE)))
    ctx = (cache_mla_ckv.reshape(nbs, past, KV_LORA), krb_c,
           cache_diff_k.reshape(nbs, H_B, past, 2 * DIFF_HD), cache_diff_v.reshape(nbs, H_B, past, DIFF_VD))
    a_p = _attn0(qm_p, km_p, vm_p, dq_p, dk_p, dv_p, diff_lambda[0], gsub, None, None, seq_p, lam_init)
    a_s = _attn0(qm_s, km_s, vm_s, dq_s, dk_s, dv_s, diff_lambda[0], gsub, ctx, wukv, seq_s, lam_init)
    mtp, mts = mod_rows(0, TM_TAIL)
    wo0 = w_out_ab[0].astype(BF16)
    win, wout = w_ffn_in.astype(BF16), w_ffn_out.astype(BF16)
    xp = _tail(xp, a_p, mod, mtp, wo0, row2(g_ffn[0]), win, wout, 0, None, "tail0_ctx")
    xs = _tail(xs, a_s, mod, mts, wo0, row2(g_ffn[0]), win, wout, 0, None, "tail0_lat")

    w1, wg, bg = _prep_odd(w_in_c[0], b_gate_c[0])
    mrp, mrs = mod_rows(1, TM_PROJ1)
    gn = g_mlstm[0].reshape(DV_C, 1)
    q_p, k_p, v_p, o_p, gc_p, gr_p = _proj1(xp, mod, mrp, row2(g_mix[1]), w1, wg, bg)
    q_s, k_s, v_s, o_s, gc_s, gr_s = _proj1(xs, mod, mrs, row2(g_mix[1]), w1, wg, bg)
    hs_p, c_new, nst, mst = _mlstm(q_p, k_p, v_p, o_p, gc_p, gr_p, gn, None, seq_p, True)
    state = (state_mlstm_C.reshape(nbs, 2, H_C, DK_C, DV_C),
             state_mlstm_n.reshape(nbs, 2, H_C // 2, 1, LANES),
             jnp.broadcast_to(state_mlstm_m.reshape(nbs, 2, H_C, 1, 1), (nbs, 2, H_C, 1, LANES)))
    (hs_s,) = _mlstm(q_s, k_s, v_s, o_s, gc_s, gr_s, gn, state, seq_s, False)
    mtp, mts = mod_rows(1, TM_TAIL)
    wo1 = w_out_c[0].astype(BF16)
    gfin = row2(g_final)
    yp = _tail(xp, hs_p, mod, mtp, wo1, row2(g_ffn[1]), win, wout, 1, gfin, "tail1_ctx")
    ys = _tail(xs, hs_s, mod, mts, wo1, row2(g_ffn[1]), win, wout, 1, gfin, "tail1_lat")

    return (yp.reshape(nbp, seq_p, D_MODEL), ys.reshape(nbs, seq_s, D_MODEL),
            ckv_new.reshape(nbp, 1, seq_p, KV_LORA), kr_new.reshape(nbp, 1, seq_p, QK_ROPE),
            dk_new.reshape(nbp, 1, H_B, seq_p, 2 * DIFF_HD), dv_new.reshape(nbp, 1, H_B, seq_p, DIFF_VD),
            c_new, nst.reshape(nbp, 1, 2, H_C, DK_C), mst[:, :, 0].reshape(nbp, 1, 2, H_C))
```

```python
import functools
import math

import jax
import jax.numpy as jnp
import numpy as np
from jax import lax
from jax.experimental import pallas as pl
from jax.experimental.pallas import tpu as pltpu

F32 = jnp.float32
BF16 = jnp.bfloat16

D_MODEL = 1024
DEPTH = 2
GRID_W = 64
ROPE_BASE = 10000.0
RMS_EPS = 1e-6
H_A = 8
QK_NOPE = 64
QK_ROPE = 32
V_HD_A = 64
Q_LORA = 256
KV_LORA = 128
H_B = 4
DIFF_HD = 64
DIFF_VD = 2 * DIFF_HD
H_C = 8
DK_C = 64
DV_C = D_MODEL // H_C
D_FF = -(-8 * D_MODEL // (3 * 256)) * 256

LANES = 128
HEAD_PAD = LANES
ROPE_LANE0 = QK_NOPE
W_MLA = H_A * HEAD_PAD
W_DIFF = H_B * LANES
N_MOD = 6
COND_ROWS = 16
N_GATE = 2 * H_C
N_ROWS_N = 16

TM_PROJ = 1024
TM_PROJ_ROPE = 512
TM_PROJ1 = 1024
TQ_ATTN = 512
ATTN_SEQS_PER_STEP = 4
TM_TAIL = 1024
FF_CHUNK = 256
MLSTM_L = 128
MLSTM_UNROLL = 4
ADA_TN = 2048
VMEM_LIMIT = 56 * 1024 * 1024

assert D_FF % LANES == 0 and FF_CHUNK % LANES == 0


def _dot(a, b):
    return jnp.dot(a, b, preferred_element_type=F32)


def _dot_nt(a, b):
    return lax.dot_general(a, b, (((1,), (1,)), ((), ())), preferred_element_type=F32)


def _rms(x):
    return x * lax.rsqrt(jnp.mean(x * x, axis=-1, keepdims=True) + RMS_EPS)


def _modulate(x, g, shift, scale):
    return _rms(x) * (g * (1.0 + scale)) + shift


def _params(semantics):
    return pltpu.CompilerParams(dimension_semantics=semantics, vmem_limit_bytes=VMEM_LIMIT)


def _ada_kernel(cond_ref, w_ref, b_ref, o_ref):
    c = cond_ref[...]
    s = (c * jax.nn.sigmoid(c)).astype(BF16)
    o_ref[0] = _dot(s, w_ref[0].astype(BF16)) + b_ref[0]


def _ada(cond, w_ada, b_ada):
    n_out = w_ada.shape[-1]
    return pl.pallas_call(
        _ada_kernel,
        grid=(DEPTH, n_out // ADA_TN),
        in_specs=[
            pl.BlockSpec((COND_ROWS, D_MODEL), lambda l, n: (0, 0)),
            pl.BlockSpec((1, D_MODEL, ADA_TN), lambda l, n: (l, 0, n)),
            pl.BlockSpec((1, 1, ADA_TN), lambda l, n: (l, 0, n)),
        ],
        out_specs=pl.BlockSpec((1, COND_ROWS, ADA_TN), lambda l, n: (l, 0, n)),
        out_shape=jax.ShapeDtypeStruct((DEPTH, COND_ROWS, n_out), F32),
        compiler_params=_params(("arbitrary", "arbitrary")),
        name="ada",
    )(cond, w_ada, b_ada.reshape(DEPTH, 1, n_out))


def _rope(x, c, sa, sb, off):
    return x * c + pltpu.roll(x, LANES - off, 1) * sa + pltpu.roll(x, off, 1) * sb


def _proj0_kernel(*refs, rope, emit_cache):
    x_ref, mod_ref, g_ref, wa_ref, wd_ref, gq_ref, gkv_ref, wq_ref, wukv_ref = refs[:9]
    refs = refs[9:]
    if rope:
        cm_ref, sam_ref, sbm_ref, cd_ref, sad_ref, sbd_ref = refs[:6]
        refs = refs[6:]
    qm_ref, km_ref, vm_ref, dq_ref, dk_ref, dv_ref = refs[:6]
    refs = refs[6:]
    if emit_cache:
        ckvf_ref, krf_ref, dkf_ref, dvf_ref = refs

    h = _modulate(x_ref[...], g_ref[...], mod_ref[0, 0:1, :], mod_ref[0, 1:2, :]).astype(BF16)
    pa = _dot(h, wa_ref[...])
    pd = _dot(h, wd_ref[...])
    lane = lax.broadcasted_iota(jnp.int32, (pa.shape[0], LANES), 1)
    krb = jnp.where((lane >= ROPE_LANE0) & (lane < ROPE_LANE0 + QK_ROPE),
                    pltpu.roll(pa[:, Q_LORA + KV_LORA:Q_LORA + KV_LORA + LANES], ROPE_LANE0, 1), 0.0)
    cq = pa[:, 0:Q_LORA]
    ckv = _rms(pa[:, Q_LORA:Q_LORA + KV_LORA]) * gkv_ref[...]
    qa = _dot((_rms(cq) * gq_ref[...]).astype(BF16), wq_ref[...])
    kv = _dot(ckv.astype(BF16), wukv_ref[...])
    if emit_cache:
        ckvf_ref[...] = ckv
        krf_ref[...] = krb[:, ROPE_LANE0:ROPE_LANE0 + QK_ROPE]
    if rope:
        cm, sam, sbm = cm_ref[...], sam_ref[...], sbm_ref[...]
        cd, sad, sbd = cd_ref[...], sad_ref[...], sbd_ref[...]
        krb = _rope(krb, cm, sam, sbm, QK_ROPE // 4)
    for hd in range(H_A):
        sl = slice(hd * HEAD_PAD, (hd + 1) * HEAD_PAD)
        qh = qa[:, sl]
        if rope:
            qh = _rope(qh, cm, sam, sbm, QK_ROPE // 4)
        qm_ref[:, sl] = qh.astype(BF16)
        km_ref[:, sl] = (kv[:, sl] + krb).astype(BF16)
    vm_ref[...] = kv[:, H_A * HEAD_PAD:].astype(BF16)
    for hd in range(H_B):
        sl = slice(hd * LANES, (hd + 1) * LANES)
        dq = pd[:, hd * LANES:(hd + 1) * LANES]
        dk = pd[:, (H_B + hd) * LANES:(H_B + hd + 1) * LANES]
        dv = pd[:, (2 * H_B + hd) * LANES:(2 * H_B + hd + 1) * LANES]
        if emit_cache:
            seq = dkf_ref.shape[2]
            for bi in range(dkf_ref.shape[0]):
                dkf_ref[bi, hd] = dk[bi * seq:(bi + 1) * seq]
                dvf_ref[bi, hd] = dv[bi * seq:(bi + 1) * seq]
        if rope:
            dq = _rope(dq, cd, sad, sbd, DIFF_HD // 4)
            dk = _rope(dk, cd, sad, sbd, DIFF_HD // 4)
        dq_ref[:, sl] = (dq * (DIFF_HD ** -0.5)).astype(BF16)
        dk_ref[:, sl] = dk.astype(BF16)
        dv_ref[:, sl] = dv.astype(BF16)


def _proj0(x, mod, mod_row, g, w_ab, wd, gq, gkv, wq, wukv, rope_tabs, seq):
    n = x.shape[0]
    rope = rope_tabs is not None
    emit_cache = not rope
    tm = TM_PROJ_ROPE if rope else TM_PROJ
    tiles_per_seq = max(seq // tm, 1)
    full = lambda shape: pl.BlockSpec(shape, lambda i: (0,) * len(shape))
    in_specs = [
        pl.BlockSpec((tm, D_MODEL), lambda i: (i, 0)),
        pl.BlockSpec((1, N_MOD, D_MODEL), lambda i: (mod_row(i), 0, 0)),
        full((1, D_MODEL)), full((D_MODEL, Q_LORA + KV_LORA + LANES)), full(wd.shape),
        full((1, Q_LORA)), full((1, KV_LORA)), full(wq.shape), full(wukv.shape),
    ]
    args = [x, mod, g, w_ab, wd, gq, gkv, wq, wukv]
    if rope:
        in_specs += [pl.BlockSpec((tm, LANES), lambda i: (i % tiles_per_seq, 0))] * 6
        args += list(rope_tabs)
    row = lambda w: pl.BlockSpec((tm, w), lambda i: (i, 0))
    widths = (W_MLA, W_MLA, W_MLA, W_DIFF, W_DIFF, W_DIFF)
    out_specs = [row(w) for w in widths]
    out_shape = [jax.ShapeDtypeStruct((n, w), BF16) for w in widths]
    if emit_cache:
        assert tm % seq == 0
        cache = pl.BlockSpec((tm // seq, H_B, seq, LANES), lambda i: (i, 0, 0, 0))
        out_specs += [row(KV_LORA), row(QK_ROPE), cache, cache]
        out_shape += [jax.ShapeDtypeStruct((n, KV_LORA), F32), jax.ShapeDtypeStruct((n, QK_ROPE), F32),
                      jax.ShapeDtypeStruct((n // seq, H_B, seq, LANES), F32),
                      jax.ShapeDtypeStruct((n // seq, H_B, seq, LANES), F32)]
    return pl.pallas_call(
        functools.partial(_proj0_kernel, rope=rope, emit_cache=emit_cache),
        grid=(n // tm,),
        in_specs=in_specs, out_specs=out_specs, out_shape=out_shape,
        compiler_params=_params(("arbitrary",)),
        name="proj0_rope" if rope else "proj0_ctx",
    )(*args)


def _attn_kernel(*refs, has_ctx, lam_init, n_seq):
    q_ref, dq_ref, k_ref, v_ref, dk_ref, dv_ref, lamv_ref, gsub_ref = refs[:8]
    refs = refs[8:]
    if has_ctx:
        ckvc_ref, krc_ref, dkc_ref, dvc_ref, wukv_ref = refs[:5]
        o_ref, vx_s, dvx_s, kx_s, dkx_s = refs[5:]
    else:
        o_ref, vx_s, dvx_s = refs
    seq = k_ref.shape[0] // n_seq
    ktot = vx_s.shape[0] // n_seq
    assert n_seq == 1 or not has_ctx

    @pl.when(pl.program_id(1) == 0)
    def _():
        if has_ctx:
            kx_s[0:seq, :] = k_ref[...]
            dkx_s[0:seq, :] = dk_ref[...]
        own = n_seq * seq
        ones_own = jnp.ones((own, LANES), BF16)
        for hd in range(H_A):
            vx_s[0:own, 2 * hd * LANES:(2 * hd + 1) * LANES] = v_ref[:, hd * LANES:(hd + 1) * LANES]
            vx_s[0:own, (2 * hd + 1) * LANES:(2 * hd + 2) * LANES] = ones_own
        for hd in range(H_B):
            dvx_s[0:own, 2 * hd * LANES:(2 * hd + 1) * LANES] = dv_ref[:, hd * LANES:(hd + 1) * LANES]
            dvx_s[0:own, (2 * hd + 1) * LANES:(2 * hd + 2) * LANES] = ones_own
        if has_ctx:
            kv = _dot(ckvc_ref[0].astype(BF16), wukv_ref[...])
            krb = krc_ref[0]
            ones_ctx = jnp.ones((ktot - seq, LANES), BF16)
            for hd in range(H_A):
                sl = slice(hd * HEAD_PAD, (hd + 1) * HEAD_PAD)
                kx_s[seq:ktot, sl] = (kv[:, sl] + krb).astype(BF16)
                vx_s[seq:ktot, 2 * hd * LANES:(2 * hd + 1) * LANES] = (
                    kv[:, (H_A + hd) * HEAD_PAD:(H_A + hd + 1) * HEAD_PAD].astype(BF16))
                vx_s[seq:ktot, (2 * hd + 1) * LANES:(2 * hd + 2) * LANES] = ones_ctx
            for hd in range(H_B):
                sl = slice(hd * LANES, (hd + 1) * LANES)
                dkx_s[seq:ktot, sl] = dkc_ref[0, hd].astype(BF16)
                dvx_s[seq:ktot, 2 * hd * LANES:(2 * hd + 1) * LANES] = dvc_ref[0, hd].astype(BF16)
                dvx_s[seq:ktot, (2 * hd + 1) * LANES:(2 * hd + 2) * LANES] = ones_ctx

    log2e = 1.0 / math.log(2.0)
    tq = dq_ref.shape[0] // n_seq
    lo = lax.broadcasted_iota(jnp.int32, (tq, LANES), 1) < DIFF_HD

    keys_a, keys_b = (kx_s, dkx_s) if has_ctx else (k_ref, dk_ref)
    jobs = []
    for si in range(n_seq):
        qr = slice(si * tq, (si + 1) * tq)
        kr = slice(si * ktot, (si + 1) * ktot)
        for hd in range(H_A):
            sl = slice(hd * HEAD_PAD, (hd + 1) * HEAD_PAD)
            jobs.append((lambda qr=qr, sl=sl: q_ref[qr, sl], (keys_a, kr, sl), (vx_s, hd),
                         (QK_NOPE + QK_ROPE) ** -0.5 * log2e))
        for hd in range(H_B):
            sl = slice(hd * LANES, (hd + 1) * LANES)
            for part in range(2):
                def qfn(qr=qr, sl=sl, part=part):
                    dq = dq_ref[qr, sl].astype(F32)
                    return (jnp.where(lo, dq, 0.0) if part == 0 else jnp.where(lo, 0.0, dq)).astype(BF16)
                jobs.append((qfn, (keys_b, kr, sl), (dvx_s, hd), log2e))

    def scores(job):
        qfn, (kref, kr, sl), _, _ = job
        return _dot_nt(qfn(), kref[kr, sl])

    def finish(s, job):
        _, (_, kr, _), (vref, hd), c = job
        m = jnp.max(s, axis=-1, keepdims=True)
        e = jnp.exp2((s - m) * c).astype(BF16)
        res = _dot(e, vref[kr, 2 * hd * LANES:(2 * hd + 2) * LANES])
        return res[:, 0:LANES] * (1.0 / res[:, LANES:2 * LANES])

    outs = []
    s_next = scores(jobs[0])
    for i, job in enumerate(jobs):
        s_cur = s_next
        if i + 1 < len(jobs):
            s_next = scores(jobs[i + 1])
        outs.append(finish(s_cur, job))

    lv = lamv_ref[...]
    lam = (jnp.exp(jnp.sum(lv[0:1] * lv[1:2], axis=-1, keepdims=True))
           - jnp.exp(jnp.sum(lv[2:3] * lv[3:4], axis=-1, keepdims=True)) + lam_init)
    per_seq = H_A + 2 * H_B
    for si in range(n_seq):
        qr = slice(si * tq, (si + 1) * tq)
        o = outs[si * per_seq:(si + 1) * per_seq]
        for j in range(H_A // 2):
            o_ref[qr, j * LANES:(j + 1) * LANES] = (o[2 * j] + o[2 * j + 1]).astype(BF16)
        for hd in range(H_B):
            acc = o[H_A + 2 * hd] - lam * o[H_A + 2 * hd + 1]
            ob = (_rms(acc) * gsub_ref[...]) * (1.0 - lam_init)
            o_ref[qr, H_A * V_HD_A + hd * LANES:H_A * V_HD_A + (hd + 1) * LANES] = ob.astype(BF16)


def _attn0(qm, km, vm, dq, dk, dv, lamv, gsub, ctx, wukv, seq, lam_init):
    n = qm.shape[0]
    nb = n // seq
    tq = min(TQ_ATTN, seq)
    has_ctx = ctx is not None
    n_seq = 1 if has_ctx or tq != seq else ATTN_SEQS_PER_STEP
    assert nb % n_seq == 0
    full = lambda shape: pl.BlockSpec(shape, lambda b, t: (0,) * len(shape))
    qrow = lambda w: pl.BlockSpec((n_seq * tq, w), lambda b, t: (b * (seq // tq) + t, 0))
    krow = lambda w: pl.BlockSpec((n_seq * seq, w), lambda b, t: (b, 0))
    in_specs = [qrow(W_MLA), qrow(W_DIFF), krow(W_MLA), krow(W_MLA), krow(W_DIFF), krow(W_DIFF),
                full((4, DIFF_HD)), full((1, DIFF_VD))]
    args = [qm, dq, km, vm, dk, dv, lamv, gsub]
    past = 0
    if has_ctx:
        ckv_c, krb_c, dk_c, dv_c = ctx
        past = ckv_c.shape[1]
        in_specs += [pl.BlockSpec((1, past, LANES), lambda b, t: (b, 0, 0)),
                     pl.BlockSpec((1, past, LANES), lambda b, t: (b, 0, 0)),
                     pl.BlockSpec((1, H_B, past, LANES), lambda b, t: (b, 0, 0, 0)),
                     pl.BlockSpec((1, H_B, past, LANES), lambda b, t: (b, 0, 0, 0)),
                     full(wukv.shape)]
        args += [ckv_c, krb_c, dk_c, dv_c, wukv]
    ktot = n_seq * (seq + past)
    scratch = [pltpu.VMEM((ktot, 2 * W_MLA), BF16), pltpu.VMEM((ktot, 2 * W_DIFF), BF16)]
    if has_ctx:
        scratch += [pltpu.VMEM((ktot, W_MLA), BF16), pltpu.VMEM((ktot, W_DIFF), BF16)]
    return pl.pallas_call(
        functools.partial(_attn_kernel, has_ctx=has_ctx, lam_init=lam_init, n_seq=n_seq),
        grid=(nb // n_seq, seq // tq),
        in_specs=in_specs,
        out_specs=pl.BlockSpec((n_seq * tq, D_MODEL), lambda b, t: (b * (seq // tq) + t, 0)),
        out_shape=jax.ShapeDtypeStruct((n, D_MODEL), BF16),
        scratch_shapes=scratch,
        compiler_params=_params(("arbitrary", "arbitrary")),
        name="attn0_lat" if has_ctx else "attn0_ctx",
    )(*args)


def _tail_kernel(*refs, final):
    x_ref, a_ref, mod_ref, wo_ref, gf_ref, win_ref, wout_ref = refs[:7]
    if final:
        gfin_ref, o_ref = refs[7:]
    else:
        (o_ref,) = refs[7:]
    x1 = x_ref[...] + mod_ref[0, 2:3, :] * _dot(a_ref[...], wo_ref[...])
    h = _modulate(x1, gf_ref[...], mod_ref[0, 3:4, :], mod_ref[0, 4:5, :]).astype(BF16)
    bounds = list(range(0, D_FF, FF_CHUNK)) + [D_FF]
    acc = None
    for lo, hi in zip(bounds[:-1], bounds[1:]):
        a = _dot(h, win_ref[0, :, lo:hi])
        b = _dot(h, win_ref[0, :, D_FF + lo:D_FF + hi])
        act = ((a * jax.nn.sigmoid(a)) * b).astype(BF16)
        part = _dot(act, wout_ref[0, lo:hi, :])
        acc = part if acc is None else acc + part
    x2 = x1 + mod_ref[0, 5:6, :] * acc
    if final:
        x2 = _rms(x2) * gfin_ref[...]
    o_ref[...] = x2


def _tail(x, a, mod, mod_row, wo, gf, win, wout, layer, gfin, name):
    n = x.shape[0]
    tm = TM_TAIL
    final = gfin is not None
    full = lambda shape: pl.BlockSpec(shape, lambda i: (0,) * len(shape))
    resident = lambda shape: pl.BlockSpec((1,) + shape[1:], lambda i: (layer, 0, 0),
                                          pipeline_mode=pl.Buffered(1))
    in_specs = [
        pl.BlockSpec((tm, D_MODEL), lambda i: (i, 0)),
        pl.BlockSpec((tm, D_MODEL), lambda i: (i, 0)),
        pl.BlockSpec((1, N_MOD, D_MODEL), lambda i: (mod_row(i), 0, 0)),
        full(wo.shape), full((1, D_MODEL)), resident(win.shape), resident(wout.shape),
    ]
    args = [x, a, mod, wo, gf, win, wout]
    if final:
        in_specs.append(full((1, D_MODEL)))
        args.append(gfin)
    return pl.pallas_call(
        functools.partial(_tail_kernel, final=final),
        grid=(n // tm,),
        in_specs=in_specs,
        out_specs=pl.BlockSpec((tm, D_MODEL), lambda i: (i, 0)),
        out_shape=jax.ShapeDtypeStruct((n, D_MODEL), F32),
        compiler_params=_params(("arbitrary",)),
        name=name,
    )(*args)


def _split3(x):
    hi = x.astype(BF16)
    r1 = x - hi.astype(F32)
    mid = r1.astype(BF16)
    lo = (r1 - mid.astype(F32)).astype(BF16)
    return hi, mid, lo


def _proj1_kernel(x_ref, mod_ref, g_ref, w1_ref, wg_ref, bg_ref, q_ref, k_ref, v_ref, o_ref, gc_ref, gr_ref):
    tm = x_ref.shape[0]
    L = MLSTM_L
    hk = H_C * DK_C
    hv = H_C * DV_C
    h = _modulate(x_ref[...], g_ref[...], mod_ref[0, 0:1, :], mod_ref[0, 1:2, :]).astype(BF16)
    gates = _dot(h, wg_ref[...]) + bg_ref[...]
    proj = _dot(h, w1_ref[...])
    for blk in range(hk // LANES):
        sl = slice(blk * LANES, (blk + 1) * LANES)
        q_ref[sl, :] = jnp.transpose(proj[:, sl]).astype(BF16)
    for blk in range(hv // LANES):
        sl = slice(blk * LANES, (blk + 1) * LANES)
        v_ref[sl, :] = jnp.transpose(proj[:, 2 * hk + blk * LANES:2 * hk + (blk + 1) * LANES]).astype(BF16)
    k_ref[...] = proj[:, hk:2 * hk] * (DK_C ** -0.5)
    o_ref[...] = proj[:, 2 * hk + hv:2 * hk + 2 * hv]
    lf = jnp.minimum(gates, 0.0) - jnp.log1p(jnp.exp(-jnp.abs(gates)))
    r = lax.broadcasted_iota(jnp.int32, (L, L), 0)
    c = lax.broadcasted_iota(jnp.int32, (L, L), 1)
    pre = jnp.where(c <= r, 1.0, 0.0).astype(BF16)
    suf = jnp.where(c >= r, 1.0, 0.0).astype(BF16)
    parts = jnp.concatenate(_split3(lf), axis=1)
    fold = lambda t: t[:, 0:LANES] + t[:, LANES:2 * LANES] + t[:, 2 * LANES:3 * LANES]
    chunks = [parts[ck * L:(ck + 1) * L] for ck in range(tm // L)]
    lane = lax.broadcasted_iota(jnp.int32, (tm, LANES), 1)
    b_sum = jnp.where((lane % 4) >= 2,
                      jnp.concatenate([fold(_dot(suf, p)) for p in chunks], axis=0),
                      jnp.concatenate([fold(_dot(pre, p)) for p in chunks], axis=0))
    b = pltpu.roll(b_sum, LANES - N_GATE, 1)
    u = gates - b
    pos = lax.broadcasted_iota(jnp.int32, (tm, LANES), 0) % L
    bwd = (lane % 4) >= 2
    cm = u
    step = 1
    while step < L:
        below = jnp.where(pos >= step, pltpu.roll(cm, step, 0), -jnp.inf)
        above = jnp.where(pos < L - step, pltpu.roll(cm, tm - step, 0), -jnp.inf)
        cm = jnp.maximum(cm, jnp.where(bwd, above, below))
        step *= 2
    low = lane < N_GATE
    packed = (jnp.where(low, b, 0.0) + pltpu.roll(jnp.where(low, cm, 0.0), N_GATE, 1)
              + pltpu.roll(jnp.where(low, u, 0.0), 2 * N_GATE, 1))
    gc_ref[...] = packed
    gr_ref[...] = jnp.transpose(packed)[0:3 * N_GATE]


def _proj1(x, mod, mod_row, g, w1, wg, bg):
    n = x.shape[0]
    tm = TM_PROJ1
    assert tm % MLSTM_L == 0
    full = lambda shape: pl.BlockSpec(shape, lambda i: (0,) * len(shape))
    row = lambda w: pl.BlockSpec((tm, w), lambda i: (i, 0))
    col = lambda w: pl.BlockSpec((w, tm), lambda i: (0, i))
    hk, hv = H_C * DK_C, H_C * DV_C
    return pl.pallas_call(
        _proj1_kernel,
        grid=(n // tm,),
        in_specs=[row(D_MODEL), pl.BlockSpec((1, N_MOD, D_MODEL), lambda i: (mod_row(i), 0, 0)),
                  full((1, D_MODEL)), full((D_MODEL, 2 * hk + 2 * hv)), full(wg.shape), full((1, LANES))],
        out_specs=[col(hk), row(hk), col(hv), row(hv), row(LANES),
                   pl.BlockSpec((3 * N_GATE, tm), lambda i: (0, i))],
        out_shape=[jax.ShapeDtypeStruct((hk, n), BF16), jax.ShapeDtypeStruct((n, hk), F32),
                   jax.ShapeDtypeStruct((hv, n), BF16), jax.ShapeDtypeStruct((n, hv), F32),
                   jax.ShapeDtypeStruct((n, LANES), F32), jax.ShapeDtypeStruct((3 * N_GATE, n), F32)],
        compiler_params=_params(("arbitrary",)),
        name="proj1",
    )(x, mod, g, w1, wg, bg)


def _chain(j, d, hh):
    return (2 * j + d) * 2 + hh


def _mlstm_kernel(*refs, has_state, emit_state):
    q_ref, k_ref, v_ref, o_ref, gc_ref, gr_ref, gn_ref = refs[:7]
    refs = refs[7:]
    if has_state:
        c0_ref, n0_ref, m0_ref = refs[:3]
        refs = refs[3:]
    hs_ref = refs[0]
    refs = refs[1:]
    if emit_state:
        cf_ref, nst_ref, mst_ref = refs[:3]
        refs = refs[3:]
    cx_s, m_s, h_s = refs

    L = MLSTM_L
    seq = k_ref.shape[0]
    nc = seq // L
    npair = H_C // 2
    lane = lax.broadcasted_iota(jnp.int32, (1, LANES), 1)
    head_mask = [lane < DK_C, lane >= DK_C]
    ri = lax.broadcasted_iota(jnp.int32, (L, L), 0)
    ci = lax.broadcasted_iota(jnp.int32, (L, L), 1)
    causal = [ri <= ci, ri >= ci]
    ones_blk = jnp.ones((N_ROWS_N, L), BF16)
    chains = [(j, d, hh) for d in range(2) for j in range(npair) for hh in range(2)]

    h_s[...] = jnp.zeros_like(h_s)
    for j, d, hh in chains:
        ch = _chain(j, d, hh)
        if has_state:
            zpad = jnp.zeros((DK_C, DV_C), F32)
            c0 = c0_ref[0, d, 2 * j + hh]
            c0 = jnp.concatenate([c0, zpad] if hh == 0 else [zpad, c0], axis=0)
            cx_s[ch, 0:DV_C, :] = jnp.transpose(c0)
            n_row = jnp.where(head_mask[hh], n0_ref[0, d, j], 0.0)
            cx_s[ch, DV_C:DV_C + N_ROWS_N, :] = jnp.broadcast_to(n_row, (N_ROWS_N, LANES))
            m_s[ch] = m0_ref[0, d, 2 * j + hh]
        else:
            cx_s[ch] = jnp.zeros((DV_C + N_ROWS_N, LANES), F32)
            m_s[ch] = jnp.zeros((1, LANES), F32)

    def chunk_step(i, carry):
        sl = [pl.ds(pl.multiple_of(i * L, L), L), pl.ds(pl.multiple_of((nc - 1 - i) * L, L), L)]
        gcol = [gc_ref[sl[d], :] for d in range(2)]
        grow = [gr_ref[:, sl[d]] for d in range(2)]
        kpair = {(d, j): k_ref[sl[d], j * LANES:(j + 1) * LANES] for d in range(2) for j in range(npair)}
        kbf = {key: kk.astype(BF16) for key, kk in kpair.items()}
        zq = jnp.zeros((DK_C, L), BF16)
        qt = {}
        for d in range(2):
            for j in range(npair):
                qt[(j, d, 0)] = jnp.concatenate([q_ref[j * LANES:j * LANES + DK_C, sl[d]], zq], axis=0)
                qt[(j, d, 1)] = jnp.concatenate([zq, q_ref[j * LANES + DK_C:(j + 1) * LANES, sl[d]]], axis=0)
        st, u_bc, vt, row = {}, {}, {}, {}
        for d in range(2):
            for j in range(npair):
                pair_scores = _dot(kbf[(d, j)], jnp.concatenate([qt[(j, d, 0)], qt[(j, d, 1)]], axis=1))
                st[(j, d, 0)], st[(j, d, 1)] = pair_scores[:, 0:L], pair_scores[:, L:2 * L]
        for j, d, hh in chains:
            key = (j, d, hh)
            idx = 4 * j + 2 * d + hh
            edge = L - 1 if d == 0 else 0
            b_row = grow[d][idx:idx + 1, :]
            cm_row = grow[d][N_GATE + idx:N_GATE + idx + 1, :]
            u_col = gcol[d][:, 2 * N_GATE + idx:2 * N_GATE + idx + 1]
            m_prev = m_s[_chain(j, d, hh)][:, 0:1]
            row[key] = (b_row, cm_row, b_row[:, edge:edge + 1], cm_row[:, edge:edge + 1], m_prev)
            u_bc[key] = jnp.broadcast_to(u_col, (L, L))
            vt[key] = v_ref[(2 * j + hh) * DV_C:(2 * j + hh + 1) * DV_C, sl[d]]
        for j, d, hh in chains:
            key = (j, d, hh)
            ch = _chain(j, d, hh)
            b_row, cm_row, g_tot, cm_last, m_prev = row[key]
            head = slice((2 * j + hh) * DV_C, (2 * j + hh + 1) * DV_C)
            cx = cx_s[ch]
            big_m = jnp.maximum(m_prev, cm_row)
            s = st[key] * jnp.exp(jnp.where(causal[d], u_bc[key] - big_m, -jnp.inf))
            inter = jnp.exp(m_prev - big_m)
            vx = jnp.concatenate([vt[key], ones_blk], axis=0)
            lhs = jnp.concatenate([vx, cx.astype(BF16)], axis=1)
            rhs = jnp.concatenate([s.astype(BF16), (inter * qt[key].astype(F32)).astype(BF16)], axis=0)
            res = _dot(lhs, rhs)
            den = res[DV_C:DV_C + 1, :]
            hval = res[0:DV_C, :] * (1.0 / jnp.maximum(jnp.abs(den), jnp.exp(-(b_row + big_m))))
            h_s[head, sl[d]] = h_s[head, sl[d]] + hval

            m_top = jnp.maximum(m_prev, cm_last)
            kw = kpair[(d, j)] * jnp.exp(u_bc[key] - m_top)
            cx_s[ch] = jnp.exp(m_prev - m_top) * cx + _dot(vx, kw.astype(BF16))
            m_s[ch] = jnp.broadcast_to(g_tot + m_top, (1, LANES))
        return carry

    lax.fori_loop(0, nc, chunk_step, 0, unroll=min(nc, MLSTM_UNROLL))

    gn_col = jnp.broadcast_to(gn_ref[...], (DV_C, LANES))
    gn_col = jnp.concatenate([gn_col] * (seq // LANES), axis=1)
    for hd in range(H_C):
        sl = slice(hd * DV_C, (hd + 1) * DV_C)
        ht = h_s[sl, :]
        ms = jnp.mean(ht * ht, axis=0, keepdims=True)
        y = jnp.transpose((ht * lax.rsqrt(ms + RMS_EPS)) * gn_col) * jax.nn.sigmoid(o_ref[:, sl])
        hs_ref[:, sl] = y.astype(BF16)
    if emit_state:
        for j, d, hh in chains:
            ch = _chain(j, d, hh)
            cf_ref[0, 0, d, 2 * j + hh] = jnp.transpose(cx_s[ch, 0:DV_C, :])[hh * DK_C:(hh + 1) * DK_C, :]
            mst_ref[0, d * H_C + 2 * j + hh:d * H_C + 2 * j + hh + 1, :] = m_s[ch]
        for j in range(npair):
            for d in range(2):
                nst_ref[0, d * npair + j:d * npair + j + 1, :] = jnp.where(
                    head_mask[0], cx_s[_chain(j, d, 0), DV_C:DV_C + 1, :], cx_s[_chain(j, d, 1), DV_C:DV_C + 1, :])


def _mlstm(q, k, v, o, gc, gr, gn, state, seq, emit_state):
    n = k.shape[0]
    nb = n // seq
    npair = H_C // 2
    has_state = state is not None
    hk, hv = H_C * DK_C, H_C * DV_C
    in_specs = [
        pl.BlockSpec((hk, seq), lambda b: (0, b)),
        pl.BlockSpec((seq, hk), lambda b: (b, 0)),
        pl.BlockSpec((hv, seq), lambda b: (0, b)),
        pl.BlockSpec((seq, hv), lambda b: (b, 0)),
        pl.BlockSpec((seq, LANES), lambda b: (b, 0)),
        pl.BlockSpec((3 * N_GATE, seq), lambda b: (0, b)),
        pl.BlockSpec((DV_C, 1), lambda b: (0, 0)),
    ]
    args = [q, k, v, o, gc, gr, gn]
    if has_state:
        c0, n0, m0 = state
        in_specs += [pl.BlockSpec((1, 2, H_C, DK_C, DV_C), lambda b: (b, 0, 0, 0, 0)),
                     pl.BlockSpec((1, 2, npair, 1, LANES), lambda b: (b, 0, 0, 0, 0)),
                     pl.BlockSpec((1, 2, H_C, 1, LANES), lambda b: (b, 0, 0, 0, 0))]
        args += [c0, n0, m0]
    out_specs = [pl.BlockSpec((seq, hv), lambda b: (b, 0))]
    out_shape = [jax.ShapeDtypeStruct((n, hv), BF16)]
    if emit_state:
        out_specs += [pl.BlockSpec((1, 1, 2, H_C, DK_C, DV_C), lambda b: (b, 0, 0, 0, 0, 0)),
                      pl.BlockSpec((1, 2 * npair, LANES), lambda b: (b, 0, 0)),
                      pl.BlockSpec((1, 2 * H_C, LANES), lambda b: (b, 0, 0))]
        out_shape += [jax.ShapeDtypeStruct((nb, 1, 2, H_C, DK_C, DV_C), F32),
                      jax.ShapeDtypeStruct((nb, 2 * npair, LANES), F32),
                      jax.ShapeDtypeStruct((nb, 2 * H_C, LANES), F32)]
    n_chain = 2 * H_C
    return pl.pallas_call(
        functools.partial(_mlstm_kernel, has_state=has_state, emit_state=emit_state),
        grid=(nb,),
        in_specs=in_specs, out_specs=out_specs, out_shape=out_shape,
        scratch_shapes=[pltpu.VMEM((n_chain, DV_C + N_ROWS_N, LANES), F32), pltpu.VMEM((n_chain, 1, LANES), F32),
                        pltpu.VMEM((hv, seq), F32)],
        compiler_params=_params(("arbitrary",)),
        name="mlstm_lat" if has_state else "mlstm_ctx",
    )(*args)


def _rope_tables(n_tok):
    t = np.arange(n_tok)
    rows = (t // GRID_W).astype(np.float64)
    cols = (t % GRID_W).astype(np.float64)

    def axis_tabs(width, lane0):
        half = width // 2
        quarter = half // 2
        freqs = np.power(ROPE_BASE, -np.arange(quarter, dtype=np.float64) / quarter)
        c = np.ones((n_tok, LANES))
        sa = np.zeros((n_tok, LANES))
        sb = np.zeros((n_tok, LANES))
        for g, pos in enumerate((rows, cols)):
            ang = pos[:, None] * freqs[None, :]
            a0 = lane0 + g * half
            c[:, a0:a0 + quarter] = np.cos(ang)
            c[:, a0 + quarter:a0 + half] = np.cos(ang)
            sa[:, a0:a0 + quarter] = -np.sin(ang)
            sb[:, a0 + quarter:a0 + half] = np.sin(ang)
        return c, sa, sb

    cm, sam, sbm = axis_tabs(QK_ROPE, ROPE_LANE0)
    c0, sa0, sb0 = axis_tabs(DIFF_HD, 0)
    c1, sa1, sb1 = axis_tabs(DIFF_HD, DIFF_HD)
    cd = np.where(np.arange(LANES)[None, :] < DIFF_HD, c0, c1)
    return tuple(jnp.asarray(a, F32) for a in (cm, sam, sbm, cd, sa0 + sa1, sb0 + sb1))


def _prep_even(w_in_ab, w_uq, w_ukv):
    c2 = Q_LORA + KV_LORA
    w_ab = w_in_ab.astype(BF16)
    wd = w_ab[:, c2 + QK_ROPE:]
    wq = jnp.pad(w_uq.reshape(Q_LORA, H_A, QK_NOPE + QK_ROPE),
                 ((0, 0), (0, 0), (0, HEAD_PAD - QK_NOPE - QK_ROPE))).reshape(Q_LORA, H_A * HEAD_PAD)
    kvw = w_ukv.reshape(KV_LORA, H_A, QK_NOPE + V_HD_A)
    kpad = jnp.pad(kvw[..., :QK_NOPE], ((0, 0), (0, 0), (0, HEAD_PAD - QK_NOPE)))
    vw = kvw[..., QK_NOPE:]
    zv = jnp.zeros_like(vw)
    odd = (jnp.arange(H_A) % 2 == 1)[None, :, None]
    vpad = jnp.where(odd, jnp.concatenate([zv, vw], -1), jnp.concatenate([vw, zv], -1))
    wukv = jnp.concatenate([kpad.reshape(KV_LORA, -1), vpad.reshape(KV_LORA, -1)], axis=1)
    return (w_ab, wd), wq.astype(BF16), wukv.astype(BF16)


def _gate_order(g):
    lead = g.shape[:-1]
    g = g.reshape(lead + (2, 2, H_C // 2, 2))
    perm = tuple(range(len(lead))) + tuple(len(lead) + a for a in (1, 2, 0, 3))
    return g.transpose(perm).reshape(lead + (4 * H_C,))


def _prep_odd(w_in_c, b_gate_c):
    ng = 4 * H_C
    base = w_in_c.shape[1] - ng
    w1 = w_in_c.astype(BF16)
    wg = jnp.pad(_gate_order(w1[:, base:]), ((0, 0), (0, LANES - ng)))
    bg = jnp.pad(_gate_order(b_gate_c), (0, LANES - ng)).reshape(1, LANES)
    return w1, wg, bg


def kernel(x_prompt, x_sample, cache_mla_ckv, cache_mla_krope, cache_diff_k, cache_diff_v,
           state_mlstm_C, state_mlstm_n, state_mlstm_m, c, c_ctx,
           w_ada, b_ada, g_mix, g_ffn, w_ffn_in, w_ffn_out,
           w_in_ab, g_q_lora, g_kv_lora, w_uq, w_ukv, diff_lambda, g_diff_subln, w_out_ab,
           w_in_c, b_gate_c, g_mlstm, w_out_c, g_final):
    nbp, seq_p, _ = x_prompt.shape
    nbs, seq_s, _ = x_sample.shape
    past = cache_mla_ckv.shape[2]
    assert DEPTH == 2 and 1 + nbs <= COND_ROWS
    assert cache_mla_ckv.shape[1] == 1 and state_mlstm_C.shape[1] == 1

    cond = jnp.concatenate([c_ctx[None], c, jnp.zeros((COND_ROWS - 1 - nbs, D_MODEL), F32)], axis=0)
    mod = _ada(cond, w_ada, b_ada).reshape(DEPTH * COND_ROWS, N_MOD, D_MODEL)

    xp = x_prompt.reshape(nbp * seq_p, D_MODEL)
    xs = x_sample.reshape(nbs * seq_s, D_MODEL)
    row2 = lambda v: v.reshape(1, -1)

    def mod_rows(layer, tm):
        prompt = lambda i: layer * COND_ROWS
        sample = lambda i: layer * COND_ROWS + 1 + i // (seq_s // tm)
        return prompt, sample

    lam_init = 0.8 - 0.6 * math.exp(-0.3 * 0)
    w0, wq, wukv = _prep_even(w_in_ab[0], w_uq[0], w_ukv[0])
    rope_tabs = _rope_tables(seq_s)
    mrp, _ = mod_rows(0, TM_PROJ)
    _, mrs = mod_rows(0, TM_PROJ_ROPE)
    gq, gkv = row2(g_q_lora[0]), row2(g_kv_lora[0])
    outs_p = _proj0(xp, mod, mrp, row2(g_mix[0]), *w0, gq, gkv, wq, wukv, None, seq_p)
    outs_s = _proj0(xs, mod, mrs, row2(g_mix[0]), *w0, gq, gkv, wq, wukv, rope_tabs, seq_s)
    qm_p, km_p, vm_p, dq_p, dk_p, dv_p, ckv_new, kr_new, dk_new, dv_new = outs_p
    qm_s, km_s, vm_s, dq_s, dk_s, dv_s = outs_s
    gsub = row2(g_diff_subln[0])
    krb_c = jnp.pad(cache_mla_krope.reshape(nbs, past, QK_ROPE),
                    ((0, 0), (0, 0), (ROPE_LANE0, LANES - ROPE_LANE0 - QK_ROPE)))
    ctx = (cache_mla_ckv.reshape(nbs, past, KV_LORA), krb_c,
           cache_diff_k.reshape(nbs, H_B, past, 2 * DIFF_HD), cache_diff_v.reshape(nbs, H_B, past, DIFF_VD))
    a_p = _attn0(qm_p, km_p, vm_p, dq_p, dk_p, dv_p, diff_lambda[0], gsub, None, None, seq_p, lam_init)
    a_s = _attn0(qm_s, km_s, vm_s, dq_s, dk_s, dv_s, diff_lambda[0], gsub, ctx, wukv, seq_s, lam_init)
    mtp, mts = mod_rows(0, TM_TAIL)
    wo0 = w_out_ab[0].astype(BF16)
    win, wout = w_ffn_in.astype(BF16), w_ffn_out.astype(BF16)
    xp = _tail(xp, a_p, mod, mtp, wo0, row2(g_ffn[0]), win, wout, 0, None, "tail0_ctx")
    xs = _tail(xs, a_s, mod, mts, wo0, row2(g_ffn[0]), win, wout, 0, None, "tail0_lat")

    w1, wg, bg = _prep_odd(w_in_c[0], b_gate_c[0])
    mrp, mrs = mod_rows(1, TM_PROJ1)
    gn = g_mlstm[0].reshape(DV_C, 1)
    q_p, k_p, v_p, o_p, gc_p, gr_p = _proj1(xp, mod, mrp, row2(g_mix[1]), w1, wg, bg)
    q_s, k_s, v_s, o_s, gc_s, gr_s = _proj1(xs, mod, mrs, row2(g_mix[1]), w1, wg, bg)
    hs_p, c_new, nst, mst = _mlstm(q_p, k_p, v_p, o_p, gc_p, gr_p, gn, None, seq_p, True)
    state = (state_mlstm_C.reshape(nbs, 2, H_C, DK_C, DV_C),
             state_mlstm_n.reshape(nbs, 2, H_C // 2, 1, LANES),
             jnp.broadcast_to(state_mlstm_m.reshape(nbs, 2, H_C, 1, 1), (nbs, 2, H_C, 1, LANES)))
    (hs_s,) = _mlstm(q_s, k_s, v_s, o_s, gc_s, gr_s, gn, state, seq_s, False)
    mtp, mts = mod_rows(1, TM_TAIL)
    wo1 = w_out_c[0].astype(BF16)
    gfin = row2(g_final)
    yp = _tail(xp, hs_p, mod, mtp, wo1, row2(g_ffn[1]), win, wout, 1, gfin, "tail1_ctx")
    ys = _tail(xs, hs_s, mod, mts, wo1, row2(g_ffn[1]), win, wout, 1, gfin, "tail1_lat")

    return (yp.reshape(nbp, seq_p, D_MODEL), ys.reshape(nbs, seq_s, D_MODEL),
            ckv_new.reshape(nbp, 1, seq_p, KV_LORA), kr_new.reshape(nbp, 1, seq_p, QK_ROPE),
            dk_new.reshape(nbp, 1, H_B, seq_p, 2 * DIFF_HD), dv_new.reshape(nbp, 1, H_B, seq_p, DIFF_VD),
            c_new, nst.reshape(nbp, 1, 2, H_C, DK_C), mst[:, :, 0].reshape(nbp, 1, 2, H_C))
```
